```python
import jax, jax.numpy as jnp
from jax import lax
import numpy as np

D_MODEL = 2048
BATCH = 8
SEQ = 4096
DEPTH = 2

MEM_LEN = 256
EPS = 1e-6
HEAD_DIM = 64
N_Q_HEADS = 16
N_KV_HEADS = 2
Q_PER_KV = N_Q_HEADS // N_KV_HEADS
ATTN_WIDTH = N_Q_HEADS * HEAD_DIM
KV_WIDTH = N_KV_HEADS * HEAD_DIM
WINDOW = 128
BLOCK = 128
ROPE_DIM = HEAD_DIM // 4
ROPE_THETA = 500000.0
SGU_GROUPS = 8
SGU_WIDTH = D_MODEL // 2
SGU_GROUP_DIM = SGU_WIDTH // SGU_GROUPS
CHUNK = 128
IN_WIDTH = ATTN_WIDTH + 2 * KV_WIDTH + 2 * SGU_WIDTH
MIX_WIDTH = ATTN_WIDTH + SGU_WIDTH
POOL_WINDOWS = (2, 4, 8, 16)
N_POOL_GROUPS = len(POOL_WINDOWS)
POOL_GROUP_DIM = D_MODEL // N_POOL_GROUPS
X_HEADS = 4
X_HEAD_DIM = 128
X_WIDTH = X_HEADS * X_HEAD_DIM
D_FF = 5632
N_NORMS = 8
N_EVEN = (DEPTH + 1) // 2
N_ODD = DEPTH // 2

kernel_name = "hybrid_swa_sgu_pool_macaron"


def rms_norm(x, g):
    xf = x.astype(jnp.float32)
    y = xf * lax.rsqrt(jnp.mean(xf * xf, axis=-1, keepdims=True) + EPS)
    return (y * g.astype(jnp.float32)).astype(x.dtype)


def swiglu(h, wg, wu, wd):
    return (jax.nn.silu(h @ wg) * (h @ wu)) @ wd


def rope_tables(seq):
    half = ROPE_DIM // 2
    inv = ROPE_THETA ** (-jnp.arange(half, dtype=jnp.float32) * 2.0 / ROPE_DIM)
    ang = jnp.arange(seq, dtype=jnp.float32)[:, None] * inv[None, :]
    return jnp.cos(ang)[:, None, :], jnp.sin(ang)[:, None, :]


def partial_rope(x, cos, sin):
    xf = x.astype(jnp.float32)
    half = ROPE_DIM // 2
    x1 = xf[..., :half]
    x2 = xf[..., half:ROPE_DIM]
    rot = jnp.concatenate([x1 * cos - x2 * sin, x2 * cos + x1 * sin, xf[..., ROPE_DIM:]], axis=-1)
    return rot.astype(x.dtype)


def swa_sink_attention(q, k, v, sinks):
    b, s = q.shape[0], q.shape[1]
    nb = s // BLOCK
    qb = q.reshape(b, nb, BLOCK, N_KV_HEADS, Q_PER_KV, HEAD_DIM)
    pad = ((0, 0), (BLOCK, 0), (0, 0), (0, 0))
    kp = jnp.pad(k, pad).reshape(b, nb + 1, BLOCK, N_KV_HEADS, HEAD_DIM)
    vp = jnp.pad(v, pad).reshape(b, nb + 1, BLOCK, N_KV_HEADS, HEAD_DIM)
    kb = jnp.concatenate([kp[:, :-1], kp[:, 1:]], axis=2)
    vb = jnp.concatenate([vp[:, :-1], vp[:, 1:]], axis=2)
    scores = jnp.einsum('bnqhgd,bnkhd->bnhgqk', qb, kb,
                        preferred_element_type=jnp.float32) * (HEAD_DIM ** -0.5)
    qi = jnp.arange(BLOCK)[:, None]
    kj = jnp.arange(2 * BLOCK)[None, :]
    rel = qi + BLOCK - kj
    band = (rel >= 0) & (rel < WINDOW)
    not_pad = (jnp.arange(nb)[:, None, None] > 0) | (kj >= BLOCK)[None]
    valid = band[None] & not_pad
    scores = jnp.where(valid[None, :, None, None], scores, jnp.float32(-1e30))
    sink = jnp.broadcast_to(
        sinks.astype(jnp.float32).reshape(N_KV_HEADS, Q_PER_KV)[None, None, :, :, None, None],
        scores.shape[:-1] + (1,))
    probs = jax.nn.softmax(jnp.concatenate([scores, sink], axis=-1), axis=-1)[..., :-1]
    out = jnp.einsum('bnhgqk,bnkhd->bnqhgd', probs.astype(v.dtype), vb)
    return out.reshape(b, s, ATTN_WIDTH)


def chunked_spatial_gating(u, v, ln_g, ln_b, w_s, b_s):
    b, s = u.shape[0], u.shape[1]
    nc = s // CHUNK
    vf = v.astype(jnp.float32)
    mu = jnp.mean(vf, axis=-1, keepdims=True)
    var = jnp.mean(jnp.square(vf - mu), axis=-1, keepdims=True)
    vn = ((vf - mu) * lax.rsqrt(var + EPS) * ln_g.astype(jnp.float32) + ln_b.astype(jnp.float32)).astype(v.dtype)
    vc = vn.reshape(b, nc, CHUNK, SGU_GROUPS, SGU_GROUP_DIM)
    causal = jnp.tril(jnp.ones((CHUNK, CHUNK), dtype=w_s.dtype))
    mixed = jnp.einsum('gij,bnjgc->bnigc', w_s * causal[None], vc) \
        + jnp.transpose(b_s)[None, None, :, :, None]
    return u * mixed.reshape(b, s, SGU_WIDTH).astype(u.dtype)


def attn_sgu_mixer(h, w_in, w_out, sinks, ln_g, ln_b, w_s, b_s, cos, sin):
    b, s = h.shape[0], h.shape[1]
    z = h @ w_in
    o1 = ATTN_WIDTH
    o2 = o1 + KV_WIDTH
    o3 = o2 + KV_WIDTH
    o4 = o3 + SGU_WIDTH
    q = partial_rope(z[..., :o1].reshape(b, s, N_Q_HEADS, HEAD_DIM), cos, sin)
    k = partial_rope(z[..., o1:o2].reshape(b, s, N_KV_HEADS, HEAD_DIM), cos, sin)
    v = z[..., o2:o3].reshape(b, s, N_KV_HEADS, HEAD_DIM)
    attn = swa_sink_attention(q, k, v, sinks)
    gate = chunked_spatial_gating(jax.nn.gelu(z[..., o3:o4]), jax.nn.gelu(z[..., o4:]), ln_g, ln_b, w_s, b_s)
    return jnp.concatenate([attn, gate], axis=-1) @ w_out


def multiscale_pool_mixer(h, pool_w, pool_scale):
    b, s = h.shape[0], h.shape[1]
    hf = h.astype(jnp.float32).reshape(b, s, N_POOL_GROUPS, POOL_GROUP_DIM)
    cs = jnp.cumsum(hf, axis=1)
    count = jnp.arange(1, s + 1, dtype=jnp.float32)
    outs = []
    for gi, w in enumerate(POOL_WINDOWS):
        c = cs[:, :, gi]
        prev = jnp.pad(c, ((0, 0), (w, 0), (0, 0)))[:, :s]
        mean = (c - prev) / jnp.minimum(count, jnp.float32(w))[None, :, None]
        outs.append(mean - hf[:, :, gi])
    pooled = jnp.stack(outs, axis=2).astype(h.dtype)
    y = jnp.einsum('bsgc,gcd->bsgd', pooled, pool_w).reshape(b, s, D_MODEL)
    return y * pool_scale


def memory_cross_attention(h, mem_n, wq, wk, wv, wo):
    b, s = h.shape[0], h.shape[1]
    m = mem_n.shape[1]
    q = (h @ wq).reshape(b, s, X_HEADS, X_HEAD_DIM)
    k = (mem_n @ wk).reshape(b, m, X_HEADS, X_HEAD_DIM)
    v = (mem_n @ wv).reshape(b, m, X_HEADS, X_HEAD_DIM)
    sc = jnp.einsum('bshd,bmhd->bhsm', q, k, preferred_element_type=jnp.float32) * (X_HEAD_DIM ** -0.5)
    p = jax.nn.softmax(sc, axis=-1)
    o = jnp.einsum('bhsm,bmhd->bshd', p.astype(v.dtype), v).reshape(b, s, X_WIDTH)
    return o @ wo


def _fwd_setup_inputs(seed: int = 0) -> dict:
    key = jax.random.key(seed)
    ks = jax.random.split(key, 24)
    f32 = jnp.float32

    def w(k, shape, fan_in):
        return jax.random.normal(k, shape, f32) * (fan_in ** -0.5)

    return {
        "x": jax.random.normal(ks[0], (BATCH, SEQ, D_MODEL), f32),
        "mem": jax.random.normal(ks[1], (BATCH, MEM_LEN, D_MODEL), f32),
        "norms": 1.0 + 0.1 * jax.random.normal(ks[2], (DEPTH, N_NORMS, D_MODEL), f32),
        "mem_norm": 1.0 + 0.1 * jax.random.normal(ks[3], (DEPTH, D_MODEL), f32),
        "ffn1_wg": w(ks[4], (DEPTH, D_MODEL, D_FF), D_MODEL),
        "ffn1_wu": w(ks[5], (DEPTH, D_MODEL, D_FF), D_MODEL),
        "ffn1_wd": w(ks[6], (DEPTH, D_FF, D_MODEL), D_FF),
        "ffn2_wg": w(ks[7], (DEPTH, D_MODEL, D_FF), D_MODEL),
        "ffn2_wu": w(ks[8], (DEPTH, D_MODEL, D_FF), D_MODEL),
        "ffn2_wd": w(ks[9], (DEPTH, D_FF, D_MODEL), D_FF),
        "x_wq": w(ks[10], (DEPTH, D_MODEL, X_WIDTH), D_MODEL),
        "x_wk": w(ks[11], (DEPTH, D_MODEL, X_WIDTH), D_MODEL),
        "x_wv": w(ks[12], (DEPTH, D_MODEL, X_WIDTH), D_MODEL),
        "x_wo": w(ks[13], (DEPTH, X_WIDTH, D_MODEL), X_WIDTH),
        "mix_w_in": w(ks[14], (N_EVEN, D_MODEL, IN_WIDTH), D_MODEL),
        "mix_w_out": w(ks[15], (N_EVEN, MIX_WIDTH, D_MODEL), MIX_WIDTH),
        "attn_sinks": 0.5 * jax.random.normal(ks[16], (N_EVEN, N_Q_HEADS), f32),
        "sgu_ln_g": 1.0 + 0.1 * jax.random.normal(ks[17], (N_EVEN, SGU_WIDTH), f32),
        "sgu_ln_b": 0.02 * jax.random.normal(ks[18], (N_EVEN, SGU_WIDTH), f32),
        "sgu_w": w(ks[19], (N_EVEN, SGU_GROUPS, CHUNK, CHUNK), CHUNK),
        "sgu_b": 1.0 + 0.1 * jax.random.normal(ks[20], (N_EVEN, SGU_GROUPS, CHUNK), f32),
        "pool_w": w(ks[21], (N_ODD, N_POOL_GROUPS, POOL_GROUP_DIM, POOL_GROUP_DIM), POOL_GROUP_DIM),
        "pool_scale": 1.0 + 0.2 * jax.random.normal(ks[22], (N_ODD, D_MODEL), f32),
    }


def _fwd_reference(x, mem, norms, mem_norm, ffn1_wg, ffn1_wu, ffn1_wd, ffn2_wg, ffn2_wu, ffn2_wd,
              x_wq, x_wk, x_wv, x_wo, mix_w_in, mix_w_out, attn_sinks, sgu_ln_g, sgu_ln_b,
              sgu_w, sgu_b, pool_w, pool_scale):
    cos, sin = rope_tables(x.shape[1])
    for layer in range(DEPTH):
        g = norms[layer]
        h = rms_norm(x, g[0])
        x = x + 0.5 * rms_norm(swiglu(h, ffn1_wg[layer], ffn1_wu[layer], ffn1_wd[layer]), g[1])
        h = rms_norm(x, g[2])
        i = layer // 2
        if layer % 2 == 0:
            m = attn_sgu_mixer(h, mix_w_in[i], mix_w_out[i], attn_sinks[i], sgu_ln_g[i], sgu_ln_b[i],
                               sgu_w[i], sgu_b[i], cos, sin)
        else:
            m = multiscale_pool_mixer(h, pool_w[i], pool_scale[i])
        x = x + rms_norm(m, g[3])
        h = rms_norm(x, g[4])
        mem_n = rms_norm(mem, mem_norm[layer])
        x = x + rms_norm(memory_cross_attention(h, mem_n, x_wq[layer], x_wk[layer], x_wv[layer], x_wo[layer]), g[5])
        h = rms_norm(x, g[6])
        x = x + 0.5 * rms_norm(swiglu(h, ffn2_wg[layer], ffn2_wu[layer], ffn2_wd[layer]), g[7])
    return x


import jax as _jax
import jax.numpy as _jnp

TWIN_FORMAT = 'train_step'
FWD_PARAMS = ['x', 'mem', 'norms', 'mem_norm', 'ffn1_wg', 'ffn1_wu', 'ffn1_wd', 'ffn2_wg', 'ffn2_wu', 'ffn2_wd', 'x_wq', 'x_wk', 'x_wv', 'x_wo', 'mix_w_in', 'mix_w_out', 'attn_sinks', 'sgu_ln_g', 'sgu_ln_b', 'sgu_w', 'sgu_b', 'pool_w', 'pool_scale']
TWIN_WEIGHTS = ['norms', 'mem_norm', 'ffn1_wg', 'ffn1_wu', 'ffn1_wd', 'ffn2_wg', 'ffn2_wu', 'ffn2_wd', 'x_wq', 'x_wk', 'x_wv', 'x_wo', 'mix_w_in', 'mix_w_out', 'attn_sinks', 'sgu_ln_g', 'sgu_ln_b', 'sgu_w', 'sgu_b', 'pool_w', 'pool_scale']
TWIN_DIFF_INPUT = 'x'
TWIN_INPUTS = ['x', 'mem', 'norms', 'mem_norm', 'ffn1_wg', 'ffn1_wu', 'ffn1_wd', 'ffn2_wg', 'ffn2_wu', 'ffn2_wd', 'x_wq', 'x_wk', 'x_wv', 'x_wo', 'mix_w_in', 'mix_w_out', 'attn_sinks', 'sgu_ln_g', 'sgu_ln_b', 'sgu_w', 'sgu_b', 'pool_w', 'pool_scale', 'loss_target', 'm_norms', 'm_mem_norm', 'm_ffn1_wg', 'm_ffn1_wu', 'm_ffn1_wd', 'm_ffn2_wg', 'm_ffn2_wu', 'm_ffn2_wd', 'm_x_wq', 'm_x_wk', 'm_x_wv', 'm_x_wo', 'm_mix_w_in', 'm_mix_w_out', 'm_attn_sinks', 'm_sgu_ln_g', 'm_sgu_ln_b', 'm_sgu_w', 'm_sgu_b', 'm_pool_w', 'm_pool_scale', 'v_norms', 'v_mem_norm', 'v_ffn1_wg', 'v_ffn1_wu', 'v_ffn1_wd', 'v_ffn2_wg', 'v_ffn2_wu', 'v_ffn2_wd', 'v_x_wq', 'v_x_wk', 'v_x_wv', 'v_x_wo', 'v_mix_w_in', 'v_mix_w_out', 'v_attn_sinks', 'v_sgu_ln_g', 'v_sgu_ln_b', 'v_sgu_w', 'v_sgu_b', 'v_pool_w', 'v_pool_scale']
TWIN_OUTPUTS = ['loss', 'grad_x', 'grad_norms', 'grad_mem_norm', 'grad_ffn1_wg', 'grad_ffn1_wu', 'grad_ffn1_wd', 'grad_ffn2_wg', 'grad_ffn2_wu', 'grad_ffn2_wd', 'grad_x_wq', 'grad_x_wk', 'grad_x_wv', 'grad_x_wo', 'grad_mix_w_in', 'grad_mix_w_out', 'grad_attn_sinks', 'grad_sgu_ln_g', 'grad_sgu_ln_b', 'grad_sgu_w', 'grad_sgu_b', 'grad_pool_w', 'grad_pool_scale', 'delta_norms', 'delta_mem_norm', 'delta_ffn1_wg', 'delta_ffn1_wu', 'delta_ffn1_wd', 'delta_ffn2_wg', 'delta_ffn2_wu', 'delta_ffn2_wd', 'delta_x_wq', 'delta_x_wk', 'delta_x_wv', 'delta_x_wo', 'delta_mix_w_in', 'delta_mix_w_out', 'delta_attn_sinks', 'delta_sgu_ln_g', 'delta_sgu_ln_b', 'delta_sgu_w', 'delta_sgu_b', 'delta_pool_w', 'delta_pool_scale', 'new_m_norms', 'new_m_mem_norm', 'new_m_ffn1_wg', 'new_m_ffn1_wu', 'new_m_ffn1_wd', 'new_m_ffn2_wg', 'new_m_ffn2_wu', 'new_m_ffn2_wd', 'new_m_x_wq', 'new_m_x_wk', 'new_m_x_wv', 'new_m_x_wo', 'new_m_mix_w_in', 'new_m_mix_w_out', 'new_m_attn_sinks', 'new_m_sgu_ln_g', 'new_m_sgu_ln_b', 'new_m_sgu_w', 'new_m_sgu_b', 'new_m_pool_w', 'new_m_pool_scale', 'new_v_norms', 'new_v_mem_norm', 'new_v_ffn1_wg', 'new_v_ffn1_wu', 'new_v_ffn1_wd', 'new_v_ffn2_wg', 'new_v_ffn2_wu', 'new_v_ffn2_wd', 'new_v_x_wq', 'new_v_x_wk', 'new_v_x_wv', 'new_v_x_wo', 'new_v_mix_w_in', 'new_v_mix_w_out', 'new_v_attn_sinks', 'new_v_sgu_ln_g', 'new_v_sgu_ln_b', 'new_v_sgu_w', 'new_v_sgu_b', 'new_v_pool_w', 'new_v_pool_scale']
TWIN_LEAF_KINDS = {'loss': 'loss', 'grad_x': 'grad_x', 'grad_norms': 'grad_w', 'grad_mem_norm': 'grad_w', 'grad_ffn1_wg': 'grad_w', 'grad_ffn1_wu': 'grad_w', 'grad_ffn1_wd': 'grad_w', 'grad_ffn2_wg': 'grad_w', 'grad_ffn2_wu': 'grad_w', 'grad_ffn2_wd': 'grad_w', 'grad_x_wq': 'grad_w', 'grad_x_wk': 'grad_w', 'grad_x_wv': 'grad_w', 'grad_x_wo': 'grad_w', 'grad_mix_w_in': 'grad_w', 'grad_mix_w_out': 'grad_w', 'grad_attn_sinks': 'grad_w', 'grad_sgu_ln_g': 'grad_w', 'grad_sgu_ln_b': 'grad_w', 'grad_sgu_w': 'grad_w', 'grad_sgu_b': 'grad_w', 'grad_pool_w': 'grad_w', 'grad_pool_scale': 'grad_w', 'delta_norms': 'delta_w', 'delta_mem_norm': 'delta_w', 'delta_ffn1_wg': 'delta_w', 'delta_ffn1_wu': 'delta_w', 'delta_ffn1_wd': 'delta_w', 'delta_ffn2_wg': 'delta_w', 'delta_ffn2_wu': 'delta_w', 'delta_ffn2_wd': 'delta_w', 'delta_x_wq': 'delta_w', 'delta_x_wk': 'delta_w', 'delta_x_wv': 'delta_w', 'delta_x_wo': 'delta_w', 'delta_mix_w_in': 'delta_w', 'delta_mix_w_out': 'delta_w', 'delta_attn_sinks': 'delta_w', 'delta_sgu_ln_g': 'delta_w', 'delta_sgu_ln_b': 'delta_w', 'delta_sgu_w': 'delta_w', 'delta_sgu_b': 'delta_w', 'delta_pool_w': 'delta_w', 'delta_pool_scale': 'delta_w', 'new_m_norms': 'new_m', 'new_m_mem_norm': 'new_m', 'new_m_ffn1_wg': 'new_m', 'new_m_ffn1_wu': 'new_m', 'new_m_ffn1_wd': 'new_m', 'new_m_ffn2_wg': 'new_m', 'new_m_ffn2_wu': 'new_m', 'new_m_ffn2_wd': 'new_m', 'new_m_x_wq': 'new_m', 'new_m_x_wk': 'new_m', 'new_m_x_wv': 'new_m', 'new_m_x_wo': 'new_m', 'new_m_mix_w_in': 'new_m', 'new_m_mix_w_out': 'new_m', 'new_m_attn_sinks': 'new_m', 'new_m_sgu_ln_g': 'new_m', 'new_m_sgu_ln_b': 'new_m', 'new_m_sgu_w': 'new_m', 'new_m_sgu_b': 'new_m', 'new_m_pool_w': 'new_m', 'new_m_pool_scale': 'new_m', 'new_v_norms': 'new_v', 'new_v_mem_norm': 'new_v', 'new_v_ffn1_wg': 'new_v', 'new_v_ffn1_wu': 'new_v', 'new_v_ffn1_wd': 'new_v', 'new_v_ffn2_wg': 'new_v', 'new_v_ffn2_wu': 'new_v', 'new_v_ffn2_wd': 'new_v', 'new_v_x_wq': 'new_v', 'new_v_x_wk': 'new_v', 'new_v_x_wv': 'new_v', 'new_v_x_wo': 'new_v', 'new_v_mix_w_in': 'new_v', 'new_v_mix_w_out': 'new_v', 'new_v_attn_sinks': 'new_v', 'new_v_sgu_ln_g': 'new_v', 'new_v_sgu_ln_b': 'new_v', 'new_v_sgu_w': 'new_v', 'new_v_sgu_b': 'new_v', 'new_v_pool_w': 'new_v', 'new_v_pool_scale': 'new_v'}


def _forward(args):
    return _fwd_reference(*[args[k] for k in FWD_PARAMS])


def _output_shape():
    def fwd():
        inp = _fwd_setup_inputs(0)
        return _fwd_reference(*[inp[k] for k in FWD_PARAMS])
    out = _jax.eval_shape(fwd)
    return out.shape, out.dtype

N_MICROBATCH = 1
ADAM_LR = 0.001
ADAM_B1 = 0.9
ADAM_B2 = 0.999
ADAM_EPS = 1e-08
ADAM_WD = 0.01
ADAM_STEP = 10
PER_EXAMPLE_BATCH_AXIS = {'x': 0, 'mem': 0, 'loss_target': 0}
SHARED_INPUTS = []
_WEIGHT_DTYPES = {'norms': _jnp.float32, 'mem_norm': _jnp.float32, 'ffn1_wg': _jnp.float32, 'ffn1_wu': _jnp.float32, 'ffn1_wd': _jnp.float32, 'ffn2_wg': _jnp.float32, 'ffn2_wu': _jnp.float32, 'ffn2_wd': _jnp.float32, 'x_wq': _jnp.float32, 'x_wk': _jnp.float32, 'x_wv': _jnp.float32, 'x_wo': _jnp.float32, 'mix_w_in': _jnp.float32, 'mix_w_out': _jnp.float32, 'attn_sinks': _jnp.float32, 'sgu_ln_g': _jnp.float32, 'sgu_ln_b': _jnp.float32, 'sgu_w': _jnp.float32, 'sgu_b': _jnp.float32, 'pool_w': _jnp.float32, 'pool_scale': _jnp.float32}
MOMENT_SCALE = {'norms': 8.648623e+00, 'mem_norm': 2.854933e+00, 'ffn1_wg': 1.944292e-01, 'ffn1_wu': 2.415039e-01, 'ffn1_wd': 4.001084e-01, 'ffn2_wg': 2.228883e-01, 'ffn2_wu': 3.351732e-01, 'ffn2_wd': 5.565828e-01, 'x_wq': 1.579040e+00, 'x_wk': 1.656317e+00, 'x_wv': 5.340425e+00, 'x_wo': 2.781386e+00, 'mix_w_in': 4.393830e-01, 'mix_w_out': 2.028517e+00, 'attn_sinks': 1.011267e-01, 'sgu_ln_g': 3.546472e-01, 'sgu_ln_b': 3.439333e-01, 'sgu_w': 3.057548e-01, 'sgu_b': 5.104595e-01, 'pool_w': 1.150841e+00, 'pool_scale': 3.421642e+00}


def _to_microbatches(a, axis):
    t = _jnp.moveaxis(a, axis, 0)
    t = t.reshape((N_MICROBATCH, t.shape[0] // N_MICROBATCH) + t.shape[1:])
    return _jnp.moveaxis(t, 1, axis + 1)


def setup_inputs(seed: int = 0) -> dict:
    inp = _fwd_setup_inputs(seed)
    key = _jax.random.fold_in(_jax.random.key(seed), 7919)
    shape, _ = _output_shape()
    out = dict(inp)
    out["loss_target"] = _jax.random.normal(_jax.random.fold_in(key, 0), shape, _jnp.float32)
    for i, name in enumerate(TWIN_WEIGHTS):
        w = inp[name].astype(_jnp.float32)
        if MOMENT_SCALE is None:
            s = _jnp.sqrt(_jnp.mean(_jnp.square(w)) + 1e-30)
        else:
            s = MOMENT_SCALE[name]
        km, kv = _jax.random.split(_jax.random.fold_in(key, i + 1))
        out[name] = w
        out["m_" + name] = s * _jax.random.normal(km, w.shape, _jnp.float32)
        out["v_" + name] = (s * s) * _jax.random.uniform(kv, w.shape, _jnp.float32, 0.5, 1.5)
    if N_MICROBATCH > 1:
        for name, axis in PER_EXAMPLE_BATCH_AXIS.items():
            out[name] = _to_microbatches(out[name], axis)
    return {'x': out['x'], 'mem': out['mem'], 'norms': out['norms'], 'mem_norm': out['mem_norm'], 'ffn1_wg': out['ffn1_wg'], 'ffn1_wu': out['ffn1_wu'], 'ffn1_wd': out['ffn1_wd'], 'ffn2_wg': out['ffn2_wg'], 'ffn2_wu': out['ffn2_wu'], 'ffn2_wd': out['ffn2_wd'], 'x_wq': out['x_wq'], 'x_wk': out['x_wk'], 'x_wv': out['x_wv'], 'x_wo': out['x_wo'], 'mix_w_in': out['mix_w_in'], 'mix_w_out': out['mix_w_out'], 'attn_sinks': out['attn_sinks'], 'sgu_ln_g': out['sgu_ln_g'], 'sgu_ln_b': out['sgu_ln_b'], 'sgu_w': out['sgu_w'], 'sgu_b': out['sgu_b'], 'pool_w': out['pool_w'], 'pool_scale': out['pool_scale'], 'loss_target': out['loss_target'], 'm_norms': out['m_norms'], 'm_mem_norm': out['m_mem_norm'], 'm_ffn1_wg': out['m_ffn1_wg'], 'm_ffn1_wu': out['m_ffn1_wu'], 'm_ffn1_wd': out['m_ffn1_wd'], 'm_ffn2_wg': out['m_ffn2_wg'], 'm_ffn2_wu': out['m_ffn2_wu'], 'm_ffn2_wd': out['m_ffn2_wd'], 'm_x_wq': out['m_x_wq'], 'm_x_wk': out['m_x_wk'], 'm_x_wv': out['m_x_wv'], 'm_x_wo': out['m_x_wo'], 'm_mix_w_in': out['m_mix_w_in'], 'm_mix_w_out': out['m_mix_w_out'], 'm_attn_sinks': out['m_attn_sinks'], 'm_sgu_ln_g': out['m_sgu_ln_g'], 'm_sgu_ln_b': out['m_sgu_ln_b'], 'm_sgu_w': out['m_sgu_w'], 'm_sgu_b': out['m_sgu_b'], 'm_pool_w': out['m_pool_w'], 'm_pool_scale': out['m_pool_scale'], 'v_norms': out['v_norms'], 'v_mem_norm': out['v_mem_norm'], 'v_ffn1_wg': out['v_ffn1_wg'], 'v_ffn1_wu': out['v_ffn1_wu'], 'v_ffn1_wd': out['v_ffn1_wd'], 'v_ffn2_wg': out['v_ffn2_wg'], 'v_ffn2_wu': out['v_ffn2_wu'], 'v_ffn2_wd': out['v_ffn2_wd'], 'v_x_wq': out['v_x_wq'], 'v_x_wk': out['v_x_wk'], 'v_x_wv': out['v_x_wv'], 'v_x_wo': out['v_x_wo'], 'v_mix_w_in': out['v_mix_w_in'], 'v_mix_w_out': out['v_mix_w_out'], 'v_attn_sinks': out['v_attn_sinks'], 'v_sgu_ln_g': out['v_sgu_ln_g'], 'v_sgu_ln_b': out['v_sgu_ln_b'], 'v_sgu_w': out['v_sgu_w'], 'v_sgu_b': out['v_sgu_b'], 'v_pool_w': out['v_pool_w'], 'v_pool_scale': out['v_pool_scale']}


def _loss(weights, diff, rest, loss_target):
    with _jax.named_scope("forward"):
        args = {**rest, TWIN_DIFF_INPUT: diff, **{k: w.astype(_WEIGHT_DTYPES[k]) for k, w in weights.items()}}
        y = _forward(args)
    with _jax.named_scope("loss_head"):
        err = _jnp.square(y.astype(_jnp.float32) - loss_target)
        return 0.5 * _jnp.sum(_jnp.mean(err, axis=-1)) if err.ndim else 0.5 * err


def _adamw(w, g, m, v):
    m = ADAM_B1 * m + (1.0 - ADAM_B1) * g
    v = ADAM_B2 * v + (1.0 - ADAM_B2) * _jnp.square(g)
    m_hat = m / (1.0 - ADAM_B1 ** ADAM_STEP)
    v_hat = v / (1.0 - ADAM_B2 ** ADAM_STEP)
    delta = -ADAM_LR * (m_hat / (_jnp.sqrt(v_hat) + ADAM_EPS) + ADAM_WD * w)
    return delta, m, v


def reference(x, mem, norms, mem_norm, ffn1_wg, ffn1_wu, ffn1_wd, ffn2_wg, ffn2_wu, ffn2_wd, x_wq, x_wk, x_wv, x_wo, mix_w_in, mix_w_out, attn_sinks, sgu_ln_g, sgu_ln_b, sgu_w, sgu_b, pool_w, pool_scale, loss_target, m_norms, m_mem_norm, m_ffn1_wg, m_ffn1_wu, m_ffn1_wd, m_ffn2_wg, m_ffn2_wu, m_ffn2_wd, m_x_wq, m_x_wk, m_x_wv, m_x_wo, m_mix_w_in, m_mix_w_out, m_attn_sinks, m_sgu_ln_g, m_sgu_ln_b, m_sgu_w, m_sgu_b, m_pool_w, m_pool_scale, v_norms, v_mem_norm, v_ffn1_wg, v_ffn1_wu, v_ffn1_wd, v_ffn2_wg, v_ffn2_wu, v_ffn2_wd, v_x_wq, v_x_wk, v_x_wv, v_x_wo, v_mix_w_in, v_mix_w_out, v_attn_sinks, v_sgu_ln_g, v_sgu_ln_b, v_sgu_w, v_sgu_b, v_pool_w, v_pool_scale):
    given = dict(x=x, mem=mem, norms=norms, mem_norm=mem_norm, ffn1_wg=ffn1_wg, ffn1_wu=ffn1_wu, ffn1_wd=ffn1_wd, ffn2_wg=ffn2_wg, ffn2_wu=ffn2_wu, ffn2_wd=ffn2_wd, x_wq=x_wq, x_wk=x_wk, x_wv=x_wv, x_wo=x_wo, mix_w_in=mix_w_in, mix_w_out=mix_w_out, attn_sinks=attn_sinks, sgu_ln_g=sgu_ln_g, sgu_ln_b=sgu_ln_b, sgu_w=sgu_w, sgu_b=sgu_b, pool_w=pool_w, pool_scale=pool_scale, loss_target=loss_target, m_norms=m_norms, m_mem_norm=m_mem_norm, m_ffn1_wg=m_ffn1_wg, m_ffn1_wu=m_ffn1_wu, m_ffn1_wd=m_ffn1_wd, m_ffn2_wg=m_ffn2_wg, m_ffn2_wu=m_ffn2_wu, m_ffn2_wd=m_ffn2_wd, m_x_wq=m_x_wq, m_x_wk=m_x_wk, m_x_wv=m_x_wv, m_x_wo=m_x_wo, m_mix_w_in=m_mix_w_in, m_mix_w_out=m_mix_w_out, m_attn_sinks=m_attn_sinks, m_sgu_ln_g=m_sgu_ln_g, m_sgu_ln_b=m_sgu_ln_b, m_sgu_w=m_sgu_w, m_sgu_b=m_sgu_b, m_pool_w=m_pool_w, m_pool_scale=m_pool_scale, v_norms=v_norms, v_mem_norm=v_mem_norm, v_ffn1_wg=v_ffn1_wg, v_ffn1_wu=v_ffn1_wu, v_ffn1_wd=v_ffn1_wd, v_ffn2_wg=v_ffn2_wg, v_ffn2_wu=v_ffn2_wu, v_ffn2_wd=v_ffn2_wd, v_x_wq=v_x_wq, v_x_wk=v_x_wk, v_x_wv=v_x_wv, v_x_wo=v_x_wo, v_mix_w_in=v_mix_w_in, v_mix_w_out=v_mix_w_out, v_attn_sinks=v_attn_sinks, v_sgu_ln_g=v_sgu_ln_g, v_sgu_ln_b=v_sgu_ln_b, v_sgu_w=v_sgu_w, v_sgu_b=v_sgu_b, v_pool_w=v_pool_w, v_pool_scale=v_pool_scale)
    weights = {n: given[n] for n in TWIN_WEIGHTS}
    shared = {n: given[n] for n in SHARED_INPUTS}
    per_example = {n: given[n] for n in ['x', 'mem']}
    grad_fn = _jax.value_and_grad(_loss, argnums=(0, 1))

    def one_microbatch(ex, loss_target):
        ex = dict(ex)
        diff = ex.pop(TWIN_DIFF_INPUT)
        return grad_fn(weights, diff, {**shared, **ex}, loss_target)

    if N_MICROBATCH == 1:
        loss, (grad_w, grad_x) = one_microbatch(per_example, given["loss_target"])
    else:
        def body(carry, xs):
            loss_sum, grad_sum = carry
            l_k, (gw_k, gx_k) = one_microbatch(xs[0], xs[1])
            with _jax.named_scope("update"):
                return (loss_sum + l_k, _jax.tree.map(_jnp.add, grad_sum, gw_k)), gx_k

        init = (_jnp.zeros((), _jnp.float32), _jax.tree.map(_jnp.zeros_like, weights))
        (loss, grad_w), grad_x = _jax.lax.scan(body, init, (per_example, given["loss_target"]))
    with _jax.named_scope("update"):
        delta_w, new_m, new_v = {}, {}, {}
        for n in TWIN_WEIGHTS:
            delta_w[n], new_m[n], new_v[n] = _adamw(weights[n], grad_w[n], given["m_" + n], given["v_" + n])
    return (loss, grad_x, *[grad_w[n] for n in TWIN_WEIGHTS], *[delta_w[n] for n in TWIN_WEIGHTS],
            *[new_m[n] for n in TWIN_WEIGHTS], *[new_v[n] for n in TWIN_WEIGHTS])
```

```python
import math

import jax
import jax.numpy as jnp
from jax import lax
from jax.experimental import pallas as pl
from jax.experimental.pallas import tpu as pltpu

F32 = jnp.float32
BF16 = jnp.bfloat16
NDEV = 8
MIB = 1024 * 1024
LANES = 128

RMS_EPS = 1e-6
HEAD_DIM = 64
N_Q_HEADS = 16
Q_PER_KV = 8
ATTN_WIDTH = 1024
KV_WIDTH = 128
BLOCK = 128
ROPE_DIM = 16
ROPE_THETA = 500000.0
SGU_GROUPS = 8
SGU_WIDTH = 1024
CHUNK = 128
POOL_WINDOWS = (2, 4, 8, 16)
POOL_GROUP_DIM = 512
POOL_HALO = 16
X_HEADS = 4
X_HEAD_DIM = 128
ZQ, ZU, ZV, ZK = 0, 1024, 2048, 3072
IN_WIDTH = 3328

ADAM_LR = 0.001
ADAM_B1 = 0.9
ADAM_B2 = 0.999
ADAM_EPS = 1e-08
ADAM_WD = 0.01
ADAM_STEP = 10

_DN = {
    "nn": (((1,), (0,)), ((), ())),
    "nt": (((1,), (1,)), ((), ())),
    "tn": (((0,), (0,)), ((), ())),
}


def _cp(naxes, vmem_mib=48):
    return pltpu.CompilerParams(dimension_semantics=("arbitrary",) * naxes, vmem_limit_bytes=vmem_mib * MIB)


def _tile(n, pref):
    t = min(n, pref)
    while n % t:
        t //= 2
    return t


def _dot(a, b, dims="nn"):
    return lax.dot_general(a, b, _DN[dims], preferred_element_type=F32)


def _me():
    x, y, c = lax.axis_index("x"), lax.axis_index("y"), lax.axis_index("c")
    return x, y, c, 4 * x + 2 * y + c


def _peer(k):
    x, y, c, _ = _me()
    px = 1 - x if k & 4 else x
    py = 1 - y if k & 2 else y
    pc = 1 - c if k & 1 else c
    return (px, py, pc), 4 * px + 2 * py + pc


def _exchange(name, arrs, gather):
    n = len(arrs)

    def body(*refs):
        ins, outs = refs[:n], refs[n:2 * n]
        send, recv, loc = refs[2 * n:]
        me = _me()[3]
        local = []
        for i in range(n):
            cp = pltpu.make_async_copy(ins[i] if gather else ins[i].at[me], outs[i].at[me], loc.at[i])
            cp.start()
            local.append(cp)

        def remote(i, k, dst_slot):
            dev, slot = _peer(k)
            return pltpu.make_async_remote_copy(
                src_ref=ins[i] if gather else ins[i].at[slot],
                dst_ref=outs[i].at[me if dst_slot is None else dst_slot],
                send_sem=send.at[i * (NDEV - 1) + k - 1],
                recv_sem=recv.at[i * (NDEV - 1) + k - 1],
                device_id=dev,
                device_id_type=pl.DeviceIdType.MESH,
            )

        sent = []
        for k in range(1, NDEV):
            for i in range(n):
                cp = remote(i, k, None)
                cp.start()
                sent.append(cp)
        for cp in sent:
            cp.wait_send()
        for k in range(1, NDEV):
            for i in range(n):
                remote(i, k, _peer(k)[1]).wait_recv()
        for cp in local:
            cp.wait()

    hbm = pl.BlockSpec(memory_space=pltpu.HBM)
    out_shape = [jax.ShapeDtypeStruct(((NDEV,) + a.shape) if gather else a.shape, a.dtype) for a in arrs]
    return pl.pallas_call(
        body,
        out_shape=out_shape,
        in_specs=[hbm] * n,
        out_specs=[hbm] * n,
        scratch_shapes=[
            pltpu.SemaphoreType.DMA((n * (NDEV - 1),)),
            pltpu.SemaphoreType.DMA((n * (NDEV - 1),)),
            pltpu.SemaphoreType.DMA((n,)),
        ],
        name=name,
    )(*arrs)


def _pack(arrs, dtype=F32):
    flat = jnp.concatenate([a.astype(dtype).reshape(-1) for a in arrs])
    n = flat.shape[0]
    total = -(-n // (16 * LANES)) * (16 * LANES)
    return jnp.pad(flat, (0, total - n)).reshape(total // LANES, LANES)


def _unpack(packed, shapes, lead=()):
    flat = packed.reshape(lead + (-1,))
    out, off = [], 0
    for s in shapes:
        n = math.prod(s)
        out.append(flat[..., off:off + n].reshape(lead + tuple(s)))
        off += n
    return out


def _mm(name, a, b, *, dims, grid, a_spec, b_spec, o_spec, out_shape, acc_shape=None, nk=1, vmem_mib=48):
    nax = len(grid)

    def body(a_ref, b_ref, o_ref, *scratch):
        p = _dot(a_ref[...], b_ref[...], dims)
        if nk == 1:
            o_ref[...] = p.astype(o_ref.dtype)
            return
        acc = scratch[0]
        k = pl.program_id(nax - 1)

        @pl.when(k == 0)
        def _():
            acc[...] = p

        @pl.when(k > 0)
        def _():
            acc[...] += p

        @pl.when(k == nk - 1)
        def _():
            o_ref[...] = acc[...].astype(o_ref.dtype)

    return pl.pallas_call(
        body,
        out_shape=out_shape,
        grid=grid,
        in_specs=[a_spec, b_spec],
        out_specs=o_spec,
        scratch_shapes=[pltpu.VMEM(acc_shape, F32)] if nk > 1 else [],
        compiler_params=_cp(nax, vmem_mib),
        name=name,
    )(a, b)


def mm_nn(name, a, b, out_dtype, tn=None):
    m, k = a.shape
    n = b.shape[1]
    tm = _tile(m, 512)
    tn = n if tn is None else tn
    return _mm(
        name, a, b, dims="nn", grid=(n // tn, m // tm),
        a_spec=pl.BlockSpec((tm, k), lambda j, i: (i, 0)),
        b_spec=pl.BlockSpec((k, tn), lambda j, i: (0, j)),
        o_spec=pl.BlockSpec((tm, tn), lambda j, i: (i, j)),
        out_shape=jax.ShapeDtypeStruct((m, n), out_dtype),
    )


def mm_nt(name, a, b, out_dtype, tn=None):
    m, k = a.shape
    n = b.shape[0]
    tm = _tile(m, 512)
    tn = n if tn is None else tn
    return _mm(
        name, a, b, dims="nt", grid=(n // tn, m // tm),
        a_spec=pl.BlockSpec((tm, k), lambda j, i: (i, 0)),
        b_spec=pl.BlockSpec((tn, k), lambda j, i: (j, 0)),
        o_spec=pl.BlockSpec((tm, tn), lambda j, i: (i, j)),
        out_shape=jax.ShapeDtypeStruct((m, n), out_dtype),
    )


def mm_tn(name, a, b, out_dtype, tmo=None, tno=None):
    k, m = a.shape
    n = b.shape[1]
    tk = _tile(k, 512)
    tmo = _tile(m, 1024) if tmo is None else tmo
    tno = n if tno is None else tno
    return _mm(
        name, a, b, dims="tn", grid=(m // tmo, n // tno, k // tk),
        a_spec=pl.BlockSpec((tk, tmo), lambda i, j, kk: (kk, i)),
        b_spec=pl.BlockSpec((tk, tno), lambda i, j, kk: (kk, j)),
        o_spec=pl.BlockSpec((tmo, tno), lambda i, j, kk: (i, j)),
        out_shape=jax.ShapeDtypeStruct((m, n), out_dtype),
        acc_shape=(tmo, tno), nk=k // tk,
    )


def _rstd(x):
    return lax.rsqrt(jnp.mean(x * x, axis=-1, keepdims=True) + RMS_EPS)


def norm_fwd(name, x, g, out_dtype):
    t, d = x.shape
    tm = _tile(t, 256)

    def body(x_ref, g_ref, o_ref):
        xv = x_ref[...]
        o_ref[...] = (xv * _rstd(xv) * g_ref[...]).astype(o_ref.dtype)

    row = pl.BlockSpec((tm, d), lambda i: (i, 0))
    return pl.pallas_call(
        body, out_shape=jax.ShapeDtypeStruct((t, d), out_dtype), grid=(t // tm,),
        in_specs=[row, pl.BlockSpec((1, d), lambda i: (0, 0))], out_specs=row,
        compiler_params=_cp(1), name=name,
    )(x, g)


def resid_norm_fwd(name, x, m, g, scale):
    t, d = x.shape
    tm = _tile(t, 256)

    def body(x_ref, m_ref, g_ref, o_ref):
        mv = m_ref[...]
        o_ref[...] = x_ref[...] + scale * (mv * _rstd(mv) * g_ref[...])

    row = pl.BlockSpec((tm, d), lambda i: (i, 0))
    return pl.pallas_call(
        body, out_shape=jax.ShapeDtypeStruct((t, d), F32), grid=(t // tm,),
        in_specs=[row, row, pl.BlockSpec((1, d), lambda i: (0, 0))], out_specs=row,
        compiler_params=_cp(1), name=name,
    )(x, m, g)


def norm_bwd(name, u, g, dy, scale, resid, out_dtype):
    t, d = u.shape
    tm = _tile(t, 256)
    has_resid = resid is not None

    def body(*refs):
        if has_resid:
            u_ref, g_ref, dy_ref, r_ref, du_ref, dg_ref = refs
        else:
            u_ref, g_ref, dy_ref, du_ref, dg_ref = refs
        uv = u_ref[...]
        dyv = dy_ref[...].astype(F32) * scale
        r = _rstd(uv)
        uh = uv * r

        @pl.when(pl.program_id(0) == 0)
        def _():
            dg_ref[...] = jnp.zeros_like(dg_ref)

        dg_ref[...] += jnp.sum(dyv * uh, axis=0, keepdims=True)
        dyg = dyv * g_ref[...]
        du = r * (dyg - uh * jnp.mean(dyg * uh, axis=-1, keepdims=True))
        if has_resid:
            du = du + r_ref[...]
        du_ref[...] = du.astype(du_ref.dtype)

    row = pl.BlockSpec((tm, d), lambda i: (i, 0))
    vec = pl.BlockSpec((1, d), lambda i: (0, 0))
    args = (u, g, dy) + ((resid,) if has_resid else ())
    return pl.pallas_call(
        body,
        out_shape=(jax.ShapeDtypeStruct((t, d), out_dtype), jax.ShapeDtypeStruct((1, d), F32)),
        grid=(t // tm,),
        in_specs=[row, vec, row] + ([row] if has_resid else []),
        out_specs=(row, vec),
        compiler_params=_cp(1), name=name,
    )(*args)


def loss_grad(y, target):
    t, d = y.shape
    tm = _tile(t, 256)
    nt = t // tm

    def body(y_ref, t_ref, dy_ref, loss_ref, acc):
        i = pl.program_id(0)
        e = y_ref[...] - t_ref[...]
        dy_ref[...] = e * (1.0 / d)

        @pl.when(i == 0)
        def _():
            acc[...] = jnp.zeros_like(acc)

        acc[...] += jnp.sum(e * e, axis=0, keepdims=True)

        @pl.when(i == nt - 1)
        def _():
            loss_ref[...] = (0.5 / d) * jnp.sum(acc[...], axis=1, keepdims=True)

    row = pl.BlockSpec((tm, d), lambda i: (i, 0))
    return pl.pallas_call(
        body,
        out_shape=(jax.ShapeDtypeStruct((t, d), F32), jax.ShapeDtypeStruct((1, 1), F32)),
        grid=(nt,), in_specs=[row, row], out_specs=(row, pl.BlockSpec((1, 1), lambda i: (0, 0))),
        scratch_shapes=[pltpu.VMEM((1, d), F32)], compiler_params=_cp(1), name="loss_grad",
    )(y, target)


def _row_tile(r, c):
    if r * c * 4 <= MIB:
        return r
    best = None
    for t in range(16, r, 16):
        if r % t == 0 and t * c * 4 <= MIB:
            best = t
    return r if best is None else best


def adamw(name, w, m, v, pieces):
    r, c = w.shape
    tr = _row_tile(r, c)
    bc1 = 1.0 - ADAM_B1 ** ADAM_STEP
    bc2 = 1.0 - ADAM_B2 ** ADAM_STEP

    def body(w_ref, m_ref, v_ref, p_ref, g_ref, d_ref, nm_ref, nv_ref):
        g = p_ref[0].astype(F32)
        for j in range(1, NDEV):
            g = g + p_ref[j].astype(F32)
        m1 = ADAM_B1 * m_ref[...] + (1.0 - ADAM_B1) * g
        v1 = ADAM_B2 * v_ref[...] + (1.0 - ADAM_B2) * (g * g)
        m_hat = m1 / bc1
        v_hat = v1 / bc2
        g_ref[...] = g
        d_ref[...] = -ADAM_LR * (m_hat / (jnp.sqrt(v_hat) + ADAM_EPS) + ADAM_WD * w_ref[...])
        nm_ref[...] = m1
        nv_ref[...] = v1

    row = pl.BlockSpec((tr, c), lambda i: (i, 0))
    out = jax.ShapeDtypeStruct((r, c), F32)
    return pl.pallas_call(
        body, out_shape=(out, out, out, out), grid=(r // tr,),
        in_specs=[row, row, row, pl.BlockSpec((NDEV, tr, c), lambda i: (0, i, 0))],
        out_specs=(row, row, row, row), compiler_params=_cp(1), name=name,
    )(w, m, v, pieces)


def _sigmoid(a):
    return 1.0 / (1.0 + jnp.exp(-a))


def ffn_fwd(name, h, wg, wu, wd, layer):
    t, d = h.shape
    ns, _, _, f = wg.shape
    tm = _tile(t, 512)

    def body(h_ref, wg_ref, wu_ref, wd_ref, a_ref, b_ref, hid_ref, m_ref, acc):
        j = pl.program_id(1)
        hv = h_ref[...]
        a = _dot(hv, wg_ref[...])
        b = _dot(hv, wu_ref[...])
        hid = ((a * _sigmoid(a)) * b).astype(BF16)
        a_ref[...] = a.astype(BF16)
        b_ref[...] = b.astype(BF16)
        hid_ref[...] = hid
        p = _dot(hid, wd_ref[...])

        @pl.when(j == 0)
        def _():
            acc[...] = p

        @pl.when(j > 0)
        def _():
            acc[...] += p

        @pl.when(j == ns - 1)
        def _():
            m_ref[...] = acc[...]

    w_in = pl.BlockSpec((None, None, d, f), lambda i, j: (j, layer, 0, 0))
    act = pl.BlockSpec((None, tm, f), lambda i, j: (j, i, 0))
    act_shape = jax.ShapeDtypeStruct((ns, t, f), BF16)
    return pl.pallas_call(
        body,
        out_shape=(act_shape, act_shape, act_shape, jax.ShapeDtypeStruct((t, d), F32)),
        grid=(t // tm, ns),
        in_specs=[pl.BlockSpec((tm, d), lambda i, j: (i, 0)), w_in, w_in,
                  pl.BlockSpec((None, None, f, d), lambda i, j: (j, layer, 0, 0))],
        out_specs=(act, act, act, pl.BlockSpec((tm, d), lambda i, j: (i, 0))),
        scratch_shapes=[pltpu.VMEM((tm, d), F32)],
        compiler_params=_cp(2, 56), name=name,
    )(h, wg, wu, wd)


def ffn_bwd(name, dm, a, b, wg, wu, wd, layer):
    t, d = dm.shape
    ns, _, _, f = wg.shape
    tm = _tile(t, 512)

    def body(dm_ref, a_ref, b_ref, wg_ref, wu_ref, wd_ref, da_ref, db_ref, dh_ref, acc):
        j = pl.program_id(1)
        dhid = _dot(dm_ref[...], wd_ref[...], "nt")
        av = a_ref[...].astype(F32)
        bv = b_ref[...].astype(F32)
        sig = _sigmoid(av)
        da = (dhid * bv * (sig * (1.0 + av * (1.0 - sig)))).astype(BF16)
        db = (dhid * (av * sig)).astype(BF16)
        da_ref[...] = da
        db_ref[...] = db
        p = _dot(da, wg_ref[...], "nt") + _dot(db, wu_ref[...], "nt")

        @pl.when(j == 0)
        def _():
            acc[...] = p

        @pl.when(j > 0)
        def _():
            acc[...] += p

        @pl.when(j == ns - 1)
        def _():
            dh_ref[...] = acc[...]

    w_in = pl.BlockSpec((None, None, d, f), lambda i, j: (j, layer, 0, 0))
    act = pl.BlockSpec((None, tm, f), lambda i, j: (j, i, 0))
    row = pl.BlockSpec((tm, d), lambda i, j: (i, 0))
    act_shape = jax.ShapeDtypeStruct((ns, t, f), BF16)
    return pl.pallas_call(
        body,
        out_shape=(act_shape, act_shape, jax.ShapeDtypeStruct((t, d), F32)),
        grid=(t // tm, ns),
        in_specs=[row, act, act, w_in, w_in, pl.BlockSpec((None, None, f, d), lambda i, j: (j, layer, 0, 0))],
        out_specs=(act, act, row),
        scratch_shapes=[pltpu.VMEM((tm, d), F32)],
        compiler_params=_cp(2, 56), name=name,
    )(dm, a, b, wg, wu, wd)


def ffn_wgrad_in(name, h, dact):
    t, d = h.shape
    ns, _, f = dact.shape
    tk = _tile(t, 512)
    tmo = _tile(d, 1024)
    return _mm(
        name, h, dact, dims="tn", grid=(ns, d // tmo, t // tk),
        a_spec=pl.BlockSpec((tk, tmo), lambda j, i, k: (k, i)),
        b_spec=pl.BlockSpec((None, tk, f), lambda j, i, k: (j, k, 0)),
        o_spec=pl.BlockSpec((None, tmo, f), lambda j, i, k: (j, i, 0)),
        out_shape=jax.ShapeDtypeStruct((ns, d, f), BF16), acc_shape=(tmo, f), nk=t // tk,
    )


def ffn_wgrad_out(name, hid, dm):
    ns, t, f = hid.shape
    d = dm.shape[1]
    tk = _tile(t, 512)
    tno = _tile(d, 1024)
    return _mm(
        name, hid, dm, dims="tn", grid=(ns, d // tno, t // tk),
        a_spec=pl.BlockSpec((None, tk, f), lambda j, i, k: (j, k, 0)),
        b_spec=pl.BlockSpec((tk, tno), lambda j, i, k: (k, i)),
        o_spec=pl.BlockSpec((None, f, tno), lambda j, i, k: (j, 0, i)),
        out_shape=jax.ShapeDtypeStruct((ns, f, d), BF16), acc_shape=(f, tno), nk=t // tk,
    )


def rope_table(t):
    half = ROPE_DIM // 2
    inv = ROPE_THETA ** (-jnp.arange(half, dtype=F32) * 2.0 / ROPE_DIM)
    ang = jnp.arange(t, dtype=F32)[:, None] * inv[None, :]
    cos, sin = jnp.cos(ang), jnp.sin(ang)
    rest = HEAD_DIM - ROPE_DIM
    c = jnp.concatenate([cos, cos, jnp.ones((t, rest), F32)], axis=1)
    sm = jnp.concatenate([-sin, jnp.zeros((t, half + rest), F32)], axis=1)
    sp = jnp.concatenate([jnp.zeros((t, half), F32), sin, jnp.zeros((t, rest), F32)], axis=1)
    return jnp.concatenate([jnp.tile(c, (1, 2)), jnp.tile(sm, (1, 2)), jnp.tile(sp, (1, 2))], axis=1)


def _rope(x, tab, sign):
    w = x.shape[1]
    rep = w // LANES
    c, sm, sp = tab[:, 0:LANES], tab[:, LANES:2 * LANES], tab[:, 2 * LANES:3 * LANES]
    if rep > 1:
        c, sm, sp = jnp.tile(c, (1, rep)), jnp.tile(sm, (1, rep)), jnp.tile(sp, (1, rep))
    half = ROPE_DIM // 2
    return x * c + sign * (pltpu.roll(x, w - half, 1) * sm + pltpu.roll(x, half, 1) * sp)


def _attn_specs():
    prev = lambda n: jnp.maximum(n - 1, 0)
    kblk, vblk = ZK // LANES, ZK // LANES + 1
    return [
        pl.BlockSpec((BLOCK, ATTN_WIDTH), lambda n: (n, 0)),
        pl.BlockSpec((BLOCK, KV_WIDTH), lambda n: (n, kblk)),
        pl.BlockSpec((BLOCK, KV_WIDTH), lambda n: (prev(n), kblk)),
        pl.BlockSpec((BLOCK, KV_WIDTH), lambda n: (n, vblk)),
        pl.BlockSpec((BLOCK, KV_WIDTH), lambda n: (prev(n), vblk)),
        pl.BlockSpec((BLOCK, 3 * LANES), lambda n: (n, 0)),
        pl.BlockSpec((BLOCK, 3 * LANES), lambda n: (prev(n), 0)),
        pl.BlockSpec(memory_space=pltpu.SMEM),
    ]


def _attn_prologue(n, zq_ref, zk_ref, zkp_ref, zv_ref, zvp_ref, tab_ref, tabp_ref):
    q = (_rope(zq_ref[...], tab_ref[...], 1.0) * (HEAD_DIM ** -0.5)).astype(BF16)
    kcat = jnp.concatenate(
        [_rope(zkp_ref[...], tabp_ref[...], 1.0), _rope(zk_ref[...], tab_ref[...], 1.0)], axis=0).astype(BF16)
    vcat = jnp.concatenate([zvp_ref[...], zv_ref[...]], axis=0).astype(BF16)
    qi = lax.broadcasted_iota(jnp.int32, (BLOCK, 2 * BLOCK), 0)
    kj = lax.broadcasted_iota(jnp.int32, (BLOCK, 2 * BLOCK), 1)
    valid = (kj <= qi + BLOCK) & (kj > qi) & ((n > 0) | (kj >= BLOCK))
    return q, kcat, vcat, valid


def _attn_probs(qh, kh, valid, sink):
    s = jnp.where(valid, _dot(qh, kh, "nt"), -1e30)
    mx = jnp.maximum(jnp.max(s, axis=1, keepdims=True), sink)
    p = jnp.exp(s - mx)
    p_sink = jnp.exp(sink - mx)
    inv = 1.0 / (jnp.sum(p, axis=1, keepdims=True) + p_sink)
    return p * inv, p_sink * inv


def attn_fwd(z, tab, sinks):
    t = z.shape[0]

    def body(zq_ref, zk_ref, zkp_ref, zv_ref, zvp_ref, tab_ref, tabp_ref, sink_ref, o_ref):
        n = pl.program_id(0)
        q, kcat, vcat, valid = _attn_prologue(n, zq_ref, zk_ref, zkp_ref, zv_ref, zvp_ref, tab_ref, tabp_ref)
        outs = []
        for h in range(N_Q_HEADS):
            kv = slice((h // Q_PER_KV) * HEAD_DIM, (h // Q_PER_KV + 1) * HEAD_DIM)
            p, _ = _attn_probs(q[:, h * HEAD_DIM:(h + 1) * HEAD_DIM], kcat[:, kv], valid, sink_ref[0, h])
            outs.append(_dot(p.astype(BF16), vcat[:, kv]))
        o_ref[...] = jnp.concatenate(outs, axis=1).astype(BF16)

    return pl.pallas_call(
        body, out_shape=jax.ShapeDtypeStruct((t, ATTN_WIDTH), BF16), grid=(t // BLOCK,),
        in_specs=_attn_specs(), out_specs=pl.BlockSpec((BLOCK, ATTN_WIDTH), lambda n: (n, 0)),
        compiler_params=_cp(1), name="attn_fwd",
    )(z, z, z, z, z, tab, tab, sinks)


def attn_bwd(z, tab, sinks, dcat):
    t = z.shape[0]
    nb = t // BLOCK

    def body(zq_ref, zk_ref, zkp_ref, zv_ref, zvp_ref, tab_ref, tabp_ref, sink_ref, do_ref,
             dq_ref, dkv_ref, dsink_ref):
        n = pl.program_id(0)
        q, kcat, vcat, valid = _attn_prologue(n, zq_ref, zk_ref, zkp_ref, zv_ref, zvp_ref, tab_ref, tabp_ref)
        do = do_ref[...]
        lane = lax.broadcasted_iota(jnp.int32, (1, LANES), 1)
        dqs, dks, dvs = [], [], []
        dsink = jnp.zeros((1, LANES), F32)
        for hk in range(N_Q_HEADS // Q_PER_KV):
            kv = slice(hk * HEAD_DIM, (hk + 1) * HEAD_DIM)
            kh, vh = kcat[:, kv], vcat[:, kv]
            dk = jnp.zeros((2 * BLOCK, HEAD_DIM), F32)
            dv = jnp.zeros((2 * BLOCK, HEAD_DIM), F32)
            for g in range(Q_PER_KV):
                h = hk * Q_PER_KV + g
                hs = slice(h * HEAD_DIM, (h + 1) * HEAD_DIM)
                qh, doh = q[:, hs], do[:, hs]
                p, p_sink = _attn_probs(qh, kh, valid, sink_ref[0, h])
                dv = dv + _dot(p.astype(BF16), doh, "tn")
                dp = _dot(doh, vh, "nt")
                rd = jnp.sum(p * dp, axis=1, keepdims=True)
                ds = (p * (dp - rd) * (HEAD_DIM ** -0.5)).astype(BF16)
                dqs.append(_dot(ds, kh))
                dk = dk + _dot(ds, qh, "tn") * (HEAD_DIM ** 0.5)
                dsink = dsink + jnp.where(lane == h, -jnp.sum(p_sink * rd, axis=0, keepdims=True), 0.0)
            dks.append(dk)
            dvs.append(dv)
        dq_ref[...] = _rope(jnp.concatenate(dqs, axis=1), tab_ref[...], -1.0).astype(BF16)
        dkc = jnp.concatenate(dks, axis=1)
        dk_pre = jnp.concatenate(
            [_rope(dkc[:BLOCK], tabp_ref[...], -1.0), _rope(dkc[BLOCK:], tab_ref[...], -1.0)], axis=0)
        dkv_ref[...] = jnp.concatenate([dk_pre, jnp.concatenate(dvs, axis=1)], axis=1)

        @pl.when(n == 0)
        def _():
            dsink_ref[...] = jnp.zeros_like(dsink_ref)

        dsink_ref[...] += dsink

    return pl.pallas_call(
        body,
        out_shape=(jax.ShapeDtypeStruct((t, ATTN_WIDTH), BF16),
                   jax.ShapeDtypeStruct((nb, 2 * BLOCK, 2 * KV_WIDTH), F32),
                   jax.ShapeDtypeStruct((1, LANES), F32)),
        grid=(nb,),
        in_specs=_attn_specs() + [pl.BlockSpec((BLOCK, ATTN_WIDTH), lambda n: (n, 0))],
        out_specs=(pl.BlockSpec((BLOCK, ATTN_WIDTH), lambda n: (n, 0)),
                   pl.BlockSpec((None, 2 * BLOCK, 2 * KV_WIDTH), lambda n: (n, 0, 0)),
                   pl.BlockSpec((1, LANES), lambda n: (0, 0))),
        compiler_params=_cp(1), name="attn_bwd",
    )(z, z, z, z, z, tab, tab, sinks, dcat)


def _gelu(x):
    k = math.sqrt(2.0 / math.pi)
    th = jnp.tanh(k * (x + 0.044715 * (x * x * x)))
    return 0.5 * x * (1.0 + th), th


def _gelu_grad(x, th):
    k = math.sqrt(2.0 / math.pi)
    return 0.5 * (1.0 + th) + 0.5 * x * (1.0 - th * th) * (k * (1.0 + 3.0 * 0.044715 * (x * x)))


def _sgu_core(zu_ref, zv_ref, lng_ref, lnb_ref, w_ref, bt_ref):
    up, vp = zu_ref[...], zv_ref[...]
    u, thu = _gelu(up)
    v, thv = _gelu(vp)
    mu = jnp.mean(v, axis=-1, keepdims=True)
    vc = v - mu
    rstd = lax.rsqrt(jnp.mean(vc * vc, axis=-1, keepdims=True) + RMS_EPS)
    xhat = vc * rstd
    vn = (xhat * lng_ref[...] + lnb_ref[...]).astype(BF16)
    row = lax.broadcasted_iota(jnp.int32, (CHUNK, CHUNK), 0)
    col = lax.broadcasted_iota(jnp.int32, (CHUNK, CHUNK), 1)
    mixed = []
    for g in range(SGU_GROUPS):
        wc = jnp.where(row >= col, w_ref[g], 0.0).astype(BF16)
        mixed.append(_dot(wc, vn[:, g * CHUNK:(g + 1) * CHUNK]) + bt_ref[:, g:g + 1])
    return up, vp, u, thu, thv, rstd, xhat, vn, jnp.concatenate(mixed, axis=1)


def _sgu_specs():
    full = lambda shape: pl.BlockSpec(shape, lambda n: (0,) * len(shape))
    return [
        pl.BlockSpec((CHUNK, SGU_WIDTH), lambda n: (n, ZU // SGU_WIDTH)),
        pl.BlockSpec((CHUNK, SGU_WIDTH), lambda n: (n, ZV // SGU_WIDTH)),
        full((1, SGU_WIDTH)), full((1, SGU_WIDTH)),
        full((SGU_GROUPS, CHUNK, CHUNK)), full((CHUNK, SGU_GROUPS)),
    ]


def sgu_fwd(z, ln_g, ln_b, w, b_t):
    t = z.shape[0]

    def body(zu_ref, zv_ref, lng_ref, lnb_ref, w_ref, bt_ref, o_ref):
        _, _, u, _, _, _, _, _, mixed = _sgu_core(zu_ref, zv_ref, lng_ref, lnb_ref, w_ref, bt_ref)
        o_ref[...] = (u * mixed).astype(BF16)

    return pl.pallas_call(
        body, out_shape=jax.ShapeDtypeStruct((t, SGU_WIDTH), BF16), grid=(t // CHUNK,),
        in_specs=_sgu_specs(), out_specs=pl.BlockSpec((CHUNK, SGU_WIDTH), lambda n: (n, 0)),
        compiler_params=_cp(1), name="sgu_fwd",
    )(z, z, ln_g, ln_b, w, b_t)


def sgu_bwd(z, ln_g, ln_b, w, w_t, b_t, dcat):
    t = z.shape[0]

    def body(zu_ref, zv_ref, lng_ref, lnb_ref, w_ref, bt_ref, wt_ref, dg_ref,
             du_ref, dv_ref, dw_ref, dbt_ref, dlng_ref, dlnb_ref):
        up, vp, u, thu, thv, rstd, xhat, vn, mixed = _sgu_core(zu_ref, zv_ref, lng_ref, lnb_ref, w_ref, bt_ref)
        dgate = dg_ref[...].astype(F32)
        du_ref[...] = (dgate * mixed * _gelu_grad(up, thu)).astype(BF16)
        dmixed = dgate * u
        row = lax.broadcasted_iota(jnp.int32, (CHUNK, CHUNK), 0)
        col = lax.broadcasted_iota(jnp.int32, (CHUNK, CHUNK), 1)

        @pl.when(pl.program_id(0) == 0)
        def _():
            dw_ref[...] = jnp.zeros_like(dw_ref)
            dbt_ref[...] = jnp.zeros_like(dbt_ref)
            dlng_ref[...] = jnp.zeros_like(dlng_ref)
            dlnb_ref[...] = jnp.zeros_like(dlnb_ref)

        dvn, dbt = [], jnp.zeros((CHUNK, LANES), F32)
        for g in range(SGU_GROUPS):
            gs = slice(g * CHUNK, (g + 1) * CHUNK)
            dmx = dmixed[:, gs]
            dmxb = dmx.astype(BF16)
            dbt = dbt + jnp.where(col == g, jnp.sum(dmx, axis=1, keepdims=True), 0.0)
            dw_ref[g] += jnp.where(row >= col, _dot(dmxb, vn[:, gs], "nt"), 0.0)
            wtc = jnp.where(col >= row, wt_ref[g], 0.0).astype(BF16)
            dvn.append(_dot(wtc, dmxb))
        dbt_ref[...] += dbt
        dvn = jnp.concatenate(dvn, axis=1)
        dlnb_ref[...] += jnp.sum(dvn, axis=0, keepdims=True)
        dlng_ref[...] += jnp.sum(dvn * xhat, axis=0, keepdims=True)
        dxh = dvn * lng_ref[...]
        dv = rstd * (dxh - jnp.mean(dxh, axis=-1, keepdims=True) - xhat * jnp.mean(dxh * xhat, axis=-1, keepdims=True))
        dv_ref[...] = (dv * _gelu_grad(vp, thv)).astype(BF16)

    full = lambda shape: pl.BlockSpec(shape, lambda n: (0,) * len(shape))
    act = pl.BlockSpec((CHUNK, SGU_WIDTH), lambda n: (n, 0))
    act_shape = jax.ShapeDtypeStruct((t, SGU_WIDTH), BF16)
    vec = jax.ShapeDtypeStruct((1, SGU_WIDTH), F32)
    return pl.pallas_call(
        body,
        out_shape=(act_shape, act_shape, jax.ShapeDtypeStruct((SGU_GROUPS, CHUNK, CHUNK), F32),
                   jax.ShapeDtypeStruct((CHUNK, LANES), F32), vec, vec),
        grid=(t // CHUNK,),
        in_specs=_sgu_specs() + [full((SGU_GROUPS, CHUNK, CHUNK)),
                                 pl.BlockSpec((CHUNK, SGU_WIDTH), lambda n: (n, 1))],
        out_specs=(act, act, full((SGU_GROUPS, CHUNK, CHUNK)), full((CHUNK, LANES)),
                   full((1, SGU_WIDTH)), full((1, SGU_WIDTH))),
        compiler_params=_cp(1), name="sgu_bwd",
    )(z, z, ln_g, ln_b, w, b_t, w_t, dcat)


def dz_assemble(dq, dkv, du, dv):
    t = dq.shape[0]
    nb = t // BLOCK

    def body(dq_ref, cur_ref, nxt_ref, du_ref, dv_ref, o_ref):
        n = pl.program_id(0)
        o_ref[:, ZQ:ZQ + ATTN_WIDTH] = dq_ref[...]
        o_ref[:, ZU:ZU + SGU_WIDTH] = du_ref[...]
        o_ref[:, ZV:ZV + SGU_WIDTH] = dv_ref[...]
        kv = cur_ref[BLOCK:, :] + jnp.where(n < nb - 1, nxt_ref[:BLOCK, :], 0.0)
        o_ref[:, ZK:ZK + 2 * KV_WIDTH] = kv.astype(BF16)

    act = pl.BlockSpec((BLOCK, ATTN_WIDTH), lambda n: (n, 0))
    return pl.pallas_call(
        body, out_shape=jax.ShapeDtypeStruct((t, IN_WIDTH), BF16), grid=(nb,),
        in_specs=[act,
                  pl.BlockSpec((None, 2 * BLOCK, 2 * KV_WIDTH), lambda n: (n, 0, 0)),
                  pl.BlockSpec((None, 2 * BLOCK, 2 * KV_WIDTH), lambda n: (jnp.minimum(n + 1, nb - 1), 0, 0)),
                  act, act],
        out_specs=pl.BlockSpec((BLOCK, IN_WIDTH), lambda n: (n, 0)),
        compiler_params=_cp(1), name="dz_assemble",
    )(dq, dkv, dkv, du, dv)


def _pool_count(i, tp, w):
    t_idx = i * tp + lax.broadcasted_iota(jnp.int32, (tp, 1), 0)
    return jnp.minimum(t_idx + 1, w).astype(F32)


def pool_fwd(h, pw, pscale):
    t, d = h.shape
    tp = _tile(t, 256)
    per = tp // POOL_HALO

    def body(h_ref, halo_ref, pw_ref, ps_ref, m_ref, pooled_ref):
        i = pl.program_id(0)
        cur = h_ref[...]
        ext = jnp.concatenate([jnp.where(i > 0, halo_ref[...], 0.0), cur], axis=0)
        ys, pooled = [], []
        for gi, w in enumerate(POOL_WINDOWS):
            gs = slice(gi * POOL_GROUP_DIM, (gi + 1) * POOL_GROUP_DIM)
            s = ext[:, gs]
            sh = 1
            while sh < w:
                s = s + pltpu.roll(s, sh, 0)
                sh *= 2
            pg = (s[POOL_HALO:, :] / _pool_count(i, tp, w) - cur[:, gs]).astype(BF16)
            pooled.append(pg)
            ys.append(_dot(pg, pw_ref[gi]))
        pooled_ref[...] = jnp.concatenate(pooled, axis=1)
        m_ref[...] = jnp.concatenate(ys, axis=1) * ps_ref[...]

    row = pl.BlockSpec((tp, d), lambda i: (i, 0))
    return pl.pallas_call(
        body,
        out_shape=(jax.ShapeDtypeStruct((t, d), F32), jax.ShapeDtypeStruct((t, d), BF16)),
        grid=(t // tp,),
        in_specs=[row, pl.BlockSpec((POOL_HALO, d), lambda i: (jnp.maximum(i * per - 1, 0), 0)),
                  pl.BlockSpec(pw.shape, lambda i: (0, 0, 0)), pl.BlockSpec((1, d), lambda i: (0, 0))],
        out_specs=(row, row), compiler_params=_cp(1), name="pool_fwd",
    )(h, h, pw, pscale)


def pool_bwd_proj(dm, pooled, pw, pscale):
    t, d = dm.shape
    tp = _tile(t, 256)

    def body(dm_ref, pooled_ref, pw_ref, ps_ref, dp_ref, dy_ref, dps_ref):
        dmv = dm_ref[...]
        dy = (dmv * ps_ref[...]).astype(BF16)
        dy_ref[...] = dy
        ys, dps = [], []
        for gi in range(len(POOL_WINDOWS)):
            gs = slice(gi * POOL_GROUP_DIM, (gi + 1) * POOL_GROUP_DIM)
            ys.append(_dot(pooled_ref[:, gs], pw_ref[gi]))
            dps.append(_dot(dy[:, gs], pw_ref[gi], "nt"))
        dp_ref[...] = jnp.concatenate(dps, axis=1)

        @pl.when(pl.program_id(0) == 0)
        def _():
            dps_ref[...] = jnp.zeros_like(dps_ref)

        dps_ref[...] += jnp.sum(dmv * jnp.concatenate(ys, axis=1), axis=0, keepdims=True)

    row = pl.BlockSpec((tp, d), lambda i: (i, 0))
    vec = pl.BlockSpec((1, d), lambda i: (0, 0))
    return pl.pallas_call(
        body,
        out_shape=(jax.ShapeDtypeStruct((t, d), F32), jax.ShapeDtypeStruct((t, d), BF16),
                   jax.ShapeDtypeStruct((1, d), F32)),
        grid=(t // tp,),
        in_specs=[row, row, pl.BlockSpec(pw.shape, lambda i: (0, 0, 0)), vec],
        out_specs=(row, row, vec), compiler_params=_cp(1), name="pool_bwd_proj",
    )(dm, pooled, pw, pscale)


def pool_bwd_window(dp):
    t, d = dp.shape
    tp = _tile(t, 256)
    per = tp // POOL_HALO
    last = t // POOL_HALO - 1
    nt = t // tp

    def body(dp_ref, halo_ref, dh_ref):
        i = pl.program_id(0)
        cur = dp_ref[...]
        halo = jnp.where(i < nt - 1, halo_ref[...], 0.0)
        outs = []
        for gi, w in enumerate(POOL_WINDOWS):
            gs = slice(gi * POOL_GROUP_DIM, (gi + 1) * POOL_GROUP_DIM)
            s = jnp.concatenate([cur[:, gs] / _pool_count(i, tp, w), halo[:, gs] / float(w)], axis=0)
            sh = 1
            while sh < w:
                s = s + pltpu.roll(s, tp + POOL_HALO - sh, 0)
                sh *= 2
            outs.append(s[:tp, :] - cur[:, gs])
        dh_ref[...] = jnp.concatenate(outs, axis=1)

    row = pl.BlockSpec((tp, d), lambda i: (i, 0))
    return pl.pallas_call(
        body, out_shape=jax.ShapeDtypeStruct((t, d), F32), grid=(nt,),
        in_specs=[row, pl.BlockSpec((POOL_HALO, d), lambda i: (jnp.minimum((i + 1) * per, last), 0))],
        out_specs=row, compiler_params=_cp(1), name="pool_bwd_window",
    )(dp, dp)


def pool_wgrad(pooled, dy):
    t, d = pooled.shape
    ng = d // POOL_GROUP_DIM
    tk = _tile(t, 512)
    blk = pl.BlockSpec((tk, POOL_GROUP_DIM), lambda g, k: (k, g))
    return _mm(
        "pool_wgrad", pooled, dy, dims="tn", grid=(ng, t // tk), a_spec=blk, b_spec=blk,
        o_spec=pl.BlockSpec((None, POOL_GROUP_DIM, POOL_GROUP_DIM), lambda g, k: (g, 0, 0)),
        out_shape=jax.ShapeDtypeStruct((ng, POOL_GROUP_DIM, POOL_GROUP_DIM), F32),
        acc_shape=(POOL_GROUP_DIM, POOL_GROUP_DIM), nk=t // tk,
    )


def _xattn_probs(qh, kh):
    s = _dot(qh, kh, "nt") * (X_HEAD_DIM ** -0.5)
    p = jnp.exp(s - jnp.max(s, axis=1, keepdims=True))
    return p * (1.0 / jnp.sum(p, axis=1, keepdims=True))


def xattn_fwd(name, q, k, v):
    t, xw = q.shape
    tm = _tile(t, 512)

    def body(q_ref, k_ref, v_ref, o_ref):
        outs = []
        for h in range(X_HEADS):
            hs = slice(h * X_HEAD_DIM, (h + 1) * X_HEAD_DIM)
            p = _xattn_probs(q_ref[:, hs], k_ref[:, hs])
            outs.append(_dot(p.astype(BF16), v_ref[:, hs]))
        o_ref[...] = jnp.concatenate(outs, axis=1).astype(BF16)

    row = pl.BlockSpec((tm, xw), lambda i: (i, 0))
    kv = pl.BlockSpec(k.shape, lambda i: (0, 0))
    return pl.pallas_call(
        body, out_shape=jax.ShapeDtypeStruct((t, xw), BF16), grid=(t // tm,),
        in_specs=[row, kv, kv], out_specs=row, compiler_params=_cp(1), name=name,
    )(q, k, v)


def xattn_bwd(name, q, k, v, do):
    t, xw = q.shape
    tm = _tile(t, 512)

    def body(q_ref, k_ref, v_ref, do_ref, dq_ref, dk_ref, dv_ref):
        @pl.when(pl.program_id(0) == 0)
        def _():
            dk_ref[...] = jnp.zeros_like(dk_ref)
            dv_ref[...] = jnp.zeros_like(dv_ref)

        dqs = []
        for h in range(X_HEADS):
            hs = slice(h * X_HEAD_DIM, (h + 1) * X_HEAD_DIM)
            qh, kh, vh, doh = q_ref[:, hs], k_ref[:, hs], v_ref[:, hs], do_ref[:, hs]
            p = _xattn_probs(qh, kh)
            dv_ref[:, hs] += _dot(p.astype(BF16), doh, "tn")
            dp = _dot(doh, vh, "nt")
            ds = (p * (dp - jnp.sum(p * dp, axis=1, keepdims=True)) * (X_HEAD_DIM ** -0.5)).astype(BF16)
            dqs.append(_dot(ds, kh))
            dk_ref[:, hs] += _dot(ds, qh, "tn")
        dq_ref[...] = jnp.concatenate(dqs, axis=1).astype(BF16)

    row = pl.BlockSpec((tm, xw), lambda i: (i, 0))
    kv = pl.BlockSpec(k.shape, lambda i: (0, 0))
    kv_shape = jax.ShapeDtypeStruct(k.shape, F32)
    return pl.pallas_call(
        body, out_shape=(jax.ShapeDtypeStruct((t, xw), BF16), kv_shape, kv_shape), grid=(t // tm,),
        in_specs=[row, kv, kv, row], out_specs=(row, kv, kv), compiler_params=_cp(1), name=name,
    )(q, k, v, do)


def xattn_out(name, o, wo, layer):
    t, xw = o.shape
    ns, _, _, dn = wo.shape
    tm = _tile(t, 512)
    return _mm(
        name, o, wo, dims="nn", grid=(t // tm, ns),
        a_spec=pl.BlockSpec((tm, xw), lambda i, j: (i, 0)),
        b_spec=pl.BlockSpec((None, None, xw, dn), lambda i, j: (j, layer, 0, 0)),
        o_spec=pl.BlockSpec((tm, dn), lambda i, j: (i, j)),
        out_shape=jax.ShapeDtypeStruct((t, ns * dn), F32),
    )


def xattn_out_bwd(name, dm, wo, layer):
    t = dm.shape[0]
    ns, _, xw, dn = wo.shape
    tm = _tile(t, 512)
    return _mm(
        name, dm, wo, dims="nt", grid=(t // tm, ns),
        a_spec=pl.BlockSpec((tm, dn), lambda i, j: (i, j)),
        b_spec=pl.BlockSpec((None, None, xw, dn), lambda i, j: (j, layer, 0, 0)),
        o_spec=pl.BlockSpec((tm, xw), lambda i, j: (i, 0)),
        out_shape=jax.ShapeDtypeStruct((t, xw), BF16), acc_shape=(tm, xw), nk=ns,
    )


def xattn_out_wgrad(name, o, dm, ns):
    t, xw = o.shape
    dn = dm.shape[1] // ns
    tk = _tile(t, 512)
    return _mm(
        name, o, dm, dims="tn", grid=(ns, t // tk),
        a_spec=pl.BlockSpec((tk, xw), lambda j, k: (k, 0)),
        b_spec=pl.BlockSpec((tk, dn), lambda j, k: (k, j)),
        o_spec=pl.BlockSpec((None, xw, dn), lambda j, k: (j, 0, 0)),
        out_shape=jax.ShapeDtypeStruct((ns, xw, dn), BF16), acc_shape=(xw, dn), nk=t // tk,
    )


def kernel(x, mem, norms, mem_norm, ffn1_wg, ffn1_wu, ffn1_wd, ffn2_wg, ffn2_wu, ffn2_wd, x_wq, x_wk, x_wv, x_wo, mix_w_in, mix_w_out, attn_sinks, sgu_ln_g, sgu_ln_b, sgu_w, sgu_b, pool_w, pool_scale, loss_target, m_norms, m_mem_norm, m_ffn1_wg, m_ffn1_wu, m_ffn1_wd, m_ffn2_wg, m_ffn2_wu, m_ffn2_wd, m_x_wq, m_x_wk, m_x_wv, m_x_wo, m_mix_w_in, m_mix_w_out, m_attn_sinks, m_sgu_ln_g, m_sgu_ln_b, m_sgu_w, m_sgu_b, m_pool_w, m_pool_scale, v_norms, v_mem_norm, v_ffn1_wg, v_ffn1_wu, v_ffn1_wd, v_ffn2_wg, v_ffn2_wu, v_ffn2_wd, v_x_wq, v_x_wk, v_x_wv, v_x_wo, v_mix_w_in, v_mix_w_out, v_attn_sinks, v_sgu_ln_g, v_sgu_ln_b, v_sgu_w, v_sgu_b, v_pool_w, v_pool_scale):
    params = dict(norms=norms, mem_norm=mem_norm, ffn1_wg=ffn1_wg, ffn1_wu=ffn1_wu, ffn1_wd=ffn1_wd,
                  ffn2_wg=ffn2_wg, ffn2_wu=ffn2_wu, ffn2_wd=ffn2_wd, x_wq=x_wq, x_wk=x_wk, x_wv=x_wv, x_wo=x_wo,
                  mix_w_in=mix_w_in, mix_w_out=mix_w_out, attn_sinks=attn_sinks, sgu_ln_g=sgu_ln_g,
                  sgu_ln_b=sgu_ln_b, sgu_w=sgu_w, sgu_b=sgu_b, pool_w=pool_w, pool_scale=pool_scale)
    mom1 = dict(norms=m_norms, mem_norm=m_mem_norm, ffn1_wg=m_ffn1_wg, ffn1_wu=m_ffn1_wu, ffn1_wd=m_ffn1_wd,
                ffn2_wg=m_ffn2_wg, ffn2_wu=m_ffn2_wu, ffn2_wd=m_ffn2_wd, x_wq=m_x_wq, x_wk=m_x_wk, x_wv=m_x_wv,
                x_wo=m_x_wo, mix_w_in=m_mix_w_in, mix_w_out=m_mix_w_out, attn_sinks=m_attn_sinks,
                sgu_ln_g=m_sgu_ln_g, sgu_ln_b=m_sgu_ln_b, sgu_w=m_sgu_w, sgu_b=m_sgu_b, pool_w=m_pool_w,
                pool_scale=m_pool_scale)
    mom2 = dict(norms=v_norms, mem_norm=v_mem_norm, ffn1_wg=v_ffn1_wg, ffn1_wu=v_ffn1_wu, ffn1_wd=v_ffn1_wd,
                ffn2_wg=v_ffn2_wg, ffn2_wu=v_ffn2_wu, ffn2_wd=v_ffn2_wd, x_wq=v_x_wq, x_wk=v_x_wk, x_wv=v_x_wv,
                x_wo=v_x_wo, mix_w_in=v_mix_w_in, mix_w_out=v_mix_w_out, attn_sinks=v_attn_sinks,
                sgu_ln_g=v_sgu_ln_g, sgu_ln_b=v_sgu_ln_b, sgu_w=v_sgu_w, sgu_b=v_sgu_b, pool_w=v_pool_w,
                pool_scale=v_pool_scale)
    order = list(params)

    xs, memb, target = x[0], mem[0], loss_target[0]
    t, d = xs.shape
    depth = norms.shape[0]
    dsh = d // NDEV

    small_shapes = [norms.shape, pool_scale.shape, pool_w.shape]
    (small_all,) = _exchange("gather_small", [_pack([norms, pool_scale, pool_w])], True)
    norms_sh, pscale_sh, pw_sh = _unpack(small_all, small_shapes, (NDEV,))
    norms_full = norms_sh.transpose(1, 2, 0, 3).reshape(depth, norms.shape[1], d)
    pscale_full = pscale_sh.transpose(1, 0, 2).reshape(1, d)
    pw_full = pw_sh[:, 0].transpose(1, 0, 2, 3).reshape(len(POOL_WINDOWS), POOL_GROUP_DIM, POOL_GROUP_DIM).astype(BF16)

    bf = lambda a: a.astype(BF16)
    wg1, wu1, wd1, wg2, wu2, wd2 = _exchange(
        "gather_ffn", [bf(ffn1_wg), bf(ffn1_wu), bf(ffn1_wd), bf(ffn2_wg), bf(ffn2_wu), bf(ffn2_wd)], True)
    wq_all, wk_all, wv_all, wo_all, win_all, wout_all = _exchange(
        "gather_mix", [bf(x_wq), bf(x_wk), bf(x_wv), bf(x_wo), bf(mix_w_in), bf(mix_w_out)], True)
    w_in = win_all[:, 0].transpose(1, 0, 2).reshape(d, IN_WIDTH)
    o_k, o_u = ATTN_WIDTH, ATTN_WIDTH + 2 * KV_WIDTH
    w_in = jnp.concatenate([w_in[:, :o_k], w_in[:, o_u:], w_in[:, o_k:o_u]], axis=1)
    w_out = wout_all[:, 0].reshape(d, d)

    tab = rope_table(t)
    sgu_w0 = sgu_w[0]
    sgu_wt0 = sgu_w0.transpose(0, 2, 1)
    sgu_bt0 = sgu_b[0].T
    gain = lambda l, i: norms_full[l, i][None, :]

    saved = []
    xc = xs
    for l in range(depth):
        s = {}
        wq = wq_all[:, l].reshape(d, -1)
        wk = wk_all[:, l].reshape(d, -1)
        wv = wv_all[:, l].reshape(d, -1)
        s["wq"], s["wkv"], s["wk"], s["wv"] = wq, jnp.concatenate([wk, wv], axis=1), wk, wv

        s["x0"] = xc
        s["h1"] = norm_fwd(f"norm_f1_{l}", xc, gain(l, 0), BF16)
        s["a1"], s["b1"], s["hid1"], s["m1"] = ffn_fwd(f"ffn1_fwd_{l}", s["h1"], wg1, wu1, wd1, l)
        xc = resid_norm_fwd(f"resid_f1_{l}", xc, s["m1"], gain(l, 1), 0.5)

        s["x1"] = xc
        if l % 2 == 0:
            s["h2"] = norm_fwd(f"norm_mix_{l}", xc, gain(l, 2), BF16)
            s["z"] = mm_nn("mix_in", s["h2"], w_in, F32, tn=IN_WIDTH // 2)
            attn = attn_fwd(s["z"], tab, attn_sinks)
            gate = sgu_fwd(s["z"], sgu_ln_g, sgu_ln_b, sgu_w0, sgu_bt0)
            s["cat"] = jnp.concatenate([attn, gate], axis=1)
            s["m2"] = mm_nn("mix_out", s["cat"], w_out, F32)
        else:
            s["h2"] = norm_fwd(f"norm_mix_{l}", xc, gain(l, 2), F32)
            s["m2"], s["pooled"] = pool_fwd(s["h2"], pw_full, pscale_full)
        xc = resid_norm_fwd(f"resid_mix_{l}", xc, s["m2"], gain(l, 3), 1.0)

        s["x2"] = xc
        s["h3"] = norm_fwd(f"norm_x_{l}", xc, gain(l, 4), BF16)
        s["mem_n"] = norm_fwd(f"norm_mem_{l}", memb, mem_norm[l][None, :], BF16)
        s["q"] = mm_nn(f"x_q_{l}", s["h3"], wq, BF16)
        s["k"] = mm_nn(f"x_k_{l}", s["mem_n"], wk, BF16)
        s["v"] = mm_nn(f"x_v_{l}", s["mem_n"], wv, BF16)
        s["o"] = xattn_fwd(f"xattn_fwd_{l}", s["q"], s["k"], s["v"])
        s["m3"] = xattn_out(f"x_o_{l}", s["o"], wo_all, l)
        xc = resid_norm_fwd(f"resid_x_{l}", xc, s["m3"], gain(l, 5), 1.0)

        s["x3"] = xc
        s["h4"] = norm_fwd(f"norm_f2_{l}", xc, gain(l, 6), BF16)
        s["a4"], s["b4"], s["hid4"], s["m4"] = ffn_fwd(f"ffn2_fwd_{l}", s["h4"], wg2, wu2, wd2, l)
        xc = resid_norm_fwd(f"resid_f2_{l}", xc, s["m4"], gain(l, 7), 0.5)
        saved.append(s)

    dx, loss11 = loss_grad(xc, target)
    loss = lax.psum(loss11[0, 0], ("x", "y", "c"))

    grads = {k: [None] * depth for k in
             ("ffn1_wg", "ffn1_wu", "ffn1_wd", "ffn2_wg", "ffn2_wu", "ffn2_wd", "x_wq", "x_wk", "x_wv", "x_wo",
              "norms", "mem_norm")}
    for l in reversed(range(depth)):
        s = saved[l]
        dg = [None] * 8

        def ffn_block(tag, dx, x_in, h, a, b, hid, m, wg, wu, wd, gi):
            dm, dg[gi + 1] = norm_bwd(f"{tag}_post_bwd_{l}", m, gain(l, gi + 1), dx, 0.5, None, BF16)
            da, db, dh = ffn_bwd(f"{tag}_bwd_{l}", dm, a, b, wg, wu, wd, l)
            grads[f"{tag}_wg"][l] = ffn_wgrad_in(f"{tag}_dwg_{l}", h, da)
            grads[f"{tag}_wu"][l] = ffn_wgrad_in(f"{tag}_dwu_{l}", h, db)
            grads[f"{tag}_wd"][l] = ffn_wgrad_out(f"{tag}_dwd_{l}", hid, dm)
            dx, dg[gi] = norm_bwd(f"{tag}_pre_bwd_{l}", x_in, gain(l, gi), dh, 1.0, dx, F32)
            return dx

        dx = ffn_block("ffn2", dx, s["x3"], s["h4"], s["a4"], s["b4"], s["hid4"], s["m4"], wg2, wu2, wd2, 6)

        dm, dg[5] = norm_bwd(f"x_post_bwd_{l}", s["m3"], gain(l, 5), dx, 1.0, None, BF16)
        do = xattn_out_bwd(f"x_do_{l}", dm, wo_all, l)
        grads["x_wo"][l] = xattn_out_wgrad(f"x_dwo_{l}", s["o"], dm, NDEV)
        dq, dk, dv = xattn_bwd(f"xattn_bwd_{l}", s["q"], s["k"], s["v"], do)
        dkb, dvb = dk.astype(BF16), dv.astype(BF16)
        grads["x_wq"][l] = mm_tn(f"x_dwq_{l}", s["h3"], dq, BF16)
        grads["x_wk"][l] = mm_tn(f"x_dwk_{l}", s["mem_n"], dkb, BF16)
        grads["x_wv"][l] = mm_tn(f"x_dwv_{l}", s["mem_n"], dvb, BF16)
        dh = mm_nt(f"x_dh_{l}", dq, s["wq"], F32)
        dmem_n = mm_nt(f"x_dmem_{l}", jnp.concatenate([dkb, dvb], axis=1), s["wkv"], F32)
        _, grads["mem_norm"][l] = norm_bwd(f"mem_norm_bwd_{l}", memb, mem_norm[l][None, :], dmem_n, 1.0, None, F32)
        dx, dg[4] = norm_bwd(f"x_pre_bwd_{l}", s["x2"], gain(l, 4), dh, 1.0, dx, F32)

        if l % 2 == 0:
            dm, dg[3] = norm_bwd(f"mix_post_bwd_{l}", s["m2"], gain(l, 3), dx, 1.0, None, BF16)
            dcat = mm_nt("mix_dcat", dm, w_out, BF16)
            g_wout = mm_tn("mix_dwout", s["cat"], dm, BF16, tno=d // 2)
            dq_a, dkv_a, dsink = attn_bwd(s["z"], tab, attn_sinks, dcat)
            du_s, dv_s, g_sgu_w, g_sgu_bt, g_ln_g, g_ln_b = sgu_bwd(
                s["z"], sgu_ln_g, sgu_ln_b, sgu_w0, sgu_wt0, sgu_bt0, dcat)
            dz = dz_assemble(dq_a, dkv_a, du_s, dv_s)
            dh = mm_nt("mix_dh", dz, w_in, F32, tn=d // 2)
            g_win = mm_tn("mix_dwin", s["h2"], dz, BF16, tno=IN_WIDTH // 2)
        else:
            dm, dg[3] = norm_bwd(f"mix_post_bwd_{l}", s["m2"], gain(l, 3), dx, 1.0, None, F32)
            dp, dy, g_pscale = pool_bwd_proj(dm, s["pooled"], pw_full, pscale_full)
            g_pw = pool_wgrad(s["pooled"], dy)
            dh = pool_bwd_window(dp)
        dx, dg[2] = norm_bwd(f"mix_pre_bwd_{l}", s["x1"], gain(l, 2), dh, 1.0, dx, F32)

        dx = ffn_block("ffn1", dx, s["x0"], s["h1"], s["a1"], s["b1"], s["hid1"], s["m1"], wg1, wu1, wd1, 0)
        grads["norms"][l] = jnp.concatenate(dg, axis=0)

    stack = lambda k: jnp.stack(grads[k], axis=1)
    big = {k: stack(k) for k in ("ffn1_wg", "ffn1_wu", "ffn1_wd", "ffn2_wg", "ffn2_wu", "ffn2_wd", "x_wo")}
    for k in ("x_wq", "x_wk", "x_wv"):
        big[k] = jnp.stack([g.reshape(NDEV, dsh, -1) for g in grads[k]], axis=1)
    g_win = jnp.concatenate(
        [g_win[:, :ATTN_WIDTH], g_win[:, ZK:], g_win[:, ZU:ZK]], axis=1)
    big["mix_w_in"] = g_win.reshape(d, NDEV, -1).transpose(1, 0, 2)[:, None]
    big["mix_w_out"] = g_wout.reshape(NDEV, 1, dsh, d)
    big["pool_w"] = g_pw.reshape(len(POOL_WINDOWS), NDEV, -1, POOL_GROUP_DIM).transpose(1, 0, 2, 3)[:, None]
    big_names = list(big)
    recv_ffn = _exchange("scatter_ffn", [big[k] for k in big_names[:6]], False)
    recv_mix = _exchange("scatter_mix", [big[k] for k in big_names[6:]], False)
    recv = dict(zip(big_names, list(recv_ffn) + list(recv_mix)))

    g_norms = jnp.stack(grads["norms"], axis=0).reshape(depth, norms.shape[1], NDEV, dsh).transpose(2, 0, 1, 3)
    g_pscale_p = g_pscale.reshape(1, NDEV, dsh).transpose(1, 0, 2)
    sharded_small = ["norms", "pool_scale"]
    pieces_small = jnp.stack([_pack([g_norms[j], g_pscale_p[j]]) for j in range(NDEV)], axis=0)
    (recv_small,) = _exchange("scatter_small", [pieces_small], False)

    replicated = ["mem_norm", "attn_sinks", "sgu_ln_g", "sgu_ln_b", "sgu_w", "sgu_b"]
    rep_grads = [jnp.concatenate(grads["mem_norm"], axis=0), dsink[:, :N_Q_HEADS], g_ln_g, g_ln_b,
                 g_sgu_w[None], g_sgu_bt[:, :SGU_GROUPS].T[None]]
    (recv_rep,) = _exchange("gather_rep", [_pack(rep_grads)], True)

    out = {}
    for k in big_names:
        shp = params[k].shape
        r2 = (math.prod(shp[:-1]), shp[-1])
        res = adamw(f"adamw_{k}", params[k].reshape(r2), mom1[k].reshape(r2), mom2[k].reshape(r2),
                    recv[k].reshape((NDEV,) + r2))
        out[k] = [a.reshape(shp) for a in res]
    for names, pieces in ((sharded_small, recv_small), (replicated, recv_rep)):
        shapes = [params[k].shape for k in names]
        res = adamw("adamw_" + names[0] + "_pack", _pack([params[k] for k in names]),
                    _pack([mom1[k] for k in names]), _pack([mom2[k] for k in names]), pieces)
        for which in range(4):
            for k, a in zip(names, _unpack(res[which], shapes)):
                out.setdefault(k, [None] * 4)[which] = a

    outputs = [loss, dx[None]]
    for which in range(4):
        outputs += [out[k][which] for k in order]
    return tuple(outputs)
```

```python
import math

import jax
import jax.numpy as jnp
from jax import lax
from jax.experimental import pallas as pl
from jax.experimental.pallas import tpu as pltpu

F32 = jnp.float32
BF16 = jnp.bfloat16
NDEV = 8
MIB = 1024 * 1024
LANES = 128

RMS_EPS = 1e-6
HEAD_DIM = 64
N_Q_HEADS = 16
Q_PER_KV = 8
ATTN_WIDTH = 1024
KV_WIDTH = 128
BLOCK = 128
ROPE_DIM = 16
ROPE_THETA = 500000.0
SGU_GROUPS = 8
SGU_WIDTH = 1024
CHUNK = 128
POOL_WINDOWS = (2, 4, 8, 16)
POOL_GROUP_DIM = 512
POOL_HALO = 16
X_HEADS = 4
X_HEAD_DIM = 128
ZQ, ZU, ZV, ZK = 0, 1024, 2048, 3072
IN_WIDTH = 3328

ADAM_LR = 0.001
ADAM_B1 = 0.9
ADAM_B2 = 0.999
ADAM_EPS = 1e-08
ADAM_WD = 0.01
ADAM_STEP = 10

LINK_US_PER_SLOT_MIB = 91.0
HOST_US_FFN_BWD = 420.0
HOST_US_WGRAD = 150.0
HOST_US_ADAMW = 40.0

_DN = {
    "nn": (((1,), (0,)), ((), ())),
    "nt": (((1,), (1,)), ((), ())),
    "tn": (((0,), (0,)), ((), ())),
}


def _cp(naxes, vmem_mib=48):
    return pltpu.CompilerParams(dimension_semantics=("arbitrary",) * naxes, vmem_limit_bytes=vmem_mib * MIB)


def _tile(n, pref):
    t = min(n, pref)
    while n % t:
        t //= 2
    return t


def _dot(a, b, dims="nn"):
    return lax.dot_general(a, b, _DN[dims], preferred_element_type=F32)


def _me():
    x, y, c = lax.axis_index("x"), lax.axis_index("y"), lax.axis_index("c")
    return x, y, c, 4 * x + 2 * y + c


def _peer(k):
    x, y, c, _ = _me()
    px = 1 - x if k & 4 else x
    py = 1 - y if k & 2 else y
    pc = 1 - c if k & 1 else c
    return (px, py, pc), 4 * px + 2 * py + pc


class _Exchange:
    def __init__(self, arrs, out_shape):
        self.arrs = list(arrs)
        self.n = len(self.arrs)
        self.out_shape = list(out_shape)
        self.scratch = [
            pltpu.SemaphoreType.DMA((self.n * (NDEV - 1),)),
            pltpu.SemaphoreType.DMA((self.n * (NDEV - 1),)),
            pltpu.SemaphoreType.DMA((self.n,)),
        ]
        self.result = None

    @staticmethod
    def _copy(src, dst, sems, i, k, dev):
        send, recv, _ = sems
        return pltpu.make_async_remote_copy(
            src_ref=src, dst_ref=dst, send_sem=send.at[i * (NDEV - 1) + k - 1],
            recv_sem=recv.at[i * (NDEV - 1) + k - 1], device_id=dev, device_id_type=pl.DeviceIdType.MESH)


class _Gather(_Exchange):
    def __init__(self, arrs):
        super().__init__(arrs, [jax.ShapeDtypeStruct((NDEV,) + a.shape, a.dtype) for a in arrs])

    def start(self, ins, outs, sems):
        me = _me()[3]
        for i in range(self.n):
            pltpu.make_async_copy(ins[i], outs[i].at[me], sems[2].at[i]).start()
        for k in (1, 2, 4, 6):
            dev, _ = _peer(k)
            for i in range(self.n):
                self._copy(ins[i], outs[i].at[me], sems, i, k, dev).start()

    def finish(self, ins, outs, sems):
        me = _me()[3]
        sibling, _ = _peer(1)
        for k in (2, 4, 6):
            dev, slot = _peer(k)
            for i in range(self.n):
                block = outs[i].at[slot]
                self._copy(ins[i], block, sems, i, k, dev).wait_recv()
                self._copy(block, block, sems, i, k + 1, sibling).start()
        for k in (1, 3, 5, 7):
            dev, slot = _peer(k)
            for i in range(self.n):
                self._copy(ins[i], outs[i].at[slot], sems, i, k, dev).wait_recv()
        for k in range(1, NDEV):
            for i in range(self.n):
                self._copy(ins[i], outs[i].at[me], sems, i, k, sibling).wait_send()
        for i in range(self.n):
            pltpu.make_async_copy(ins[i], outs[i].at[me], sems[2].at[i]).wait()


class _Scatter(_Exchange):
    def __init__(self, units):
        self.rows = [(r0, n) for _, r0, n in units]
        super().__init__([a for a, _, _ in units],
                         [jax.ShapeDtypeStruct((NDEV, n) + a.shape[2:], a.dtype) for a, _, n in units])

    def _src(self, ins, i, slot):
        r0, n = self.rows[i]
        return ins[i].at[slot, pl.ds(r0, n)]

    def start(self, ins, outs, sems):
        me = _me()[3]
        for i in range(self.n):
            pltpu.make_async_copy(self._src(ins, i, me), outs[i].at[me], sems[2].at[i]).start()
        for k in range(1, NDEV):
            dev, slot = _peer(k)
            for i in range(self.n):
                self._copy(self._src(ins, i, slot), outs[i].at[me], sems, i, k, dev).start()

    def finish(self, ins, outs, sems):
        me = _me()[3]
        for k in range(1, NDEV):
            dev, slot = _peer(k)
            for i in range(self.n):
                cp = self._copy(self._src(ins, i, slot), outs[i].at[slot], sems, i, k, dev)
                cp.wait_send()
                cp.wait_recv()
        for i in range(self.n):
            pltpu.make_async_copy(self._src(ins, i, me), outs[i].at[me], sems[2].at[i]).wait()


class _Multi:
    def __init__(self, jobs):
        self.jobs = list(jobs)
        self.arrs = [a for j in self.jobs for a in j.arrs]
        self.n = len(self.arrs)
        self.out_shape = [s for j in self.jobs for s in j.out_shape]
        self.scratch = [s for j in self.jobs for s in j.scratch]
        self._result = None

    def _parts(self, ins, outs, sems):
        o = 0
        for idx, j in enumerate(self.jobs):
            yield j, ins[o:o + j.n], outs[o:o + j.n], sems[3 * idx:3 * idx + 3]
            o += j.n

    def start(self, ins, outs, sems):
        for j, i, o, s in self._parts(ins, outs, sems):
            j.start(i, o, s)

    def finish(self, ins, outs, sems):
        for j, i, o, s in self._parts(ins, outs, sems):
            j.finish(i, o, s)

    @property
    def result(self):
        return self._result

    @result.setter
    def result(self, res):
        self._result = res
        o = 0
        for j in self.jobs:
            j.result = list(res[o:o + j.n])
            o += j.n


def _call(name, body, *, grid, in_specs, out_specs, out_shape, args, scratch=(), comm=None, vmem_mib=48):
    in_specs, out_specs, out_shape = list(in_specs), list(out_specs), list(out_shape)
    scratch, args = list(scratch), list(args)
    ni, no, ns = len(in_specs), len(out_specs), len(scratch)
    kernel_fn = body
    if comm is not None:
        cn = comm.n
        hbm = pl.BlockSpec(memory_space=pltpu.HBM)

        def kernel_fn(*refs):
            ins, c_in = refs[:ni], refs[ni:ni + cn]
            outs, c_out = refs[ni + cn:ni + cn + no], refs[ni + cn + no:ni + 2 * cn + no]
            scr, c_scr = refs[ni + 2 * cn + no:ni + 2 * cn + no + ns], refs[ni + 2 * cn + no + ns:]
            if not grid:
                comm.start(c_in, c_out, c_scr)
                body(*ins, *outs, *scr)
                comm.finish(c_in, c_out, c_scr)
                return
            first = pl.program_id(0) == 0
            last = pl.program_id(0) == grid[0] - 1
            for ax in range(1, len(grid)):
                first = first & (pl.program_id(ax) == 0)
                last = last & (pl.program_id(ax) == grid[ax] - 1)

            @pl.when(first)
            def _():
                comm.start(c_in, c_out, c_scr)

            body(*ins, *outs, *scr)

            @pl.when(last)
            def _():
                comm.finish(c_in, c_out, c_scr)

        in_specs += [hbm] * cn
        out_specs += [hbm] * cn
        out_shape += comm.out_shape
        scratch += comm.scratch
        args += comm.arrs
    params = _cp(len(grid), vmem_mib) if grid else None
    res = pl.pallas_call(
        kernel_fn, out_shape=out_shape, grid=grid, in_specs=in_specs, out_specs=out_specs,
        scratch_shapes=scratch, compiler_params=params, name=name,
    )(*args)
    if comm is not None:
        comm.result = list(res[no:])
    return tuple(res[:no])


def _run_exchange(name, comm):
    _call(name, lambda: None, grid=(), in_specs=[], out_specs=[], out_shape=[], args=[], comm=comm)
    return comm.result


def _pack(arrs, dtype=F32):
    flat = jnp.concatenate([a.astype(dtype).reshape(-1) for a in arrs])
    n = flat.shape[0]
    total = -(-n // (16 * LANES)) * (16 * LANES)
    return jnp.pad(flat, (0, total - n)).reshape(total // LANES, LANES)


def _unpack(packed, shapes, lead=()):
    flat = packed.reshape(lead + (-1,))
    out, off = [], 0
    for s in shapes:
        n = math.prod(s)
        out.append(flat[..., off:off + n].reshape(lead + tuple(s)))
        off += n
    return out


def _mm(name, a, b, *, dims, grid, a_spec, b_spec, o_spec, out_shape, acc_shape=None, nk=1, vmem_mib=48, comm=None):
    nax = len(grid)

    def body(a_ref, b_ref, o_ref, *scratch):
        p = _dot(a_ref[...], b_ref[...], dims)
        if nk == 1:
            o_ref[...] = p.astype(o_ref.dtype)
            return
        acc = scratch[0]
        k = pl.program_id(nax - 1)

        @pl.when(k == 0)
        def _():
            acc[...] = p

        @pl.when(k > 0)
        def _():
            acc[...] += p

        @pl.when(k == nk - 1)
        def _():
            o_ref[...] = acc[...].astype(o_ref.dtype)

    return _call(
        name, body, grid=grid, in_specs=[a_spec, b_spec], out_specs=[o_spec], out_shape=[out_shape], args=[a, b],
        scratch=[pltpu.VMEM(acc_shape, F32)] if nk > 1 else [], comm=comm, vmem_mib=vmem_mib,
    )[0]


def mm_nn(name, a, b, out_dtype, tn=None):
    m, k = a.shape
    n = b.shape[1]
    tm = _tile(m, 512)
    tn = n if tn is None else tn
    return _mm(
        name, a, b, dims="nn", grid=(n // tn, m // tm),
        a_spec=pl.BlockSpec((tm, k), lambda j, i: (i, 0)),
        b_spec=pl.BlockSpec((k, tn), lambda j, i: (0, j)),
        o_spec=pl.BlockSpec((tm, tn), lambda j, i: (i, j)),
        out_shape=jax.ShapeDtypeStruct((m, n), out_dtype),
    )


def mm_nt(name, a, b, out_dtype, tn=None):
    m, k = a.shape
    n = b.shape[0]
    tm = _tile(m, 512)
    tn = n if tn is None else tn
    return _mm(
        name, a, b, dims="nt", grid=(n // tn, m // tm),
        a_spec=pl.BlockSpec((tm, k), lambda j, i: (i, 0)),
        b_spec=pl.BlockSpec((tn, k), lambda j, i: (j, 0)),
        o_spec=pl.BlockSpec((tm, tn), lambda j, i: (i, j)),
        out_shape=jax.ShapeDtypeStruct((m, n), out_dtype),
    )


def mm_kred(name, a_t, b, out_dtype, tno=None, comm=None):
    m, k = a_t.shape
    n = b.shape[1]
    tk = _tile(k, 512)
    tmo = _tile(m, 1024)
    tno = n if tno is None else tno
    return _mm(
        name, a_t, b, dims="nn", grid=(m // tmo, n // tno, k // tk),
        a_spec=pl.BlockSpec((tmo, tk), lambda i, j, kk: (i, kk)),
        b_spec=pl.BlockSpec((tk, tno), lambda i, j, kk: (kk, j)),
        o_spec=pl.BlockSpec((tmo, tno), lambda i, j, kk: (i, j)),
        out_shape=jax.ShapeDtypeStruct((m, n), out_dtype),
        acc_shape=(tmo, tno), nk=k // tk, comm=comm,
    )


def mm_tn(name, a, b, out_dtype, tmo=None, tno=None):
    k, m = a.shape
    n = b.shape[1]
    tk = _tile(k, 512)
    tmo = _tile(m, 1024) if tmo is None else tmo
    tno = n if tno is None else tno
    return _mm(
        name, a, b, dims="tn", grid=(m // tmo, n // tno, k // tk),
        a_spec=pl.BlockSpec((tk, tmo), lambda i, j, kk: (kk, i)),
        b_spec=pl.BlockSpec((tk, tno), lambda i, j, kk: (kk, j)),
        o_spec=pl.BlockSpec((tmo, tno), lambda i, j, kk: (i, j)),
        out_shape=jax.ShapeDtypeStruct((m, n), out_dtype),
        acc_shape=(tmo, tno), nk=k // tk,
    )


def _rstd(x):
    return lax.rsqrt(jnp.mean(x * x, axis=-1, keepdims=True) + RMS_EPS)


def norm_fwd(name, x, g, out_dtype, with_t=False):
    t, d = x.shape
    tm = _tile(t, 256)

    def body(x_ref, g_ref, o_ref, *t_ref):
        xv = x_ref[...]
        h = xv * _rstd(xv) * g_ref[...]
        o_ref[...] = h.astype(o_ref.dtype)
        if with_t:
            t_ref[0][...] = h.T.astype(out_dtype)

    row = pl.BlockSpec((tm, d), lambda i: (i, 0))
    out_shape = [jax.ShapeDtypeStruct((t, d), out_dtype)]
    out_specs = [row]
    if with_t:
        out_shape.append(jax.ShapeDtypeStruct((d, t), out_dtype))
        out_specs.append(pl.BlockSpec((d, tm), lambda i: (0, i)))
    res = pl.pallas_call(
        body, out_shape=out_shape, grid=(t // tm,),
        in_specs=[row, pl.BlockSpec((1, d), lambda i: (0, 0))], out_specs=out_specs,
        compiler_params=_cp(1), name=name,
    )(x, g)
    return tuple(res) if with_t else res[0]


def resid_norm_fwd(name, x, m, g, scale):
    t, d = x.shape
    tm = _tile(t, 256)

    def body(x_ref, m_ref, g_ref, o_ref):
        mv = m_ref[...]
        o_ref[...] = x_ref[...] + scale * (mv * _rstd(mv) * g_ref[...])

    row = pl.BlockSpec((tm, d), lambda i: (i, 0))
    return pl.pallas_call(
        body, out_shape=jax.ShapeDtypeStruct((t, d), F32), grid=(t // tm,),
        in_specs=[row, row, pl.BlockSpec((1, d), lambda i: (0, 0))], out_specs=row,
        compiler_params=_cp(1), name=name,
    )(x, m, g)


def norm_bwd(name, u, g, dy, scale, resid, out_dtype):
    t, d = u.shape
    tm = _tile(t, 256)
    has_resid = resid is not None

    def body(*refs):
        if has_resid:
            u_ref, g_ref, dy_ref, r_ref, du_ref, dg_ref = refs
        else:
            u_ref, g_ref, dy_ref, du_ref, dg_ref = refs
        uv = u_ref[...]
        dyv = dy_ref[...].astype(F32) * scale
        r = _rstd(uv)
        uh = uv * r

        @pl.when(pl.program_id(0) == 0)
        def _():
            dg_ref[...] = jnp.zeros_like(dg_ref)

        dg_ref[...] += jnp.sum(dyv * uh, axis=0, keepdims=True)
        dyg = dyv * g_ref[...]
        du = r * (dyg - uh * jnp.mean(dyg * uh, axis=-1, keepdims=True))
        if has_resid:
            du = du + r_ref[...]
        du_ref[...] = du.astype(du_ref.dtype)

    row = pl.BlockSpec((tm, d), lambda i: (i, 0))
    vec = pl.BlockSpec((1, d), lambda i: (0, 0))
    args = (u, g, dy) + ((resid,) if has_resid else ())
    return pl.pallas_call(
        body,
        out_shape=(jax.ShapeDtypeStruct((t, d), out_dtype), jax.ShapeDtypeStruct((1, d), F32)),
        grid=(t // tm,),
        in_specs=[row, vec, row] + ([row] if has_resid else []),
        out_specs=(row, vec),
        compiler_params=_cp(1), name=name,
    )(*args)


def loss_grad(y, target):
    t, d = y.shape
    tm = _tile(t, 256)
    nt = t // tm

    def body(y_ref, t_ref, dy_ref, loss_ref, acc):
        i = pl.program_id(0)
        e = y_ref[...] - t_ref[...]
        dy_ref[...] = e * (1.0 / d)

        @pl.when(i == 0)
        def _():
            acc[...] = jnp.zeros_like(acc)

        acc[...] += jnp.sum(e * e, axis=0, keepdims=True)

        @pl.when(i == nt - 1)
        def _():
            loss_ref[...] = (0.5 / d) * jnp.sum(acc[...], axis=1, keepdims=True)

    row = pl.BlockSpec((tm, d), lambda i: (i, 0))
    return pl.pallas_call(
        body,
        out_shape=(jax.ShapeDtypeStruct((t, d), F32), jax.ShapeDtypeStruct((1, 1), F32)),
        grid=(nt,), in_specs=[row, row], out_specs=(row, pl.BlockSpec((1, 1), lambda i: (0, 0))),
        scratch_shapes=[pltpu.VMEM((1, d), F32)], compiler_params=_cp(1), name="loss_grad",
    )(y, target)


def _row_tile(r, c):
    if r * c * 4 <= MIB:
        return r
    best = None
    for t in range(16, r, 16):
        if r % t == 0 and t * c * 4 <= MIB:
            best = t
    return r if best is None else best


def adamw(name, w, m, v, pieces, comm=None):
    nl = len(pieces)
    r, c = pieces[0].shape[1:]
    tr = _row_tile(r, c)
    nr = r // tr
    bc1 = 1.0 - ADAM_B1 ** ADAM_STEP
    bc2 = 1.0 - ADAM_B2 ** ADAM_STEP

    def body(w_ref, m_ref, v_ref, *rest):
        p_refs, (g_ref, d_ref, nm_ref, nv_ref) = rest[:nl], rest[nl:]

        def update(p_ref):
            g = p_ref[0].astype(F32)
            for j in range(1, NDEV):
                g = g + p_ref[j].astype(F32)
            m1 = ADAM_B1 * m_ref[...] + (1.0 - ADAM_B1) * g
            v1 = ADAM_B2 * v_ref[...] + (1.0 - ADAM_B2) * (g * g)
            m_hat = m1 / bc1
            v_hat = v1 / bc2
            g_ref[...] = g
            d_ref[...] = -ADAM_LR * (m_hat / (jnp.sqrt(v_hat) + ADAM_EPS) + ADAM_WD * w_ref[...])
            nm_ref[...] = m1
            nv_ref[...] = v1

        if nl == 1:
            update(p_refs[0])
        else:
            for ll in range(nl):
                pl.when(pl.program_id(0) == ll)(lambda ll=ll: update(p_refs[ll]))

    def piece_spec(ll):
        return pl.BlockSpec((NDEV, tr, c), lambda l, i: (0, jnp.where(l == ll, i, jnp.where(l > ll, nr - 1, 0)), 0))

    row = pl.BlockSpec((tr, c), lambda l, i: (l * nr + i, 0))
    out = jax.ShapeDtypeStruct((nl * r, c), F32)
    return _call(
        name, body, grid=(nl, nr), in_specs=[row, row, row] + [piece_spec(ll) for ll in range(nl)],
        out_specs=[row] * 4, out_shape=[out] * 4, args=[w, m, v] + list(pieces), comm=comm,
    )


def _sigmoid(a):
    return 1.0 / (1.0 + jnp.exp(-a))


def ffn_fwd(name, h, wg, wu, wd, comm=None):
    t, d = h.shape
    ns, _, f = wg.shape
    tm = _tile(t, 512)

    def body(h_ref, wg_ref, wu_ref, wd_ref, a_ref, b_ref, hidt_ref, m_ref, acc):
        j = pl.program_id(1)
        hv = h_ref[...]
        a = _dot(hv, wg_ref[...])
        b = _dot(hv, wu_ref[...])
        hid32 = (a * _sigmoid(a)) * b
        hid = hid32.astype(BF16)
        a_ref[...] = a.astype(BF16)
        b_ref[...] = b.astype(BF16)
        hidt_ref[...] = hid32.T.astype(BF16)
        p = _dot(hid, wd_ref[...])

        @pl.when(j == 0)
        def _():
            acc[...] = p

        @pl.when(j > 0)
        def _():
            acc[...] += p

        @pl.when(j == ns - 1)
        def _():
            m_ref[...] = acc[...]

    w_in = pl.BlockSpec((None, d, f), lambda i, j: (j, 0, 0))
    act = pl.BlockSpec((None, tm, f), lambda i, j: (j, i, 0))
    act_shape = jax.ShapeDtypeStruct((ns, t, f), BF16)
    return _call(
        name, body, grid=(t // tm, ns),
        in_specs=[pl.BlockSpec((tm, d), lambda i, j: (i, 0)), w_in, w_in,
                  pl.BlockSpec((None, f, d), lambda i, j: (j, 0, 0))],
        out_specs=[act, act, pl.BlockSpec((None, f, tm), lambda i, j: (j, 0, i)),
                   pl.BlockSpec((tm, d), lambda i, j: (i, 0))],
        out_shape=[act_shape, act_shape, jax.ShapeDtypeStruct((ns, f, t), BF16), jax.ShapeDtypeStruct((t, d), F32)],
        args=[h, wg, wu, wd], scratch=[pltpu.VMEM((tm, d), F32)], comm=comm, vmem_mib=56,
    )


def ffn_bwd(name, dm, a, b, wg, wu, wd, comm=None):
    t, d = dm.shape
    ns, _, f = wg.shape
    tm = _tile(t, 512)

    def body(dm_ref, a_ref, b_ref, wg_ref, wu_ref, wd_ref, da_ref, db_ref, dh_ref, acc):
        j = pl.program_id(1)
        dhid = _dot(dm_ref[...], wd_ref[...], "nt")
        av = a_ref[...].astype(F32)
        bv = b_ref[...].astype(F32)
        sig = _sigmoid(av)
        da = (dhid * bv * (sig * (1.0 + av * (1.0 - sig)))).astype(BF16)
        db = (dhid * (av * sig)).astype(BF16)
        da_ref[...] = da
        db_ref[...] = db
        p = _dot(da, wg_ref[...], "nt") + _dot(db, wu_ref[...], "nt")

        @pl.when(j == 0)
        def _():
            acc[...] = p

        @pl.when(j > 0)
        def _():
            acc[...] += p

        @pl.when(j == ns - 1)
        def _():
            dh_ref[...] = acc[...]

    w_in = pl.BlockSpec((None, d, f), lambda i, j: (j, 0, 0))
    act = pl.BlockSpec((None, tm, f), lambda i, j: (j, i, 0))
    row = pl.BlockSpec((tm, d), lambda i, j: (i, 0))
    act_shape = jax.ShapeDtypeStruct((ns, t, f), BF16)
    return _call(
        name, body, grid=(t // tm, ns),
        in_specs=[row, act, act, w_in, w_in, pl.BlockSpec((None, f, d), lambda i, j: (j, 0, 0))],
        out_specs=[act, act, row],
        out_shape=[act_shape, act_shape, jax.ShapeDtypeStruct((t, d), F32)],
        args=[dm, a, b, wg, wu, wd], scratch=[pltpu.VMEM((tm, d), F32)], comm=comm, vmem_mib=56,
    )


def ffn_wgrad_in(name, h_t, dact, comm=None):
    d, t = h_t.shape
    ns, _, f = dact.shape
    tk = _tile(t, 512)
    return _mm(
        name, h_t, dact, dims="nn", grid=(ns, t // tk),
        a_spec=pl.BlockSpec((d, tk), lambda j, k: (0, k)),
        b_spec=pl.BlockSpec((None, tk, f), lambda j, k: (j, k, 0)),
        o_spec=pl.BlockSpec((None, d, f), lambda j, k: (j, 0, 0)),
        out_shape=jax.ShapeDtypeStruct((ns, d, f), BF16), acc_shape=(d, f), nk=t // tk, comm=comm,
    )


def ffn_wgrad_out(name, hid_t, dm, comm=None):
    ns, f, t = hid_t.shape
    d = dm.shape[1]
    tk = _tile(t, 512)
    return _mm(
        name, hid_t, dm, dims="nn", grid=(ns, t // tk),
        a_spec=pl.BlockSpec((None, f, tk), lambda j, k: (j, 0, k)),
        b_spec=pl.BlockSpec((tk, d), lambda j, k: (k, 0)),
        o_spec=pl.BlockSpec((None, f, d), lambda j, k: (j, 0, 0)),
        out_shape=jax.ShapeDtypeStruct((ns, f, d), BF16), acc_shape=(f, d), nk=t // tk, comm=comm,
    )


def rope_table(t):
    half = ROPE_DIM // 2
    inv = ROPE_THETA ** (-jnp.arange(half, dtype=F32) * 2.0 / ROPE_DIM)
    ang = jnp.arange(t, dtype=F32)[:, None] * inv[None, :]
    cos, sin = jnp.cos(ang), jnp.sin(ang)
    rest = HEAD_DIM - ROPE_DIM
    c = jnp.concatenate([cos, cos, jnp.ones((t, rest), F32)], axis=1)
    sm = jnp.concatenate([-sin, jnp.zeros((t, half + rest), F32)], axis=1)
    sp = jnp.concatenate([jnp.zeros((t, half), F32), sin, jnp.zeros((t, rest), F32)], axis=1)
    return jnp.concatenate([jnp.tile(c, (1, 2)), jnp.tile(sm, (1, 2)), jnp.tile(sp, (1, 2))], axis=1)


def _rope(x, tab, sign):
    w = x.shape[1]
    rep = w // LANES
    c, sm, sp = tab[:, 0:LANES], tab[:, LANES:2 * LANES], tab[:, 2 * LANES:3 * LANES]
    if rep > 1:
        c, sm, sp = jnp.tile(c, (1, rep)), jnp.tile(sm, (1, rep)), jnp.tile(sp, (1, rep))
    half = ROPE_DIM // 2
    return x * c + sign * (pltpu.roll(x, w - half, 1) * sm + pltpu.roll(x, half, 1) * sp)


def _attn_specs():
    prev = lambda n: jnp.maximum(n - 1, 0)
    kblk, vblk = ZK // LANES, ZK // LANES + 1
    return [
        pl.BlockSpec((BLOCK, ATTN_WIDTH), lambda n: (n, 0)),
        pl.BlockSpec((BLOCK, KV_WIDTH), lambda n: (n, kblk)),
        pl.BlockSpec((BLOCK, KV_WIDTH), lambda n: (prev(n), kblk)),
        pl.BlockSpec((BLOCK, KV_WIDTH), lambda n: (n, vblk)),
        pl.BlockSpec((BLOCK, KV_WIDTH), lambda n: (prev(n), vblk)),
        pl.BlockSpec((BLOCK, 3 * LANES), lambda n: (n, 0)),
        pl.BlockSpec((BLOCK, 3 * LANES), lambda n: (prev(n), 0)),
        pl.BlockSpec(memory_space=pltpu.SMEM),
    ]


def _attn_prologue(n, zq_ref, zk_ref, zkp_ref, zv_ref, zvp_ref, tab_ref, tabp_ref):
    q = (_rope(zq_ref[...], tab_ref[...], 1.0) * (HEAD_DIM ** -0.5)).astype(BF16)
    kcat = jnp.concatenate(
        [_rope(zkp_ref[...], tabp_ref[...], 1.0), _rope(zk_ref[...], tab_ref[...], 1.0)], axis=0).astype(BF16)
    vcat = jnp.concatenate([zvp_ref[...], zv_ref[...]], axis=0).astype(BF16)
    qi = lax.broadcasted_iota(jnp.int32, (BLOCK, 2 * BLOCK), 0)
    kj = lax.broadcasted_iota(jnp.int32, (BLOCK, 2 * BLOCK), 1)
    valid = (kj <= qi + BLOCK) & (kj > qi) & ((n > 0) | (kj >= BLOCK))
    return q, kcat, vcat, valid


def _attn_probs(qh, kh, valid, sink):
    s = jnp.where(valid, _dot(qh, kh, "nt"), -1e30)
    mx = jnp.maximum(jnp.max(s, axis=1, keepdims=True), sink)
    p = jnp.exp(s - mx)
    p_sink = jnp.exp(sink - mx)
    inv = 1.0 / (jnp.sum(p, axis=1, keepdims=True) + p_sink)
    return p * inv, p_sink * inv


def attn_fwd(z, tab, sinks, comm=None):
    t = z.shape[0]

    def body(zq_ref, zk_ref, zkp_ref, zv_ref, zvp_ref, tab_ref, tabp_ref, sink_ref, o_ref):
        n = pl.program_id(0)
        q, kcat, vcat, valid = _attn_prologue(n, zq_ref, zk_ref, zkp_ref, zv_ref, zvp_ref, tab_ref, tabp_ref)
        outs = []
        for h in range(N_Q_HEADS):
            kv = slice((h // Q_PER_KV) * HEAD_DIM, (h // Q_PER_KV + 1) * HEAD_DIM)
            p, _ = _attn_probs(q[:, h * HEAD_DIM:(h + 1) * HEAD_DIM], kcat[:, kv], valid, sink_ref[0, h])
            outs.append(_dot(p.astype(BF16), vcat[:, kv]))
        o_ref[...] = jnp.concatenate(outs, axis=1).astype(BF16)

    return _call(
        "attn_fwd", body, grid=(t // BLOCK,), in_specs=_attn_specs(),
        out_specs=[pl.BlockSpec((BLOCK, ATTN_WIDTH), lambda n: (n, 0))],
        out_shape=[jax.ShapeDtypeStruct((t, ATTN_WIDTH), BF16)],
        args=[z, z, z, z, z, tab, tab, sinks], comm=comm,
    )[0]


def attn_bwd(z, tab, sinks, dcat):
    t = z.shape[0]
    nb = t // BLOCK

    def body(zq_ref, zk_ref, zkp_ref, zv_ref, zvp_ref, tab_ref, tabp_ref, sink_ref, do_ref,
             dq_ref, dkv_ref, dsink_ref):
        n = pl.program_id(0)
        q, kcat, vcat, valid = _attn_prologue(n, zq_ref, zk_ref, zkp_ref, zv_ref, zvp_ref, tab_ref, tabp_ref)
        do = do_ref[...]
        lane = lax.broadcasted_iota(jnp.int32, (1, LANES), 1)
        dqs, dks, dvs = [], [], []
        dsink = jnp.zeros((1, LANES), F32)
        for hk in range(N_Q_HEADS // Q_PER_KV):
            kv = slice(hk * HEAD_DIM, (hk + 1) * HEAD_DIM)
            kh, vh = kcat[:, kv], vcat[:, kv]
            dk = jnp.zeros((2 * BLOCK, HEAD_DIM), F32)
            dv = jnp.zeros((2 * BLOCK, HEAD_DIM), F32)
            for g in range(Q_PER_KV):
                h = hk * Q_PER_KV + g
                hs = slice(h * HEAD_DIM, (h + 1) * HEAD_DIM)
                qh, doh = q[:, hs], do[:, hs]
                p, p_sink = _attn_probs(qh, kh, valid, sink_ref[0, h])
                dv = dv + _dot(p.astype(BF16), doh, "tn")
                dp = _dot(doh, vh, "nt")
                rd = jnp.sum(p * dp, axis=1, keepdims=True)
                ds = (p * (dp - rd) * (HEAD_DIM ** -0.5)).astype(BF16)
                dqs.append(_dot(ds, kh))
                dk = dk + _dot(ds, qh, "tn") * (HEAD_DIM ** 0.5)
                dsink = dsink + jnp.where(lane == h, -jnp.sum(p_sink * rd, axis=0, keepdims=True), 0.0)
            dks.append(dk)
            dvs.append(dv)
        dq_ref[...] = _rope(jnp.concatenate(dqs, axis=1), tab_ref[...], -1.0).astype(BF16)
        dkc = jnp.concatenate(dks, axis=1)
        dk_pre = jnp.concatenate(
            [_rope(dkc[:BLOCK], tabp_ref[...], -1.0), _rope(dkc[BLOCK:], tab_ref[...], -1.0)], axis=0)
        dkv_ref[...] = jnp.concatenate([dk_pre, jnp.concatenate(dvs, axis=1)], axis=1)

        @pl.when(n == 0)
        def _():
            dsink_ref[...] = jnp.zeros_like(dsink_ref)

        dsink_ref[...] += dsink

    return pl.pallas_call(
        body,
        out_shape=(jax.ShapeDtypeStruct((t, ATTN_WIDTH), BF16),
                   jax.ShapeDtypeStruct((nb, 2 * BLOCK, 2 * KV_WIDTH), F32),
                   jax.ShapeDtypeStruct((1, LANES), F32)),
        grid=(nb,),
        in_specs=_attn_specs() + [pl.BlockSpec((BLOCK, ATTN_WIDTH), lambda n: (n, 0))],
        out_specs=(pl.BlockSpec((BLOCK, ATTN_WIDTH), lambda n: (n, 0)),
                   pl.BlockSpec((None, 2 * BLOCK, 2 * KV_WIDTH), lambda n: (n, 0, 0)),
                   pl.BlockSpec((1, LANES), lambda n: (0, 0))),
        compiler_params=_cp(1), name="attn_bwd",
    )(z, z, z, z, z, tab, tab, sinks, dcat)


def _gelu(x):
    k = math.sqrt(2.0 / math.pi)
    th = jnp.tanh(k * (x + 0.044715 * (x * x * x)))
    return 0.5 * x * (1.0 + th), th


def _gelu_grad(x, th):
    k = math.sqrt(2.0 / math.pi)
    return 0.5 * (1.0 + th) + 0.5 * x * (1.0 - th * th) * (k * (1.0 + 3.0 * 0.044715 * (x * x)))


def _sgu_core(zu_ref, zv_ref, lng_ref, lnb_ref, w_ref, bt_ref):
    up, vp = zu_ref[...], zv_ref[...]
    u, thu = _gelu(up)
    v, thv = _gelu(vp)
    mu = jnp.mean(v, axis=-1, keepdims=True)
    vc = v - mu
    rstd = lax.rsqrt(jnp.mean(vc * vc, axis=-1, keepdims=True) + RMS_EPS)
    xhat = vc * rstd
    vn = (xhat * lng_ref[...] + lnb_ref[...]).astype(BF16)
    row = lax.broadcasted_iota(jnp.int32, (CHUNK, CHUNK), 0)
    col = lax.broadcasted_iota(jnp.int32, (CHUNK, CHUNK), 1)
    mixed = []
    for g in range(SGU_GROUPS):
        wc = jnp.where(row >= col, w_ref[g], 0.0).astype(BF16)
        mixed.append(_dot(wc, vn[:, g * CHUNK:(g + 1) * CHUNK]) + bt_ref[:, g:g + 1])
    return up, vp, u, thu, thv, rstd, xhat, vn, jnp.concatenate(mixed, axis=1)


def _sgu_specs():
    full = lambda shape: pl.BlockSpec(shape, lambda n: (0,) * len(shape))
    return [
        pl.BlockSpec((CHUNK, SGU_WIDTH), lambda n: (n, ZU // SGU_WIDTH)),
        pl.BlockSpec((CHUNK, SGU_WIDTH), lambda n: (n, ZV // SGU_WIDTH)),
        full((1, SGU_WIDTH)), full((1, SGU_WIDTH)),
        full((SGU_GROUPS, CHUNK, CHUNK)), full((CHUNK, SGU_GROUPS)),
    ]


def sgu_fwd(z, ln_g, ln_b, w, b_t):
    t = z.shape[0]

    def body(zu_ref, zv_ref, lng_ref, lnb_ref, w_ref, bt_ref, o_ref):
        _, _, u, _, _, _, _, _, mixed = _sgu_core(zu_ref, zv_ref, lng_ref, lnb_ref, w_ref, bt_ref)
        o_ref[...] = (u * mixed).astype(BF16)

    return pl.pallas_call(
        body, out_shape=jax.ShapeDtypeStruct((t, SGU_WIDTH), BF16), grid=(t // CHUNK,),
        in_specs=_sgu_specs(), out_specs=pl.BlockSpec((CHUNK, SGU_WIDTH), lambda n: (n, 0)),
        compiler_params=_cp(1), name="sgu_fwd",
    )(z, z, ln_g, ln_b, w, b_t)


def sgu_bwd(z, ln_g, ln_b, w, w_t, b_t, dcat):
    t = z.shape[0]

    def body(zu_ref, zv_ref, lng_ref, lnb_ref, w_ref, bt_ref, wt_ref, dg_ref,
             du_ref, dv_ref, dw_ref, dbt_ref, dlng_ref, dlnb_ref):
        up, vp, u, thu, thv, rstd, xhat, vn, mixed = _sgu_core(zu_ref, zv_ref, lng_ref, lnb_ref, w_ref, bt_ref)
        dgate = dg_ref[...].astype(F32)
        du_ref[...] = (dgate * mixed * _gelu_grad(up, thu)).astype(BF16)
        dmixed = dgate * u
        row = lax.broadcasted_iota(jnp.int32, (CHUNK, CHUNK), 0)
        col = lax.broadcasted_iota(jnp.int32, (CHUNK, CHUNK), 1)

        @pl.when(pl.program_id(0) == 0)
        def _():
            dw_ref[...] = jnp.zeros_like(dw_ref)
            dbt_ref[...] = jnp.zeros_like(dbt_ref)
            dlng_ref[...] = jnp.zeros_like(dlng_ref)
            dlnb_ref[...] = jnp.zeros_like(dlnb_ref)

        dvn, dbt = [], jnp.zeros((CHUNK, LANES), F32)
        for g in range(SGU_GROUPS):
            gs = slice(g * CHUNK, (g + 1) * CHUNK)
            dmx = dmixed[:, gs]
            dmxb = dmx.astype(BF16)
            dbt = dbt + jnp.where(col == g, jnp.sum(dmx, axis=1, keepdims=True), 0.0)
            dw_ref[g] += jnp.where(row >= col, _dot(dmxb, vn[:, gs], "nt"), 0.0)
            wtc = jnp.where(col >= row, wt_ref[g], 0.0).astype(BF16)
            dvn.append(_dot(wtc, dmxb))
        dbt_ref[...] += dbt
        dvn = jnp.concatenate(dvn, axis=1)
        dlnb_ref[...] += jnp.sum(dvn, axis=0, keepdims=True)
        dlng_ref[...] += jnp.sum(dvn * xhat, axis=0, keepdims=True)
        dxh = dvn * lng_ref[...]
        dv = rstd * (dxh - jnp.mean(dxh, axis=-1, keepdims=True) - xhat * jnp.mean(dxh * xhat, axis=-1, keepdims=True))
        dv_ref[...] = (dv * _gelu_grad(vp, thv)).astype(BF16)

    full = lambda shape: pl.BlockSpec(shape, lambda n: (0,) * len(shape))
    act = pl.BlockSpec((CHUNK, SGU_WIDTH), lambda n: (n, 0))
    act_shape = jax.ShapeDtypeStruct((t, SGU_WIDTH), BF16)
    vec = jax.ShapeDtypeStruct((1, SGU_WIDTH), F32)
    return pl.pallas_call(
        body,
        out_shape=(act_shape, act_shape, jax.ShapeDtypeStruct((SGU_GROUPS, CHUNK, CHUNK), F32),
                   jax.ShapeDtypeStruct((CHUNK, LANES), F32), vec, vec),
        grid=(t // CHUNK,),
        in_specs=_sgu_specs() + [full((SGU_GROUPS, CHUNK, CHUNK)),
                                 pl.BlockSpec((CHUNK, SGU_WIDTH), lambda n: (n, 1))],
        out_specs=(act, act, full((SGU_GROUPS, CHUNK, CHUNK)), full((CHUNK, LANES)),
                   full((1, SGU_WIDTH)), full((1, SGU_WIDTH))),
        compiler_params=_cp(1), name="sgu_bwd",
    )(z, z, ln_g, ln_b, w, b_t, w_t, dcat)


def dz_assemble(dq, dkv, du, dv):
    t = dq.shape[0]
    nb = t // BLOCK

    def body(dq_ref, cur_ref, nxt_ref, du_ref, dv_ref, o_ref):
        n = pl.program_id(0)
        o_ref[:, ZQ:ZQ + ATTN_WIDTH] = dq_ref[...]
        o_ref[:, ZU:ZU + SGU_WIDTH] = du_ref[...]
        o_ref[:, ZV:ZV + SGU_WIDTH] = dv_ref[...]
        kv = cur_ref[BLOCK:, :] + jnp.where(n < nb - 1, nxt_ref[:BLOCK, :], 0.0)
        o_ref[:, ZK:ZK + 2 * KV_WIDTH] = kv.astype(BF16)

    act = pl.BlockSpec((BLOCK, ATTN_WIDTH), lambda n: (n, 0))
    return pl.pallas_call(
        body, out_shape=jax.ShapeDtypeStruct((t, IN_WIDTH), BF16), grid=(nb,),
        in_specs=[act,
                  pl.BlockSpec((None, 2 * BLOCK, 2 * KV_WIDTH), lambda n: (n, 0, 0)),
                  pl.BlockSpec((None, 2 * BLOCK, 2 * KV_WIDTH), lambda n: (jnp.minimum(n + 1, nb - 1), 0, 0)),
                  act, act],
        out_specs=pl.BlockSpec((BLOCK, IN_WIDTH), lambda n: (n, 0)),
        compiler_params=_cp(1), name="dz_assemble",
    )(dq, dkv, dkv, du, dv)


def _pool_count(i, tp, w):
    t_idx = i * tp + lax.broadcasted_iota(jnp.int32, (tp, 1), 0)
    return jnp.minimum(t_idx + 1, w).astype(F32)


def pool_fwd(h, pw, pscale):
    t, d = h.shape
    tp = _tile(t, 256)
    per = tp // POOL_HALO

    def body(h_ref, halo_ref, pw_ref, ps_ref, m_ref, pooled_ref):
        i = pl.program_id(0)
        cur = h_ref[...]
        ext = jnp.concatenate([jnp.where(i > 0, halo_ref[...], 0.0), cur], axis=0)
        ys, pooled = [], []
        for gi, w in enumerate(POOL_WINDOWS):
            gs = slice(gi * POOL_GROUP_DIM, (gi + 1) * POOL_GROUP_DIM)
            s = ext[:, gs]
            sh = 1
            while sh < w:
                s = s + pltpu.roll(s, sh, 0)
                sh *= 2
            pg = (s[POOL_HALO:, :] / _pool_count(i, tp, w) - cur[:, gs]).astype(BF16)
            pooled.append(pg)
            ys.append(_dot(pg, pw_ref[gi]))
        pooled_ref[...] = jnp.concatenate(pooled, axis=1)
        m_ref[...] = jnp.concatenate(ys, axis=1) * ps_ref[...]

    row = pl.BlockSpec((tp, d), lambda i: (i, 0))
    return pl.pallas_call(
        body,
        out_shape=(jax.ShapeDtypeStruct((t, d), F32), jax.ShapeDtypeStruct((t, d), BF16)),
        grid=(t // tp,),
        in_specs=[row, pl.BlockSpec((POOL_HALO, d), lambda i: (jnp.maximum(i * per - 1, 0), 0)),
                  pl.BlockSpec(pw.shape, lambda i: (0, 0, 0)), pl.BlockSpec((1, d), lambda i: (0, 0))],
        out_specs=(row, row), compiler_params=_cp(1), name="pool_fwd",
    )(h, h, pw, pscale)


def pool_bwd_proj(dm, pooled, pw, pscale):
    t, d = dm.shape
    tp = _tile(t, 256)

    def body(dm_ref, pooled_ref, pw_ref, ps_ref, dp_ref, dy_ref, dps_ref):
        dmv = dm_ref[...]
        dy = (dmv * ps_ref[...]).astype(BF16)
        dy_ref[...] = dy
        ys, dps = [], []
        for gi in range(len(POOL_WINDOWS)):
            gs = slice(gi * POOL_GROUP_DIM, (gi + 1) * POOL_GROUP_DIM)
            ys.append(_dot(pooled_ref[:, gs], pw_ref[gi]))
            dps.append(_dot(dy[:, gs], pw_ref[gi], "nt"))
        dp_ref[...] = jnp.concatenate(dps, axis=1)

        @pl.when(pl.program_id(0) == 0)
        def _():
            dps_ref[...] = jnp.zeros_like(dps_ref)

        dps_ref[...] += jnp.sum(dmv * jnp.concatenate(ys, axis=1), axis=0, keepdims=True)

    row = pl.BlockSpec((tp, d), lambda i: (i, 0))
    vec = pl.BlockSpec((1, d), lambda i: (0, 0))
    return pl.pallas_call(
        body,
        out_shape=(jax.ShapeDtypeStruct((t, d), F32), jax.ShapeDtypeStruct((t, d), BF16),
                   jax.ShapeDtypeStruct((1, d), F32)),
        grid=(t // tp,),
        in_specs=[row, row, pl.BlockSpec(pw.shape, lambda i: (0, 0, 0)), vec],
        out_specs=(row, row, vec), compiler_params=_cp(1), name="pool_bwd_proj",
    )(dm, pooled, pw, pscale)


def pool_bwd_window(dp):
    t, d = dp.shape
    tp = _tile(t, 256)
    per = tp // POOL_HALO
    last = t // POOL_HALO - 1
    nt = t // tp

    def body(dp_ref, halo_ref, dh_ref):
        i = pl.program_id(0)
        cur = dp_ref[...]
        halo = jnp.where(i < nt - 1, halo_ref[...], 0.0)
        outs = []
        for gi, w in enumerate(POOL_WINDOWS):
            gs = slice(gi * POOL_GROUP_DIM, (gi + 1) * POOL_GROUP_DIM)
            s = jnp.concatenate([cur[:, gs] / _pool_count(i, tp, w), halo[:, gs] / float(w)], axis=0)
            sh = 1
            while sh < w:
                s = s + pltpu.roll(s, tp + POOL_HALO - sh, 0)
                sh *= 2
            outs.append(s[:tp, :] - cur[:, gs])
        dh_ref[...] = jnp.concatenate(outs, axis=1)

    row = pl.BlockSpec((tp, d), lambda i: (i, 0))
    return pl.pallas_call(
        body, out_shape=jax.ShapeDtypeStruct((t, d), F32), grid=(nt,),
        in_specs=[row, pl.BlockSpec((POOL_HALO, d), lambda i: (jnp.minimum((i + 1) * per, last), 0))],
        out_specs=row, compiler_params=_cp(1), name="pool_bwd_window",
    )(dp, dp)


def pool_wgrad(pooled, dy):
    t, d = pooled.shape
    ng = d // POOL_GROUP_DIM
    tk = _tile(t, 512)
    blk = pl.BlockSpec((tk, POOL_GROUP_DIM), lambda g, k: (k, g))
    return _mm(
        "pool_wgrad", pooled, dy, dims="tn", grid=(ng, t // tk), a_spec=blk, b_spec=blk,
        o_spec=pl.BlockSpec((None, POOL_GROUP_DIM, POOL_GROUP_DIM), lambda g, k: (g, 0, 0)),
        out_shape=jax.ShapeDtypeStruct((ng, POOL_GROUP_DIM, POOL_GROUP_DIM), F32),
        acc_shape=(POOL_GROUP_DIM, POOL_GROUP_DIM), nk=t // tk,
    )


def _xattn_probs(qh, kh):
    s = _dot(qh, kh, "nt") * (X_HEAD_DIM ** -0.5)
    p = jnp.exp(s - jnp.max(s, axis=1, keepdims=True))
    return p * (1.0 / jnp.sum(p, axis=1, keepdims=True))


def xattn_fwd(name, q, k, v):
    t, xw = q.shape
    tm = _tile(t, 512)

    def body(q_ref, k_ref, v_ref, o_ref):
        outs = []
        for h in range(X_HEADS):
            hs = slice(h * X_HEAD_DIM, (h + 1) * X_HEAD_DIM)
            p = _xattn_probs(q_ref[:, hs], k_ref[:, hs])
            outs.append(_dot(p.astype(BF16), v_ref[:, hs]))
        o_ref[...] = jnp.concatenate(outs, axis=1).astype(BF16)

    row = pl.BlockSpec((tm, xw), lambda i: (i, 0))
    kv = pl.BlockSpec(k.shape, lambda i: (0, 0))
    return pl.pallas_call(
        body, out_shape=jax.ShapeDtypeStruct((t, xw), BF16), grid=(t // tm,),
        in_specs=[row, kv, kv], out_specs=row, compiler_params=_cp(1), name=name,
    )(q, k, v)


def xattn_bwd(name, q, k, v, do):
    t, xw = q.shape
    tm = _tile(t, 512)

    def body(q_ref, k_ref, v_ref, do_ref, dq_ref, dk_ref, dv_ref):
        @pl.when(pl.program_id(0) == 0)
        def _():
            dk_ref[...] = jnp.zeros_like(dk_ref)
            dv_ref[...] = jnp.zeros_like(dv_ref)

        dqs = []
        for h in range(X_HEADS):
            hs = slice(h * X_HEAD_DIM, (h + 1) * X_HEAD_DIM)
            qh, kh, vh, doh = q_ref[:, hs], k_ref[:, hs], v_ref[:, hs], do_ref[:, hs]
            p = _xattn_probs(qh, kh)
            dv_ref[:, hs] += _dot(p.astype(BF16), doh, "tn")
            dp = _dot(doh, vh, "nt")
            ds = (p * (dp - jnp.sum(p * dp, axis=1, keepdims=True)) * (X_HEAD_DIM ** -0.5)).astype(BF16)
            dqs.append(_dot(ds, kh))
            dk_ref[:, hs] += _dot(ds, qh, "tn")
        dq_ref[...] = jnp.concatenate(dqs, axis=1).astype(BF16)

    row = pl.BlockSpec((tm, xw), lambda i: (i, 0))
    kv = pl.BlockSpec(k.shape, lambda i: (0, 0))
    kv_shape = jax.ShapeDtypeStruct(k.shape, F32)
    return pl.pallas_call(
        body, out_shape=(jax.ShapeDtypeStruct((t, xw), BF16), kv_shape, kv_shape), grid=(t // tm,),
        in_specs=[row, kv, kv, row], out_specs=(row, kv, kv), compiler_params=_cp(1), name=name,
    )(q, k, v, do)


def xattn_out(name, o, wo):
    t, xw = o.shape
    ns, _, dn = wo.shape
    tm = _tile(t, 512)
    return _mm(
        name, o, wo, dims="nn", grid=(t // tm, ns),
        a_spec=pl.BlockSpec((tm, xw), lambda i, j: (i, 0)),
        b_spec=pl.BlockSpec((None, xw, dn), lambda i, j: (j, 0, 0)),
        o_spec=pl.BlockSpec((tm, dn), lambda i, j: (i, j)),
        out_shape=jax.ShapeDtypeStruct((t, ns * dn), F32),
    )


def xattn_out_bwd(name, dm, wo):
    t = dm.shape[0]
    ns, xw, dn = wo.shape
    tm = _tile(t, 512)
    return _mm(
        name, dm, wo, dims="nt", grid=(t // tm, ns),
        a_spec=pl.BlockSpec((tm, dn), lambda i, j: (i, j)),
        b_spec=pl.BlockSpec((None, xw, dn), lambda i, j: (j, 0, 0)),
        o_spec=pl.BlockSpec((tm, xw), lambda i, j: (i, 0)),
        out_shape=jax.ShapeDtypeStruct((t, xw), BF16), acc_shape=(tm, xw), nk=ns,
    )


def xattn_out_wgrad(name, o, dm, ns):
    t, xw = o.shape
    dn = dm.shape[1] // ns
    tk = _tile(t, 512)
    return _mm(
        name, o, dm, dims="tn", grid=(ns, t // tk),
        a_spec=pl.BlockSpec((tk, xw), lambda j, k: (k, 0)),
        b_spec=pl.BlockSpec((tk, dn), lambda j, k: (k, j)),
        o_spec=pl.BlockSpec((None, xw, dn), lambda j, k: (j, 0, 0)),
        out_shape=jax.ShapeDtypeStruct((ns, xw, dn), BF16), acc_shape=(xw, dn), nk=t // tk,
    )


def kernel(x, mem, norms, mem_norm, ffn1_wg, ffn1_wu, ffn1_wd, ffn2_wg, ffn2_wu, ffn2_wd, x_wq, x_wk, x_wv, x_wo, mix_w_in, mix_w_out, attn_sinks, sgu_ln_g, sgu_ln_b, sgu_w, sgu_b, pool_w, pool_scale, loss_target, m_norms, m_mem_norm, m_ffn1_wg, m_ffn1_wu, m_ffn1_wd, m_ffn2_wg, m_ffn2_wu, m_ffn2_wd, m_x_wq, m_x_wk, m_x_wv, m_x_wo, m_mix_w_in, m_mix_w_out, m_attn_sinks, m_sgu_ln_g, m_sgu_ln_b, m_sgu_w, m_sgu_b, m_pool_w, m_pool_scale, v_norms, v_mem_norm, v_ffn1_wg, v_ffn1_wu, v_ffn1_wd, v_ffn2_wg, v_ffn2_wu, v_ffn2_wd, v_x_wq, v_x_wk, v_x_wv, v_x_wo, v_mix_w_in, v_mix_w_out, v_attn_sinks, v_sgu_ln_g, v_sgu_ln_b, v_sgu_w, v_sgu_b, v_pool_w, v_pool_scale):
    params = dict(norms=norms, mem_norm=mem_norm, ffn1_wg=ffn1_wg, ffn1_wu=ffn1_wu, ffn1_wd=ffn1_wd,
                  ffn2_wg=ffn2_wg, ffn2_wu=ffn2_wu, ffn2_wd=ffn2_wd, x_wq=x_wq, x_wk=x_wk, x_wv=x_wv, x_wo=x_wo,
                  mix_w_in=mix_w_in, mix_w_out=mix_w_out, attn_sinks=attn_sinks, sgu_ln_g=sgu_ln_g,
                  sgu_ln_b=sgu_ln_b, sgu_w=sgu_w, sgu_b=sgu_b, pool_w=pool_w, pool_scale=pool_scale)
    mom1 = dict(norms=m_norms, mem_norm=m_mem_norm, ffn1_wg=m_ffn1_wg, ffn1_wu=m_ffn1_wu, ffn1_wd=m_ffn1_wd,
                ffn2_wg=m_ffn2_wg, ffn2_wu=m_ffn2_wu, ffn2_wd=m_ffn2_wd, x_wq=m_x_wq, x_wk=m_x_wk, x_wv=m_x_wv,
                x_wo=m_x_wo, mix_w_in=m_mix_w_in, mix_w_out=m_mix_w_out, attn_sinks=m_attn_sinks,
                sgu_ln_g=m_sgu_ln_g, sgu_ln_b=m_sgu_ln_b, sgu_w=m_sgu_w, sgu_b=m_sgu_b, pool_w=m_pool_w,
                pool_scale=m_pool_scale)
    mom2 = dict(norms=v_norms, mem_norm=v_mem_norm, ffn1_wg=v_ffn1_wg, ffn1_wu=v_ffn1_wu, ffn1_wd=v_ffn1_wd,
                ffn2_wg=v_ffn2_wg, ffn2_wu=v_ffn2_wu, ffn2_wd=v_ffn2_wd, x_wq=v_x_wq, x_wk=v_x_wk, x_wv=v_x_wv,
                x_wo=v_x_wo, mix_w_in=v_mix_w_in, mix_w_out=v_mix_w_out, attn_sinks=v_attn_sinks,
                sgu_ln_g=v_sgu_ln_g, sgu_ln_b=v_sgu_ln_b, sgu_w=v_sgu_w, sgu_b=v_sgu_b, pool_w=v_pool_w,
                pool_scale=v_pool_scale)
    order = list(params)

    xs, memb, target = x[0], mem[0], loss_target[0]
    t, d = xs.shape
    depth = norms.shape[0]
    dsh = d // NDEV

    bf = lambda a: a.astype(BF16)
    wts = {}

    def gather_job(keys):
        return _Gather([bf(params[name][l]) for name, l in keys]), keys

    def land(job_keys):
        job, keys = job_keys
        for key, a in zip(keys, job.result):
            wts[key] = a

    ffn_keys = lambda tag, l: [(f"{tag}_wg", l), (f"{tag}_wu", l), (f"{tag}_wd", l)]
    x_keys = lambda l: [("x_wq", l), ("x_wk", l), ("x_wv", l), ("x_wo", l)]
    small_shapes = [norms.shape, pool_scale.shape, pool_w.shape]
    head = gather_job(ffn_keys("ffn1", 0))
    head_small = _Gather([_pack([norms, pool_scale, pool_w])])
    _run_exchange("gather_head", _Multi([head_small, head[0]]))
    land(head)
    norms_sh, pscale_sh, pw_sh = _unpack(head_small.result[0], small_shapes, (NDEV,))
    norms_full = norms_sh.transpose(1, 2, 0, 3).reshape(depth, norms.shape[1], d)
    pscale_full = pscale_sh.transpose(1, 0, 2).reshape(1, d)
    pw_full = pw_sh[:, 0].transpose(1, 0, 2, 3).reshape(len(POOL_WINDOWS), POOL_GROUP_DIM, POOL_GROUP_DIM).astype(BF16)
    fwd_jobs = {
        ("ffn1", 0): gather_job([("mix_w_in", 0), ("mix_w_out", 0)] + x_keys(0) + ffn_keys("ffn2", 0)[:2]),
        ("attn", 0): gather_job(ffn_keys("ffn2", 0)[2:]),
        ("ffn2", 0): gather_job(ffn_keys("ffn1", 1)),
        ("ffn1", 1): gather_job(x_keys(1) + ffn_keys("ffn2", 1)),
    }

    tab = rope_table(t)
    sgu_w0 = sgu_w[0]
    sgu_wt0 = sgu_w0.transpose(0, 2, 1)
    sgu_bt0 = sgu_b[0].T
    gain = lambda l, i: norms_full[l, i][None, :]

    saved = []
    xc = xs
    for l in range(depth):
        s = {}

        def ffn_forward(tag, xc, gi, l=l, s=s):
            h, s[tag + "_ht"] = norm_fwd(f"norm_{tag}_{l}", xc, gain(l, gi), BF16, with_t=True)
            job = fwd_jobs.get((tag, l))
            s[tag + "_a"], s[tag + "_b"], s[tag + "_hidt"], s[tag + "_m"] = ffn_fwd(
                f"{tag}_fwd_{l}", h, wts[tag + "_wg", l], wts[tag + "_wu", l], wts[tag + "_wd", l],
                comm=job and job[0])
            if job:
                land(job)
            return resid_norm_fwd(f"resid_{tag}_{l}", xc, s[tag + "_m"], gain(l, gi + 1), 0.5)

        s["x0"] = xc
        xc = ffn_forward("ffn1", xc, 0)

        s["x1"] = xc
        if l % 2 == 0:
            w_in = wts["mix_w_in", l].transpose(1, 0, 2).reshape(d, IN_WIDTH)
            o_k, o_u = ATTN_WIDTH, ATTN_WIDTH + 2 * KV_WIDTH
            w_in = jnp.concatenate([w_in[:, :o_k], w_in[:, o_u:], w_in[:, o_k:o_u]], axis=1)
            w_out = wts["mix_w_out", l].reshape(d, d)
            h2, s["h2t"] = norm_fwd(f"norm_mix_{l}", xc, gain(l, 2), BF16, with_t=True)
            s["z"] = mm_nn("mix_in", h2, w_in, F32, tn=IN_WIDTH // 2)
            job = fwd_jobs[("attn", l)]
            attn = attn_fwd(s["z"], tab, attn_sinks, comm=job[0])
            land(job)
            gate = sgu_fwd(s["z"], sgu_ln_g, sgu_ln_b, sgu_w0, sgu_bt0)
            s["cat"] = jnp.concatenate([attn, gate], axis=1)
            s["m2"] = mm_nn("mix_out", s["cat"], w_out, F32)
        else:
            h2 = norm_fwd(f"norm_mix_{l}", xc, gain(l, 2), F32)
            s["m2"], s["pooled"] = pool_fwd(h2, pw_full, pscale_full)
        xc = resid_norm_fwd(f"resid_mix_{l}", xc, s["m2"], gain(l, 3), 1.0)

        s["x2"] = xc
        wq, wk, wv = (wts[k, l].reshape(d, -1) for k in ("x_wq", "x_wk", "x_wv"))
        s["wq"], s["wkv"] = wq, jnp.concatenate([wk, wv], axis=1)
        h3, s["h3t"] = norm_fwd(f"norm_x_{l}", xc, gain(l, 4), BF16, with_t=True)
        s["mem_n"] = norm_fwd(f"norm_mem_{l}", memb, mem_norm[l][None, :], BF16)
        s["q"] = mm_nn(f"x_q_{l}", h3, wq, BF16)
        s["k"] = mm_nn(f"x_k_{l}", s["mem_n"], wk, BF16)
        s["v"] = mm_nn(f"x_v_{l}", s["mem_n"], wv, BF16)
        s["o"] = xattn_fwd(f"xattn_fwd_{l}", s["q"], s["k"], s["v"])
        s["m3"] = xattn_out(f"x_o_{l}", s["o"], wts["x_wo", l])
        xc = resid_norm_fwd(f"resid_x_{l}", xc, s["m3"], gain(l, 5), 1.0)

        s["x3"] = xc
        xc = ffn_forward("ffn2", xc, 6)
        saved.append(s)

    dx, loss11 = loss_grad(xc, target)
    loss = lax.psum(loss11[0, 0], ("x", "y", "c"))

    pending = []
    recv = {}

    def emit(name, l, arr):
        slot_mib = math.prod(arr.shape[1:]) * arr.dtype.itemsize / MIB
        parts = 2 if slot_mib > 1.5 else 1
        rows = arr.shape[1] // parts
        for part in range(parts):
            pending.append(((name, l, part), (arr, part * rows, rows), LINK_US_PER_SLOT_MIB * slot_mib / parts))

    def hosted(budget_us, fn, *args, extra=(), **kw):
        items, used = [], 0.0
        while pending and (not items or used + pending[0][2] <= budget_us):
            items.append(pending.pop(0))
            used += items[-1][2]
        jobs = ([_Scatter([u for _, u, _ in items])] if items else []) + list(extra)
        res = fn(*args, comm=_Multi(jobs) if jobs else None, **kw)
        if items:
            for (key, _, _), a in zip(items, jobs[0].result):
                recv[key] = a
        return res

    grads = {k: [None] * depth for k in ("norms", "mem_norm")}
    small_jobs = []
    for l in reversed(range(depth)):
        s = saved[l]
        dg = [None] * 8

        def ffn_block(tag, dx, x_in, gi, extra=(), l=l, s=s, dg=dg):
            dm, dg[gi + 1] = norm_bwd(f"{tag}_post_bwd_{l}", s[tag + "_m"], gain(l, gi + 1), dx, 0.5, None, BF16)
            da, db, dh = hosted(HOST_US_FFN_BWD, ffn_bwd, f"{tag}_bwd_{l}", dm, s[tag + "_a"], s[tag + "_b"],
                                wts[tag + "_wg", l], wts[tag + "_wu", l], wts[tag + "_wd", l], extra=extra)
            emit(tag + "_wg", l, hosted(HOST_US_WGRAD, ffn_wgrad_in, f"{tag}_dwg_{l}", s[tag + "_ht"], da))
            emit(tag + "_wu", l, hosted(HOST_US_WGRAD, ffn_wgrad_in, f"{tag}_dwu_{l}", s[tag + "_ht"], db))
            emit(tag + "_wd", l, hosted(HOST_US_WGRAD, ffn_wgrad_out, f"{tag}_dwd_{l}", s[tag + "_hidt"], dm))
            dx, dg[gi] = norm_bwd(f"{tag}_pre_bwd_{l}", x_in, gain(l, gi), dh, 1.0, dx, F32)
            return dx

        dx = ffn_block("ffn2", dx, s["x3"], 6)

        dm, dg[5] = norm_bwd(f"x_post_bwd_{l}", s["m3"], gain(l, 5), dx, 1.0, None, BF16)
        do = xattn_out_bwd(f"x_do_{l}", dm, wts["x_wo", l])
        emit("x_wo", l, xattn_out_wgrad(f"x_dwo_{l}", s["o"], dm, NDEV))
        dq, dk, dv = xattn_bwd(f"xattn_bwd_{l}", s["q"], s["k"], s["v"], do)
        dkb, dvb = dk.astype(BF16), dv.astype(BF16)
        emit("x_wq", l, mm_kred(f"x_dwq_{l}", s["h3t"], dq, BF16).reshape(NDEV, dsh, -1))
        emit("x_wk", l, mm_tn(f"x_dwk_{l}", s["mem_n"], dkb, BF16).reshape(NDEV, dsh, -1))
        emit("x_wv", l, mm_tn(f"x_dwv_{l}", s["mem_n"], dvb, BF16).reshape(NDEV, dsh, -1))
        dh = mm_nt(f"x_dh_{l}", dq, s["wq"], F32)
        dmem_n = mm_nt(f"x_dmem_{l}", jnp.concatenate([dkb, dvb], axis=1), s["wkv"], F32)
        _, grads["mem_norm"][l] = norm_bwd(f"mem_norm_bwd_{l}", memb, mem_norm[l][None, :], dmem_n, 1.0, None, F32)
        dx, dg[4] = norm_bwd(f"x_pre_bwd_{l}", s["x2"], gain(l, 4), dh, 1.0, dx, F32)

        if l % 2 == 0:
            dm, dg[3] = norm_bwd(f"mix_post_bwd_{l}", s["m2"], gain(l, 3), dx, 1.0, None, BF16)
            dcat = mm_nt("mix_dcat", dm, w_out, BF16)
            emit("mix_w_out", l, mm_tn("mix_dwout", s["cat"], dm, BF16, tno=d // 2).reshape(NDEV, dsh, d))
            dq_a, dkv_a, dsink = attn_bwd(s["z"], tab, attn_sinks, dcat)
            du_s, dv_s, g_sgu_w, g_sgu_bt, g_ln_g, g_ln_b = sgu_bwd(
                s["z"], sgu_ln_g, sgu_ln_b, sgu_w0, sgu_wt0, sgu_bt0, dcat)
            dz = dz_assemble(dq_a, dkv_a, du_s, dv_s)
            dh = mm_nt("mix_dh", dz, w_in, F32, tn=d // 2)
            g_win = mm_kred("mix_dwin", s["h2t"], dz, BF16, tno=IN_WIDTH // 2)
            g_win = jnp.concatenate([g_win[:, :ATTN_WIDTH], g_win[:, ZK:], g_win[:, ZU:ZK]], axis=1)
            emit("mix_w_in", l, g_win.reshape(d, NDEV, -1).transpose(1, 0, 2))
        else:
            dm, dg[3] = norm_bwd(f"mix_post_bwd_{l}", s["m2"], gain(l, 3), dx, 1.0, None, F32)
            dp, dy, g_pscale = pool_bwd_proj(dm, s["pooled"], pw_full, pscale_full)
            g_pw = pool_wgrad(s["pooled"], dy)
            emit("pool_w", 0, g_pw.reshape(len(POOL_WINDOWS), NDEV, -1, POOL_GROUP_DIM).transpose(1, 0, 2, 3)
                 .reshape(NDEV, -1, POOL_GROUP_DIM))
            dh = pool_bwd_window(dp)
        dx, dg[2] = norm_bwd(f"mix_pre_bwd_{l}", s["x1"], gain(l, 2), dh, 1.0, dx, F32)

        if l == 0:
            replicated = ["mem_norm", "attn_sinks", "sgu_ln_g", "sgu_ln_b", "sgu_w", "sgu_b"]
            rep_grads = [jnp.concatenate(grads["mem_norm"], axis=0), dsink[:, :N_Q_HEADS], g_ln_g, g_ln_b,
                         g_sgu_w[None], g_sgu_bt[:, :SGU_GROUPS].T[None]]
            small_jobs.append(_Gather([_pack(rep_grads)]))
        dx = ffn_block("ffn1", dx, s["x0"], 0, extra=small_jobs if l == 0 else ())
        grads["norms"][l] = jnp.concatenate(dg, axis=0)

    g_norms = jnp.stack(grads["norms"], axis=0).reshape(depth, norms.shape[1], NDEV, dsh).transpose(2, 0, 1, 3)
    g_pscale_p = g_pscale.reshape(1, NDEV, dsh).transpose(1, 0, 2)
    sharded_small = ["norms", "pool_scale"]
    pieces_small = jnp.stack([_pack([g_norms[j], g_pscale_p[j]]) for j in range(NDEV)], axis=0)
    small_scatter = _Scatter([(pieces_small, 0, pieces_small.shape[1])])

    out = {}

    def update(k, extra=()):
        waiting = [key for key, _, _ in pending if key[0] == k]
        assert not waiting, waiting
        shp = params[k].shape
        c = shp[-1]
        view = lambda a: a.reshape(-1, c)
        pieces = [recv[key] for key in sorted(key for key in recv if key[0] == k)]
        res = hosted(HOST_US_ADAMW, adamw, f"adamw_{k}", view(params[k]), view(mom1[k]), view(mom2[k]), pieces,
                     extra=extra)
        out[k] = [a.reshape(shp) for a in res]

    def update_pack(names, pieces):
        shapes = [params[k].shape for k in names]
        res = adamw("adamw_" + names[0] + "_pack", _pack([params[k] for k in names]),
                    _pack([mom1[k] for k in names]), _pack([mom2[k] for k in names]), [pieces])
        for which in range(4):
            for k, a in zip(names, _unpack(res[which], shapes)):
                out.setdefault(k, [None] * 4)[which] = a

    last = ("ffn1_wg", "ffn1_wu", "ffn1_wd")
    early = [k for k in order if k not in last and k not in sharded_small and k not in replicated]
    for i, k in enumerate(early):
        update(k, extra=[small_scatter] if i == 0 else ())
    if pending:
        hosted(float("inf"), lambda comm: _run_exchange("scatter_tail", comm))
    update_pack(replicated, small_jobs[0].result[0])
    update_pack(sharded_small, small_scatter.result[0])
    for k in last:
        update(k)

    outputs = [loss, dx[None]]
    for which in range(4):
        outputs += [out[k][which] for k in order]
    return tuple(outputs)
```

```python
import math

import jax
import jax.numpy as jnp
from jax import lax
from jax.experimental import pallas as pl
from jax.experimental.pallas import tpu as pltpu

F32 = jnp.float32
BF16 = jnp.bfloat16
NDEV = 8
MIB = 1024 * 1024
LANES = 128

RMS_EPS = 1e-6
HEAD_DIM = 64
N_Q_HEADS = 16
Q_PER_KV = 8
ATTN_WIDTH = 1024
KV_WIDTH = 128
BLOCK = 128
ROPE_DIM = 16
ROPE_THETA = 500000.0
SGU_GROUPS = 8
SGU_WIDTH = 1024
CHUNK = 128
POOL_WINDOWS = (2, 4, 8, 16)
POOL_GROUP_DIM = 512
POOL_HALO = 16
X_HEADS = 4
X_HEAD_DIM = 128
ZQ, ZU, ZV, ZK = 0, 1024, 2048, 3072
IN_WIDTH = 3328

ADAM_LR = 0.001
ADAM_B1 = 0.9
ADAM_B2 = 0.999
ADAM_EPS = 1e-08
ADAM_WD = 0.01
ADAM_STEP = 10

FFN_FWD_ROWS = 1024
FFN_BWD_ROWS = 512

LINK_US_PER_MIB_ALL = 91.0
LINK_US_PER_MIB_CHIPS = 45.0
PAIR_SWAP_US = 20.0
HOST_US_FFN_BWD = 420.0
HOST_US_WGRAD = 100.0
HOST_US_ATTN_BWD = 240.0
HOST_US_SMALL = 40.0
HOST_US_ADAMW = 40.0

_DN = {
    "nn": (((1,), (0,)), ((), ())),
    "nt": (((1,), (1,)), ((), ())),
    "tn": (((0,), (0,)), ((), ())),
}


def _cp(naxes, vmem_mib=48):
    return pltpu.CompilerParams(dimension_semantics=("arbitrary",) * naxes, vmem_limit_bytes=vmem_mib * MIB)


def _tile(n, pref):
    t = min(n, pref)
    while n % t:
        t //= 2
    return t


def _dot(a, b, dims="nn"):
    return lax.dot_general(a, b, _DN[dims], preferred_element_type=F32)


def _me():
    x, y, c = lax.axis_index("x"), lax.axis_index("y"), lax.axis_index("c")
    return x, y, c, 4 * x + 2 * y + c


def _peer(k):
    x, y, c, _ = _me()
    px = 1 - x if k & 4 else x
    py = 1 - y if k & 2 else y
    pc = 1 - c if k & 1 else c
    return (px, py, pc), 4 * px + 2 * py + pc


class _Exchange:
    def __init__(self, arrs, out_shape, remote_per=NDEV - 1, local_per=1):
        self.arrs = list(arrs)
        self.n = len(self.arrs)
        self.out_shape = list(out_shape)
        self.remote_per = remote_per
        self.scratch = [
            pltpu.SemaphoreType.DMA((self.n * remote_per,)),
            pltpu.SemaphoreType.DMA((self.n * remote_per,)),
            pltpu.SemaphoreType.DMA((self.n * local_per,)),
        ]
        self.result = None

    def _copy(self, src, dst, sems, i, k, dev):
        send, recv, _ = sems
        return pltpu.make_async_remote_copy(
            src_ref=src, dst_ref=dst, send_sem=send.at[i * self.remote_per + k - 1],
            recv_sem=recv.at[i * self.remote_per + k - 1], device_id=dev, device_id_type=pl.DeviceIdType.MESH)


class _Gather(_Exchange):
    def __init__(self, arrs):
        super().__init__(arrs, [jax.ShapeDtypeStruct((NDEV,) + a.shape, a.dtype) for a in arrs])

    def start(self, ins, outs, sems):
        me = _me()[3]
        for i in range(self.n):
            pltpu.make_async_copy(ins[i], outs[i].at[me], sems[2].at[i]).start()
        for k in (1, 2, 4, 6):
            dev, _ = _peer(k)
            for i in range(self.n):
                self._copy(ins[i], outs[i].at[me], sems, i, k, dev).start()

    def finish(self, ins, outs, sems):
        me = _me()[3]
        sibling, _ = _peer(1)
        for k in (2, 4, 6):
            dev, slot = _peer(k)
            for i in range(self.n):
                block = outs[i].at[slot]
                self._copy(ins[i], block, sems, i, k, dev).wait_recv()
                self._copy(block, block, sems, i, k + 1, sibling).start()
        for k in (1, 3, 5, 7):
            dev, slot = _peer(k)
            for i in range(self.n):
                self._copy(ins[i], outs[i].at[slot], sems, i, k, dev).wait_recv()
        for k in range(1, NDEV):
            for i in range(self.n):
                self._copy(ins[i], outs[i].at[me], sems, i, k, sibling).wait_send()
        for i in range(self.n):
            pltpu.make_async_copy(ins[i], outs[i].at[me], sems[2].at[i]).wait()


class _Scatter(_Exchange):
    def __init__(self, units):
        self.rows = [(r0, n) for _, r0, n in units]
        super().__init__([a for a, _, _ in units],
                         [jax.ShapeDtypeStruct((NDEV, n) + a.shape[2:], a.dtype) for a, _, n in units])

    def _src(self, ins, i, slot):
        r0, n = self.rows[i]
        return ins[i].at[slot, pl.ds(r0, n)]

    def start(self, ins, outs, sems):
        me = _me()[3]
        for i in range(self.n):
            pltpu.make_async_copy(self._src(ins, i, me), outs[i].at[me], sems[2].at[i]).start()
        for k in range(1, NDEV):
            dev, slot = _peer(k)
            for i in range(self.n):
                self._copy(self._src(ins, i, slot), outs[i].at[me], sems, i, k, dev).start()

    def finish(self, ins, outs, sems):
        me = _me()[3]
        for k in range(1, NDEV):
            dev, slot = _peer(k)
            for i in range(self.n):
                cp = self._copy(self._src(ins, i, slot), outs[i].at[slot], sems, i, k, dev)
                cp.wait_send()
                cp.wait_recv()
        for i in range(self.n):
            pltpu.make_async_copy(self._src(ins, i, me), outs[i].at[me], sems[2].at[i]).wait()


NCHIP = NDEV // 2


class _PairSwap(_Exchange):
    def __init__(self, arrs):
        half = lambda a: jax.ShapeDtypeStruct((NCHIP,) + a.shape[2:], a.dtype)
        super().__init__(arrs, [half(a) for a in arrs for _ in range(2)], remote_per=NCHIP, local_per=NCHIP)

    def _copies(self, ins, outs, sems):
        c = _me()[2]
        sibling, _ = _peer(1)
        for i in range(self.n):
            for q in range(NCHIP):
                local = pltpu.make_async_copy(ins[i].at[q, c], outs[2 * i].at[q], sems[2].at[i * NCHIP + q])
                remote = self._copy(ins[i].at[q, 1 - c], outs[2 * i + 1].at[q], sems, i, q + 1, sibling)
                yield local, remote

    def start(self, ins, outs, sems):
        for local, remote in self._copies(ins, outs, sems):
            local.start()
            remote.start()

    def finish(self, ins, outs, sems):
        for local, remote in self._copies(ins, outs, sems):
            remote.wait_send()
            remote.wait_recv()
            local.wait()


class _ChipScatter(_Exchange):
    def __init__(self, units):
        self.rows = [(r0, n) for _, r0, n in units]
        super().__init__([a for a, _, _ in units],
                         [jax.ShapeDtypeStruct((NCHIP, n) + a.shape[2:], a.dtype) for a, _, n in units],
                         remote_per=NCHIP - 1)

    def _src(self, ins, i, chip):
        r0, n = self.rows[i]
        return ins[i].at[chip, pl.ds(r0, n)]

    @staticmethod
    def _chip(k):
        dev, slot = _peer(2 * k)
        return dev, slot // 2

    def start(self, ins, outs, sems):
        mine = _me()[3] // 2
        for i in range(self.n):
            pltpu.make_async_copy(self._src(ins, i, mine), outs[i].at[mine], sems[2].at[i]).start()
        for k in range(1, NCHIP):
            dev, chip = self._chip(k)
            for i in range(self.n):
                self._copy(self._src(ins, i, chip), outs[i].at[mine], sems, i, k, dev).start()

    def finish(self, ins, outs, sems):
        mine = _me()[3] // 2
        for k in range(1, NCHIP):
            dev, chip = self._chip(k)
            for i in range(self.n):
                cp = self._copy(self._src(ins, i, chip), outs[i].at[chip], sems, i, k, dev)
                cp.wait_send()
                cp.wait_recv()
        for i in range(self.n):
            pltpu.make_async_copy(self._src(ins, i, mine), outs[i].at[mine], sems[2].at[i]).wait()


class _Multi:
    def __init__(self, jobs):
        self.jobs = list(jobs)
        self.arrs = [a for j in self.jobs for a in j.arrs]
        self.out_shape = [s for j in self.jobs for s in j.out_shape]
        self.scratch = [s for j in self.jobs for s in j.scratch]
        self._result = None

    def _parts(self, ins, outs, sems):
        oi = oo = 0
        for idx, j in enumerate(self.jobs):
            ni, no = len(j.arrs), len(j.out_shape)
            yield j, ins[oi:oi + ni], outs[oo:oo + no], sems[3 * idx:3 * idx + 3]
            oi += ni
            oo += no

    def start(self, ins, outs, sems):
        for j, i, o, s in self._parts(ins, outs, sems):
            j.start(i, o, s)

    def finish(self, ins, outs, sems):
        for j, i, o, s in self._parts(ins, outs, sems):
            j.finish(i, o, s)

    @property
    def result(self):
        return self._result

    @result.setter
    def result(self, res):
        self._result = res
        o = 0
        for j in self.jobs:
            j.result = list(res[o:o + len(j.out_shape)])
            o += len(j.out_shape)


def _call(name, body, *, grid, in_specs, out_specs, out_shape, args, scratch=(), comm=None, vmem_mib=48):
    in_specs, out_specs, out_shape = list(in_specs), list(out_specs), list(out_shape)
    scratch, args = list(scratch), list(args)
    ni, no, ns = len(in_specs), len(out_specs), len(scratch)
    kernel_fn = body
    if comm is not None:
        ci, co = len(comm.arrs), len(comm.out_shape)
        hbm = pl.BlockSpec(memory_space=pltpu.HBM)

        def kernel_fn(*refs):
            refs = list(refs)
            ins, c_in, outs, c_out, scr, c_scr = (
                [refs.pop(0) for _ in range(cnt)] for cnt in (ni, ci, no, co, ns, len(comm.scratch)))
            if not grid:
                comm.start(c_in, c_out, c_scr)
                body(*ins, *outs, *scr)
                comm.finish(c_in, c_out, c_scr)
                return
            first = pl.program_id(0) == 0
            last = pl.program_id(0) == grid[0] - 1
            for ax in range(1, len(grid)):
                first = first & (pl.program_id(ax) == 0)
                last = last & (pl.program_id(ax) == grid[ax] - 1)

            @pl.when(first)
            def _():
                comm.start(c_in, c_out, c_scr)

            body(*ins, *outs, *scr)

            @pl.when(last)
            def _():
                comm.finish(c_in, c_out, c_scr)

        in_specs += [hbm] * ci
        out_specs += [hbm] * co
        out_shape += comm.out_shape
        scratch += comm.scratch
        args += comm.arrs
    params = _cp(len(grid), vmem_mib) if grid else None
    res = pl.pallas_call(
        kernel_fn, out_shape=out_shape, grid=grid, in_specs=in_specs, out_specs=out_specs,
        scratch_shapes=scratch, compiler_params=params, name=name,
    )(*args)
    if comm is not None:
        comm.result = list(res[no:])
    return tuple(res[:no])


def _run_exchange(name, comm):
    _call(name, lambda: None, grid=(), in_specs=[], out_specs=[], out_shape=[], args=[], comm=comm)
    return comm.result


def _pack(arrs, dtype=F32):
    flat = jnp.concatenate([a.astype(dtype).reshape(-1) for a in arrs])
    n = flat.shape[0]
    total = -(-n // (16 * LANES)) * (16 * LANES)
    return jnp.pad(flat, (0, total - n)).reshape(total // LANES, LANES)


def _unpack(packed, shapes, lead=()):
    flat = packed.reshape(lead + (-1,))
    out, off = [], 0
    for s in shapes:
        n = math.prod(s)
        out.append(flat[..., off:off + n].reshape(lead + tuple(s)))
        off += n
    return out


def _mm(name, a, b, *, dims, grid, a_spec, b_spec, o_spec, out_shape, acc_shape=None, nk=1, vmem_mib=48, comm=None):
    nax = len(grid)

    def body(a_ref, b_ref, o_ref, *scratch):
        p = _dot(a_ref[...], b_ref[...], dims)
        if nk == 1:
            o_ref[...] = p.astype(o_ref.dtype)
            return
        acc = scratch[0]
        k = pl.program_id(nax - 1)

        @pl.when(k == 0)
        def _():
            acc[...] = p

        @pl.when(k > 0)
        def _():
            acc[...] += p

        @pl.when(k == nk - 1)
        def _():
            o_ref[...] = acc[...].astype(o_ref.dtype)

    return _call(
        name, body, grid=grid, in_specs=[a_spec, b_spec], out_specs=[o_spec], out_shape=[out_shape], args=[a, b],
        scratch=[pltpu.VMEM(acc_shape, F32)] if nk > 1 else [], comm=comm, vmem_mib=vmem_mib,
    )[0]


def mm_nn(name, a, b, out_dtype, tn=None):
    m, k = a.shape
    n = b.shape[1]
    tm = _tile(m, 512)
    tn = n if tn is None else tn
    return _mm(
        name, a, b, dims="nn", grid=(n // tn, m // tm),
        a_spec=pl.BlockSpec((tm, k), lambda j, i: (i, 0)),
        b_spec=pl.BlockSpec((k, tn), lambda j, i: (0, j)),
        o_spec=pl.BlockSpec((tm, tn), lambda j, i: (i, j)),
        out_shape=jax.ShapeDtypeStruct((m, n), out_dtype),
    )


def mm_nt(name, a, b, out_dtype, tn=None):
    m, k = a.shape
    n = b.shape[0]
    tm = _tile(m, 512)
    tn = n if tn is None else tn
    return _mm(
        name, a, b, dims="nt", grid=(n // tn, m // tm),
        a_spec=pl.BlockSpec((tm, k), lambda j, i: (i, 0)),
        b_spec=pl.BlockSpec((tn, k), lambda j, i: (j, 0)),
        o_spec=pl.BlockSpec((tm, tn), lambda j, i: (i, j)),
        out_shape=jax.ShapeDtypeStruct((m, n), out_dtype),
    )


def mm_kred(name, a_t, b, out_dtype, tno=None, comm=None):
    m, k = a_t.shape
    n = b.shape[1]
    tk = _tile(k, 512)
    tmo = _tile(m, 1024)
    tno = n if tno is None else tno
    return _mm(
        name, a_t, b, dims="nn", grid=(m // tmo, n // tno, k // tk),
        a_spec=pl.BlockSpec((tmo, tk), lambda i, j, kk: (i, kk)),
        b_spec=pl.BlockSpec((tk, tno), lambda i, j, kk: (kk, j)),
        o_spec=pl.BlockSpec((tmo, tno), lambda i, j, kk: (i, j)),
        out_shape=jax.ShapeDtypeStruct((m, n), out_dtype),
        acc_shape=(tmo, tno), nk=k // tk, comm=comm,
    )


def mm_tn(name, a, b, out_dtype, tmo=None, tno=None):
    k, m = a.shape
    n = b.shape[1]
    tk = _tile(k, 512)
    tmo = _tile(m, 1024) if tmo is None else tmo
    tno = n if tno is None else tno
    return _mm(
        name, a, b, dims="tn", grid=(m // tmo, n // tno, k // tk),
        a_spec=pl.BlockSpec((tk, tmo), lambda i, j, kk: (kk, i)),
        b_spec=pl.BlockSpec((tk, tno), lambda i, j, kk: (kk, j)),
        o_spec=pl.BlockSpec((tmo, tno), lambda i, j, kk: (i, j)),
        out_shape=jax.ShapeDtypeStruct((m, n), out_dtype),
        acc_shape=(tmo, tno), nk=k // tk,
    )


def _rstd(x):
    return lax.rsqrt(jnp.mean(x * x, axis=-1, keepdims=True) + RMS_EPS)


def norm_fwd(name, x, g, out_dtype, with_t=False):
    t, d = x.shape
    tm = _tile(t, 256)

    def body(x_ref, g_ref, o_ref, *t_ref):
        xv = x_ref[...]
        h = xv * _rstd(xv) * g_ref[...]
        o_ref[...] = h.astype(o_ref.dtype)
        if with_t:
            t_ref[0][...] = h.T.astype(out_dtype)

    row = pl.BlockSpec((tm, d), lambda i: (i, 0))
    out_shape = [jax.ShapeDtypeStruct((t, d), out_dtype)]
    out_specs = [row]
    if with_t:
        out_shape.append(jax.ShapeDtypeStruct((d, t), out_dtype))
        out_specs.append(pl.BlockSpec((d, tm), lambda i: (0, i)))
    res = pl.pallas_call(
        body, out_shape=out_shape, grid=(t // tm,),
        in_specs=[row, pl.BlockSpec((1, d), lambda i: (0, 0))], out_specs=out_specs,
        compiler_params=_cp(1), name=name,
    )(x, g)
    return tuple(res) if with_t else res[0]


def resid_norm_fwd(name, x, m, g, scale):
    t, d = x.shape
    tm = _tile(t, 256)

    def body(x_ref, m_ref, g_ref, o_ref):
        mv = m_ref[...]
        o_ref[...] = x_ref[...] + scale * (mv * _rstd(mv) * g_ref[...])

    row = pl.BlockSpec((tm, d), lambda i: (i, 0))
    return pl.pallas_call(
        body, out_shape=jax.ShapeDtypeStruct((t, d), F32), grid=(t // tm,),
        in_specs=[row, row, pl.BlockSpec((1, d), lambda i: (0, 0))], out_specs=row,
        compiler_params=_cp(1), name=name,
    )(x, m, g)


def norm_bwd(name, u, g, dy, scale, resid, out_dtype, comm=None):
    t, d = u.shape
    tm = _tile(t, 256)
    has_resid = resid is not None

    def body(*refs):
        if has_resid:
            u_ref, g_ref, dy_ref, r_ref, du_ref, dg_ref = refs
        else:
            u_ref, g_ref, dy_ref, du_ref, dg_ref = refs
        uv = u_ref[...]
        dyv = dy_ref[...].astype(F32) * scale
        r = _rstd(uv)
        uh = uv * r

        @pl.when(pl.program_id(0) == 0)
        def _():
            dg_ref[...] = jnp.zeros_like(dg_ref)

        dg_ref[...] += jnp.sum(dyv * uh, axis=0, keepdims=True)
        dyg = dyv * g_ref[...]
        du = r * (dyg - uh * jnp.mean(dyg * uh, axis=-1, keepdims=True))
        if has_resid:
            du = du + r_ref[...]
        du_ref[...] = du.astype(du_ref.dtype)

    row = pl.BlockSpec((tm, d), lambda i: (i, 0))
    vec = pl.BlockSpec((1, d), lambda i: (0, 0))
    args = (u, g, dy) + ((resid,) if has_resid else ())
    return _call(
        name, body, grid=(t // tm,), in_specs=[row, vec, row] + ([row] if has_resid else []),
        out_specs=[row, vec], out_shape=[jax.ShapeDtypeStruct((t, d), out_dtype), jax.ShapeDtypeStruct((1, d), F32)],
        args=args, comm=comm,
    )


def loss_grad(y, target):
    t, d = y.shape
    tm = _tile(t, 256)
    nt = t // tm

    def body(y_ref, t_ref, dy_ref, loss_ref, acc):
        i = pl.program_id(0)
        e = y_ref[...] - t_ref[...]
        dy_ref[...] = e * (1.0 / d)

        @pl.when(i == 0)
        def _():
            acc[...] = jnp.zeros_like(acc)

        acc[...] += jnp.sum(e * e, axis=0, keepdims=True)

        @pl.when(i == nt - 1)
        def _():
            loss_ref[...] = (0.5 / d) * jnp.sum(acc[...], axis=1, keepdims=True)

    row = pl.BlockSpec((tm, d), lambda i: (i, 0))
    return pl.pallas_call(
        body,
        out_shape=(jax.ShapeDtypeStruct((t, d), F32), jax.ShapeDtypeStruct((1, 1), F32)),
        grid=(nt,), in_specs=[row, row], out_specs=(row, pl.BlockSpec((1, 1), lambda i: (0, 0))),
        scratch_shapes=[pltpu.VMEM((1, d), F32)], compiler_params=_cp(1), name="loss_grad",
    )(y, target)


def _row_tile(r, c):
    if r * c * 4 <= MIB:
        return r
    best = None
    for t in range(16, r, 16):
        if r % t == 0 and t * c * 4 <= MIB:
            best = t
    return r if best is None else best


def pair_sum(name, mine, theirs):
    nq, r, c = mine.shape
    tr = _tile(r, 1024)

    def body(a_ref, b_ref, o_ref):
        o_ref[...] = (a_ref[...].astype(F32) + b_ref[...].astype(F32)).astype(o_ref.dtype)

    blk = pl.BlockSpec((None, tr, c), lambda q, i: (q, i, 0))
    return pl.pallas_call(
        body, out_shape=jax.ShapeDtypeStruct(mine.shape, mine.dtype), grid=(nq, r // tr),
        in_specs=[blk, blk], out_specs=blk, compiler_params=_cp(2), name=name,
    )(mine, theirs)


def adamw(name, w, m, v, pieces, comm=None):
    nl = len(pieces)
    npiece, r, c = pieces[0].shape
    tr = _row_tile(r, c)
    nr = r // tr
    bc1 = 1.0 - ADAM_B1 ** ADAM_STEP
    bc2 = 1.0 - ADAM_B2 ** ADAM_STEP

    def body(w_ref, m_ref, v_ref, *rest):
        p_refs, (g_ref, d_ref, nm_ref, nv_ref) = rest[:nl], rest[nl:]

        def update(p_ref):
            g = p_ref[0].astype(F32)
            for j in range(1, npiece):
                g = g + p_ref[j].astype(F32)
            m1 = ADAM_B1 * m_ref[...] + (1.0 - ADAM_B1) * g
            v1 = ADAM_B2 * v_ref[...] + (1.0 - ADAM_B2) * (g * g)
            m_hat = m1 / bc1
            v_hat = v1 / bc2
            g_ref[...] = g
            d_ref[...] = -ADAM_LR * (m_hat / (jnp.sqrt(v_hat) + ADAM_EPS) + ADAM_WD * w_ref[...])
            nm_ref[...] = m1
            nv_ref[...] = v1

        if nl == 1:
            update(p_refs[0])
        else:
            for ll in range(nl):
                pl.when(pl.program_id(0) == ll)(lambda ll=ll: update(p_refs[ll]))

    def piece_spec(ll):
        return pl.BlockSpec((npiece, tr, c), lambda l, i: (0, jnp.where(l == ll, i, jnp.where(l > ll, nr - 1, 0)), 0))

    row = pl.BlockSpec((tr, c), lambda l, i: (l * nr + i, 0))
    out = jax.ShapeDtypeStruct((nl * r, c), F32)
    return _call(
        name, body, grid=(nl, nr), in_specs=[row, row, row] + [piece_spec(ll) for ll in range(nl)],
        out_specs=[row] * 4, out_shape=[out] * 4, args=[w, m, v] + list(pieces), comm=comm,
    )


def _sigmoid(a):
    return 1.0 / (1.0 + jnp.exp(-a))


def ffn_fwd(name, h, wg, wu, wd, comm=None):
    t, d = h.shape
    ns, _, f = wg.shape
    tm = _tile(t, FFN_FWD_ROWS)

    def body(h_ref, wg_ref, wu_ref, wd_ref, a_ref, b_ref, hidt_ref, m_ref):
        j = pl.program_id(1)
        hv = h_ref[...]
        a = _dot(hv, wg_ref[...])
        b = _dot(hv, wu_ref[...])
        hid32 = (a * _sigmoid(a)) * b
        hid = hid32.astype(BF16)
        a_ref[...] = a.astype(BF16)
        b_ref[...] = b.astype(BF16)
        hidt_ref[...] = hid32.T.astype(BF16)
        p = _dot(hid, wd_ref[...])

        @pl.when(j == 0)
        def _():
            m_ref[...] = p

        @pl.when(j > 0)
        def _():
            m_ref[...] += p

    once = pl.Buffered(1)
    w_in = pl.BlockSpec((None, d, f), lambda i, j: (j, 0, 0))
    act = pl.BlockSpec((None, tm, f), lambda i, j: (j, i, 0))
    act_shape = jax.ShapeDtypeStruct((ns, t, f), BF16)
    return _call(
        name, body, grid=(t // tm, ns),
        in_specs=[pl.BlockSpec((tm, d), lambda i, j: (i, 0), pipeline_mode=once), w_in, w_in,
                  pl.BlockSpec((None, f, d), lambda i, j: (j, 0, 0))],
        out_specs=[act, act, pl.BlockSpec((None, f, tm), lambda i, j: (j, 0, i)),
                   pl.BlockSpec((tm, d), lambda i, j: (i, 0), pipeline_mode=once)],
        out_shape=[act_shape, act_shape, jax.ShapeDtypeStruct((ns, f, t), BF16), jax.ShapeDtypeStruct((t, d), F32)],
        args=[h, wg, wu, wd], comm=comm, vmem_mib=56,
    )


def ffn_bwd(name, dm, a, b, wg, wu, wd, comm=None):
    t, d = dm.shape
    ns, _, f = wg.shape
    tm = _tile(t, FFN_BWD_ROWS)

    def body(dm_ref, a_ref, b_ref, wg_ref, wu_ref, wd_ref, da_ref, db_ref, dh_ref):
        j = pl.program_id(1)
        dhid = _dot(dm_ref[...], wd_ref[...], "nt")
        av = a_ref[...].astype(F32)
        bv = b_ref[...].astype(F32)
        sig = _sigmoid(av)
        da = (dhid * bv * (sig * (1.0 + av * (1.0 - sig)))).astype(BF16)
        db = (dhid * (av * sig)).astype(BF16)
        da_ref[...] = da
        db_ref[...] = db
        p = _dot(da, wg_ref[...], "nt") + _dot(db, wu_ref[...], "nt")

        @pl.when(j == 0)
        def _():
            dh_ref[...] = p

        @pl.when(j > 0)
        def _():
            dh_ref[...] += p

    once = pl.Buffered(1)
    w_in = pl.BlockSpec((None, d, f), lambda i, j: (j, 0, 0))
    act = pl.BlockSpec((None, tm, f), lambda i, j: (j, i, 0))
    act_shape = jax.ShapeDtypeStruct((ns, t, f), BF16)
    return _call(
        name, body, grid=(t // tm, ns),
        in_specs=[pl.BlockSpec((tm, d), lambda i, j: (i, 0), pipeline_mode=once), act, act, w_in, w_in,
                  pl.BlockSpec((None, f, d), lambda i, j: (j, 0, 0))],
        out_specs=[act, act, pl.BlockSpec((tm, d), lambda i, j: (i, 0), pipeline_mode=once)],
        out_shape=[act_shape, act_shape, jax.ShapeDtypeStruct((t, d), F32)],
        args=[dm, a, b, wg, wu, wd], comm=comm, vmem_mib=56,
    )


def ffn_wgrad_in(name, h_t, dact, comm=None):
    d, t = h_t.shape
    ns, _, f = dact.shape
    tmo = _tile(d, 1024)
    return _mm(
        name, h_t, dact, dims="nn", grid=(ns, d // tmo),
        a_spec=pl.BlockSpec((tmo, t), lambda j, i: (i, 0)),
        b_spec=pl.BlockSpec((None, t, f), lambda j, i: (j, 0, 0)),
        o_spec=pl.BlockSpec((None, tmo, f), lambda j, i: (j, i, 0)),
        out_shape=jax.ShapeDtypeStruct((ns, d, f), BF16), vmem_mib=56, comm=comm,
    )


def ffn_wgrad_out(name, hid_t, dm, comm=None):
    ns, f, t = hid_t.shape
    d = dm.shape[1]
    return _mm(
        name, hid_t, dm, dims="nn", grid=(ns,),
        a_spec=pl.BlockSpec((None, f, t), lambda j: (j, 0, 0)),
        b_spec=pl.BlockSpec((t, d), lambda j: (0, 0), pipeline_mode=pl.Buffered(1)),
        o_spec=pl.BlockSpec((None, f, d), lambda j: (j, 0, 0)),
        out_shape=jax.ShapeDtypeStruct((ns, f, d), BF16), vmem_mib=56, comm=comm,
    )


def rope_table(t):
    half = ROPE_DIM // 2
    inv = ROPE_THETA ** (-jnp.arange(half, dtype=F32) * 2.0 / ROPE_DIM)
    ang = jnp.arange(t, dtype=F32)[:, None] * inv[None, :]
    cos, sin = jnp.cos(ang), jnp.sin(ang)
    rest = HEAD_DIM - ROPE_DIM
    c = jnp.concatenate([cos, cos, jnp.ones((t, rest), F32)], axis=1)
    sm = jnp.concatenate([-sin, jnp.zeros((t, half + rest), F32)], axis=1)
    sp = jnp.concatenate([jnp.zeros((t, half), F32), sin, jnp.zeros((t, rest), F32)], axis=1)
    return jnp.concatenate([jnp.tile(c, (1, 2)), jnp.tile(sm, (1, 2)), jnp.tile(sp, (1, 2))], axis=1)


def _rope(x, tab, sign):
    w = x.shape[1]
    rep = w // LANES
    c, sm, sp = tab[:, 0:LANES], tab[:, LANES:2 * LANES], tab[:, 2 * LANES:3 * LANES]
    if rep > 1:
        c, sm, sp = jnp.tile(c, (1, rep)), jnp.tile(sm, (1, rep)), jnp.tile(sp, (1, rep))
    half = ROPE_DIM // 2
    return x * c + sign * (pltpu.roll(x, w - half, 1) * sm + pltpu.roll(x, half, 1) * sp)


def _attn_specs():
    prev = lambda n: jnp.maximum(n - 1, 0)
    kblk, vblk = ZK // LANES, ZK // LANES + 1
    return [
        pl.BlockSpec((BLOCK, ATTN_WIDTH), lambda n: (n, 0)),
        pl.BlockSpec((BLOCK, KV_WIDTH), lambda n: (n, kblk)),
        pl.BlockSpec((BLOCK, KV_WIDTH), lambda n: (prev(n), kblk)),
        pl.BlockSpec((BLOCK, KV_WIDTH), lambda n: (n, vblk)),
        pl.BlockSpec((BLOCK, KV_WIDTH), lambda n: (prev(n), vblk)),
        pl.BlockSpec((BLOCK, 3 * LANES), lambda n: (n, 0)),
        pl.BlockSpec((BLOCK, 3 * LANES), lambda n: (prev(n), 0)),
        pl.BlockSpec(memory_space=pltpu.SMEM),
    ]


def _attn_prologue(n, zq_ref, zk_ref, zkp_ref, zv_ref, zvp_ref, tab_ref, tabp_ref):
    q = (_rope(zq_ref[...], tab_ref[...], 1.0) * (HEAD_DIM ** -0.5)).astype(BF16)
    kcat = jnp.concatenate(
        [_rope(zkp_ref[...], tabp_ref[...], 1.0), _rope(zk_ref[...], tab_ref[...], 1.0)], axis=0).astype(BF16)
    vcat = jnp.concatenate([zvp_ref[...], zv_ref[...]], axis=0).astype(BF16)
    qi = lax.broadcasted_iota(jnp.int32, (BLOCK, 2 * BLOCK), 0)
    kj = lax.broadcasted_iota(jnp.int32, (BLOCK, 2 * BLOCK), 1)
    valid = (kj <= qi + BLOCK) & (kj > qi) & ((n > 0) | (kj >= BLOCK))
    return q, kcat, vcat, valid


def _attn_probs(qh, kh, valid, sink):
    s = jnp.where(valid, _dot(qh, kh, "nt"), -1e30)
    mx = jnp.maximum(jnp.max(s, axis=1, keepdims=True), sink)
    p = jnp.exp(s - mx)
    p_sink = jnp.exp(sink - mx)
    inv = 1.0 / (jnp.sum(p, axis=1, keepdims=True) + p_sink)
    return p * inv, p_sink * inv


def attn_fwd(z, tab, sinks, comm=None):
    t = z.shape[0]

    def body(zq_ref, zk_ref, zkp_ref, zv_ref, zvp_ref, tab_ref, tabp_ref, sink_ref, o_ref):
        n = pl.program_id(0)
        q, kcat, vcat, valid = _attn_prologue(n, zq_ref, zk_ref, zkp_ref, zv_ref, zvp_ref, tab_ref, tabp_ref)
        outs = []
        for h in range(N_Q_HEADS):
            kv = slice((h // Q_PER_KV) * HEAD_DIM, (h // Q_PER_KV + 1) * HEAD_DIM)
            p, _ = _attn_probs(q[:, h * HEAD_DIM:(h + 1) * HEAD_DIM], kcat[:, kv], valid, sink_ref[0, h])
            outs.append(_dot(p.astype(BF16), vcat[:, kv]))
        o_ref[...] = jnp.concatenate(outs, axis=1).astype(BF16)

    return _call(
        "attn_fwd", body, grid=(t // BLOCK,), in_specs=_attn_specs(),
        out_specs=[pl.BlockSpec((BLOCK, ATTN_WIDTH), lambda n: (n, 0))],
        out_shape=[jax.ShapeDtypeStruct((t, ATTN_WIDTH), BF16)],
        args=[z, z, z, z, z, tab, tab, sinks], comm=comm,
    )[0]


def attn_bwd(z, tab, sinks, dcat, comm=None):
    t = z.shape[0]
    nb = t // BLOCK

    def body(zq_ref, zk_ref, zkp_ref, zv_ref, zvp_ref, tab_ref, tabp_ref, sink_ref, do_ref,
             dq_ref, dkv_ref, dsink_ref):
        n = pl.program_id(0)
        q, kcat, vcat, valid = _attn_prologue(n, zq_ref, zk_ref, zkp_ref, zv_ref, zvp_ref, tab_ref, tabp_ref)
        do = do_ref[...]
        lane = lax.broadcasted_iota(jnp.int32, (1, LANES), 1)
        dqs, dks, dvs = [], [], []
        dsink = jnp.zeros((1, LANES), F32)
        for hk in range(N_Q_HEADS // Q_PER_KV):
            kv = slice(hk * HEAD_DIM, (hk + 1) * HEAD_DIM)
            kh, vh = kcat[:, kv], vcat[:, kv]
            dk = jnp.zeros((2 * BLOCK, HEAD_DIM), F32)
            dv = jnp.zeros((2 * BLOCK, HEAD_DIM), F32)
            for g in range(Q_PER_KV):
                h = hk * Q_PER_KV + g
                hs = slice(h * HEAD_DIM, (h + 1) * HEAD_DIM)
                qh, doh = q[:, hs], do[:, hs]
                p, p_sink = _attn_probs(qh, kh, valid, sink_ref[0, h])
                dv = dv + _dot(p.astype(BF16), doh, "tn")
                dp = _dot(doh, vh, "nt")
                rd = jnp.sum(p * dp, axis=1, keepdims=True)
                ds = (p * (dp - rd) * (HEAD_DIM ** -0.5)).astype(BF16)
                dqs.append(_dot(ds, kh))
                dk = dk + _dot(ds, qh, "tn") * (HEAD_DIM ** 0.5)
                dsink = dsink + jnp.where(lane == h, -jnp.sum(p_sink * rd, axis=0, keepdims=True), 0.0)
            dks.append(dk)
            dvs.append(dv)
        dq_ref[...] = _rope(jnp.concatenate(dqs, axis=1), tab_ref[...], -1.0).astype(BF16)
        dkc = jnp.concatenate(dks, axis=1)
        dk_pre = jnp.concatenate(
            [_rope(dkc[:BLOCK], tabp_ref[...], -1.0), _rope(dkc[BLOCK:], tab_ref[...], -1.0)], axis=0)
        dkv_ref[...] = jnp.concatenate([dk_pre, jnp.concatenate(dvs, axis=1)], axis=1)

        @pl.when(n == 0)
        def _():
            dsink_ref[...] = jnp.zeros_like(dsink_ref)

        dsink_ref[...] += dsink

    return _call(
        "attn_bwd", body, grid=(nb,),
        in_specs=_attn_specs() + [pl.BlockSpec((BLOCK, ATTN_WIDTH), lambda n: (n, 0))],
        out_specs=[pl.BlockSpec((BLOCK, ATTN_WIDTH), lambda n: (n, 0)),
                   pl.BlockSpec((None, 2 * BLOCK, 2 * KV_WIDTH), lambda n: (n, 0, 0)),
                   pl.BlockSpec((1, LANES), lambda n: (0, 0))],
        out_shape=[jax.ShapeDtypeStruct((t, ATTN_WIDTH), BF16),
                   jax.ShapeDtypeStruct((nb, 2 * BLOCK, 2 * KV_WIDTH), F32),
                   jax.ShapeDtypeStruct((1, LANES), F32)],
        args=[z, z, z, z, z, tab, tab, sinks, dcat], comm=comm,
    )


def _gelu(x):
    k = math.sqrt(2.0 / math.pi)
    th = jnp.tanh(k * (x + 0.044715 * (x * x * x)))
    return 0.5 * x * (1.0 + th), th


def _gelu_grad(x, th):
    k = math.sqrt(2.0 / math.pi)
    return 0.5 * (1.0 + th) + 0.5 * x * (1.0 - th * th) * (k * (1.0 + 3.0 * 0.044715 * (x * x)))


def _sgu_core(zu_ref, zv_ref, lng_ref, lnb_ref, w_ref, bt_ref):
    up, vp = zu_ref[...], zv_ref[...]
    u, thu = _gelu(up)
    v, thv = _gelu(vp)
    mu = jnp.mean(v, axis=-1, keepdims=True)
    vc = v - mu
    rstd = lax.rsqrt(jnp.mean(vc * vc, axis=-1, keepdims=True) + RMS_EPS)
    xhat = vc * rstd
    vn = (xhat * lng_ref[...] + lnb_ref[...]).astype(BF16)
    row = lax.broadcasted_iota(jnp.int32, (CHUNK, CHUNK), 0)
    col = lax.broadcasted_iota(jnp.int32, (CHUNK, CHUNK), 1)
    mixed = []
    for g in range(SGU_GROUPS):
        wc = jnp.where(row >= col, w_ref[g], 0.0).astype(BF16)
        mixed.append(_dot(wc, vn[:, g * CHUNK:(g + 1) * CHUNK]) + bt_ref[:, g:g + 1])
    return up, vp, u, thu, thv, rstd, xhat, vn, jnp.concatenate(mixed, axis=1)


def _sgu_specs():
    full = lambda shape: pl.BlockSpec(shape, lambda n: (0,) * len(shape))
    return [
        pl.BlockSpec((CHUNK, SGU_WIDTH), lambda n: (n, ZU // SGU_WIDTH)),
        pl.BlockSpec((CHUNK, SGU_WIDTH), lambda n: (n, ZV // SGU_WIDTH)),
        full((1, SGU_WIDTH)), full((1, SGU_WIDTH)),
        full((SGU_GROUPS, CHUNK, CHUNK)), full((CHUNK, SGU_GROUPS)),
    ]


def sgu_fwd(z, ln_g, ln_b, w, b_t):
    t = z.shape[0]

    def body(zu_ref, zv_ref, lng_ref, lnb_ref, w_ref, bt_ref, o_ref):
        _, _, u, _, _, _, _, _, mixed = _sgu_core(zu_ref, zv_ref, lng_ref, lnb_ref, w_ref, bt_ref)
        o_ref[...] = (u * mixed).astype(BF16)

    return pl.pallas_call(
        body, out_shape=jax.ShapeDtypeStruct((t, SGU_WIDTH), BF16), grid=(t // CHUNK,),
        in_specs=_sgu_specs(), out_specs=pl.BlockSpec((CHUNK, SGU_WIDTH), lambda n: (n, 0)),
        compiler_params=_cp(1), name="sgu_fwd",
    )(z, z, ln_g, ln_b, w, b_t)


def sgu_bwd(z, ln_g, ln_b, w, w_t, b_t, dcat):
    t = z.shape[0]

    def body(zu_ref, zv_ref, lng_ref, lnb_ref, w_ref, bt_ref, wt_ref, dg_ref,
             du_ref, dv_ref, dw_ref, dbt_ref, dlng_ref, dlnb_ref):
        up, vp, u, thu, thv, rstd, xhat, vn, mixed = _sgu_core(zu_ref, zv_ref, lng_ref, lnb_ref, w_ref, bt_ref)
        dgate = dg_ref[...].astype(F32)
        du_ref[...] = (dgate * mixed * _gelu_grad(up, thu)).astype(BF16)
        dmixed = dgate * u
        row = lax.broadcasted_iota(jnp.int32, (CHUNK, CHUNK), 0)
        col = lax.broadcasted_iota(jnp.int32, (CHUNK, CHUNK), 1)

        @pl.when(pl.program_id(0) == 0)
        def _():
            dw_ref[...] = jnp.zeros_like(dw_ref)
            dbt_ref[...] = jnp.zeros_like(dbt_ref)
            dlng_ref[...] = jnp.zeros_like(dlng_ref)
            dlnb_ref[...] = jnp.zeros_like(dlnb_ref)

        dvn, dbt = [], jnp.zeros((CHUNK, LANES), F32)
        for g in range(SGU_GROUPS):
            gs = slice(g * CHUNK, (g + 1) * CHUNK)
            dmx = dmixed[:, gs]
            dmxb = dmx.astype(BF16)
            dbt = dbt + jnp.where(col == g, jnp.sum(dmx, axis=1, keepdims=True), 0.0)
            dw_ref[g] += jnp.where(row >= col, _dot(dmxb, vn[:, gs], "nt"), 0.0)
            wtc = jnp.where(col >= row, wt_ref[g], 0.0).astype(BF16)
            dvn.append(_dot(wtc, dmxb))
        dbt_ref[...] += dbt
        dvn = jnp.concatenate(dvn, axis=1)
        dlnb_ref[...] += jnp.sum(dvn, axis=0, keepdims=True)
        dlng_ref[...] += jnp.sum(dvn * xhat, axis=0, keepdims=True)
        dxh = dvn * lng_ref[...]
        dv = rstd * (dxh - jnp.mean(dxh, axis=-1, keepdims=True) - xhat * jnp.mean(dxh * xhat, axis=-1, keepdims=True))
        dv_ref[...] = (dv * _gelu_grad(vp, thv)).astype(BF16)

    full = lambda shape: pl.BlockSpec(shape, lambda n: (0,) * len(shape))
    act = pl.BlockSpec((CHUNK, SGU_WIDTH), lambda n: (n, 0))
    act_shape = jax.ShapeDtypeStruct((t, SGU_WIDTH), BF16)
    vec = jax.ShapeDtypeStruct((1, SGU_WIDTH), F32)
    return pl.pallas_call(
        body,
        out_shape=(act_shape, act_shape, jax.ShapeDtypeStruct((SGU_GROUPS, CHUNK, CHUNK), F32),
                   jax.ShapeDtypeStruct((CHUNK, LANES), F32), vec, vec),
        grid=(t // CHUNK,),
        in_specs=_sgu_specs() + [full((SGU_GROUPS, CHUNK, CHUNK)),
                                 pl.BlockSpec((CHUNK, SGU_WIDTH), lambda n: (n, 1))],
        out_specs=(act, act, full((SGU_GROUPS, CHUNK, CHUNK)), full((CHUNK, LANES)),
                   full((1, SGU_WIDTH)), full((1, SGU_WIDTH))),
        compiler_params=_cp(1), name="sgu_bwd",
    )(z, z, ln_g, ln_b, w, b_t, w_t, dcat)


def dz_assemble(dq, dkv, du, dv):
    t = dq.shape[0]
    nb = t // BLOCK

    def body(dq_ref, cur_ref, nxt_ref, du_ref, dv_ref, o_ref):
        n = pl.program_id(0)
        o_ref[:, ZQ:ZQ + ATTN_WIDTH] = dq_ref[...]
        o_ref[:, ZU:ZU + SGU_WIDTH] = du_ref[...]
        o_ref[:, ZV:ZV + SGU_WIDTH] = dv_ref[...]
        kv = cur_ref[BLOCK:, :] + jnp.where(n < nb - 1, nxt_ref[:BLOCK, :], 0.0)
        o_ref[:, ZK:ZK + 2 * KV_WIDTH] = kv.astype(BF16)

    act = pl.BlockSpec((BLOCK, ATTN_WIDTH), lambda n: (n, 0))
    return pl.pallas_call(
        body, out_shape=jax.ShapeDtypeStruct((t, IN_WIDTH), BF16), grid=(nb,),
        in_specs=[act,
                  pl.BlockSpec((None, 2 * BLOCK, 2 * KV_WIDTH), lambda n: (n, 0, 0)),
                  pl.BlockSpec((None, 2 * BLOCK, 2 * KV_WIDTH), lambda n: (jnp.minimum(n + 1, nb - 1), 0, 0)),
                  act, act],
        out_specs=pl.BlockSpec((BLOCK, IN_WIDTH), lambda n: (n, 0)),
        compiler_params=_cp(1), name="dz_assemble",
    )(dq, dkv, dkv, du, dv)


def _pool_count(i, tp, w):
    t_idx = i * tp + lax.broadcasted_iota(jnp.int32, (tp, 1), 0)
    return jnp.minimum(t_idx + 1, w).astype(F32)


def pool_fwd(h, pw, pscale):
    t, d = h.shape
    tp = _tile(t, 256)
    per = tp // POOL_HALO

    def body(h_ref, halo_ref, pw_ref, ps_ref, m_ref, pooled_ref):
        i = pl.program_id(0)
        cur = h_ref[...]
        ext = jnp.concatenate([jnp.where(i > 0, halo_ref[...], 0.0), cur], axis=0)
        ys, pooled = [], []
        for gi, w in enumerate(POOL_WINDOWS):
            gs = slice(gi * POOL_GROUP_DIM, (gi + 1) * POOL_GROUP_DIM)
            s = ext[:, gs]
            sh = 1
            while sh < w:
                s = s + pltpu.roll(s, sh, 0)
                sh *= 2
            pg = (s[POOL_HALO:, :] / _pool_count(i, tp, w) - cur[:, gs]).astype(BF16)
            pooled.append(pg)
            ys.append(_dot(pg, pw_ref[gi]))
        pooled_ref[...] = jnp.concatenate(pooled, axis=1)
        m_ref[...] = jnp.concatenate(ys, axis=1) * ps_ref[...]

    row = pl.BlockSpec((tp, d), lambda i: (i, 0))
    return pl.pallas_call(
        body,
        out_shape=(jax.ShapeDtypeStruct((t, d), F32), jax.ShapeDtypeStruct((t, d), BF16)),
        grid=(t // tp,),
        in_specs=[row, pl.BlockSpec((POOL_HALO, d), lambda i: (jnp.maximum(i * per - 1, 0), 0)),
                  pl.BlockSpec(pw.shape, lambda i: (0, 0, 0)), pl.BlockSpec((1, d), lambda i: (0, 0))],
        out_specs=(row, row), compiler_params=_cp(1), name="pool_fwd",
    )(h, h, pw, pscale)


def pool_bwd_proj(dm, pooled, pw, pscale):
    t, d = dm.shape
    tp = _tile(t, 256)

    def body(dm_ref, pooled_ref, pw_ref, ps_ref, dp_ref, dy_ref, dps_ref):
        dmv = dm_ref[...]
        dy = (dmv * ps_ref[...]).astype(BF16)
        dy_ref[...] = dy
        ys, dps = [], []
        for gi in range(len(POOL_WINDOWS)):
            gs = slice(gi * POOL_GROUP_DIM, (gi + 1) * POOL_GROUP_DIM)
            ys.append(_dot(pooled_ref[:, gs], pw_ref[gi]))
            dps.append(_dot(dy[:, gs], pw_ref[gi], "nt"))
        dp_ref[...] = jnp.concatenate(dps, axis=1)

        @pl.when(pl.program_id(0) == 0)
        def _():
            dps_ref[...] = jnp.zeros_like(dps_ref)

        dps_ref[...] += jnp.sum(dmv * jnp.concatenate(ys, axis=1), axis=0, keepdims=True)

    row = pl.BlockSpec((tp, d), lambda i: (i, 0))
    vec = pl.BlockSpec((1, d), lambda i: (0, 0))
    return pl.pallas_call(
        body,
        out_shape=(jax.ShapeDtypeStruct((t, d), F32), jax.ShapeDtypeStruct((t, d), BF16),
                   jax.ShapeDtypeStruct((1, d), F32)),
        grid=(t // tp,),
        in_specs=[row, row, pl.BlockSpec(pw.shape, lambda i: (0, 0, 0)), vec],
        out_specs=(row, row, vec), compiler_params=_cp(1), name="pool_bwd_proj",
    )(dm, pooled, pw, pscale)


def pool_bwd_window(dp):
    t, d = dp.shape
    tp = _tile(t, 256)
    per = tp // POOL_HALO
    last = t // POOL_HALO - 1
    nt = t // tp

    def body(dp_ref, halo_ref, dh_ref):
        i = pl.program_id(0)
        cur = dp_ref[...]
        halo = jnp.where(i < nt - 1, halo_ref[...], 0.0)
        outs = []
        for gi, w in enumerate(POOL_WINDOWS):
            gs = slice(gi * POOL_GROUP_DIM, (gi + 1) * POOL_GROUP_DIM)
            s = jnp.concatenate([cur[:, gs] / _pool_count(i, tp, w), halo[:, gs] / float(w)], axis=0)
            sh = 1
            while sh < w:
                s = s + pltpu.roll(s, tp + POOL_HALO - sh, 0)
                sh *= 2
            outs.append(s[:tp, :] - cur[:, gs])
        dh_ref[...] = jnp.concatenate(outs, axis=1)

    row = pl.BlockSpec((tp, d), lambda i: (i, 0))
    return pl.pallas_call(
        body, out_shape=jax.ShapeDtypeStruct((t, d), F32), grid=(nt,),
        in_specs=[row, pl.BlockSpec((POOL_HALO, d), lambda i: (jnp.minimum((i + 1) * per, last), 0))],
        out_specs=row, compiler_params=_cp(1), name="pool_bwd_window",
    )(dp, dp)


def pool_wgrad(pooled, dy):
    t, d = pooled.shape
    ng = d // POOL_GROUP_DIM
    tk = _tile(t, 512)
    blk = pl.BlockSpec((tk, POOL_GROUP_DIM), lambda g, k: (k, g))
    return _mm(
        "pool_wgrad", pooled, dy, dims="tn", grid=(ng, t // tk), a_spec=blk, b_spec=blk,
        o_spec=pl.BlockSpec((None, POOL_GROUP_DIM, POOL_GROUP_DIM), lambda g, k: (g, 0, 0)),
        out_shape=jax.ShapeDtypeStruct((ng, POOL_GROUP_DIM, POOL_GROUP_DIM), F32),
        acc_shape=(POOL_GROUP_DIM, POOL_GROUP_DIM), nk=t // tk,
    )


def _xattn_probs(qh, kh):
    s = _dot(qh, kh, "nt") * (X_HEAD_DIM ** -0.5)
    p = jnp.exp(s - jnp.max(s, axis=1, keepdims=True))
    return p * (1.0 / jnp.sum(p, axis=1, keepdims=True))


def xattn_fwd(name, q, k, v):
    t, xw = q.shape
    tm = _tile(t, 512)

    def body(q_ref, k_ref, v_ref, o_ref):
        outs = []
        for h in range(X_HEADS):
            hs = slice(h * X_HEAD_DIM, (h + 1) * X_HEAD_DIM)
            p = _xattn_probs(q_ref[:, hs], k_ref[:, hs])
            outs.append(_dot(p.astype(BF16), v_ref[:, hs]))
        o_ref[...] = jnp.concatenate(outs, axis=1).astype(BF16)

    row = pl.BlockSpec((tm, xw), lambda i: (i, 0))
    kv = pl.BlockSpec(k.shape, lambda i: (0, 0))
    return pl.pallas_call(
        body, out_shape=jax.ShapeDtypeStruct((t, xw), BF16), grid=(t // tm,),
        in_specs=[row, kv, kv], out_specs=row, compiler_params=_cp(1), name=name,
    )(q, k, v)


def xattn_bwd(name, q, k, v, do):
    t, xw = q.shape
    tm = _tile(t, 512)

    def body(q_ref, k_ref, v_ref, do_ref, dq_ref, dk_ref, dv_ref):
        @pl.when(pl.program_id(0) == 0)
        def _():
            dk_ref[...] = jnp.zeros_like(dk_ref)
            dv_ref[...] = jnp.zeros_like(dv_ref)

        dqs = []
        for h in range(X_HEADS):
            hs = slice(h * X_HEAD_DIM, (h + 1) * X_HEAD_DIM)
            qh, kh, vh, doh = q_ref[:, hs], k_ref[:, hs], v_ref[:, hs], do_ref[:, hs]
            p = _xattn_probs(qh, kh)
            dv_ref[:, hs] += _dot(p.astype(BF16), doh, "tn")
            dp = _dot(doh, vh, "nt")
            ds = (p * (dp - jnp.sum(p * dp, axis=1, keepdims=True)) * (X_HEAD_DIM ** -0.5)).astype(BF16)
            dqs.append(_dot(ds, kh))
            dk_ref[:, hs] += _dot(ds, qh, "tn")
        dq_ref[...] = jnp.concatenate(dqs, axis=1).astype(BF16)

    row = pl.BlockSpec((tm, xw), lambda i: (i, 0))
    kv = pl.BlockSpec(k.shape, lambda i: (0, 0))
    kv_shape = jax.ShapeDtypeStruct(k.shape, F32)
    return pl.pallas_call(
        body, out_shape=(jax.ShapeDtypeStruct((t, xw), BF16), kv_shape, kv_shape), grid=(t // tm,),
        in_specs=[row, kv, kv, row], out_specs=(row, kv, kv), compiler_params=_cp(1), name=name,
    )(q, k, v, do)


def xattn_out(name, o, wo):
    t, xw = o.shape
    ns, _, dn = wo.shape
    tm = _tile(t, 512)
    return _mm(
        name, o, wo, dims="nn", grid=(t // tm, ns),
        a_spec=pl.BlockSpec((tm, xw), lambda i, j: (i, 0)),
        b_spec=pl.BlockSpec((None, xw, dn), lambda i, j: (j, 0, 0)),
        o_spec=pl.BlockSpec((tm, dn), lambda i, j: (i, j)),
        out_shape=jax.ShapeDtypeStruct((t, ns * dn), F32),
    )


def xattn_out_bwd(name, dm, wo):
    t = dm.shape[0]
    ns, xw, dn = wo.shape
    tm = _tile(t, 512)
    return _mm(
        name, dm, wo, dims="nt", grid=(t // tm, ns),
        a_spec=pl.BlockSpec((tm, dn), lambda i, j: (i, j)),
        b_spec=pl.BlockSpec((None, xw, dn), lambda i, j: (j, 0, 0)),
        o_spec=pl.BlockSpec((tm, xw), lambda i, j: (i, 0)),
        out_shape=jax.ShapeDtypeStruct((t, xw), BF16), acc_shape=(tm, xw), nk=ns,
    )


def xattn_out_wgrad(name, o, dm, ns):
    t, xw = o.shape
    dn = dm.shape[1] // ns
    tk = _tile(t, 512)
    return _mm(
        name, o, dm, dims="tn", grid=(ns, t // tk),
        a_spec=pl.BlockSpec((tk, xw), lambda j, k: (k, 0)),
        b_spec=pl.BlockSpec((tk, dn), lambda j, k: (k, j)),
        o_spec=pl.BlockSpec((None, xw, dn), lambda j, k: (j, 0, 0)),
        out_shape=jax.ShapeDtypeStruct((ns, xw, dn), BF16), acc_shape=(xw, dn), nk=t // tk,
    )


def kernel(x, mem, norms, mem_norm, ffn1_wg, ffn1_wu, ffn1_wd, ffn2_wg, ffn2_wu, ffn2_wd, x_wq, x_wk, x_wv, x_wo, mix_w_in, mix_w_out, attn_sinks, sgu_ln_g, sgu_ln_b, sgu_w, sgu_b, pool_w, pool_scale, loss_target, m_norms, m_mem_norm, m_ffn1_wg, m_ffn1_wu, m_ffn1_wd, m_ffn2_wg, m_ffn2_wu, m_ffn2_wd, m_x_wq, m_x_wk, m_x_wv, m_x_wo, m_mix_w_in, m_mix_w_out, m_attn_sinks, m_sgu_ln_g, m_sgu_ln_b, m_sgu_w, m_sgu_b, m_pool_w, m_pool_scale, v_norms, v_mem_norm, v_ffn1_wg, v_ffn1_wu, v_ffn1_wd, v_ffn2_wg, v_ffn2_wu, v_ffn2_wd, v_x_wq, v_x_wk, v_x_wv, v_x_wo, v_mix_w_in, v_mix_w_out, v_attn_sinks, v_sgu_ln_g, v_sgu_ln_b, v_sgu_w, v_sgu_b, v_pool_w, v_pool_scale):
    params = dict(norms=norms, mem_norm=mem_norm, ffn1_wg=ffn1_wg, ffn1_wu=ffn1_wu, ffn1_wd=ffn1_wd,
                  ffn2_wg=ffn2_wg, ffn2_wu=ffn2_wu, ffn2_wd=ffn2_wd, x_wq=x_wq, x_wk=x_wk, x_wv=x_wv, x_wo=x_wo,
                  mix_w_in=mix_w_in, mix_w_out=mix_w_out, attn_sinks=attn_sinks, sgu_ln_g=sgu_ln_g,
                  sgu_ln_b=sgu_ln_b, sgu_w=sgu_w, sgu_b=sgu_b, pool_w=pool_w, pool_scale=pool_scale)
    mom1 = dict(norms=m_norms, mem_norm=m_mem_norm, ffn1_wg=m_ffn1_wg, ffn1_wu=m_ffn1_wu, ffn1_wd=m_ffn1_wd,
                ffn2_wg=m_ffn2_wg, ffn2_wu=m_ffn2_wu, ffn2_wd=m_ffn2_wd, x_wq=m_x_wq, x_wk=m_x_wk, x_wv=m_x_wv,
                x_wo=m_x_wo, mix_w_in=m_mix_w_in, mix_w_out=m_mix_w_out, attn_sinks=m_attn_sinks,
                sgu_ln_g=m_sgu_ln_g, sgu_ln_b=m_sgu_ln_b, sgu_w=m_sgu_w, sgu_b=m_sgu_b, pool_w=m_pool_w,
                pool_scale=m_pool_scale)
    mom2 = dict(norms=v_norms, mem_norm=v_mem_norm, ffn1_wg=v_ffn1_wg, ffn1_wu=v_ffn1_wu, ffn1_wd=v_ffn1_wd,
                ffn2_wg=v_ffn2_wg, ffn2_wu=v_ffn2_wu, ffn2_wd=v_ffn2_wd, x_wq=v_x_wq, x_wk=v_x_wk, x_wv=v_x_wv,
                x_wo=v_x_wo, mix_w_in=v_mix_w_in, mix_w_out=v_mix_w_out, attn_sinks=v_attn_sinks,
                sgu_ln_g=v_sgu_ln_g, sgu_ln_b=v_sgu_ln_b, sgu_w=v_sgu_w, sgu_b=v_sgu_b, pool_w=v_pool_w,
                pool_scale=v_pool_scale)
    order = list(params)

    xs, memb, target = x[0], mem[0], loss_target[0]
    t, d = xs.shape
    depth = norms.shape[0]
    dsh = d // NDEV

    bf = lambda a: a.astype(BF16)
    wts = {}

    def gather_job(keys):
        return _Gather([bf(params[name][l]) for name, l in keys]), keys

    def land(job_keys):
        job, keys = job_keys
        for key, a in zip(keys, job.result):
            wts[key] = a

    ffn_keys = lambda tag, l: [(f"{tag}_wg", l), (f"{tag}_wu", l), (f"{tag}_wd", l)]
    x_keys = lambda l: [("x_wq", l), ("x_wk", l), ("x_wv", l), ("x_wo", l)]
    small_shapes = [norms.shape, pool_scale.shape, pool_w.shape]
    head = gather_job(ffn_keys("ffn1", 0))
    head_small = _Gather([_pack([norms, pool_scale, pool_w])])
    _run_exchange("gather_head", _Multi([head_small, head[0]]))
    land(head)
    norms_sh, pscale_sh, pw_sh = _unpack(head_small.result[0], small_shapes, (NDEV,))
    norms_full = norms_sh.transpose(1, 2, 0, 3).reshape(depth, norms.shape[1], d)
    pscale_full = pscale_sh.transpose(1, 0, 2).reshape(1, d)
    pw_full = pw_sh[:, 0].transpose(1, 0, 2, 3).reshape(len(POOL_WINDOWS), POOL_GROUP_DIM, POOL_GROUP_DIM).astype(BF16)
    fwd_jobs = {
        ("ffn1", 0): gather_job([("mix_w_in", 0), ("mix_w_out", 0)] + x_keys(0) + ffn_keys("ffn2", 0)[:2]),
        ("attn", 0): gather_job(ffn_keys("ffn2", 0)[2:]),
        ("ffn2", 0): gather_job(ffn_keys("ffn1", 1)),
        ("ffn1", 1): gather_job(x_keys(1) + ffn_keys("ffn2", 1)),
    }

    tab = rope_table(t)
    sgu_w0 = sgu_w[0]
    sgu_wt0 = sgu_w0.transpose(0, 2, 1)
    sgu_bt0 = sgu_b[0].T
    gain = lambda l, i: norms_full[l, i][None, :]

    saved = []
    xc = xs
    for l in range(depth):
        s = {}

        def ffn_forward(tag, xc, gi, l=l, s=s):
            h, s[tag + "_ht"] = norm_fwd(f"norm_{tag}_{l}", xc, gain(l, gi), BF16, with_t=True)
            job = fwd_jobs.get((tag, l))
            s[tag + "_a"], s[tag + "_b"], s[tag + "_hidt"], s[tag + "_m"] = ffn_fwd(
                f"{tag}_fwd_{l}", h, wts[tag + "_wg", l], wts[tag + "_wu", l], wts[tag + "_wd", l],
                comm=job and job[0])
            if job:
                land(job)
            return resid_norm_fwd(f"resid_{tag}_{l}", xc, s[tag + "_m"], gain(l, gi + 1), 0.5)

        s["x0"] = xc
        xc = ffn_forward("ffn1", xc, 0)

        s["x1"] = xc
        if l % 2 == 0:
            w_in = wts["mix_w_in", l].transpose(1, 0, 2).reshape(d, IN_WIDTH)
            o_k, o_u = ATTN_WIDTH, ATTN_WIDTH + 2 * KV_WIDTH
            w_in = jnp.concatenate([w_in[:, :o_k], w_in[:, o_u:], w_in[:, o_k:o_u]], axis=1)
            w_out = wts["mix_w_out", l].reshape(d, d)
            h2, s["h2t"] = norm_fwd(f"norm_mix_{l}", xc, gain(l, 2), BF16, with_t=True)
            s["z"] = mm_nn("mix_in", h2, w_in, F32, tn=IN_WIDTH // 2)
            job = fwd_jobs[("attn", l)]
            attn = attn_fwd(s["z"], tab, attn_sinks, comm=job[0])
            land(job)
            gate = sgu_fwd(s["z"], sgu_ln_g, sgu_ln_b, sgu_w0, sgu_bt0)
            s["cat"] = jnp.concatenate([attn, gate], axis=1)
            s["m2"] = mm_nn("mix_out", s["cat"], w_out, F32)
        else:
            h2 = norm_fwd(f"norm_mix_{l}", xc, gain(l, 2), F32)
            s["m2"], s["pooled"] = pool_fwd(h2, pw_full, pscale_full)
        xc = resid_norm_fwd(f"resid_mix_{l}", xc, s["m2"], gain(l, 3), 1.0)

        s["x2"] = xc
        wq, wk, wv = (wts[k, l].reshape(d, -1) for k in ("x_wq", "x_wk", "x_wv"))
        s["wq"], s["wkv"] = wq, jnp.concatenate([wk, wv], axis=1)
        h3, s["h3t"] = norm_fwd(f"norm_x_{l}", xc, gain(l, 4), BF16, with_t=True)
        s["mem_n"] = norm_fwd(f"norm_mem_{l}", memb, mem_norm[l][None, :], BF16)
        s["q"] = mm_nn(f"x_q_{l}", h3, wq, BF16)
        s["k"] = mm_nn(f"x_k_{l}", s["mem_n"], wk, BF16)
        s["v"] = mm_nn(f"x_v_{l}", s["mem_n"], wv, BF16)
        s["o"] = xattn_fwd(f"xattn_fwd_{l}", s["q"], s["k"], s["v"])
        s["m3"] = xattn_out(f"x_o_{l}", s["o"], wts["x_wo", l])
        xc = resid_norm_fwd(f"resid_x_{l}", xc, s["m3"], gain(l, 5), 1.0)

        s["x3"] = xc
        xc = ffn_forward("ffn2", xc, 6)
        saved.append(s)

    dx, loss11 = loss_grad(xc, target)
    loss = lax.psum(loss11[0, 0], ("x", "y", "c"))

    swaps = []
    pending = []
    recv = {}

    def emit_units(name, l, arr, among_chips):
        piece_mib = math.prod(arr.shape[1:]) * arr.dtype.itemsize / MIB
        parts = 2 if piece_mib > 1.5 else 1
        rows = arr.shape[1] // parts
        cost = (LINK_US_PER_MIB_CHIPS if among_chips else LINK_US_PER_MIB_ALL) * piece_mib / parts
        for part in range(parts):
            pending.append(((name, l, part), among_chips, (arr, part * rows, rows), cost))

    def emit(name, l, arr, two_level=False):
        if two_level:
            swaps.append((name, l, arr.reshape((NCHIP, 2) + arr.shape[1:])))
        else:
            emit_units(name, l, arr, False)

    def hosted(budget_us, fn, *args, extra=(), force=True, **kw):
        jobs = list(extra)
        swapped = swaps[:]
        del swaps[:]
        used = PAIR_SWAP_US * len(swapped)
        if swapped:
            jobs.append(_PairSwap([g for _, _, g in swapped]))
        items, kept = [], []
        for it in pending:
            if (force and not items and not swapped) or used + it[3] <= budget_us:
                items.append(it)
                used += it[3]
            else:
                kept.append(it)
        pending[:] = kept
        groups = [[it for it in items if it[1] == flag] for flag in (False, True)]
        unit_jobs = [cls([it[2] for it in grp]) if grp else None
                     for cls, grp in zip((_Scatter, _ChipScatter), groups)]
        jobs += [j for j in unit_jobs if j is not None]
        res = fn(*args, comm=_Multi(jobs) if jobs else None, **kw)
        for job, grp in zip(unit_jobs, groups):
            for it, a in zip(grp, job.result if job else ()):
                recv[it[0]] = a
        if swapped:
            halves = jobs[len(extra)].result
            for i, (name, l, _) in enumerate(swapped):
                emit_units(name, l, pair_sum(f"pairsum_{name}_{l}", halves[2 * i], halves[2 * i + 1]), True)
        return res

    grads = {k: [None] * depth for k in ("norms", "mem_norm")}
    small_jobs = []
    for l in reversed(range(depth)):
        s = saved[l]
        dg = [None] * 8

        def ffn_block(tag, dx, x_in, gi, extra=(), l=l, s=s, dg=dg):
            dm, dg[gi + 1] = norm_bwd(f"{tag}_post_bwd_{l}", s[tag + "_m"], gain(l, gi + 1), dx, 0.5, None, BF16)
            da, db, dh = hosted(HOST_US_FFN_BWD, ffn_bwd, f"{tag}_bwd_{l}", dm, s[tag + "_a"], s[tag + "_b"],
                                wts[tag + "_wg", l], wts[tag + "_wu", l], wts[tag + "_wd", l], extra=extra)
            emit(tag + "_wg", l, hosted(HOST_US_WGRAD, ffn_wgrad_in, f"{tag}_dwg_{l}", s[tag + "_ht"], da), True)
            emit(tag + "_wu", l, hosted(HOST_US_WGRAD, ffn_wgrad_in, f"{tag}_dwu_{l}", s[tag + "_ht"], db), True)
            emit(tag + "_wd", l, hosted(HOST_US_WGRAD, ffn_wgrad_out, f"{tag}_dwd_{l}", s[tag + "_hidt"], dm), True)
            dx, dg[gi] = hosted(HOST_US_SMALL, norm_bwd, f"{tag}_pre_bwd_{l}", x_in, gain(l, gi), dh, 1.0, dx, F32,
                                force=False)
            return dx

        dx = ffn_block("ffn2", dx, s["x3"], 6)

        dm, dg[5] = norm_bwd(f"x_post_bwd_{l}", s["m3"], gain(l, 5), dx, 1.0, None, BF16)
        do = xattn_out_bwd(f"x_do_{l}", dm, wts["x_wo", l])
        emit("x_wo", l, xattn_out_wgrad(f"x_dwo_{l}", s["o"], dm, NDEV))
        dq, dk, dv = xattn_bwd(f"xattn_bwd_{l}", s["q"], s["k"], s["v"], do)
        dkb, dvb = dk.astype(BF16), dv.astype(BF16)
        emit("x_wq", l, mm_kred(f"x_dwq_{l}", s["h3t"], dq, BF16).reshape(NDEV, dsh, -1))
        emit("x_wk", l, mm_tn(f"x_dwk_{l}", s["mem_n"], dkb, BF16).reshape(NDEV, dsh, -1))
        emit("x_wv", l, mm_tn(f"x_dwv_{l}", s["mem_n"], dvb, BF16).reshape(NDEV, dsh, -1))
        dh = mm_nt(f"x_dh_{l}", dq, s["wq"], F32)
        dmem_n = mm_nt(f"x_dmem_{l}", jnp.concatenate([dkb, dvb], axis=1), s["wkv"], F32)
        _, grads["mem_norm"][l] = norm_bwd(f"mem_norm_bwd_{l}", memb, mem_norm[l][None, :], dmem_n, 1.0, None, F32)
        dx, dg[4] = norm_bwd(f"x_pre_bwd_{l}", s["x2"], gain(l, 4), dh, 1.0, dx, F32)

        if l % 2 == 0:
            dm, dg[3] = norm_bwd(f"mix_post_bwd_{l}", s["m2"], gain(l, 3), dx, 1.0, None, BF16)
            dcat = mm_nt("mix_dcat", dm, w_out, BF16)
            emit("mix_w_out", l, mm_tn("mix_dwout", s["cat"], dm, BF16, tno=d // 2).reshape(NDEV, dsh, d))
            dq_a, dkv_a, dsink = hosted(HOST_US_ATTN_BWD, attn_bwd, s["z"], tab, attn_sinks, dcat, force=False)
            du_s, dv_s, g_sgu_w, g_sgu_bt, g_ln_g, g_ln_b = sgu_bwd(
                s["z"], sgu_ln_g, sgu_ln_b, sgu_w0, sgu_wt0, sgu_bt0, dcat)
            dz = dz_assemble(dq_a, dkv_a, du_s, dv_s)
            dh = mm_nt("mix_dh", dz, w_in, F32, tn=d // 2)
            g_win = hosted(2 * HOST_US_SMALL, mm_kred, "mix_dwin", s["h2t"], dz, BF16, tno=IN_WIDTH // 2, force=False)
            g_win = jnp.concatenate([g_win[:, :ATTN_WIDTH], g_win[:, ZK:], g_win[:, ZU:ZK]], axis=1)
            emit("mix_w_in", l, g_win.reshape(d, NDEV, -1).transpose(1, 0, 2))
        else:
            dm, dg[3] = norm_bwd(f"mix_post_bwd_{l}", s["m2"], gain(l, 3), dx, 1.0, None, F32)
            dp, dy, g_pscale = pool_bwd_proj(dm, s["pooled"], pw_full, pscale_full)
            g_pw = pool_wgrad(s["pooled"], dy)
            emit("pool_w", 0, g_pw.reshape(len(POOL_WINDOWS), NDEV, -1, POOL_GROUP_DIM).transpose(1, 0, 2, 3)
                 .reshape(NDEV, -1, POOL_GROUP_DIM))
            dh = pool_bwd_window(dp)
        dx, dg[2] = norm_bwd(f"mix_pre_bwd_{l}", s["x1"], gain(l, 2), dh, 1.0, dx, F32)

        if l == 0:
            replicated = ["mem_norm", "attn_sinks", "sgu_ln_g", "sgu_ln_b", "sgu_w", "sgu_b"]
            rep_grads = [jnp.concatenate(grads["mem_norm"], axis=0), dsink[:, :N_Q_HEADS], g_ln_g, g_ln_b,
                         g_sgu_w[None], g_sgu_bt[:, :SGU_GROUPS].T[None]]
            small_jobs.append(_Gather([_pack(rep_grads)]))
        dx = ffn_block("ffn1", dx, s["x0"], 0, extra=small_jobs if l == 0 else ())
        grads["norms"][l] = jnp.concatenate(dg, axis=0)

    g_norms = jnp.stack(grads["norms"], axis=0).reshape(depth, norms.shape[1], NDEV, dsh).transpose(2, 0, 1, 3)
    g_pscale_p = g_pscale.reshape(1, NDEV, dsh).transpose(1, 0, 2)
    sharded_small = ["norms", "pool_scale"]
    pieces_small = jnp.stack([_pack([g_norms[j], g_pscale_p[j]]) for j in range(NDEV)], axis=0)
    small_scatter = _Scatter([(pieces_small, 0, pieces_small.shape[1])])

    out = {}

    def update(k, extra=()):
        waiting = [it[0] for it in pending if it[0][0] == k] + [it[:2] for it in swaps if it[0] == k]
        assert not waiting, waiting
        shp = params[k].shape
        c = shp[-1]
        view = lambda a: a.reshape(-1, c)
        pieces = [recv[key] for key in sorted(key for key in recv if key[0] == k)]
        res = hosted(HOST_US_ADAMW, adamw, f"adamw_{k}", view(params[k]), view(mom1[k]), view(mom2[k]), pieces,
                     extra=extra)
        out[k] = [a.reshape(shp) for a in res]

    def update_pack(names, pieces):
        shapes = [params[k].shape for k in names]
        res = adamw("adamw_" + names[0] + "_pack", _pack([params[k] for k in names]),
                    _pack([mom1[k] for k in names]), _pack([mom2[k] for k in names]), [pieces])
        for which in range(4):
            for k, a in zip(names, _unpack(res[which], shapes)):
                out.setdefault(k, [None] * 4)[which] = a

    last = ("ffn1_wg", "ffn1_wu", "ffn1_wd")
    early = [k for k in order if k not in last and k not in sharded_small and k not in replicated]
    for i, k in enumerate(early):
        update(k, extra=[small_scatter] if i == 0 else ())
    flushes = 0
    while pending or swaps:
        hosted(float("inf"), lambda comm: _run_exchange(f"scatter_tail_{flushes}", comm))
        flushes += 1
    update_pack(replicated, small_jobs[0].result[0])
    update_pack(sharded_small, small_scatter.result[0])
    for k in last:
        update(k)

    outputs = [loss, dx[None]]
    for which in range(4):
        outputs += [out[k][which] for k in order]
    return tuple(outputs)
```

```python
import math

import jax
import jax.numpy as jnp
from jax import lax
from jax.experimental import pallas as pl
from jax.experimental.pallas import tpu as pltpu

F32 = jnp.float32
BF16 = jnp.bfloat16
NDEV = 8
MIB = 1024 * 1024
LANES = 128

RMS_EPS = 1e-6
HEAD_DIM = 64
N_Q_HEADS = 16
Q_PER_KV = 8
ATTN_WIDTH = 1024
KV_WIDTH = 128
BLOCK = 128
ROPE_DIM = 16
ROPE_THETA = 500000.0
SGU_GROUPS = 8
SGU_WIDTH = 1024
CHUNK = 128
POOL_WINDOWS = (2, 4, 8, 16)
POOL_GROUP_DIM = 512
POOL_HALO = 16
X_HEADS = 4
X_HEAD_DIM = 128
ZQ, ZU, ZV, ZK = 0, 1024, 2048, 3072
IN_WIDTH = 3328

ADAM_LR = 0.001
ADAM_B1 = 0.9
ADAM_B2 = 0.999
ADAM_EPS = 1e-08
ADAM_WD = 0.01
ADAM_STEP = 10

FFN_FWD_ROWS = 512
FFN_BWD_ROWS = 512

LINK_US_PER_MIB_ALL = 91.0
LINK_US_PER_MIB_CHIPS = 45.0
PAIR_SWAP_US = 20.0
HOST_US_FFN_BWD = 420.0
HOST_US_WGRAD = 100.0
HOST_US_ATTN_BWD = 240.0
HOST_US_SMALL = 40.0
HOST_US_ADAMW = 40.0

_DN = {
    "nn": (((1,), (0,)), ((), ())),
    "nt": (((1,), (1,)), ((), ())),
    "tn": (((0,), (0,)), ((), ())),
}


def _cp(naxes, vmem_mib=48):
    return pltpu.CompilerParams(dimension_semantics=("arbitrary",) * naxes, vmem_limit_bytes=vmem_mib * MIB)


def _tile(n, pref):
    t = min(n, pref)
    while n % t:
        t //= 2
    return t


def _dot(a, b, dims="nn"):
    return lax.dot_general(a, b, _DN[dims], preferred_element_type=F32)


def _me():
    x, y, c = lax.axis_index("x"), lax.axis_index("y"), lax.axis_index("c")
    return x, y, c, 4 * x + 2 * y + c


def _peer(k):
    x, y, c, _ = _me()
    px = 1 - x if k & 4 else x
    py = 1 - y if k & 2 else y
    pc = 1 - c if k & 1 else c
    return (px, py, pc), 4 * px + 2 * py + pc


class _Exchange:
    def __init__(self, arrs, out_shape, remote_per=NDEV - 1, local_per=1):
        self.arrs = list(arrs)
        self.n = len(self.arrs)
        self.out_shape = list(out_shape)
        self.remote_per = remote_per
        self.scratch = [
            pltpu.SemaphoreType.DMA((self.n * remote_per,)),
            pltpu.SemaphoreType.DMA((self.n * remote_per,)),
            pltpu.SemaphoreType.DMA((self.n * local_per,)),
        ]
        self.result = None

    def _copy(self, src, dst, sems, i, k, dev):
        send, recv, _ = sems
        return pltpu.make_async_remote_copy(
            src_ref=src, dst_ref=dst, send_sem=send.at[i * self.remote_per + k - 1],
            recv_sem=recv.at[i * self.remote_per + k - 1], device_id=dev, device_id_type=pl.DeviceIdType.MESH)


class _Gather(_Exchange):
    def __init__(self, arrs):
        super().__init__(arrs, [jax.ShapeDtypeStruct((NDEV,) + a.shape, a.dtype) for a in arrs])

    def start(self, ins, outs, sems):
        me = _me()[3]
        for i in range(self.n):
            pltpu.make_async_copy(ins[i], outs[i].at[me], sems[2].at[i]).start()
        for k in (1, 2, 4, 6):
            dev, _ = _peer(k)
            for i in range(self.n):
                self._copy(ins[i], outs[i].at[me], sems, i, k, dev).start()

    def finish(self, ins, outs, sems):
        me = _me()[3]
        sibling, _ = _peer(1)
        for k in (2, 4, 6):
            dev, slot = _peer(k)
            for i in range(self.n):
                block = outs[i].at[slot]
                self._copy(ins[i], block, sems, i, k, dev).wait_recv()
                self._copy(block, block, sems, i, k + 1, sibling).start()
        for k in (1, 3, 5, 7):
            dev, slot = _peer(k)
            for i in range(self.n):
                self._copy(ins[i], outs[i].at[slot], sems, i, k, dev).wait_recv()
        for k in range(1, NDEV):
            for i in range(self.n):
                self._copy(ins[i], outs[i].at[me], sems, i, k, sibling).wait_send()
        for i in range(self.n):
            pltpu.make_async_copy(ins[i], outs[i].at[me], sems[2].at[i]).wait()


class _Scatter(_Exchange):
    def __init__(self, units):
        self.rows = [(r0, n) for _, r0, n in units]
        super().__init__([a for a, _, _ in units],
                         [jax.ShapeDtypeStruct((NDEV, n) + a.shape[2:], a.dtype) for a, _, n in units])

    def _src(self, ins, i, slot):
        r0, n = self.rows[i]
        return ins[i].at[slot, pl.ds(r0, n)]

    def start(self, ins, outs, sems):
        me = _me()[3]
        for i in range(self.n):
            pltpu.make_async_copy(self._src(ins, i, me), outs[i].at[me], sems[2].at[i]).start()
        for k in range(1, NDEV):
            dev, slot = _peer(k)
            for i in range(self.n):
                self._copy(self._src(ins, i, slot), outs[i].at[me], sems, i, k, dev).start()

    def finish(self, ins, outs, sems):
        me = _me()[3]
        for k in range(1, NDEV):
            dev, slot = _peer(k)
            for i in range(self.n):
                cp = self._copy(self._src(ins, i, slot), outs[i].at[slot], sems, i, k, dev)
                cp.wait_send()
                cp.wait_recv()
        for i in range(self.n):
            pltpu.make_async_copy(self._src(ins, i, me), outs[i].at[me], sems[2].at[i]).wait()


NCHIP = NDEV // 2


class _PairSwap(_Exchange):
    def __init__(self, arrs):
        super().__init__(arrs, [jax.ShapeDtypeStruct((NCHIP,) + a.shape[2:], a.dtype) for a in arrs],
                         remote_per=NCHIP)

    def _copies(self, ins, outs, sems):
        c = _me()[2]
        sibling, _ = _peer(1)
        for i in range(self.n):
            for q in range(NCHIP):
                yield self._copy(ins[i].at[q, 1 - c], outs[i].at[q], sems, i, q + 1, sibling)

    def start(self, ins, outs, sems):
        for remote in self._copies(ins, outs, sems):
            remote.start()

    def finish(self, ins, outs, sems):
        for remote in self._copies(ins, outs, sems):
            remote.wait_send()
            remote.wait_recv()


class _ChipScatter(_Exchange):
    def __init__(self, units):
        self.rows = [(r0, n) for _, r0, n in units]
        super().__init__([a for a, _, _ in units],
                         [jax.ShapeDtypeStruct((NCHIP, n) + a.shape[2:], a.dtype) for a, _, n in units],
                         remote_per=NCHIP - 1)

    def _src(self, ins, i, chip):
        r0, n = self.rows[i]
        return ins[i].at[chip, pl.ds(r0, n)]

    @staticmethod
    def _chip(k):
        dev, slot = _peer(2 * k)
        return dev, slot // 2

    def start(self, ins, outs, sems):
        mine = _me()[3] // 2
        for i in range(self.n):
            pltpu.make_async_copy(self._src(ins, i, mine), outs[i].at[mine], sems[2].at[i]).start()
        for k in range(1, NCHIP):
            dev, chip = self._chip(k)
            for i in range(self.n):
                self._copy(self._src(ins, i, chip), outs[i].at[mine], sems, i, k, dev).start()

    def finish(self, ins, outs, sems):
        mine = _me()[3] // 2
        for k in range(1, NCHIP):
            dev, chip = self._chip(k)
            for i in range(self.n):
                cp = self._copy(self._src(ins, i, chip), outs[i].at[chip], sems, i, k, dev)
                cp.wait_send()
                cp.wait_recv()
        for i in range(self.n):
            pltpu.make_async_copy(self._src(ins, i, mine), outs[i].at[mine], sems[2].at[i]).wait()


class _Multi:
    def __init__(self, jobs):
        self.jobs = list(jobs)
        self.arrs = [a for j in self.jobs for a in j.arrs]
        self.out_shape = [s for j in self.jobs for s in j.out_shape]
        self.scratch = [s for j in self.jobs for s in j.scratch]
        self._result = None

    def _parts(self, ins, outs, sems):
        oi = oo = 0
        for idx, j in enumerate(self.jobs):
            ni, no = len(j.arrs), len(j.out_shape)
            yield j, ins[oi:oi + ni], outs[oo:oo + no], sems[3 * idx:3 * idx + 3]
            oi += ni
            oo += no

    def start(self, ins, outs, sems):
        for j, i, o, s in self._parts(ins, outs, sems):
            j.start(i, o, s)

    def finish(self, ins, outs, sems):
        for j, i, o, s in self._parts(ins, outs, sems):
            j.finish(i, o, s)

    @property
    def result(self):
        return self._result

    @result.setter
    def result(self, res):
        self._result = res
        o = 0
        for j in self.jobs:
            j.result = list(res[o:o + len(j.out_shape)])
            o += len(j.out_shape)


def _call(name, body, *, grid, in_specs, out_specs, out_shape, args, scratch=(), comm=None, vmem_mib=48):
    in_specs, out_specs, out_shape = list(in_specs), list(out_specs), list(out_shape)
    scratch, args = list(scratch), list(args)
    ni, no, ns = len(in_specs), len(out_specs), len(scratch)
    kernel_fn = body
    if comm is not None:
        ci, co = len(comm.arrs), len(comm.out_shape)
        hbm = pl.BlockSpec(memory_space=pltpu.HBM)

        def kernel_fn(*refs):
            refs = list(refs)
            ins, c_in, outs, c_out, scr, c_scr = (
                [refs.pop(0) for _ in range(cnt)] for cnt in (ni, ci, no, co, ns, len(comm.scratch)))
            if not grid:
                comm.start(c_in, c_out, c_scr)
                body(*ins, *outs, *scr)
                comm.finish(c_in, c_out, c_scr)
                return
            first = pl.program_id(0) == 0
            last = pl.program_id(0) == grid[0] - 1
            for ax in range(1, len(grid)):
                first = first & (pl.program_id(ax) == 0)
                last = last & (pl.program_id(ax) == grid[ax] - 1)

            @pl.when(first)
            def _():
                comm.start(c_in, c_out, c_scr)

            body(*ins, *outs, *scr)

            @pl.when(last)
            def _():
                comm.finish(c_in, c_out, c_scr)

        in_specs += [hbm] * ci
        out_specs += [hbm] * co
        out_shape += comm.out_shape
        scratch += comm.scratch
        args += comm.arrs
    params = _cp(len(grid), vmem_mib) if grid else None
    res = pl.pallas_call(
        kernel_fn, out_shape=out_shape, grid=grid, in_specs=in_specs, out_specs=out_specs,
        scratch_shapes=scratch, compiler_params=params, name=name,
    )(*args)
    if comm is not None:
        comm.result = list(res[no:])
    return tuple(res[:no])


def _run_exchange(name, comm):
    _call(name, lambda: None, grid=(), in_specs=[], out_specs=[], out_shape=[], args=[], comm=comm)
    return comm.result


def _pack(arrs, dtype=F32):
    flat = jnp.concatenate([a.astype(dtype).reshape(-1) for a in arrs])
    n = flat.shape[0]
    total = -(-n // (16 * LANES)) * (16 * LANES)
    return jnp.pad(flat, (0, total - n)).reshape(total // LANES, LANES)


def _unpack(packed, shapes, lead=()):
    flat = packed.reshape(lead + (-1,))
    out, off = [], 0
    for s in shapes:
        n = math.prod(s)
        out.append(flat[..., off:off + n].reshape(lead + tuple(s)))
        off += n
    return out


def _mm(name, a, b, *, dims, grid, a_spec, b_spec, o_spec, out_shape, acc_shape=None, nk=1, vmem_mib=48, comm=None):
    nax = len(grid)

    def body(a_ref, b_ref, o_ref, *scratch):
        p = _dot(a_ref[...], b_ref[...], dims)
        if nk == 1:
            o_ref[...] = p.astype(o_ref.dtype)
            return
        acc = scratch[0]
        k = pl.program_id(nax - 1)

        @pl.when(k == 0)
        def _():
            acc[...] = p

        @pl.when(k > 0)
        def _():
            acc[...] += p

        @pl.when(k == nk - 1)
        def _():
            o_ref[...] = acc[...].astype(o_ref.dtype)

    return _call(
        name, body, grid=grid, in_specs=[a_spec, b_spec], out_specs=[o_spec], out_shape=[out_shape], args=[a, b],
        scratch=[pltpu.VMEM(acc_shape, F32)] if nk > 1 else [], comm=comm, vmem_mib=vmem_mib,
    )[0]


def mm_nn(name, a, b, out_dtype, tn=None):
    m, k = a.shape
    n = b.shape[1]
    tm = _tile(m, 512)
    tn = n if tn is None else tn
    return _mm(
        name, a, b, dims="nn", grid=(n // tn, m // tm),
        a_spec=pl.BlockSpec((tm, k), lambda j, i: (i, 0)),
        b_spec=pl.BlockSpec((k, tn), lambda j, i: (0, j)),
        o_spec=pl.BlockSpec((tm, tn), lambda j, i: (i, j)),
        out_shape=jax.ShapeDtypeStruct((m, n), out_dtype),
    )


def mm_nt(name, a, b, out_dtype, tn=None):
    m, k = a.shape
    n = b.shape[0]
    tm = _tile(m, 512)
    tn = n if tn is None else tn
    return _mm(
        name, a, b, dims="nt", grid=(n // tn, m // tm),
        a_spec=pl.BlockSpec((tm, k), lambda j, i: (i, 0)),
        b_spec=pl.BlockSpec((tn, k), lambda j, i: (j, 0)),
        o_spec=pl.BlockSpec((tm, tn), lambda j, i: (i, j)),
        out_shape=jax.ShapeDtypeStruct((m, n), out_dtype),
    )


def mm_kred(name, a_t, b, out_dtype, tno=None, comm=None):
    m, k = a_t.shape
    n = b.shape[1]
    tk = _tile(k, 512)
    tmo = _tile(m, 1024)
    tno = n if tno is None else tno
    return _mm(
        name, a_t, b, dims="nn", grid=(m // tmo, n // tno, k // tk),
        a_spec=pl.BlockSpec((tmo, tk), lambda i, j, kk: (i, kk)),
        b_spec=pl.BlockSpec((tk, tno), lambda i, j, kk: (kk, j)),
        o_spec=pl.BlockSpec((tmo, tno), lambda i, j, kk: (i, j)),
        out_shape=jax.ShapeDtypeStruct((m, n), out_dtype),
        acc_shape=(tmo, tno), nk=k // tk, comm=comm,
    )


def mm_tn(name, a, b, out_dtype, tmo=None, tno=None):
    k, m = a.shape
    n = b.shape[1]
    tk = _tile(k, 512)
    tmo = _tile(m, 1024) if tmo is None else tmo
    tno = n if tno is None else tno
    return _mm(
        name, a, b, dims="tn", grid=(m // tmo, n // tno, k // tk),
        a_spec=pl.BlockSpec((tk, tmo), lambda i, j, kk: (kk, i)),
        b_spec=pl.BlockSpec((tk, tno), lambda i, j, kk: (kk, j)),
        o_spec=pl.BlockSpec((tmo, tno), lambda i, j, kk: (i, j)),
        out_shape=jax.ShapeDtypeStruct((m, n), out_dtype),
        acc_shape=(tmo, tno), nk=k // tk,
    )


def _rstd(x):
    return lax.rsqrt(jnp.mean(x * x, axis=-1, keepdims=True) + RMS_EPS)


def norm_fwd(name, x, g, out_dtype, with_t=False):
    t, d = x.shape
    tm = _tile(t, 256)

    def body(x_ref, g_ref, o_ref, *t_ref):
        xv = x_ref[...]
        h = xv * _rstd(xv) * g_ref[...]
        o_ref[...] = h.astype(o_ref.dtype)
        if with_t:
            t_ref[0][...] = h.T.astype(out_dtype)

    row = pl.BlockSpec((tm, d), lambda i: (i, 0))
    out_shape = [jax.ShapeDtypeStruct((t, d), out_dtype)]
    out_specs = [row]
    if with_t:
        out_shape.append(jax.ShapeDtypeStruct((d, t), out_dtype))
        out_specs.append(pl.BlockSpec((d, tm), lambda i: (0, i)))
    res = pl.pallas_call(
        body, out_shape=out_shape, grid=(t // tm,),
        in_specs=[row, pl.BlockSpec((1, d), lambda i: (0, 0))], out_specs=out_specs,
        compiler_params=_cp(1), name=name,
    )(x, g)
    return tuple(res) if with_t else res[0]


def resid_norm_fwd(name, x, m, g, scale):
    t, d = x.shape
    tm = _tile(t, 256)

    def body(x_ref, m_ref, g_ref, o_ref):
        mv = m_ref[...]
        o_ref[...] = x_ref[...] + scale * (mv * _rstd(mv) * g_ref[...])

    row = pl.BlockSpec((tm, d), lambda i: (i, 0))
    return pl.pallas_call(
        body, out_shape=jax.ShapeDtypeStruct((t, d), F32), grid=(t // tm,),
        in_specs=[row, row, pl.BlockSpec((1, d), lambda i: (0, 0))], out_specs=row,
        compiler_params=_cp(1), name=name,
    )(x, m, g)


def norm_bwd(name, u, g, dy, scale, resid, out_dtype, comm=None):
    t, d = u.shape
    tm = _tile(t, 256)
    has_resid = resid is not None

    def body(*refs):
        if has_resid:
            u_ref, g_ref, dy_ref, r_ref, du_ref, dg_ref = refs
        else:
            u_ref, g_ref, dy_ref, du_ref, dg_ref = refs
        uv = u_ref[...]
        dyv = dy_ref[...].astype(F32) * scale
        r = _rstd(uv)
        uh = uv * r

        @pl.when(pl.program_id(0) == 0)
        def _():
            dg_ref[...] = jnp.zeros_like(dg_ref)

        dg_ref[...] += jnp.sum(dyv * uh, axis=0, keepdims=True)
        dyg = dyv * g_ref[...]
        du = r * (dyg - uh * jnp.mean(dyg * uh, axis=-1, keepdims=True))
        if has_resid:
            du = du + r_ref[...]
        du_ref[...] = du.astype(du_ref.dtype)

    row = pl.BlockSpec((tm, d), lambda i: (i, 0))
    vec = pl.BlockSpec((1, d), lambda i: (0, 0))
    args = (u, g, dy) + ((resid,) if has_resid else ())
    return _call(
        name, body, grid=(t // tm,), in_specs=[row, vec, row] + ([row] if has_resid else []),
        out_specs=[row, vec], out_shape=[jax.ShapeDtypeStruct((t, d), out_dtype), jax.ShapeDtypeStruct((1, d), F32)],
        args=args, comm=comm,
    )


def loss_grad(y, target):
    t, d = y.shape
    tm = _tile(t, 256)
    nt = t // tm

    def body(y_ref, t_ref, dy_ref, loss_ref, acc):
        i = pl.program_id(0)
        e = y_ref[...] - t_ref[...]
        dy_ref[...] = e * (1.0 / d)

        @pl.when(i == 0)
        def _():
            acc[...] = jnp.zeros_like(acc)

        acc[...] += jnp.sum(e * e, axis=0, keepdims=True)

        @pl.when(i == nt - 1)
        def _():
            loss_ref[...] = (0.5 / d) * jnp.sum(acc[...], axis=1, keepdims=True)

    row = pl.BlockSpec((tm, d), lambda i: (i, 0))
    return pl.pallas_call(
        body,
        out_shape=(jax.ShapeDtypeStruct((t, d), F32), jax.ShapeDtypeStruct((1, 1), F32)),
        grid=(nt,), in_specs=[row, row], out_specs=(row, pl.BlockSpec((1, 1), lambda i: (0, 0))),
        scratch_shapes=[pltpu.VMEM((1, d), F32)], compiler_params=_cp(1), name="loss_grad",
    )(y, target)


def _row_tile(r, c):
    if r * c * 4 <= MIB:
        return r
    best = None
    for t in range(16, r, 16):
        if r % t == 0 and t * c * 4 <= MIB:
            best = t
    return r if best is None else best


def pair_sum(name, g4, theirs):
    nq, _, r, c = g4.shape
    tr = _tile(r, 1024)

    def body(g_ref, t_ref, o_ref):
        mine = g_ref[lax.axis_index("c")]
        o_ref[...] = (mine.astype(F32) + t_ref[...].astype(F32)).astype(o_ref.dtype)

    blk = pl.BlockSpec((None, tr, c), lambda q, i: (q, i, 0))
    return pl.pallas_call(
        body, out_shape=jax.ShapeDtypeStruct(theirs.shape, theirs.dtype), grid=(nq, r // tr),
        in_specs=[pl.BlockSpec((None, 2, tr, c), lambda q, i: (q, 0, i, 0)), blk], out_specs=blk,
        compiler_params=_cp(2), name=name,
    )(g4, theirs)


def adamw(name, w, m, v, pieces, comm=None):
    nl = len(pieces)
    npiece, r, c = pieces[0].shape
    tr = _row_tile(r, c)
    nr = r // tr
    bc1 = 1.0 - ADAM_B1 ** ADAM_STEP
    bc2 = 1.0 - ADAM_B2 ** ADAM_STEP

    def body(w_ref, m_ref, v_ref, *rest):
        p_refs, (g_ref, d_ref, nm_ref, nv_ref) = rest[:nl], rest[nl:]

        def update(p_ref):
            g = p_ref[0].astype(F32)
            for j in range(1, npiece):
                g = g + p_ref[j].astype(F32)
            m1 = ADAM_B1 * m_ref[...] + (1.0 - ADAM_B1) * g
            v1 = ADAM_B2 * v_ref[...] + (1.0 - ADAM_B2) * (g * g)
            m_hat = m1 / bc1
            v_hat = v1 / bc2
            g_ref[...] = g
            d_ref[...] = -ADAM_LR * (m_hat / (jnp.sqrt(v_hat) + ADAM_EPS) + ADAM_WD * w_ref[...])
            nm_ref[...] = m1
            nv_ref[...] = v1

        if nl == 1:
            update(p_refs[0])
        else:
            for ll in range(nl):
                pl.when(pl.program_id(0) == ll)(lambda ll=ll: update(p_refs[ll]))

    def piece_spec(ll):
        return pl.BlockSpec((npiece, tr, c), lambda l, i: (0, jnp.where(l == ll, i, jnp.where(l > ll, nr - 1, 0)), 0))

    row = pl.BlockSpec((tr, c), lambda l, i: (l * nr + i, 0))
    out = jax.ShapeDtypeStruct((nl * r, c), F32)
    return _call(
        name, body, grid=(nl, nr), in_specs=[row, row, row] + [piece_spec(ll) for ll in range(nl)],
        out_specs=[row] * 4, out_shape=[out] * 4, args=[w, m, v] + list(pieces), comm=comm,
    )


def _sigmoid(a):
    return 1.0 / (1.0 + jnp.exp(-a))


def ffn_fwd(name, h, wg, wu, wd, comm=None):
    t, d = h.shape
    ns, _, f = wg.shape
    tm = _tile(t, FFN_FWD_ROWS)

    def body(h_ref, wg_ref, wu_ref, wd_ref, a_ref, b_ref, hidt_ref, m_ref):
        j = pl.program_id(1)
        hv = h_ref[...]
        a = _dot(hv, wg_ref[...])
        b = _dot(hv, wu_ref[...])
        hid32 = (a * _sigmoid(a)) * b
        hid = hid32.astype(BF16)
        a_ref[...] = a.astype(BF16)
        b_ref[...] = b.astype(BF16)
        hidt_ref[...] = hid32.T.astype(BF16)
        p = _dot(hid, wd_ref[...])

        @pl.when(j == 0)
        def _():
            m_ref[...] = p

        @pl.when(j > 0)
        def _():
            m_ref[...] += p

    once = pl.Buffered(1)
    w_in = pl.BlockSpec((None, d, f), lambda i, j: (j, 0, 0))
    act = pl.BlockSpec((None, tm, f), lambda i, j: (j, i, 0))
    act_shape = jax.ShapeDtypeStruct((ns, t, f), BF16)
    return _call(
        name, body, grid=(t // tm, ns),
        in_specs=[pl.BlockSpec((tm, d), lambda i, j: (i, 0), pipeline_mode=once), w_in, w_in,
                  pl.BlockSpec((None, f, d), lambda i, j: (j, 0, 0))],
        out_specs=[act, act, pl.BlockSpec((None, f, tm), lambda i, j: (j, 0, i)),
                   pl.BlockSpec((tm, d), lambda i, j: (i, 0), pipeline_mode=once)],
        out_shape=[act_shape, act_shape, jax.ShapeDtypeStruct((ns, f, t), BF16), jax.ShapeDtypeStruct((t, d), F32)],
        args=[h, wg, wu, wd], comm=comm, vmem_mib=56,
    )


def ffn_bwd(name, dm, a, b, wg, wu, wd, comm=None):
    t, d = dm.shape
    ns, _, f = wg.shape
    tm = _tile(t, FFN_BWD_ROWS)

    def body(dm_ref, a_ref, b_ref, wg_ref, wu_ref, wd_ref, da_ref, db_ref, dh_ref):
        j = pl.program_id(1)
        dhid = _dot(dm_ref[...], wd_ref[...], "nt")
        av = a_ref[...].astype(F32)
        bv = b_ref[...].astype(F32)
        sig = _sigmoid(av)
        da = (dhid * bv * (sig * (1.0 + av * (1.0 - sig)))).astype(BF16)
        db = (dhid * (av * sig)).astype(BF16)
        da_ref[...] = da
        db_ref[...] = db
        p = _dot(da, wg_ref[...], "nt") + _dot(db, wu_ref[...], "nt")

        @pl.when(j == 0)
        def _():
            dh_ref[...] = p

        @pl.when(j > 0)
        def _():
            dh_ref[...] += p

    once = pl.Buffered(1)
    w_in = pl.BlockSpec((None, d, f), lambda i, j: (j, 0, 0))
    act = pl.BlockSpec((None, tm, f), lambda i, j: (j, i, 0))
    act_shape = jax.ShapeDtypeStruct((ns, t, f), BF16)
    return _call(
        name, body, grid=(t // tm, ns),
        in_specs=[pl.BlockSpec((tm, d), lambda i, j: (i, 0), pipeline_mode=once), act, act, w_in, w_in,
                  pl.BlockSpec((None, f, d), lambda i, j: (j, 0, 0))],
        out_specs=[act, act, pl.BlockSpec((tm, d), lambda i, j: (i, 0), pipeline_mode=once)],
        out_shape=[act_shape, act_shape, jax.ShapeDtypeStruct((t, d), F32)],
        args=[dm, a, b, wg, wu, wd], comm=comm, vmem_mib=56,
    )


def ffn_wgrad_in(name, h_t, dact, comm=None):
    d, t = h_t.shape
    ns, _, f = dact.shape
    tmo = _tile(d, 1024)
    return _mm(
        name, h_t, dact, dims="nn", grid=(ns, d // tmo),
        a_spec=pl.BlockSpec((tmo, t), lambda j, i: (i, 0)),
        b_spec=pl.BlockSpec((None, t, f), lambda j, i: (j, 0, 0)),
        o_spec=pl.BlockSpec((None, tmo, f), lambda j, i: (j, i, 0)),
        out_shape=jax.ShapeDtypeStruct((ns, d, f), BF16), vmem_mib=56, comm=comm,
    )


def ffn_wgrad_out(name, hid_t, dm, comm=None):
    ns, f, t = hid_t.shape
    d = dm.shape[1]
    return _mm(
        name, hid_t, dm, dims="nn", grid=(ns,),
        a_spec=pl.BlockSpec((None, f, t), lambda j: (j, 0, 0)),
        b_spec=pl.BlockSpec((t, d), lambda j: (0, 0), pipeline_mode=pl.Buffered(1)),
        o_spec=pl.BlockSpec((None, f, d), lambda j: (j, 0, 0)),
        out_shape=jax.ShapeDtypeStruct((ns, f, d), BF16), vmem_mib=56, comm=comm,
    )


def rope_table(t):
    half = ROPE_DIM // 2
    inv = ROPE_THETA ** (-jnp.arange(half, dtype=F32) * 2.0 / ROPE_DIM)
    ang = jnp.arange(t, dtype=F32)[:, None] * inv[None, :]
    cos, sin = jnp.cos(ang), jnp.sin(ang)
    rest = HEAD_DIM - ROPE_DIM
    c = jnp.concatenate([cos, cos, jnp.ones((t, rest), F32)], axis=1)
    sm = jnp.concatenate([-sin, jnp.zeros((t, half + rest), F32)], axis=1)
    sp = jnp.concatenate([jnp.zeros((t, half), F32), sin, jnp.zeros((t, rest), F32)], axis=1)
    return jnp.concatenate([jnp.tile(c, (1, 2)), jnp.tile(sm, (1, 2)), jnp.tile(sp, (1, 2))], axis=1)


def _rope(x, tab, sign):
    w = x.shape[1]
    rep = w // LANES
    c, sm, sp = tab[:, 0:LANES], tab[:, LANES:2 * LANES], tab[:, 2 * LANES:3 * LANES]
    if rep > 1:
        c, sm, sp = jnp.tile(c, (1, rep)), jnp.tile(sm, (1, rep)), jnp.tile(sp, (1, rep))
    half = ROPE_DIM // 2
    return x * c + sign * (pltpu.roll(x, w - half, 1) * sm + pltpu.roll(x, half, 1) * sp)


def _attn_specs():
    prev = lambda n: jnp.maximum(n - 1, 0)
    kblk, vblk = ZK // LANES, ZK // LANES + 1
    return [
        pl.BlockSpec((BLOCK, ATTN_WIDTH), lambda n: (n, 0)),
        pl.BlockSpec((BLOCK, KV_WIDTH), lambda n: (n, kblk)),
        pl.BlockSpec((BLOCK, KV_WIDTH), lambda n: (prev(n), kblk)),
        pl.BlockSpec((BLOCK, KV_WIDTH), lambda n: (n, vblk)),
        pl.BlockSpec((BLOCK, KV_WIDTH), lambda n: (prev(n), vblk)),
        pl.BlockSpec((BLOCK, 3 * LANES), lambda n: (n, 0)),
        pl.BlockSpec((BLOCK, 3 * LANES), lambda n: (prev(n), 0)),
        pl.BlockSpec(memory_space=pltpu.SMEM),
    ]


def _attn_prologue(n, zq_ref, zk_ref, zkp_ref, zv_ref, zvp_ref, tab_ref, tabp_ref):
    q = (_rope(zq_ref[...], tab_ref[...], 1.0) * (HEAD_DIM ** -0.5)).astype(BF16)
    kcat = jnp.concatenate(
        [_rope(zkp_ref[...], tabp_ref[...], 1.0), _rope(zk_ref[...], tab_ref[...], 1.0)], axis=0).astype(BF16)
    vcat = jnp.concatenate([zvp_ref[...], zv_ref[...]], axis=0).astype(BF16)
    qi = lax.broadcasted_iota(jnp.int32, (BLOCK, 2 * BLOCK), 0)
    kj = lax.broadcasted_iota(jnp.int32, (BLOCK, 2 * BLOCK), 1)
    valid = (kj <= qi + BLOCK) & (kj > qi) & ((n > 0) | (kj >= BLOCK))
    return q, kcat, vcat, valid


def _attn_probs(qh, kh, valid, sink):
    s = jnp.where(valid, _dot(qh, kh, "nt"), -1e30)
    mx = jnp.maximum(jnp.max(s, axis=1, keepdims=True), sink)
    p = jnp.exp(s - mx)
    p_sink = jnp.exp(sink - mx)
    inv = 1.0 / (jnp.sum(p, axis=1, keepdims=True) + p_sink)
    return p * inv, p_sink * inv


def attn_fwd(z, tab, sinks, comm=None):
    t = z.shape[0]

    def body(zq_ref, zk_ref, zkp_ref, zv_ref, zvp_ref, tab_ref, tabp_ref, sink_ref, o_ref):
        n = pl.program_id(0)
        q, kcat, vcat, valid = _attn_prologue(n, zq_ref, zk_ref, zkp_ref, zv_ref, zvp_ref, tab_ref, tabp_ref)
        outs = []
        for h in range(N_Q_HEADS):
            kv = slice((h // Q_PER_KV) * HEAD_DIM, (h // Q_PER_KV + 1) * HEAD_DIM)
            p, _ = _attn_probs(q[:, h * HEAD_DIM:(h + 1) * HEAD_DIM], kcat[:, kv], valid, sink_ref[0, h])
            outs.append(_dot(p.astype(BF16), vcat[:, kv]))
        o_ref[...] = jnp.concatenate(outs, axis=1).astype(BF16)

    return _call(
        "attn_fwd", body, grid=(t // BLOCK,), in_specs=_attn_specs(),
        out_specs=[pl.BlockSpec((BLOCK, ATTN_WIDTH), lambda n: (n, 0))],
        out_shape=[jax.ShapeDtypeStruct((t, ATTN_WIDTH), BF16)],
        args=[z, z, z, z, z, tab, tab, sinks], comm=comm,
    )[0]


def attn_bwd(z, tab, sinks, dcat, comm=None):
    t = z.shape[0]
    nb = t // BLOCK

    def body(zq_ref, zk_ref, zkp_ref, zv_ref, zvp_ref, tab_ref, tabp_ref, sink_ref, do_ref,
             dq_ref, dkv_ref, dsink_ref):
        n = pl.program_id(0)
        q, kcat, vcat, valid = _attn_prologue(n, zq_ref, zk_ref, zkp_ref, zv_ref, zvp_ref, tab_ref, tabp_ref)
        do = do_ref[...]
        lane = lax.broadcasted_iota(jnp.int32, (1, LANES), 1)
        dqs, dks, dvs = [], [], []
        dsink = jnp.zeros((1, LANES), F32)
        for hk in range(N_Q_HEADS // Q_PER_KV):
            kv = slice(hk * HEAD_DIM, (hk + 1) * HEAD_DIM)
            kh, vh = kcat[:, kv], vcat[:, kv]
            dk = jnp.zeros((2 * BLOCK, HEAD_DIM), F32)
            dv = jnp.zeros((2 * BLOCK, HEAD_DIM), F32)
            for g in range(Q_PER_KV):
                h = hk * Q_PER_KV + g
                hs = slice(h * HEAD_DIM, (h + 1) * HEAD_DIM)
                qh, doh = q[:, hs], do[:, hs]
                p, p_sink = _attn_probs(qh, kh, valid, sink_ref[0, h])
                dv = dv + _dot(p.astype(BF16), doh, "tn")
                dp = _dot(doh, vh, "nt")
                rd = jnp.sum(p * dp, axis=1, keepdims=True)
                ds = (p * (dp - rd) * (HEAD_DIM ** -0.5)).astype(BF16)
                dqs.append(_dot(ds, kh))
                dk = dk + _dot(ds, qh, "tn") * (HEAD_DIM ** 0.5)
                dsink = dsink + jnp.where(lane == h, -jnp.sum(p_sink * rd, axis=0, keepdims=True), 0.0)
            dks.append(dk)
            dvs.append(dv)
        dq_ref[...] = _rope(jnp.concatenate(dqs, axis=1), tab_ref[...], -1.0).astype(BF16)
        dkc = jnp.concatenate(dks, axis=1)
        dk_pre = jnp.concatenate(
            [_rope(dkc[:BLOCK], tabp_ref[...], -1.0), _rope(dkc[BLOCK:], tab_ref[...], -1.0)], axis=0)
        dkv_ref[...] = jnp.concatenate([dk_pre, jnp.concatenate(dvs, axis=1)], axis=1)

        @pl.when(n == 0)
        def _():
            dsink_ref[...] = jnp.zeros_like(dsink_ref)

        dsink_ref[...] += dsink

    return _call(
        "attn_bwd", body, grid=(nb,),
        in_specs=_attn_specs() + [pl.BlockSpec((BLOCK, ATTN_WIDTH), lambda n: (n, 0))],
        out_specs=[pl.BlockSpec((BLOCK, ATTN_WIDTH), lambda n: (n, 0)),
                   pl.BlockSpec((None, 2 * BLOCK, 2 * KV_WIDTH), lambda n: (n, 0, 0)),
                   pl.BlockSpec((1, LANES), lambda n: (0, 0))],
        out_shape=[jax.ShapeDtypeStruct((t, ATTN_WIDTH), BF16),
                   jax.ShapeDtypeStruct((nb, 2 * BLOCK, 2 * KV_WIDTH), F32),
                   jax.ShapeDtypeStruct((1, LANES), F32)],
        args=[z, z, z, z, z, tab, tab, sinks, dcat], comm=comm,
    )


def _gelu(x):
    k = math.sqrt(2.0 / math.pi)
    th = jnp.tanh(k * (x + 0.044715 * (x * x * x)))
    return 0.5 * x * (1.0 + th), th


def _gelu_grad(x, th):
    k = math.sqrt(2.0 / math.pi)
    return 0.5 * (1.0 + th) + 0.5 * x * (1.0 - th * th) * (k * (1.0 + 3.0 * 0.044715 * (x * x)))


def _sgu_core(zu_ref, zv_ref, lng_ref, lnb_ref, w_ref, bt_ref):
    up, vp = zu_ref[...], zv_ref[...]
    u, thu = _gelu(up)
    v, thv = _gelu(vp)
    mu = jnp.mean(v, axis=-1, keepdims=True)
    vc = v - mu
    rstd = lax.rsqrt(jnp.mean(vc * vc, axis=-1, keepdims=True) + RMS_EPS)
    xhat = vc * rstd
    vn = (xhat * lng_ref[...] + lnb_ref[...]).astype(BF16)
    row = lax.broadcasted_iota(jnp.int32, (CHUNK, CHUNK), 0)
    col = lax.broadcasted_iota(jnp.int32, (CHUNK, CHUNK), 1)
    mixed = []
    for g in range(SGU_GROUPS):
        wc = jnp.where(row >= col, w_ref[g], 0.0).astype(BF16)
        mixed.append(_dot(wc, vn[:, g * CHUNK:(g + 1) * CHUNK]) + bt_ref[:, g:g + 1])
    return up, vp, u, thu, thv, rstd, xhat, vn, jnp.concatenate(mixed, axis=1)


def _sgu_specs():
    full = lambda shape: pl.BlockSpec(shape, lambda n: (0,) * len(shape))
    return [
        pl.BlockSpec((CHUNK, SGU_WIDTH), lambda n: (n, ZU // SGU_WIDTH)),
        pl.BlockSpec((CHUNK, SGU_WIDTH), lambda n: (n, ZV // SGU_WIDTH)),
        full((1, SGU_WIDTH)), full((1, SGU_WIDTH)),
        full((SGU_GROUPS, CHUNK, CHUNK)), full((CHUNK, SGU_GROUPS)),
    ]


def sgu_fwd(z, ln_g, ln_b, w, b_t):
    t = z.shape[0]

    def body(zu_ref, zv_ref, lng_ref, lnb_ref, w_ref, bt_ref, o_ref):
        _, _, u, _, _, _, _, _, mixed = _sgu_core(zu_ref, zv_ref, lng_ref, lnb_ref, w_ref, bt_ref)
        o_ref[...] = (u * mixed).astype(BF16)

    return pl.pallas_call(
        body, out_shape=jax.ShapeDtypeStruct((t, SGU_WIDTH), BF16), grid=(t // CHUNK,),
        in_specs=_sgu_specs(), out_specs=pl.BlockSpec((CHUNK, SGU_WIDTH), lambda n: (n, 0)),
        compiler_params=_cp(1), name="sgu_fwd",
    )(z, z, ln_g, ln_b, w, b_t)


def sgu_bwd(z, ln_g, ln_b, w, w_t, b_t, dcat):
    t = z.shape[0]

    def body(zu_ref, zv_ref, lng_ref, lnb_ref, w_ref, bt_ref, wt_ref, dg_ref,
             du_ref, dv_ref, dw_ref, dbt_ref, dlng_ref, dlnb_ref):
        up, vp, u, thu, thv, rstd, xhat, vn, mixed = _sgu_core(zu_ref, zv_ref, lng_ref, lnb_ref, w_ref, bt_ref)
        dgate = dg_ref[...].astype(F32)
        du_ref[...] = (dgate * mixed * _gelu_grad(up, thu)).astype(BF16)
        dmixed = dgate * u
        row = lax.broadcasted_iota(jnp.int32, (CHUNK, CHUNK), 0)
        col = lax.broadcasted_iota(jnp.int32, (CHUNK, CHUNK), 1)

        @pl.when(pl.program_id(0) == 0)
        def _():
            dw_ref[...] = jnp.zeros_like(dw_ref)
            dbt_ref[...] = jnp.zeros_like(dbt_ref)
            dlng_ref[...] = jnp.zeros_like(dlng_ref)
            dlnb_ref[...] = jnp.zeros_like(dlnb_ref)

        dvn, dbt = [], jnp.zeros((CHUNK, LANES), F32)
        for g in range(SGU_GROUPS):
            gs = slice(g * CHUNK, (g + 1) * CHUNK)
            dmx = dmixed[:, gs]
            dmxb = dmx.astype(BF16)
            dbt = dbt + jnp.where(col == g, jnp.sum(dmx, axis=1, keepdims=True), 0.0)
            dw_ref[g] += jnp.where(row >= col, _dot(dmxb, vn[:, gs], "nt"), 0.0)
            wtc = jnp.where(col >= row, wt_ref[g], 0.0).astype(BF16)
            dvn.append(_dot(wtc, dmxb))
        dbt_ref[...] += dbt
        dvn = jnp.concatenate(dvn, axis=1)
        dlnb_ref[...] += jnp.sum(dvn, axis=0, keepdims=True)
        dlng_ref[...] += jnp.sum(dvn * xhat, axis=0, keepdims=True)
        dxh = dvn * lng_ref[...]
        dv = rstd * (dxh - jnp.mean(dxh, axis=-1, keepdims=True) - xhat * jnp.mean(dxh * xhat, axis=-1, keepdims=True))
        dv_ref[...] = (dv * _gelu_grad(vp, thv)).astype(BF16)

    full = lambda shape: pl.BlockSpec(shape, lambda n: (0,) * len(shape))
    act = pl.BlockSpec((CHUNK, SGU_WIDTH), lambda n: (n, 0))
    act_shape = jax.ShapeDtypeStruct((t, SGU_WIDTH), BF16)
    vec = jax.ShapeDtypeStruct((1, SGU_WIDTH), F32)
    return pl.pallas_call(
        body,
        out_shape=(act_shape, act_shape, jax.ShapeDtypeStruct((SGU_GROUPS, CHUNK, CHUNK), F32),
                   jax.ShapeDtypeStruct((CHUNK, LANES), F32), vec, vec),
        grid=(t // CHUNK,),
        in_specs=_sgu_specs() + [full((SGU_GROUPS, CHUNK, CHUNK)),
                                 pl.BlockSpec((CHUNK, SGU_WIDTH), lambda n: (n, 1))],
        out_specs=(act, act, full((SGU_GROUPS, CHUNK, CHUNK)), full((CHUNK, LANES)),
                   full((1, SGU_WIDTH)), full((1, SGU_WIDTH))),
        compiler_params=_cp(1), name="sgu_bwd",
    )(z, z, ln_g, ln_b, w, b_t, w_t, dcat)


def dz_assemble(dq, dkv, du, dv):
    t = dq.shape[0]
    nb = t // BLOCK

    def body(dq_ref, cur_ref, nxt_ref, du_ref, dv_ref, o_ref):
        n = pl.program_id(0)
        o_ref[:, ZQ:ZQ + ATTN_WIDTH] = dq_ref[...]
        o_ref[:, ZU:ZU + SGU_WIDTH] = du_ref[...]
        o_ref[:, ZV:ZV + SGU_WIDTH] = dv_ref[...]
        kv = cur_ref[BLOCK:, :] + jnp.where(n < nb - 1, nxt_ref[:BLOCK, :], 0.0)
        o_ref[:, ZK:ZK + 2 * KV_WIDTH] = kv.astype(BF16)

    act = pl.BlockSpec((BLOCK, ATTN_WIDTH), lambda n: (n, 0))
    return pl.pallas_call(
        body, out_shape=jax.ShapeDtypeStruct((t, IN_WIDTH), BF16), grid=(nb,),
        in_specs=[act,
                  pl.BlockSpec((None, 2 * BLOCK, 2 * KV_WIDTH), lambda n: (n, 0, 0)),
                  pl.BlockSpec((None, 2 * BLOCK, 2 * KV_WIDTH), lambda n: (jnp.minimum(n + 1, nb - 1), 0, 0)),
                  act, act],
        out_specs=pl.BlockSpec((BLOCK, IN_WIDTH), lambda n: (n, 0)),
        compiler_params=_cp(1), name="dz_assemble",
    )(dq, dkv, dkv, du, dv)


def _pool_count(i, tp, w):
    t_idx = i * tp + lax.broadcasted_iota(jnp.int32, (tp, 1), 0)
    return jnp.minimum(t_idx + 1, w).astype(F32)


def pool_fwd(h, pw, pscale):
    t, d = h.shape
    tp = _tile(t, 256)
    per = tp // POOL_HALO

    def body(h_ref, halo_ref, pw_ref, ps_ref, m_ref, pooled_ref):
        i = pl.program_id(0)
        cur = h_ref[...]
        ext = jnp.concatenate([jnp.where(i > 0, halo_ref[...], 0.0), cur], axis=0)
        ys, pooled = [], []
        for gi, w in enumerate(POOL_WINDOWS):
            gs = slice(gi * POOL_GROUP_DIM, (gi + 1) * POOL_GROUP_DIM)
            s = ext[:, gs]
            sh = 1
            while sh < w:
                s = s + pltpu.roll(s, sh, 0)
                sh *= 2
            pg = (s[POOL_HALO:, :] / _pool_count(i, tp, w) - cur[:, gs]).astype(BF16)
            pooled.append(pg)
            ys.append(_dot(pg, pw_ref[gi]))
        pooled_ref[...] = jnp.concatenate(pooled, axis=1)
        m_ref[...] = jnp.concatenate(ys, axis=1) * ps_ref[...]

    row = pl.BlockSpec((tp, d), lambda i: (i, 0))
    return pl.pallas_call(
        body,
        out_shape=(jax.ShapeDtypeStruct((t, d), F32), jax.ShapeDtypeStruct((t, d), BF16)),
        grid=(t // tp,),
        in_specs=[row, pl.BlockSpec((POOL_HALO, d), lambda i: (jnp.maximum(i * per - 1, 0), 0)),
                  pl.BlockSpec(pw.shape, lambda i: (0, 0, 0)), pl.BlockSpec((1, d), lambda i: (0, 0))],
        out_specs=(row, row), compiler_params=_cp(1), name="pool_fwd",
    )(h, h, pw, pscale)


def pool_bwd_proj(dm, pooled, pw, pscale):
    t, d = dm.shape
    tp = _tile(t, 256)

    def body(dm_ref, pooled_ref, pw_ref, ps_ref, dp_ref, dy_ref, dps_ref):
        dmv = dm_ref[...]
        dy = (dmv * ps_ref[...]).astype(BF16)
        dy_ref[...] = dy
        ys, dps = [], []
        for gi in range(len(POOL_WINDOWS)):
            gs = slice(gi * POOL_GROUP_DIM, (gi + 1) * POOL_GROUP_DIM)
            ys.append(_dot(pooled_ref[:, gs], pw_ref[gi]))
            dps.append(_dot(dy[:, gs], pw_ref[gi], "nt"))
        dp_ref[...] = jnp.concatenate(dps, axis=1)

        @pl.when(pl.program_id(0) == 0)
        def _():
            dps_ref[...] = jnp.zeros_like(dps_ref)

        dps_ref[...] += jnp.sum(dmv * jnp.concatenate(ys, axis=1), axis=0, keepdims=True)

    row = pl.BlockSpec((tp, d), lambda i: (i, 0))
    vec = pl.BlockSpec((1, d), lambda i: (0, 0))
    return pl.pallas_call(
        body,
        out_shape=(jax.ShapeDtypeStruct((t, d), F32), jax.ShapeDtypeStruct((t, d), BF16),
                   jax.ShapeDtypeStruct((1, d), F32)),
        grid=(t // tp,),
        in_specs=[row, row, pl.BlockSpec(pw.shape, lambda i: (0, 0, 0)), vec],
        out_specs=(row, row, vec), compiler_params=_cp(1), name="pool_bwd_proj",
    )(dm, pooled, pw, pscale)


def pool_bwd_window(dp):
    t, d = dp.shape
    tp = _tile(t, 256)
    per = tp // POOL_HALO
    last = t // POOL_HALO - 1
    nt = t // tp

    def body(dp_ref, halo_ref, dh_ref):
        i = pl.program_id(0)
        cur = dp_ref[...]
        halo = jnp.where(i < nt - 1, halo_ref[...], 0.0)
        outs = []
        for gi, w in enumerate(POOL_WINDOWS):
            gs = slice(gi * POOL_GROUP_DIM, (gi + 1) * POOL_GROUP_DIM)
            s = jnp.concatenate([cur[:, gs] / _pool_count(i, tp, w), halo[:, gs] / float(w)], axis=0)
            sh = 1
            while sh < w:
                s = s + pltpu.roll(s, tp + POOL_HALO - sh, 0)
                sh *= 2
            outs.append(s[:tp, :] - cur[:, gs])
        dh_ref[...] = jnp.concatenate(outs, axis=1)

    row = pl.BlockSpec((tp, d), lambda i: (i, 0))
    return pl.pallas_call(
        body, out_shape=jax.ShapeDtypeStruct((t, d), F32), grid=(nt,),
        in_specs=[row, pl.BlockSpec((POOL_HALO, d), lambda i: (jnp.minimum((i + 1) * per, last), 0))],
        out_specs=row, compiler_params=_cp(1), name="pool_bwd_window",
    )(dp, dp)


def pool_wgrad(pooled, dy):
    t, d = pooled.shape
    ng = d // POOL_GROUP_DIM
    tk = _tile(t, 512)
    blk = pl.BlockSpec((tk, POOL_GROUP_DIM), lambda g, k: (k, g))
    return _mm(
        "pool_wgrad", pooled, dy, dims="tn", grid=(ng, t // tk), a_spec=blk, b_spec=blk,
        o_spec=pl.BlockSpec((None, POOL_GROUP_DIM, POOL_GROUP_DIM), lambda g, k: (g, 0, 0)),
        out_shape=jax.ShapeDtypeStruct((ng, POOL_GROUP_DIM, POOL_GROUP_DIM), F32),
        acc_shape=(POOL_GROUP_DIM, POOL_GROUP_DIM), nk=t // tk,
    )


def _xattn_probs(qh, kh):
    s = _dot(qh, kh, "nt") * (X_HEAD_DIM ** -0.5)
    p = jnp.exp(s - jnp.max(s, axis=1, keepdims=True))
    return p * (1.0 / jnp.sum(p, axis=1, keepdims=True))


def xattn_fwd(name, q, k, v):
    t, xw = q.shape
    tm = _tile(t, 512)

    def body(q_ref, k_ref, v_ref, o_ref):
        outs = []
        for h in range(X_HEADS):
            hs = slice(h * X_HEAD_DIM, (h + 1) * X_HEAD_DIM)
            p = _xattn_probs(q_ref[:, hs], k_ref[:, hs])
            outs.append(_dot(p.astype(BF16), v_ref[:, hs]))
        o_ref[...] = jnp.concatenate(outs, axis=1).astype(BF16)

    row = pl.BlockSpec((tm, xw), lambda i: (i, 0))
    kv = pl.BlockSpec(k.shape, lambda i: (0, 0))
    return pl.pallas_call(
        body, out_shape=jax.ShapeDtypeStruct((t, xw), BF16), grid=(t // tm,),
        in_specs=[row, kv, kv], out_specs=row, compiler_params=_cp(1), name=name,
    )(q, k, v)


def xattn_bwd(name, q, k, v, do):
    t, xw = q.shape
    tm = _tile(t, 512)

    def body(q_ref, k_ref, v_ref, do_ref, dq_ref, dk_ref, dv_ref):
        @pl.when(pl.program_id(0) == 0)
        def _():
            dk_ref[...] = jnp.zeros_like(dk_ref)
            dv_ref[...] = jnp.zeros_like(dv_ref)

        dqs = []
        for h in range(X_HEADS):
            hs = slice(h * X_HEAD_DIM, (h + 1) * X_HEAD_DIM)
            qh, kh, vh, doh = q_ref[:, hs], k_ref[:, hs], v_ref[:, hs], do_ref[:, hs]
            p = _xattn_probs(qh, kh)
            dv_ref[:, hs] += _dot(p.astype(BF16), doh, "tn")
            dp = _dot(doh, vh, "nt")
            ds = (p * (dp - jnp.sum(p * dp, axis=1, keepdims=True)) * (X_HEAD_DIM ** -0.5)).astype(BF16)
            dqs.append(_dot(ds, kh))
            dk_ref[:, hs] += _dot(ds, qh, "tn")
        dq_ref[...] = jnp.concatenate(dqs, axis=1).astype(BF16)

    row = pl.BlockSpec((tm, xw), lambda i: (i, 0))
    kv = pl.BlockSpec(k.shape, lambda i: (0, 0))
    kv_shape = jax.ShapeDtypeStruct(k.shape, F32)
    return pl.pallas_call(
        body, out_shape=(jax.ShapeDtypeStruct((t, xw), BF16), kv_shape, kv_shape), grid=(t // tm,),
        in_specs=[row, kv, kv, row], out_specs=(row, kv, kv), compiler_params=_cp(1), name=name,
    )(q, k, v, do)


def xattn_out(name, o, wo):
    t, xw = o.shape
    ns, _, dn = wo.shape
    tm = _tile(t, 512)
    return _mm(
        name, o, wo, dims="nn", grid=(t // tm, ns),
        a_spec=pl.BlockSpec((tm, xw), lambda i, j: (i, 0)),
        b_spec=pl.BlockSpec((None, xw, dn), lambda i, j: (j, 0, 0)),
        o_spec=pl.BlockSpec((tm, dn), lambda i, j: (i, j)),
        out_shape=jax.ShapeDtypeStruct((t, ns * dn), F32),
    )


def xattn_out_bwd(name, dm, wo):
    t = dm.shape[0]
    ns, xw, dn = wo.shape
    tm = _tile(t, 512)
    return _mm(
        name, dm, wo, dims="nt", grid=(t // tm, ns),
        a_spec=pl.BlockSpec((tm, dn), lambda i, j: (i, j)),
        b_spec=pl.BlockSpec((None, xw, dn), lambda i, j: (j, 0, 0)),
        o_spec=pl.BlockSpec((tm, xw), lambda i, j: (i, 0)),
        out_shape=jax.ShapeDtypeStruct((t, xw), BF16), acc_shape=(tm, xw), nk=ns,
    )


def xattn_out_wgrad(name, o, dm, ns):
    t, xw = o.shape
    dn = dm.shape[1] // ns
    tk = _tile(t, 512)
    return _mm(
        name, o, dm, dims="tn", grid=(ns, t // tk),
        a_spec=pl.BlockSpec((tk, xw), lambda j, k: (k, 0)),
        b_spec=pl.BlockSpec((tk, dn), lambda j, k: (k, j)),
        o_spec=pl.BlockSpec((None, xw, dn), lambda j, k: (j, 0, 0)),
        out_shape=jax.ShapeDtypeStruct((ns, xw, dn), BF16), acc_shape=(xw, dn), nk=t // tk,
    )


def kernel(x, mem, norms, mem_norm, ffn1_wg, ffn1_wu, ffn1_wd, ffn2_wg, ffn2_wu, ffn2_wd, x_wq, x_wk, x_wv, x_wo, mix_w_in, mix_w_out, attn_sinks, sgu_ln_g, sgu_ln_b, sgu_w, sgu_b, pool_w, pool_scale, loss_target, m_norms, m_mem_norm, m_ffn1_wg, m_ffn1_wu, m_ffn1_wd, m_ffn2_wg, m_ffn2_wu, m_ffn2_wd, m_x_wq, m_x_wk, m_x_wv, m_x_wo, m_mix_w_in, m_mix_w_out, m_attn_sinks, m_sgu_ln_g, m_sgu_ln_b, m_sgu_w, m_sgu_b, m_pool_w, m_pool_scale, v_norms, v_mem_norm, v_ffn1_wg, v_ffn1_wu, v_ffn1_wd, v_ffn2_wg, v_ffn2_wu, v_ffn2_wd, v_x_wq, v_x_wk, v_x_wv, v_x_wo, v_mix_w_in, v_mix_w_out, v_attn_sinks, v_sgu_ln_g, v_sgu_ln_b, v_sgu_w, v_sgu_b, v_pool_w, v_pool_scale):
    params = dict(norms=norms, mem_norm=mem_norm, ffn1_wg=ffn1_wg, ffn1_wu=ffn1_wu, ffn1_wd=ffn1_wd,
                  ffn2_wg=ffn2_wg, ffn2_wu=ffn2_wu, ffn2_wd=ffn2_wd, x_wq=x_wq, x_wk=x_wk, x_wv=x_wv, x_wo=x_wo,
                  mix_w_in=mix_w_in, mix_w_out=mix_w_out, attn_sinks=attn_sinks, sgu_ln_g=sgu_ln_g,
                  sgu_ln_b=sgu_ln_b, sgu_w=sgu_w, sgu_b=sgu_b, pool_w=pool_w, pool_scale=pool_scale)
    mom1 = dict(norms=m_norms, mem_norm=m_mem_norm, ffn1_wg=m_ffn1_wg, ffn1_wu=m_ffn1_wu, ffn1_wd=m_ffn1_wd,
                ffn2_wg=m_ffn2_wg, ffn2_wu=m_ffn2_wu, ffn2_wd=m_ffn2_wd, x_wq=m_x_wq, x_wk=m_x_wk, x_wv=m_x_wv,
                x_wo=m_x_wo, mix_w_in=m_mix_w_in, mix_w_out=m_mix_w_out, attn_sinks=m_attn_sinks,
                sgu_ln_g=m_sgu_ln_g, sgu_ln_b=m_sgu_ln_b, sgu_w=m_sgu_w, sgu_b=m_sgu_b, pool_w=m_pool_w,
                pool_scale=m_pool_scale)
    mom2 = dict(norms=v_norms, mem_norm=v_mem_norm, ffn1_wg=v_ffn1_wg, ffn1_wu=v_ffn1_wu, ffn1_wd=v_ffn1_wd,
                ffn2_wg=v_ffn2_wg, ffn2_wu=v_ffn2_wu, ffn2_wd=v_ffn2_wd, x_wq=v_x_wq, x_wk=v_x_wk, x_wv=v_x_wv,
                x_wo=v_x_wo, mix_w_in=v_mix_w_in, mix_w_out=v_mix_w_out, attn_sinks=v_attn_sinks,
                sgu_ln_g=v_sgu_ln_g, sgu_ln_b=v_sgu_ln_b, sgu_w=v_sgu_w, sgu_b=v_sgu_b, pool_w=v_pool_w,
                pool_scale=v_pool_scale)
    order = list(params)

    xs, memb, target = x[0], mem[0], loss_target[0]
    t, d = xs.shape
    depth = norms.shape[0]
    dsh = d // NDEV

    bf = lambda a: a.astype(BF16)
    wts = {}

    def gather_job(keys):
        return _Gather([bf(params[name][l]) for name, l in keys]), keys

    def land(job_keys):
        job, keys = job_keys
        for key, a in zip(keys, job.result):
            wts[key] = a

    ffn_keys = lambda tag, l: [(f"{tag}_wg", l), (f"{tag}_wu", l), (f"{tag}_wd", l)]
    x_keys = lambda l: [("x_wq", l), ("x_wk", l), ("x_wv", l), ("x_wo", l)]
    small_shapes = [norms.shape, pool_scale.shape, pool_w.shape]
    head = gather_job(ffn_keys("ffn1", 0))
    head_small = _Gather([_pack([norms, pool_scale, pool_w])])
    _run_exchange("gather_head", _Multi([head_small, head[0]]))
    land(head)
    norms_sh, pscale_sh, pw_sh = _unpack(head_small.result[0], small_shapes, (NDEV,))
    norms_full = norms_sh.transpose(1, 2, 0, 3).reshape(depth, norms.shape[1], d)
    pscale_full = pscale_sh.transpose(1, 0, 2).reshape(1, d)
    pw_full = pw_sh[:, 0].transpose(1, 0, 2, 3).reshape(len(POOL_WINDOWS), POOL_GROUP_DIM, POOL_GROUP_DIM).astype(BF16)
    fwd_jobs = {
        ("ffn1", 0): gather_job([("mix_w_in", 0), ("mix_w_out", 0)] + x_keys(0) + ffn_keys("ffn2", 0)[:2]),
        ("attn", 0): gather_job(ffn_keys("ffn2", 0)[2:]),
        ("ffn2", 0): gather_job(ffn_keys("ffn1", 1)),
        ("ffn1", 1): gather_job(x_keys(1) + ffn_keys("ffn2", 1)),
    }

    tab = rope_table(t)
    sgu_w0 = sgu_w[0]
    sgu_wt0 = sgu_w0.transpose(0, 2, 1)
    sgu_bt0 = sgu_b[0].T
    gain = lambda l, i: norms_full[l, i][None, :]

    saved = []
    xc = xs
    for l in range(depth):
        s = {}

        def ffn_forward(tag, xc, gi, l=l, s=s):
            h, s[tag + "_ht"] = norm_fwd(f"norm_{tag}_{l}", xc, gain(l, gi), BF16, with_t=True)
            job = fwd_jobs.get((tag, l))
            s[tag + "_a"], s[tag + "_b"], s[tag + "_hidt"], s[tag + "_m"] = ffn_fwd(
                f"{tag}_fwd_{l}", h, wts[tag + "_wg", l], wts[tag + "_wu", l], wts[tag + "_wd", l],
                comm=job and job[0])
            if job:
                land(job)
            return resid_norm_fwd(f"resid_{tag}_{l}", xc, s[tag + "_m"], gain(l, gi + 1), 0.5)

        s["x0"] = xc
        xc = ffn_forward("ffn1", xc, 0)

        s["x1"] = xc
        if l % 2 == 0:
            w_in = wts["mix_w_in", l].transpose(1, 0, 2).reshape(d, IN_WIDTH)
            o_k, o_u = ATTN_WIDTH, ATTN_WIDTH + 2 * KV_WIDTH
            w_in = jnp.concatenate([w_in[:, :o_k], w_in[:, o_u:], w_in[:, o_k:o_u]], axis=1)
            w_out = wts["mix_w_out", l].reshape(d, d)
            h2, s["h2t"] = norm_fwd(f"norm_mix_{l}", xc, gain(l, 2), BF16, with_t=True)
            s["z"] = mm_nn("mix_in", h2, w_in, F32, tn=IN_WIDTH // 2)
            job = fwd_jobs[("attn", l)]
            attn = attn_fwd(s["z"], tab, attn_sinks, comm=job[0])
            land(job)
            gate = sgu_fwd(s["z"], sgu_ln_g, sgu_ln_b, sgu_w0, sgu_bt0)
            s["cat"] = jnp.concatenate([attn, gate], axis=1)
            s["m2"] = mm_nn("mix_out", s["cat"], w_out, F32)
        else:
            h2 = norm_fwd(f"norm_mix_{l}", xc, gain(l, 2), F32)
            s["m2"], s["pooled"] = pool_fwd(h2, pw_full, pscale_full)
        xc = resid_norm_fwd(f"resid_mix_{l}", xc, s["m2"], gain(l, 3), 1.0)

        s["x2"] = xc
        wq, wk, wv = (wts[k, l].reshape(d, -1) for k in ("x_wq", "x_wk", "x_wv"))
        s["wq"], s["wkv"] = wq, jnp.concatenate([wk, wv], axis=1)
        h3, s["h3t"] = norm_fwd(f"norm_x_{l}", xc, gain(l, 4), BF16, with_t=True)
        s["mem_n"] = norm_fwd(f"norm_mem_{l}", memb, mem_norm[l][None, :], BF16)
        s["q"] = mm_nn(f"x_q_{l}", h3, wq, BF16)
        s["k"] = mm_nn(f"x_k_{l}", s["mem_n"], wk, BF16)
        s["v"] = mm_nn(f"x_v_{l}", s["mem_n"], wv, BF16)
        s["o"] = xattn_fwd(f"xattn_fwd_{l}", s["q"], s["k"], s["v"])
        s["m3"] = xattn_out(f"x_o_{l}", s["o"], wts["x_wo", l])
        xc = resid_norm_fwd(f"resid_x_{l}", xc, s["m3"], gain(l, 5), 1.0)

        s["x3"] = xc
        xc = ffn_forward("ffn2", xc, 6)
        saved.append(s)

    dx, loss11 = loss_grad(xc, target)
    loss = lax.psum(loss11[0, 0], ("x", "y", "c"))

    swaps = []
    pending = []
    recv = {}

    def emit_units(name, l, arr, among_chips):
        piece_mib = math.prod(arr.shape[1:]) * arr.dtype.itemsize / MIB
        parts = 2 if piece_mib > 1.5 else 1
        rows = arr.shape[1] // parts
        cost = (LINK_US_PER_MIB_CHIPS if among_chips else LINK_US_PER_MIB_ALL) * piece_mib / parts
        for part in range(parts):
            pending.append(((name, l, part), among_chips, (arr, part * rows, rows), cost))

    def emit(name, l, arr, two_level=False):
        if two_level:
            swaps.append((name, l, arr.reshape((NCHIP, 2) + arr.shape[1:])))
        else:
            emit_units(name, l, arr, False)

    def hosted(budget_us, fn, *args, extra=(), force=True, **kw):
        jobs = list(extra)
        swapped = swaps[:]
        del swaps[:]
        used = PAIR_SWAP_US * len(swapped)
        if swapped:
            jobs.append(_PairSwap([g for _, _, g in swapped]))
        items, kept = [], []
        for it in pending:
            if (force and not items and not swapped) or used + it[3] <= budget_us:
                items.append(it)
                used += it[3]
            else:
                kept.append(it)
        pending[:] = kept
        groups = [[it for it in items if it[1] == flag] for flag in (False, True)]
        unit_jobs = [cls([it[2] for it in grp]) if grp else None
                     for cls, grp in zip((_Scatter, _ChipScatter), groups)]
        jobs += [j for j in unit_jobs if j is not None]
        res = fn(*args, comm=_Multi(jobs) if jobs else None, **kw)
        for job, grp in zip(unit_jobs, groups):
            for it, a in zip(grp, job.result if job else ()):
                recv[it[0]] = a
        if swapped:
            for (name, l, g4), theirs in zip(swapped, jobs[len(extra)].result):
                emit_units(name, l, pair_sum(f"pairsum_{name}_{l}", g4, theirs), True)
        return res

    grads = {k: [None] * depth for k in ("norms", "mem_norm")}
    small_jobs = []
    for l in reversed(range(depth)):
        s = saved[l]
        dg = [None] * 8

        def ffn_block(tag, dx, x_in, gi, extra=(), l=l, s=s, dg=dg):
            dm, dg[gi + 1] = norm_bwd(f"{tag}_post_bwd_{l}", s[tag + "_m"], gain(l, gi + 1), dx, 0.5, None, BF16)
            da, db, dh = hosted(HOST_US_FFN_BWD, ffn_bwd, f"{tag}_bwd_{l}", dm, s[tag + "_a"], s[tag + "_b"],
                                wts[tag + "_wg", l], wts[tag + "_wu", l], wts[tag + "_wd", l], extra=extra)
            emit(tag + "_wg", l, hosted(HOST_US_WGRAD, ffn_wgrad_in, f"{tag}_dwg_{l}", s[tag + "_ht"], da), True)
            emit(tag + "_wu", l, hosted(HOST_US_WGRAD, ffn_wgrad_in, f"{tag}_dwu_{l}", s[tag + "_ht"], db), True)
            emit(tag + "_wd", l, hosted(HOST_US_WGRAD, ffn_wgrad_out, f"{tag}_dwd_{l}", s[tag + "_hidt"], dm), True)
            dx, dg[gi] = hosted(HOST_US_SMALL, norm_bwd, f"{tag}_pre_bwd_{l}", x_in, gain(l, gi), dh, 1.0, dx, F32,
                                force=False)
            return dx

        dx = ffn_block("ffn2", dx, s["x3"], 6)

        dm, dg[5] = norm_bwd(f"x_post_bwd_{l}", s["m3"], gain(l, 5), dx, 1.0, None, BF16)
        do = xattn_out_bwd(f"x_do_{l}", dm, wts["x_wo", l])
        emit("x_wo", l, xattn_out_wgrad(f"x_dwo_{l}", s["o"], dm, NDEV))
        dq, dk, dv = xattn_bwd(f"xattn_bwd_{l}", s["q"], s["k"], s["v"], do)
        dkb, dvb = dk.astype(BF16), dv.astype(BF16)
        emit("x_wq", l, mm_kred(f"x_dwq_{l}", s["h3t"], dq, BF16).reshape(NDEV, dsh, -1))
        emit("x_wk", l, mm_tn(f"x_dwk_{l}", s["mem_n"], dkb, BF16).reshape(NDEV, dsh, -1))
        emit("x_wv", l, mm_tn(f"x_dwv_{l}", s["mem_n"], dvb, BF16).reshape(NDEV, dsh, -1))
        dh = mm_nt(f"x_dh_{l}", dq, s["wq"], F32)
        dmem_n = mm_nt(f"x_dmem_{l}", jnp.concatenate([dkb, dvb], axis=1), s["wkv"], F32)
        _, grads["mem_norm"][l] = norm_bwd(f"mem_norm_bwd_{l}", memb, mem_norm[l][None, :], dmem_n, 1.0, None, F32)
        dx, dg[4] = norm_bwd(f"x_pre_bwd_{l}", s["x2"], gain(l, 4), dh, 1.0, dx, F32)

        if l % 2 == 0:
            dm, dg[3] = norm_bwd(f"mix_post_bwd_{l}", s["m2"], gain(l, 3), dx, 1.0, None, BF16)
            dcat = mm_nt("mix_dcat", dm, w_out, BF16)
            emit("mix_w_out", l, mm_tn("mix_dwout", s["cat"], dm, BF16, tno=d // 2).reshape(NDEV, dsh, d))
            dq_a, dkv_a, dsink = hosted(HOST_US_ATTN_BWD, attn_bwd, s["z"], tab, attn_sinks, dcat, force=False)
            du_s, dv_s, g_sgu_w, g_sgu_bt, g_ln_g, g_ln_b = sgu_bwd(
                s["z"], sgu_ln_g, sgu_ln_b, sgu_w0, sgu_wt0, sgu_bt0, dcat)
            dz = dz_assemble(dq_a, dkv_a, du_s, dv_s)
            dh = mm_nt("mix_dh", dz, w_in, F32, tn=d // 2)
            g_win = hosted(2 * HOST_US_SMALL, mm_kred, "mix_dwin", s["h2t"], dz, BF16, tno=IN_WIDTH // 2, force=False)
            g_win = jnp.concatenate([g_win[:, :ATTN_WIDTH], g_win[:, ZK:], g_win[:, ZU:ZK]], axis=1)
            emit("mix_w_in", l, g_win.reshape(d, NDEV, -1).transpose(1, 0, 2))
        else:
            dm, dg[3] = norm_bwd(f"mix_post_bwd_{l}", s["m2"], gain(l, 3), dx, 1.0, None, F32)
            dp, dy, g_pscale = pool_bwd_proj(dm, s["pooled"], pw_full, pscale_full)
            g_pw = pool_wgrad(s["pooled"], dy)
            emit("pool_w", 0, g_pw.reshape(len(POOL_WINDOWS), NDEV, -1, POOL_GROUP_DIM).transpose(1, 0, 2, 3)
                 .reshape(NDEV, -1, POOL_GROUP_DIM))
            dh = pool_bwd_window(dp)
        dx, dg[2] = norm_bwd(f"mix_pre_bwd_{l}", s["x1"], gain(l, 2), dh, 1.0, dx, F32)

        if l == 0:
            replicated = ["mem_norm", "attn_sinks", "sgu_ln_g", "sgu_ln_b", "sgu_w", "sgu_b"]
            rep_grads = [jnp.concatenate(grads["mem_norm"], axis=0), dsink[:, :N_Q_HEADS], g_ln_g, g_ln_b,
                         g_sgu_w[None], g_sgu_bt[:, :SGU_GROUPS].T[None]]
            small_jobs.append(_Gather([_pack(rep_grads)]))
        dx = ffn_block("ffn1", dx, s["x0"], 0, extra=small_jobs if l == 0 else ())
        grads["norms"][l] = jnp.concatenate(dg, axis=0)

    g_norms = jnp.stack(grads["norms"], axis=0).reshape(depth, norms.shape[1], NDEV, dsh).transpose(2, 0, 1, 3)
    g_pscale_p = g_pscale.reshape(1, NDEV, dsh).transpose(1, 0, 2)
    sharded_small = ["norms", "pool_scale"]
    pieces_small = jnp.stack([_pack([g_norms[j], g_pscale_p[j]]) for j in range(NDEV)], axis=0)
    small_scatter = _Scatter([(pieces_small, 0, pieces_small.shape[1])])

    out = {}

    def update(k, extra=()):
        waiting = [it[0] for it in pending if it[0][0] == k] + [it[:2] for it in swaps if it[0] == k]
        assert not waiting, waiting
        shp = params[k].shape
        c = shp[-1]
        view = lambda a: a.reshape(-1, c)
        pieces = [recv[key] for key in sorted(key for key in recv if key[0] == k)]
        res = hosted(HOST_US_ADAMW, adamw, f"adamw_{k}", view(params[k]), view(mom1[k]), view(mom2[k]), pieces,
                     extra=extra)
        out[k] = [a.reshape(shp) for a in res]

    def update_pack(names, pieces):
        shapes = [params[k].shape for k in names]
        res = adamw("adamw_" + names[0] + "_pack", _pack([params[k] for k in names]),
                    _pack([mom1[k] for k in names]), _pack([mom2[k] for k in names]), [pieces])
        for which in range(4):
            for k, a in zip(names, _unpack(res[which], shapes)):
                out.setdefault(k, [None] * 4)[which] = a

    last = ("ffn1_wg", "ffn1_wu", "ffn1_wd")
    early = [k for k in order if k not in last and k not in sharded_small and k not in replicated]
    for i, k in enumerate(early):
        update(k, extra=[small_scatter] if i == 0 else ())
    flushes = 0
    while pending or swaps:
        hosted(float("inf"), lambda comm: _run_exchange(f"scatter_tail_{flushes}", comm))
        flushes += 1
    update_pack(replicated, small_jobs[0].result[0])
    update_pack(sharded_small, small_scatter.result[0])
    for k in last:
        update(k)

    outputs = [loss, dx[None]]
    for which in range(4):
        outputs += [out[k][which] for k in order]
    return tuple(outputs)
```

```python
import math

import jax
import jax.numpy as jnp
from jax import lax
from jax.experimental import pallas as pl
from jax.experimental.pallas import tpu as pltpu

F32 = jnp.float32
BF16 = jnp.bfloat16
NDEV = 8
MIB = 1024 * 1024
LANES = 128

RMS_EPS = 1e-6
HEAD_DIM = 64
N_Q_HEADS = 16
Q_PER_KV = 8
ATTN_WIDTH = 1024
KV_WIDTH = 128
BLOCK = 128
ROPE_DIM = 16
ROPE_THETA = 500000.0
SGU_GROUPS = 8
SGU_WIDTH = 1024
CHUNK = 128
POOL_WINDOWS = (2, 4, 8, 16)
POOL_GROUP_DIM = 512
POOL_HALO = 16
X_HEADS = 4
X_HEAD_DIM = 128
ZQ, ZU, ZV, ZK = 0, 1024, 2048, 3072
IN_WIDTH = 3328

ADAM_LR = 0.001
ADAM_B1 = 0.9
ADAM_B2 = 0.999
ADAM_EPS = 1e-08
ADAM_WD = 0.01
ADAM_STEP = 10

FFN_FWD_ROWS = 512
FFN_BWD_ROWS = 512

LINK_US_PER_MIB_ALL = 91.0
LINK_US_PER_MIB_CHIPS = 45.0
PAIR_SWAP_US = 20.0
HOST_US_FFN_BWD = 420.0
HOST_US_WGRAD = 100.0
HOST_US_ATTN_BWD = 240.0
HOST_US_SMALL = 40.0
HOST_US_ADAMW = 40.0

_DN = {
    "nn": (((1,), (0,)), ((), ())),
    "nt": (((1,), (1,)), ((), ())),
    "tn": (((0,), (0,)), ((), ())),
}


def _cp(naxes, vmem_mib=48):
    return pltpu.CompilerParams(dimension_semantics=("arbitrary",) * naxes, vmem_limit_bytes=vmem_mib * MIB)


def _tile(n, pref):
    t = min(n, pref)
    while n % t:
        t //= 2
    return t


def _dot(a, b, dims="nn"):
    return lax.dot_general(a, b, _DN[dims], preferred_element_type=F32)


def _me():
    x, y, c = lax.axis_index("x"), lax.axis_index("y"), lax.axis_index("c")
    return x, y, c, 4 * x + 2 * y + c


def _peer(k):
    x, y, c, _ = _me()
    px = 1 - x if k & 4 else x
    py = 1 - y if k & 2 else y
    pc = 1 - c if k & 1 else c
    return (px, py, pc), 4 * px + 2 * py + pc


class _Exchange:
    def __init__(self, arrs, out_shape, remote_per=NDEV - 1, local_per=1):
        self.arrs = list(arrs)
        self.n = len(self.arrs)
        self.out_shape = list(out_shape)
        self.remote_per = remote_per
        self.scratch = [
            pltpu.SemaphoreType.DMA((self.n * remote_per,)),
            pltpu.SemaphoreType.DMA((self.n * remote_per,)),
            pltpu.SemaphoreType.DMA((self.n * local_per,)),
        ]
        self.result = None

    def mid(self, ins, outs, sems):
        pass

    def _copy(self, src, dst, sems, i, k, dev):
        send, recv, _ = sems
        return pltpu.make_async_remote_copy(
            src_ref=src, dst_ref=dst, send_sem=send.at[i * self.remote_per + k - 1],
            recv_sem=recv.at[i * self.remote_per + k - 1], device_id=dev, device_id_type=pl.DeviceIdType.MESH)


class _Gather(_Exchange):
    def __init__(self, arrs):
        super().__init__(arrs, [jax.ShapeDtypeStruct((NDEV,) + a.shape, a.dtype) for a in arrs])

    def start(self, ins, outs, sems):
        me = _me()[3]
        for i in range(self.n):
            pltpu.make_async_copy(ins[i], outs[i].at[me], sems[2].at[i]).start()
        for k in (1, 2, 4, 6):
            dev, _ = _peer(k)
            for i in range(self.n):
                self._copy(ins[i], outs[i].at[me], sems, i, k, dev).start()

    def mid(self, ins, outs, sems):
        sibling, _ = _peer(1)
        for k in (2, 4, 6):
            dev, slot = _peer(k)
            for i in range(self.n):
                block = outs[i].at[slot]
                self._copy(ins[i], block, sems, i, k, dev).wait_recv()
                self._copy(block, block, sems, i, k + 1, sibling).start()

    def finish(self, ins, outs, sems):
        me = _me()[3]
        sibling, _ = _peer(1)
        for k in (1, 3, 5, 7):
            dev, slot = _peer(k)
            for i in range(self.n):
                self._copy(ins[i], outs[i].at[slot], sems, i, k, dev).wait_recv()
        for k in range(1, NDEV):
            for i in range(self.n):
                self._copy(ins[i], outs[i].at[me], sems, i, k, sibling).wait_send()
        for i in range(self.n):
            pltpu.make_async_copy(ins[i], outs[i].at[me], sems[2].at[i]).wait()


class _Scatter(_Exchange):
    def __init__(self, units):
        self.rows = [(r0, n) for _, r0, n in units]
        super().__init__([a for a, _, _ in units],
                         [jax.ShapeDtypeStruct((NDEV, n) + a.shape[2:], a.dtype) for a, _, n in units])

    def _src(self, ins, i, slot):
        r0, n = self.rows[i]
        return ins[i].at[slot, pl.ds(r0, n)]

    def start(self, ins, outs, sems):
        me = _me()[3]
        for i in range(self.n):
            pltpu.make_async_copy(self._src(ins, i, me), outs[i].at[me], sems[2].at[i]).start()
        for k in range(1, NDEV):
            dev, slot = _peer(k)
            for i in range(self.n):
                self._copy(self._src(ins, i, slot), outs[i].at[me], sems, i, k, dev).start()

    def finish(self, ins, outs, sems):
        me = _me()[3]
        for k in range(1, NDEV):
            dev, slot = _peer(k)
            for i in range(self.n):
                cp = self._copy(self._src(ins, i, slot), outs[i].at[slot], sems, i, k, dev)
                cp.wait_send()
                cp.wait_recv()
        for i in range(self.n):
            pltpu.make_async_copy(self._src(ins, i, me), outs[i].at[me], sems[2].at[i]).wait()


NCHIP = NDEV // 2


class _PairSwap(_Exchange):
    def __init__(self, arrs):
        super().__init__(arrs, [jax.ShapeDtypeStruct((NCHIP,) + a.shape[2:], a.dtype) for a in arrs],
                         remote_per=NCHIP)

    def _copies(self, ins, outs, sems):
        c = _me()[2]
        sibling, _ = _peer(1)
        for i in range(self.n):
            for q in range(NCHIP):
                yield self._copy(ins[i].at[q, 1 - c], outs[i].at[q], sems, i, q + 1, sibling)

    def start(self, ins, outs, sems):
        for remote in self._copies(ins, outs, sems):
            remote.start()

    def finish(self, ins, outs, sems):
        for remote in self._copies(ins, outs, sems):
            remote.wait_send()
            remote.wait_recv()


class _ChipScatter(_Exchange):
    def __init__(self, units):
        self.rows = [(r0, n) for _, r0, n in units]
        super().__init__([a for a, _, _ in units],
                         [jax.ShapeDtypeStruct((NCHIP, n) + a.shape[2:], a.dtype) for a, _, n in units],
                         remote_per=NCHIP - 1)

    def _src(self, ins, i, chip):
        r0, n = self.rows[i]
        return ins[i].at[chip, pl.ds(r0, n)]

    @staticmethod
    def _chip(k):
        dev, slot = _peer(2 * k)
        return dev, slot // 2

    def start(self, ins, outs, sems):
        mine = _me()[3] // 2
        for i in range(self.n):
            pltpu.make_async_copy(self._src(ins, i, mine), outs[i].at[mine], sems[2].at[i]).start()
        for k in range(1, NCHIP):
            dev, chip = self._chip(k)
            for i in range(self.n):
                self._copy(self._src(ins, i, chip), outs[i].at[mine], sems, i, k, dev).start()

    def finish(self, ins, outs, sems):
        mine = _me()[3] // 2
        for k in range(1, NCHIP):
            dev, chip = self._chip(k)
            for i in range(self.n):
                cp = self._copy(self._src(ins, i, chip), outs[i].at[chip], sems, i, k, dev)
                cp.wait_send()
                cp.wait_recv()
        for i in range(self.n):
            pltpu.make_async_copy(self._src(ins, i, mine), outs[i].at[mine], sems[2].at[i]).wait()


class _Multi:
    def __init__(self, jobs):
        self.jobs = list(jobs)
        self.arrs = [a for j in self.jobs for a in j.arrs]
        self.out_shape = [s for j in self.jobs for s in j.out_shape]
        self.scratch = [s for j in self.jobs for s in j.scratch]
        self._result = None

    def _parts(self, ins, outs, sems):
        oi = oo = 0
        for idx, j in enumerate(self.jobs):
            ni, no = len(j.arrs), len(j.out_shape)
            yield j, ins[oi:oi + ni], outs[oo:oo + no], sems[3 * idx:3 * idx + 3]
            oi += ni
            oo += no

    def start(self, ins, outs, sems):
        for j, i, o, s in self._parts(ins, outs, sems):
            j.start(i, o, s)

    def mid(self, ins, outs, sems):
        for j, i, o, s in self._parts(ins, outs, sems):
            j.mid(i, o, s)

    def finish(self, ins, outs, sems):
        for j, i, o, s in self._parts(ins, outs, sems):
            j.finish(i, o, s)

    @property
    def result(self):
        return self._result

    @result.setter
    def result(self, res):
        self._result = res
        o = 0
        for j in self.jobs:
            j.result = list(res[o:o + len(j.out_shape)])
            o += len(j.out_shape)


def _call(name, body, *, grid, in_specs, out_specs, out_shape, args, scratch=(), comm=None, vmem_mib=48):
    in_specs, out_specs, out_shape = list(in_specs), list(out_specs), list(out_shape)
    scratch, args = list(scratch), list(args)
    ni, no, ns = len(in_specs), len(out_specs), len(scratch)
    kernel_fn = body
    if comm is not None:
        ci, co = len(comm.arrs), len(comm.out_shape)
        hbm = pl.BlockSpec(memory_space=pltpu.HBM)

        def kernel_fn(*refs):
            refs = list(refs)
            ins, c_in, outs, c_out, scr, c_scr = (
                [refs.pop(0) for _ in range(cnt)] for cnt in (ni, ci, no, co, ns, len(comm.scratch)))
            if not grid:
                comm.start(c_in, c_out, c_scr)
                body(*ins, *outs, *scr)
                comm.mid(c_in, c_out, c_scr)
                comm.finish(c_in, c_out, c_scr)
                return
            step = pl.program_id(0)
            for ax in range(1, len(grid)):
                step = step * grid[ax] + pl.program_id(ax)
            nsteps = math.prod(grid)
            mid_step = nsteps - 1

            @pl.when(step == 0)
            def _():
                comm.start(c_in, c_out, c_scr)

            body(*ins, *outs, *scr)

            @pl.when(step == mid_step)
            def _():
                comm.mid(c_in, c_out, c_scr)

            @pl.when(step == nsteps - 1)
            def _():
                comm.finish(c_in, c_out, c_scr)

        in_specs += [hbm] * ci
        out_specs += [hbm] * co
        out_shape += comm.out_shape
        scratch += comm.scratch
        args += comm.arrs
    params = _cp(len(grid), vmem_mib) if grid else None
    res = pl.pallas_call(
        kernel_fn, out_shape=out_shape, grid=grid, in_specs=in_specs, out_specs=out_specs,
        scratch_shapes=scratch, compiler_params=params, name=name,
    )(*args)
    if comm is not None:
        comm.result = list(res[no:])
    return tuple(res[:no])


def _run_exchange(name, comm):
    _call(name, lambda: None, grid=(), in_specs=[], out_specs=[], out_shape=[], args=[], comm=comm)
    return comm.result


def _pack(arrs, dtype=F32):
    flat = jnp.concatenate([a.astype(dtype).reshape(-1) for a in arrs])
    n = flat.shape[0]
    total = -(-n // (16 * LANES)) * (16 * LANES)
    return jnp.pad(flat, (0, total - n)).reshape(total // LANES, LANES)


def _unpack(packed, shapes, lead=()):
    flat = packed.reshape(lead + (-1,))
    out, off = [], 0
    for s in shapes:
        n = math.prod(s)
        out.append(flat[..., off:off + n].reshape(lead + tuple(s)))
        off += n
    return out


def _mm(name, a, b, *, dims, grid, a_spec, b_spec, o_spec, out_shape, acc_shape=None, nk=1, vmem_mib=48, comm=None):
    nax = len(grid)

    def body(a_ref, b_ref, o_ref, *scratch):
        p = _dot(a_ref[...], b_ref[...], dims)
        if nk == 1:
            o_ref[...] = p.astype(o_ref.dtype)
            return
        acc = scratch[0]
        k = pl.program_id(nax - 1)

        @pl.when(k == 0)
        def _():
            acc[...] = p

        @pl.when(k > 0)
        def _():
            acc[...] += p

        @pl.when(k == nk - 1)
        def _():
            o_ref[...] = acc[...].astype(o_ref.dtype)

    return _call(
        name, body, grid=grid, in_specs=[a_spec, b_spec], out_specs=[o_spec], out_shape=[out_shape], args=[a, b],
        scratch=[pltpu.VMEM(acc_shape, F32)] if nk > 1 else [], comm=comm, vmem_mib=vmem_mib,
    )[0]


def mm_nn(name, a, b, out_dtype, tn=None):
    m, k = a.shape
    n = b.shape[1]
    tm = _tile(m, 512)
    tn = n if tn is None else tn
    return _mm(
        name, a, b, dims="nn", grid=(n // tn, m // tm),
        a_spec=pl.BlockSpec((tm, k), lambda j, i: (i, 0)),
        b_spec=pl.BlockSpec((k, tn), lambda j, i: (0, j)),
        o_spec=pl.BlockSpec((tm, tn), lambda j, i: (i, j)),
        out_shape=jax.ShapeDtypeStruct((m, n), out_dtype),
    )


def mm_nt(name, a, b, out_dtype, tn=None):
    m, k = a.shape
    n = b.shape[0]
    tm = _tile(m, 512)
    tn = n if tn is None else tn
    return _mm(
        name, a, b, dims="nt", grid=(n // tn, m // tm),
        a_spec=pl.BlockSpec((tm, k), lambda j, i: (i, 0)),
        b_spec=pl.BlockSpec((tn, k), lambda j, i: (j, 0)),
        o_spec=pl.BlockSpec((tm, tn), lambda j, i: (i, j)),
        out_shape=jax.ShapeDtypeStruct((m, n), out_dtype),
    )


def mm_kred(name, a_t, b, out_dtype, tno=None, comm=None):
    m, k = a_t.shape
    n = b.shape[1]
    tk = _tile(k, 512)
    tmo = _tile(m, 1024)
    tno = n if tno is None else tno
    return _mm(
        name, a_t, b, dims="nn", grid=(m // tmo, n // tno, k // tk),
        a_spec=pl.BlockSpec((tmo, tk), lambda i, j, kk: (i, kk)),
        b_spec=pl.BlockSpec((tk, tno), lambda i, j, kk: (kk, j)),
        o_spec=pl.BlockSpec((tmo, tno), lambda i, j, kk: (i, j)),
        out_shape=jax.ShapeDtypeStruct((m, n), out_dtype),
        acc_shape=(tmo, tno), nk=k // tk, comm=comm,
    )


def mm_tn(name, a, b, out_dtype, tmo=None, tno=None):
    k, m = a.shape
    n = b.shape[1]
    tk = _tile(k, 512)
    tmo = _tile(m, 1024) if tmo is None else tmo
    tno = n if tno is None else tno
    return _mm(
        name, a, b, dims="tn", grid=(m // tmo, n // tno, k // tk),
        a_spec=pl.BlockSpec((tk, tmo), lambda i, j, kk: (kk, i)),
        b_spec=pl.BlockSpec((tk, tno), lambda i, j, kk: (kk, j)),
        o_spec=pl.BlockSpec((tmo, tno), lambda i, j, kk: (i, j)),
        out_shape=jax.ShapeDtypeStruct((m, n), out_dtype),
        acc_shape=(tmo, tno), nk=k // tk,
    )


def _rstd(x):
    return lax.rsqrt(jnp.mean(x * x, axis=-1, keepdims=True) + RMS_EPS)


def norm_fwd(name, x, g, out_dtype, with_t=False):
    t, d = x.shape
    tm = _tile(t, 256)

    def body(x_ref, g_ref, o_ref, *t_ref):
        xv = x_ref[...]
        h = xv * _rstd(xv) * g_ref[...]
        o_ref[...] = h.astype(o_ref.dtype)
        if with_t:
            t_ref[0][...] = h.T.astype(out_dtype)

    row = pl.BlockSpec((tm, d), lambda i: (i, 0))
    out_shape = [jax.ShapeDtypeStruct((t, d), out_dtype)]
    out_specs = [row]
    if with_t:
        out_shape.append(jax.ShapeDtypeStruct((d, t), out_dtype))
        out_specs.append(pl.BlockSpec((d, tm), lambda i: (0, i)))
    res = pl.pallas_call(
        body, out_shape=out_shape, grid=(t // tm,),
        in_specs=[row, pl.BlockSpec((1, d), lambda i: (0, 0))], out_specs=out_specs,
        compiler_params=_cp(1), name=name,
    )(x, g)
    return tuple(res) if with_t else res[0]


def resid_norm_fwd(name, x, m, g, scale):
    t, d = x.shape
    tm = _tile(t, 256)

    def body(x_ref, m_ref, g_ref, o_ref):
        mv = m_ref[...]
        o_ref[...] = x_ref[...] + scale * (mv * _rstd(mv) * g_ref[...])

    row = pl.BlockSpec((tm, d), lambda i: (i, 0))
    return pl.pallas_call(
        body, out_shape=jax.ShapeDtypeStruct((t, d), F32), grid=(t // tm,),
        in_specs=[row, row, pl.BlockSpec((1, d), lambda i: (0, 0))], out_specs=row,
        compiler_params=_cp(1), name=name,
    )(x, m, g)


def norm_bwd(name, u, g, dy, scale, resid, out_dtype, comm=None):
    t, d = u.shape
    tm = _tile(t, 256)
    has_resid = resid is not None

    def body(*refs):
        if has_resid:
            u_ref, g_ref, dy_ref, r_ref, du_ref, dg_ref = refs
        else:
            u_ref, g_ref, dy_ref, du_ref, dg_ref = refs
        uv = u_ref[...]
        dyv = dy_ref[...].astype(F32) * scale
        r = _rstd(uv)
        uh = uv * r

        @pl.when(pl.program_id(0) == 0)
        def _():
            dg_ref[...] = jnp.zeros_like(dg_ref)

        dg_ref[...] += jnp.sum(dyv * uh, axis=0, keepdims=True)
        dyg = dyv * g_ref[...]
        du = r * (dyg - uh * jnp.mean(dyg * uh, axis=-1, keepdims=True))
        if has_resid:
            du = du + r_ref[...]
        du_ref[...] = du.astype(du_ref.dtype)

    row = pl.BlockSpec((tm, d), lambda i: (i, 0))
    vec = pl.BlockSpec((1, d), lambda i: (0, 0))
    args = (u, g, dy) + ((resid,) if has_resid else ())
    return _call(
        name, body, grid=(t // tm,), in_specs=[row, vec, row] + ([row] if has_resid else []),
        out_specs=[row, vec], out_shape=[jax.ShapeDtypeStruct((t, d), out_dtype), jax.ShapeDtypeStruct((1, d), F32)],
        args=args, comm=comm,
    )


def loss_grad(y, target):
    t, d = y.shape
    tm = _tile(t, 256)
    nt = t // tm

    def body(y_ref, t_ref, dy_ref, loss_ref, acc):
        i = pl.program_id(0)
        e = y_ref[...] - t_ref[...]
        dy_ref[...] = e * (1.0 / d)

        @pl.when(i == 0)
        def _():
            acc[...] = jnp.zeros_like(acc)

        acc[...] += jnp.sum(e * e, axis=0, keepdims=True)

        @pl.when(i == nt - 1)
        def _():
            loss_ref[...] = (0.5 / d) * jnp.sum(acc[...], axis=1, keepdims=True)

    row = pl.BlockSpec((tm, d), lambda i: (i, 0))
    return pl.pallas_call(
        body,
        out_shape=(jax.ShapeDtypeStruct((t, d), F32), jax.ShapeDtypeStruct((1, 1), F32)),
        grid=(nt,), in_specs=[row, row], out_specs=(row, pl.BlockSpec((1, 1), lambda i: (0, 0))),
        scratch_shapes=[pltpu.VMEM((1, d), F32)], compiler_params=_cp(1), name="loss_grad",
    )(y, target)


def _row_tile(r, c):
    if r * c * 4 <= MIB:
        return r
    best = None
    for t in range(16, r, 16):
        if r % t == 0 and t * c * 4 <= MIB:
            best = t
    return r if best is None else best


def pair_sum(name, g4, theirs):
    nq, _, r, c = g4.shape
    tr = _tile(r, 1024)

    def body(g_ref, t_ref, o_ref):
        mine = g_ref[lax.axis_index("c")]
        o_ref[...] = (mine.astype(F32) + t_ref[...].astype(F32)).astype(o_ref.dtype)

    blk = pl.BlockSpec((None, tr, c), lambda q, i: (q, i, 0))
    return pl.pallas_call(
        body, out_shape=jax.ShapeDtypeStruct(theirs.shape, theirs.dtype), grid=(nq, r // tr),
        in_specs=[pl.BlockSpec((None, 2, tr, c), lambda q, i: (q, 0, i, 0)), blk], out_specs=blk,
        compiler_params=_cp(2), name=name,
    )(g4, theirs)


def adamw(name, w, m, v, pieces, comm=None):
    nl = len(pieces)
    npiece, r, c = pieces[0].shape
    tr = _row_tile(r, c)
    nr = r // tr
    bc1 = 1.0 - ADAM_B1 ** ADAM_STEP
    bc2 = 1.0 - ADAM_B2 ** ADAM_STEP

    def body(w_ref, m_ref, v_ref, *rest):
        p_refs, (g_ref, d_ref, nm_ref, nv_ref) = rest[:nl], rest[nl:]

        def update(p_ref):
            g = p_ref[0].astype(F32)
            for j in range(1, npiece):
                g = g + p_ref[j].astype(F32)
            m1 = ADAM_B1 * m_ref[...] + (1.0 - ADAM_B1) * g
            v1 = ADAM_B2 * v_ref[...] + (1.0 - ADAM_B2) * (g * g)
            m_hat = m1 / bc1
            v_hat = v1 / bc2
            g_ref[...] = g
            d_ref[...] = -ADAM_LR * (m_hat / (jnp.sqrt(v_hat) + ADAM_EPS) + ADAM_WD * w_ref[...])
            nm_ref[...] = m1
            nv_ref[...] = v1

        if nl == 1:
            update(p_refs[0])
        else:
            for ll in range(nl):
                pl.when(pl.program_id(0) == ll)(lambda ll=ll: update(p_refs[ll]))

    def piece_spec(ll):
        return pl.BlockSpec((npiece, tr, c), lambda l, i: (0, jnp.where(l == ll, i, jnp.where(l > ll, nr - 1, 0)), 0))

    row = pl.BlockSpec((tr, c), lambda l, i: (l * nr + i, 0))
    out = jax.ShapeDtypeStruct((nl * r, c), F32)
    return _call(
        name, body, grid=(nl, nr), in_specs=[row, row, row] + [piece_spec(ll) for ll in range(nl)],
        out_specs=[row] * 4, out_shape=[out] * 4, args=[w, m, v] + list(pieces), comm=comm,
    )


def _sigmoid(a):
    return 1.0 / (1.0 + jnp.exp(-a))


def ffn_fwd(name, h, wg, wu, wd, comm=None):
    t, d = h.shape
    ns, _, f = wg.shape
    tm = _tile(t, FFN_FWD_ROWS)

    def body(h_ref, wg_ref, wu_ref, wd_ref, a_ref, b_ref, hidt_ref, m_ref, acc):
        j = pl.program_id(1)
        hv = h_ref[...]
        a = _dot(hv, wg_ref[...])
        b = _dot(hv, wu_ref[...])
        hid32 = (a * _sigmoid(a)) * b
        hid = hid32.astype(BF16)
        a_ref[...] = a.astype(BF16)
        b_ref[...] = b.astype(BF16)
        hidt_ref[...] = hid32.T.astype(BF16)
        p = _dot(hid, wd_ref[...])

        @pl.when(j == 0)
        def _():
            acc[...] = p

        @pl.when(j > 0)
        def _():
            acc[...] += p

        @pl.when(j == ns - 1)
        def _():
            m_ref[...] = acc[...]

    w_in = pl.BlockSpec((None, d, f), lambda i, j: (j, 0, 0))
    act = pl.BlockSpec((None, tm, f), lambda i, j: (j, i, 0))
    act_shape = jax.ShapeDtypeStruct((ns, t, f), BF16)
    return _call(
        name, body, grid=(t // tm, ns),
        in_specs=[pl.BlockSpec((tm, d), lambda i, j: (i, 0)), w_in, w_in,
                  pl.BlockSpec((None, f, d), lambda i, j: (j, 0, 0))],
        out_specs=[act, act, pl.BlockSpec((None, f, tm), lambda i, j: (j, 0, i)),
                   pl.BlockSpec((tm, d), lambda i, j: (i, 0))],
        out_shape=[act_shape, act_shape, jax.ShapeDtypeStruct((ns, f, t), BF16), jax.ShapeDtypeStruct((t, d), F32)],
        args=[h, wg, wu, wd], scratch=[pltpu.VMEM((tm, d), F32)], comm=comm, vmem_mib=56,
    )


def ffn_bwd(name, dm, a, b, wg, wu, wd, comm=None):
    t, d = dm.shape
    ns, _, f = wg.shape
    tm = _tile(t, FFN_BWD_ROWS)

    def body(dm_ref, a_ref, b_ref, wg_ref, wu_ref, wd_ref, dat_ref, dbt_ref, dh_ref, acc):
        j = pl.program_id(1)
        dhid = _dot(dm_ref[...], wd_ref[...], "nt")
        av = a_ref[...].astype(F32)
        bv = b_ref[...].astype(F32)
        sig = _sigmoid(av)
        da32 = dhid * bv * (sig * (1.0 + av * (1.0 - sig)))
        db32 = dhid * (av * sig)
        dat_ref[...] = da32.T.astype(BF16)
        dbt_ref[...] = db32.T.astype(BF16)
        p = _dot(da32.astype(BF16), wg_ref[...], "nt") + _dot(db32.astype(BF16), wu_ref[...], "nt")

        @pl.when(j == 0)
        def _():
            acc[...] = p

        @pl.when(j > 0)
        def _():
            acc[...] += p

        @pl.when(j == ns - 1)
        def _():
            dh_ref[...] = acc[...]

    w_in = pl.BlockSpec((None, d, f), lambda i, j: (j, 0, 0))
    act = pl.BlockSpec((None, tm, f), lambda i, j: (j, i, 0))
    act_t = pl.BlockSpec((None, f, tm), lambda i, j: (j, 0, i))
    row = pl.BlockSpec((tm, d), lambda i, j: (i, 0))
    act_t_shape = jax.ShapeDtypeStruct((ns, f, t), BF16)
    return _call(
        name, body, grid=(t // tm, ns),
        in_specs=[row, act, act, w_in, w_in, pl.BlockSpec((None, f, d), lambda i, j: (j, 0, 0))],
        out_specs=[act_t, act_t, row],
        out_shape=[act_t_shape, act_t_shape, jax.ShapeDtypeStruct((t, d), F32)],
        args=[dm, a, b, wg, wu, wd], scratch=[pltpu.VMEM((tm, d), F32)], comm=comm, vmem_mib=56,
    )


def ffn_wgrad(name, act_t, x, comm=None):
    ns, f, t = act_t.shape
    d = x.shape[1]
    return _mm(
        name, act_t, x, dims="nn", grid=(ns,),
        a_spec=pl.BlockSpec((None, f, t), lambda j: (j, 0, 0)),
        b_spec=pl.BlockSpec((t, d), lambda j: (0, 0), pipeline_mode=pl.Buffered(1)),
        o_spec=pl.BlockSpec((None, f, d), lambda j: (j, 0, 0)),
        out_shape=jax.ShapeDtypeStruct((ns, f, d), BF16), vmem_mib=56, comm=comm,
    )


def rope_table(t):
    half = ROPE_DIM // 2
    inv = ROPE_THETA ** (-jnp.arange(half, dtype=F32) * 2.0 / ROPE_DIM)
    ang = jnp.arange(t, dtype=F32)[:, None] * inv[None, :]
    cos, sin = jnp.cos(ang), jnp.sin(ang)
    rest = HEAD_DIM - ROPE_DIM
    c = jnp.concatenate([cos, cos, jnp.ones((t, rest), F32)], axis=1)
    sm = jnp.concatenate([-sin, jnp.zeros((t, half + rest), F32)], axis=1)
    sp = jnp.concatenate([jnp.zeros((t, half), F32), sin, jnp.zeros((t, rest), F32)], axis=1)
    return jnp.concatenate([jnp.tile(c, (1, 2)), jnp.tile(sm, (1, 2)), jnp.tile(sp, (1, 2))], axis=1)


def _rope(x, tab, sign):
    w = x.shape[1]
    rep = w // LANES
    c, sm, sp = tab[:, 0:LANES], tab[:, LANES:2 * LANES], tab[:, 2 * LANES:3 * LANES]
    if rep > 1:
        c, sm, sp = jnp.tile(c, (1, rep)), jnp.tile(sm, (1, rep)), jnp.tile(sp, (1, rep))
    half = ROPE_DIM // 2
    return x * c + sign * (pltpu.roll(x, w - half, 1) * sm + pltpu.roll(x, half, 1) * sp)


def _attn_specs():
    prev = lambda n: jnp.maximum(n - 1, 0)
    kblk, vblk = ZK // LANES, ZK // LANES + 1
    return [
        pl.BlockSpec((BLOCK, ATTN_WIDTH), lambda n: (n, 0)),
        pl.BlockSpec((BLOCK, KV_WIDTH), lambda n: (n, kblk)),
        pl.BlockSpec((BLOCK, KV_WIDTH), lambda n: (prev(n), kblk)),
        pl.BlockSpec((BLOCK, KV_WIDTH), lambda n: (n, vblk)),
        pl.BlockSpec((BLOCK, KV_WIDTH), lambda n: (prev(n), vblk)),
        pl.BlockSpec((BLOCK, 3 * LANES), lambda n: (n, 0)),
        pl.BlockSpec((BLOCK, 3 * LANES), lambda n: (prev(n), 0)),
        pl.BlockSpec(memory_space=pltpu.SMEM),
    ]


def _attn_prologue(n, zq_ref, zk_ref, zkp_ref, zv_ref, zvp_ref, tab_ref, tabp_ref):
    q = (_rope(zq_ref[...], tab_ref[...], 1.0) * (HEAD_DIM ** -0.5)).astype(BF16)
    kcat = jnp.concatenate(
        [_rope(zkp_ref[...], tabp_ref[...], 1.0), _rope(zk_ref[...], tab_ref[...], 1.0)], axis=0).astype(BF16)
    vcat = jnp.concatenate([zvp_ref[...], zv_ref[...]], axis=0).astype(BF16)
    qi = lax.broadcasted_iota(jnp.int32, (BLOCK, 2 * BLOCK), 0)
    kj = lax.broadcasted_iota(jnp.int32, (BLOCK, 2 * BLOCK), 1)
    valid = (kj <= qi + BLOCK) & (kj > qi) & ((n > 0) | (kj >= BLOCK))
    return q, kcat, vcat, valid


def _attn_probs(qh, kh, valid, sink):
    s = jnp.where(valid, _dot(qh, kh, "nt"), -1e30)
    mx = jnp.maximum(jnp.max(s, axis=1, keepdims=True), sink)
    p = jnp.exp(s - mx)
    p_sink = jnp.exp(sink - mx)
    inv = 1.0 / (jnp.sum(p, axis=1, keepdims=True) + p_sink)
    return p * inv, p_sink * inv


def attn_fwd(z, tab, sinks, comm=None):
    t = z.shape[0]

    def body(zq_ref, zk_ref, zkp_ref, zv_ref, zvp_ref, tab_ref, tabp_ref, sink_ref, o_ref):
        n = pl.program_id(0)
        q, kcat, vcat, valid = _attn_prologue(n, zq_ref, zk_ref, zkp_ref, zv_ref, zvp_ref, tab_ref, tabp_ref)
        outs = []
        for h in range(N_Q_HEADS):
            kv = slice((h // Q_PER_KV) * HEAD_DIM, (h // Q_PER_KV + 1) * HEAD_DIM)
            p, _ = _attn_probs(q[:, h * HEAD_DIM:(h + 1) * HEAD_DIM], kcat[:, kv], valid, sink_ref[0, h])
            outs.append(_dot(p.astype(BF16), vcat[:, kv]))
        o_ref[...] = jnp.concatenate(outs, axis=1).astype(BF16)

    return _call(
        "attn_fwd", body, grid=(t // BLOCK,), in_specs=_attn_specs(),
        out_specs=[pl.BlockSpec((BLOCK, ATTN_WIDTH), lambda n: (n, 0))],
        out_shape=[jax.ShapeDtypeStruct((t, ATTN_WIDTH), BF16)],
        args=[z, z, z, z, z, tab, tab, sinks], comm=comm,
    )[0]


def attn_bwd(z, tab, sinks, dcat, comm=None):
    t = z.shape[0]
    nb = t // BLOCK

    def body(zq_ref, zk_ref, zkp_ref, zv_ref, zvp_ref, tab_ref, tabp_ref, sink_ref, do_ref,
             dq_ref, dkv_ref, dsink_ref):
        n = pl.program_id(0)
        q, kcat, vcat, valid = _attn_prologue(n, zq_ref, zk_ref, zkp_ref, zv_ref, zvp_ref, tab_ref, tabp_ref)
        do = do_ref[...]
        lane = lax.broadcasted_iota(jnp.int32, (1, LANES), 1)
        dqs, dks, dvs = [], [], []
        dsink = jnp.zeros((1, LANES), F32)
        for hk in range(N_Q_HEADS // Q_PER_KV):
            kv = slice(hk * HEAD_DIM, (hk + 1) * HEAD_DIM)
            kh, vh = kcat[:, kv], vcat[:, kv]
            dk = jnp.zeros((2 * BLOCK, HEAD_DIM), F32)
            dv = jnp.zeros((2 * BLOCK, HEAD_DIM), F32)
            for g in range(Q_PER_KV):
                h = hk * Q_PER_KV + g
                hs = slice(h * HEAD_DIM, (h + 1) * HEAD_DIM)
                qh, doh = q[:, hs], do[:, hs]
                p, p_sink = _attn_probs(qh, kh, valid, sink_ref[0, h])
                dv = dv + _dot(p.astype(BF16), doh, "tn")
                dp = _dot(doh, vh, "nt")
                rd = jnp.sum(p * dp, axis=1, keepdims=True)
                ds = (p * (dp - rd) * (HEAD_DIM ** -0.5)).astype(BF16)
                dqs.append(_dot(ds, kh))
                dk = dk + _dot(ds, qh, "tn") * (HEAD_DIM ** 0.5)
                dsink = dsink + jnp.where(lane == h, -jnp.sum(p_sink * rd, axis=0, keepdims=True), 0.0)
            dks.append(dk)
            dvs.append(dv)
        dq_ref[...] = _rope(jnp.concatenate(dqs, axis=1), tab_ref[...], -1.0).astype(BF16)
        dkc = jnp.concatenate(dks, axis=1)
        dk_pre = jnp.concatenate(
            [_rope(dkc[:BLOCK], tabp_ref[...], -1.0), _rope(dkc[BLOCK:], tab_ref[...], -1.0)], axis=0)
        dkv_ref[...] = jnp.concatenate([dk_pre, jnp.concatenate(dvs, axis=1)], axis=1)

        @pl.when(n == 0)
        def _():
            dsink_ref[...] = jnp.zeros_like(dsink_ref)

        dsink_ref[...] += dsink

    return _call(
        "attn_bwd", body, grid=(nb,),
        in_specs=_attn_specs() + [pl.BlockSpec((BLOCK, ATTN_WIDTH), lambda n: (n, 0))],
        out_specs=[pl.BlockSpec((BLOCK, ATTN_WIDTH), lambda n: (n, 0)),
                   pl.BlockSpec((None, 2 * BLOCK, 2 * KV_WIDTH), lambda n: (n, 0, 0)),
                   pl.BlockSpec((1, LANES), lambda n: (0, 0))],
        out_shape=[jax.ShapeDtypeStruct((t, ATTN_WIDTH), BF16),
                   jax.ShapeDtypeStruct((nb, 2 * BLOCK, 2 * KV_WIDTH), F32),
                   jax.ShapeDtypeStruct((1, LANES), F32)],
        args=[z, z, z, z, z, tab, tab, sinks, dcat], comm=comm,
    )


def _gelu(x):
    k = math.sqrt(2.0 / math.pi)
    th = jnp.tanh(k * (x + 0.044715 * (x * x * x)))
    return 0.5 * x * (1.0 + th), th


def _gelu_grad(x, th):
    k = math.sqrt(2.0 / math.pi)
    return 0.5 * (1.0 + th) + 0.5 * x * (1.0 - th * th) * (k * (1.0 + 3.0 * 0.044715 * (x * x)))


def _sgu_core(zu_ref, zv_ref, lng_ref, lnb_ref, w_ref, bt_ref):
    up, vp = zu_ref[...], zv_ref[...]
    u, thu = _gelu(up)
    v, thv = _gelu(vp)
    mu = jnp.mean(v, axis=-1, keepdims=True)
    vc = v - mu
    rstd = lax.rsqrt(jnp.mean(vc * vc, axis=-1, keepdims=True) + RMS_EPS)
    xhat = vc * rstd
    vn = (xhat * lng_ref[...] + lnb_ref[...]).astype(BF16)
    row = lax.broadcasted_iota(jnp.int32, (CHUNK, CHUNK), 0)
    col = lax.broadcasted_iota(jnp.int32, (CHUNK, CHUNK), 1)
    mixed = []
    for g in range(SGU_GROUPS):
        wc = jnp.where(row >= col, w_ref[g], 0.0).astype(BF16)
        mixed.append(_dot(wc, vn[:, g * CHUNK:(g + 1) * CHUNK]) + bt_ref[:, g:g + 1])
    return up, vp, u, thu, thv, rstd, xhat, vn, jnp.concatenate(mixed, axis=1)


def _sgu_specs():
    full = lambda shape: pl.BlockSpec(shape, lambda n: (0,) * len(shape))
    return [
        pl.BlockSpec((CHUNK, SGU_WIDTH), lambda n: (n, ZU // SGU_WIDTH)),
        pl.BlockSpec((CHUNK, SGU_WIDTH), lambda n: (n, ZV // SGU_WIDTH)),
        full((1, SGU_WIDTH)), full((1, SGU_WIDTH)),
        full((SGU_GROUPS, CHUNK, CHUNK)), full((CHUNK, SGU_GROUPS)),
    ]


def sgu_fwd(z, ln_g, ln_b, w, b_t):
    t = z.shape[0]

    def body(zu_ref, zv_ref, lng_ref, lnb_ref, w_ref, bt_ref, o_ref):
        _, _, u, _, _, _, _, _, mixed = _sgu_core(zu_ref, zv_ref, lng_ref, lnb_ref, w_ref, bt_ref)
        o_ref[...] = (u * mixed).astype(BF16)

    return pl.pallas_call(
        body, out_shape=jax.ShapeDtypeStruct((t, SGU_WIDTH), BF16), grid=(t // CHUNK,),
        in_specs=_sgu_specs(), out_specs=pl.BlockSpec((CHUNK, SGU_WIDTH), lambda n: (n, 0)),
        compiler_params=_cp(1), name="sgu_fwd",
    )(z, z, ln_g, ln_b, w, b_t)


def sgu_bwd(z, ln_g, ln_b, w, w_t, b_t, dcat):
    t = z.shape[0]

    def body(zu_ref, zv_ref, lng_ref, lnb_ref, w_ref, bt_ref, wt_ref, dg_ref,
             du_ref, dv_ref, dw_ref, dbt_ref, dlng_ref, dlnb_ref):
        up, vp, u, thu, thv, rstd, xhat, vn, mixed = _sgu_core(zu_ref, zv_ref, lng_ref, lnb_ref, w_ref, bt_ref)
        dgate = dg_ref[...].astype(F32)
        du_ref[...] = (dgate * mixed * _gelu_grad(up, thu)).astype(BF16)
        dmixed = dgate * u
        row = lax.broadcasted_iota(jnp.int32, (CHUNK, CHUNK), 0)
        col = lax.broadcasted_iota(jnp.int32, (CHUNK, CHUNK), 1)

        @pl.when(pl.program_id(0) == 0)
        def _():
            dw_ref[...] = jnp.zeros_like(dw_ref)
            dbt_ref[...] = jnp.zeros_like(dbt_ref)
            dlng_ref[...] = jnp.zeros_like(dlng_ref)
            dlnb_ref[...] = jnp.zeros_like(dlnb_ref)

        dvn, dbt = [], jnp.zeros((CHUNK, LANES), F32)
        for g in range(SGU_GROUPS):
            gs = slice(g * CHUNK, (g + 1) * CHUNK)
            dmx = dmixed[:, gs]
            dmxb = dmx.astype(BF16)
            dbt = dbt + jnp.where(col == g, jnp.sum(dmx, axis=1, keepdims=True), 0.0)
            dw_ref[g] += jnp.where(row >= col, _dot(dmxb, vn[:, gs], "nt"), 0.0)
            wtc = jnp.where(col >= row, wt_ref[g], 0.0).astype(BF16)
            dvn.append(_dot(wtc, dmxb))
        dbt_ref[...] += dbt
        dvn = jnp.concatenate(dvn, axis=1)
        dlnb_ref[...] += jnp.sum(dvn, axis=0, keepdims=True)
        dlng_ref[...] += jnp.sum(dvn * xhat, axis=0, keepdims=True)
        dxh = dvn * lng_ref[...]
        dv = rstd * (dxh - jnp.mean(dxh, axis=-1, keepdims=True) - xhat * jnp.mean(dxh * xhat, axis=-1, keepdims=True))
        dv_ref[...] = (dv * _gelu_grad(vp, thv)).astype(BF16)

    full = lambda shape: pl.BlockSpec(shape, lambda n: (0,) * len(shape))
    act = pl.BlockSpec((CHUNK, SGU_WIDTH), lambda n: (n, 0))
    act_shape = jax.ShapeDtypeStruct((t, SGU_WIDTH), BF16)
    vec = jax.ShapeDtypeStruct((1, SGU_WIDTH), F32)
    return pl.pallas_call(
        body,
        out_shape=(act_shape, act_shape, jax.ShapeDtypeStruct((SGU_GROUPS, CHUNK, CHUNK), F32),
                   jax.ShapeDtypeStruct((CHUNK, LANES), F32), vec, vec),
        grid=(t // CHUNK,),
        in_specs=_sgu_specs() + [full((SGU_GROUPS, CHUNK, CHUNK)),
                                 pl.BlockSpec((CHUNK, SGU_WIDTH), lambda n: (n, 1))],
        out_specs=(act, act, full((SGU_GROUPS, CHUNK, CHUNK)), full((CHUNK, LANES)),
                   full((1, SGU_WIDTH)), full((1, SGU_WIDTH))),
        compiler_params=_cp(1), name="sgu_bwd",
    )(z, z, ln_g, ln_b, w, b_t, w_t, dcat)


def dz_assemble(dq, dkv, du, dv):
    t = dq.shape[0]
    nb = t // BLOCK

    def body(dq_ref, cur_ref, nxt_ref, du_ref, dv_ref, o_ref):
        n = pl.program_id(0)
        o_ref[:, ZQ:ZQ + ATTN_WIDTH] = dq_ref[...]
        o_ref[:, ZU:ZU + SGU_WIDTH] = du_ref[...]
        o_ref[:, ZV:ZV + SGU_WIDTH] = dv_ref[...]
        kv = cur_ref[BLOCK:, :] + jnp.where(n < nb - 1, nxt_ref[:BLOCK, :], 0.0)
        o_ref[:, ZK:ZK + 2 * KV_WIDTH] = kv.astype(BF16)

    act = pl.BlockSpec((BLOCK, ATTN_WIDTH), lambda n: (n, 0))
    return pl.pallas_call(
        body, out_shape=jax.ShapeDtypeStruct((t, IN_WIDTH), BF16), grid=(nb,),
        in_specs=[act,
                  pl.BlockSpec((None, 2 * BLOCK, 2 * KV_WIDTH), lambda n: (n, 0, 0)),
                  pl.BlockSpec((None, 2 * BLOCK, 2 * KV_WIDTH), lambda n: (jnp.minimum(n + 1, nb - 1), 0, 0)),
                  act, act],
        out_specs=pl.BlockSpec((BLOCK, IN_WIDTH), lambda n: (n, 0)),
        compiler_params=_cp(1), name="dz_assemble",
    )(dq, dkv, dkv, du, dv)


def _pool_count(i, tp, w):
    t_idx = i * tp + lax.broadcasted_iota(jnp.int32, (tp, 1), 0)
    return jnp.minimum(t_idx + 1, w).astype(F32)


def pool_fwd(h, pw, pscale):
    t, d = h.shape
    tp = _tile(t, 256)
    per = tp // POOL_HALO

    def body(h_ref, halo_ref, pw_ref, ps_ref, m_ref, pooled_ref):
        i = pl.program_id(0)
        cur = h_ref[...]
        ext = jnp.concatenate([jnp.where(i > 0, halo_ref[...], 0.0), cur], axis=0)
        ys, pooled = [], []
        for gi, w in enumerate(POOL_WINDOWS):
            gs = slice(gi * POOL_GROUP_DIM, (gi + 1) * POOL_GROUP_DIM)
            s = ext[:, gs]
            sh = 1
            while sh < w:
                s = s + pltpu.roll(s, sh, 0)
                sh *= 2
            pg = (s[POOL_HALO:, :] / _pool_count(i, tp, w) - cur[:, gs]).astype(BF16)
            pooled.append(pg)
            ys.append(_dot(pg, pw_ref[gi]))
        pooled_ref[...] = jnp.concatenate(pooled, axis=1)
        m_ref[...] = jnp.concatenate(ys, axis=1) * ps_ref[...]

    row = pl.BlockSpec((tp, d), lambda i: (i, 0))
    return pl.pallas_call(
        body,
        out_shape=(jax.ShapeDtypeStruct((t, d), F32), jax.ShapeDtypeStruct((t, d), BF16)),
        grid=(t // tp,),
        in_specs=[row, pl.BlockSpec((POOL_HALO, d), lambda i: (jnp.maximum(i * per - 1, 0), 0)),
                  pl.BlockSpec(pw.shape, lambda i: (0, 0, 0)), pl.BlockSpec((1, d), lambda i: (0, 0))],
        out_specs=(row, row), compiler_params=_cp(1), name="pool_fwd",
    )(h, h, pw, pscale)


def pool_bwd_proj(dm, pooled, pw, pscale):
    t, d = dm.shape
    tp = _tile(t, 256)

    def body(dm_ref, pooled_ref, pw_ref, ps_ref, dp_ref, dy_ref, dps_ref):
        dmv = dm_ref[...]
        dy = (dmv * ps_ref[...]).astype(BF16)
        dy_ref[...] = dy
        ys, dps = [], []
        for gi in range(len(POOL_WINDOWS)):
            gs = slice(gi * POOL_GROUP_DIM, (gi + 1) * POOL_GROUP_DIM)
            ys.append(_dot(pooled_ref[:, gs], pw_ref[gi]))
            dps.append(_dot(dy[:, gs], pw_ref[gi], "nt"))
        dp_ref[...] = jnp.concatenate(dps, axis=1)

        @pl.when(pl.program_id(0) == 0)
        def _():
            dps_ref[...] = jnp.zeros_like(dps_ref)

        dps_ref[...] += jnp.sum(dmv * jnp.concatenate(ys, axis=1), axis=0, keepdims=True)

    row = pl.BlockSpec((tp, d), lambda i: (i, 0))
    vec = pl.BlockSpec((1, d), lambda i: (0, 0))
    return pl.pallas_call(
        body,
        out_shape=(jax.ShapeDtypeStruct((t, d), F32), jax.ShapeDtypeStruct((t, d), BF16),
                   jax.ShapeDtypeStruct((1, d), F32)),
        grid=(t // tp,),
        in_specs=[row, row, pl.BlockSpec(pw.shape, lambda i: (0, 0, 0)), vec],
        out_specs=(row, row, vec), compiler_params=_cp(1), name="pool_bwd_proj",
    )(dm, pooled, pw, pscale)


def pool_bwd_window(dp):
    t, d = dp.shape
    tp = _tile(t, 256)
    per = tp // POOL_HALO
    last = t // POOL_HALO - 1
    nt = t // tp

    def body(dp_ref, halo_ref, dh_ref):
        i = pl.program_id(0)
        cur = dp_ref[...]
        halo = jnp.where(i < nt - 1, halo_ref[...], 0.0)
        outs = []
        for gi, w in enumerate(POOL_WINDOWS):
            gs = slice(gi * POOL_GROUP_DIM, (gi + 1) * POOL_GROUP_DIM)
            s = jnp.concatenate([cur[:, gs] / _pool_count(i, tp, w), halo[:, gs] / float(w)], axis=0)
            sh = 1
            while sh < w:
                s = s + pltpu.roll(s, tp + POOL_HALO - sh, 0)
                sh *= 2
            outs.append(s[:tp, :] - cur[:, gs])
        dh_ref[...] = jnp.concatenate(outs, axis=1)

    row = pl.BlockSpec((tp, d), lambda i: (i, 0))
    return pl.pallas_call(
        body, out_shape=jax.ShapeDtypeStruct((t, d), F32), grid=(nt,),
        in_specs=[row, pl.BlockSpec((POOL_HALO, d), lambda i: (jnp.minimum((i + 1) * per, last), 0))],
        out_specs=row, compiler_params=_cp(1), name="pool_bwd_window",
    )(dp, dp)


def pool_wgrad(pooled, dy):
    t, d = pooled.shape
    ng = d // POOL_GROUP_DIM
    tk = _tile(t, 512)
    blk = pl.BlockSpec((tk, POOL_GROUP_DIM), lambda g, k: (k, g))
    return _mm(
        "pool_wgrad", pooled, dy, dims="tn", grid=(ng, t // tk), a_spec=blk, b_spec=blk,
        o_spec=pl.BlockSpec((None, POOL_GROUP_DIM, POOL_GROUP_DIM), lambda g, k: (g, 0, 0)),
        out_shape=jax.ShapeDtypeStruct((ng, POOL_GROUP_DIM, POOL_GROUP_DIM), F32),
        acc_shape=(POOL_GROUP_DIM, POOL_GROUP_DIM), nk=t // tk,
    )


def _xattn_probs(qh, kh):
    s = _dot(qh, kh, "nt") * (X_HEAD_DIM ** -0.5)
    p = jnp.exp(s - jnp.max(s, axis=1, keepdims=True))
    return p * (1.0 / jnp.sum(p, axis=1, keepdims=True))


def xattn_fwd(name, q, k, v):
    t, xw = q.shape
    tm = _tile(t, 512)

    def body(q_ref, k_ref, v_ref, o_ref):
        outs = []
        for h in range(X_HEADS):
            hs = slice(h * X_HEAD_DIM, (h + 1) * X_HEAD_DIM)
            p = _xattn_probs(q_ref[:, hs], k_ref[:, hs])
            outs.append(_dot(p.astype(BF16), v_ref[:, hs]))
        o_ref[...] = jnp.concatenate(outs, axis=1).astype(BF16)

    row = pl.BlockSpec((tm, xw), lambda i: (i, 0))
    kv = pl.BlockSpec(k.shape, lambda i: (0, 0))
    return pl.pallas_call(
        body, out_shape=jax.ShapeDtypeStruct((t, xw), BF16), grid=(t // tm,),
        in_specs=[row, kv, kv], out_specs=row, compiler_params=_cp(1), name=name,
    )(q, k, v)


def xattn_bwd(name, q, k, v, do):
    t, xw = q.shape
    tm = _tile(t, 512)

    def body(q_ref, k_ref, v_ref, do_ref, dq_ref, dk_ref, dv_ref):
        @pl.when(pl.program_id(0) == 0)
        def _():
            dk_ref[...] = jnp.zeros_like(dk_ref)
            dv_ref[...] = jnp.zeros_like(dv_ref)

        dqs = []
        for h in range(X_HEADS):
            hs = slice(h * X_HEAD_DIM, (h + 1) * X_HEAD_DIM)
            qh, kh, vh, doh = q_ref[:, hs], k_ref[:, hs], v_ref[:, hs], do_ref[:, hs]
            p = _xattn_probs(qh, kh)
            dv_ref[:, hs] += _dot(p.astype(BF16), doh, "tn")
            dp = _dot(doh, vh, "nt")
            ds = (p * (dp - jnp.sum(p * dp, axis=1, keepdims=True)) * (X_HEAD_DIM ** -0.5)).astype(BF16)
            dqs.append(_dot(ds, kh))
            dk_ref[:, hs] += _dot(ds, qh, "tn")
        dq_ref[...] = jnp.concatenate(dqs, axis=1).astype(BF16)

    row = pl.BlockSpec((tm, xw), lambda i: (i, 0))
    kv = pl.BlockSpec(k.shape, lambda i: (0, 0))
    kv_shape = jax.ShapeDtypeStruct(k.shape, F32)
    return pl.pallas_call(
        body, out_shape=(jax.ShapeDtypeStruct((t, xw), BF16), kv_shape, kv_shape), grid=(t // tm,),
        in_specs=[row, kv, kv, row], out_specs=(row, kv, kv), compiler_params=_cp(1), name=name,
    )(q, k, v, do)


def kernel(x, mem, norms, mem_norm, ffn1_wg, ffn1_wu, ffn1_wd, ffn2_wg, ffn2_wu, ffn2_wd, x_wq, x_wk, x_wv, x_wo, mix_w_in, mix_w_out, attn_sinks, sgu_ln_g, sgu_ln_b, sgu_w, sgu_b, pool_w, pool_scale, loss_target, m_norms, m_mem_norm, m_ffn1_wg, m_ffn1_wu, m_ffn1_wd, m_ffn2_wg, m_ffn2_wu, m_ffn2_wd, m_x_wq, m_x_wk, m_x_wv, m_x_wo, m_mix_w_in, m_mix_w_out, m_attn_sinks, m_sgu_ln_g, m_sgu_ln_b, m_sgu_w, m_sgu_b, m_pool_w, m_pool_scale, v_norms, v_mem_norm, v_ffn1_wg, v_ffn1_wu, v_ffn1_wd, v_ffn2_wg, v_ffn2_wu, v_ffn2_wd, v_x_wq, v_x_wk, v_x_wv, v_x_wo, v_mix_w_in, v_mix_w_out, v_attn_sinks, v_sgu_ln_g, v_sgu_ln_b, v_sgu_w, v_sgu_b, v_pool_w, v_pool_scale):
    params = dict(norms=norms, mem_norm=mem_norm, ffn1_wg=ffn1_wg, ffn1_wu=ffn1_wu, ffn1_wd=ffn1_wd,
                  ffn2_wg=ffn2_wg, ffn2_wu=ffn2_wu, ffn2_wd=ffn2_wd, x_wq=x_wq, x_wk=x_wk, x_wv=x_wv, x_wo=x_wo,
                  mix_w_in=mix_w_in, mix_w_out=mix_w_out, attn_sinks=attn_sinks, sgu_ln_g=sgu_ln_g,
                  sgu_ln_b=sgu_ln_b, sgu_w=sgu_w, sgu_b=sgu_b, pool_w=pool_w, pool_scale=pool_scale)
    mom1 = dict(norms=m_norms, mem_norm=m_mem_norm, ffn1_wg=m_ffn1_wg, ffn1_wu=m_ffn1_wu, ffn1_wd=m_ffn1_wd,
                ffn2_wg=m_ffn2_wg, ffn2_wu=m_ffn2_wu, ffn2_wd=m_ffn2_wd, x_wq=m_x_wq, x_wk=m_x_wk, x_wv=m_x_wv,
                x_wo=m_x_wo, mix_w_in=m_mix_w_in, mix_w_out=m_mix_w_out, attn_sinks=m_attn_sinks,
                sgu_ln_g=m_sgu_ln_g, sgu_ln_b=m_sgu_ln_b, sgu_w=m_sgu_w, sgu_b=m_sgu_b, pool_w=m_pool_w,
                pool_scale=m_pool_scale)
    mom2 = dict(norms=v_norms, mem_norm=v_mem_norm, ffn1_wg=v_ffn1_wg, ffn1_wu=v_ffn1_wu, ffn1_wd=v_ffn1_wd,
                ffn2_wg=v_ffn2_wg, ffn2_wu=v_ffn2_wu, ffn2_wd=v_ffn2_wd, x_wq=v_x_wq, x_wk=v_x_wk, x_wv=v_x_wv,
                x_wo=v_x_wo, mix_w_in=v_mix_w_in, mix_w_out=v_mix_w_out, attn_sinks=v_attn_sinks,
                sgu_ln_g=v_sgu_ln_g, sgu_ln_b=v_sgu_ln_b, sgu_w=v_sgu_w, sgu_b=v_sgu_b, pool_w=v_pool_w,
                pool_scale=v_pool_scale)
    order = list(params)

    xs, memb, target = x[0], mem[0], loss_target[0]
    t, d = xs.shape
    depth = norms.shape[0]
    dsh = d // NDEV

    bf = lambda a: a.astype(BF16)
    wts = {}

    def gather_job(keys):
        return _Gather([bf(params[name][l]) for name, l in keys]), keys

    def land(job_keys):
        job, keys = job_keys
        for key, a in zip(keys, job.result):
            wts[key] = a

    ffn_keys = lambda tag, l: [(f"{tag}_wg", l), (f"{tag}_wu", l), (f"{tag}_wd", l)]
    x_keys = lambda l: [("x_wq", l), ("x_wk", l), ("x_wv", l), ("x_wo", l)]
    small_shapes = [norms.shape, pool_scale.shape, pool_w.shape]
    head = gather_job(ffn_keys("ffn1", 0))
    head_small = _Gather([_pack([norms, pool_scale, pool_w])])
    _run_exchange("gather_head", _Multi([head_small, head[0]]))
    land(head)
    norms_sh, pscale_sh, pw_sh = _unpack(head_small.result[0], small_shapes, (NDEV,))
    norms_full = norms_sh.transpose(1, 2, 0, 3).reshape(depth, norms.shape[1], d)
    pscale_full = pscale_sh.transpose(1, 0, 2).reshape(1, d)
    pw_full = pw_sh[:, 0].transpose(1, 0, 2, 3).reshape(len(POOL_WINDOWS), POOL_GROUP_DIM, POOL_GROUP_DIM).astype(BF16)
    fwd_jobs = {
        ("ffn1", 0): gather_job([("mix_w_in", 0), ("mix_w_out", 0)] + x_keys(0) + ffn_keys("ffn2", 0)[:2]),
        ("attn", 0): gather_job(ffn_keys("ffn2", 0)[2:]),
        ("ffn2", 0): gather_job(ffn_keys("ffn1", 1)),
        ("ffn1", 1): gather_job(x_keys(1) + ffn_keys("ffn2", 1)),
    }

    tab = rope_table(t)
    sgu_w0 = sgu_w[0]
    sgu_wt0 = sgu_w0.transpose(0, 2, 1)
    sgu_bt0 = sgu_b[0].T
    gain = lambda l, i: norms_full[l, i][None, :]

    saved = []
    xc = xs
    for l in range(depth):
        s = {}

        def ffn_forward(tag, xc, gi, l=l, s=s):
            h = s[tag + "_h"] = norm_fwd(f"norm_{tag}_{l}", xc, gain(l, gi), BF16)
            job = fwd_jobs.get((tag, l))
            s[tag + "_a"], s[tag + "_b"], s[tag + "_hidt"], s[tag + "_m"] = ffn_fwd(
                f"{tag}_fwd_{l}", h, wts[tag + "_wg", l], wts[tag + "_wu", l], wts[tag + "_wd", l],
                comm=job and job[0])
            if job:
                land(job)
            return resid_norm_fwd(f"resid_{tag}_{l}", xc, s[tag + "_m"], gain(l, gi + 1), 0.5)

        s["x0"] = xc
        xc = ffn_forward("ffn1", xc, 0)

        s["x1"] = xc
        if l % 2 == 0:
            w_in = wts["mix_w_in", l].transpose(1, 0, 2).reshape(d, IN_WIDTH)
            o_k, o_u = ATTN_WIDTH, ATTN_WIDTH + 2 * KV_WIDTH
            w_in = jnp.concatenate([w_in[:, :o_k], w_in[:, o_u:], w_in[:, o_k:o_u]], axis=1)
            w_out = wts["mix_w_out", l].reshape(d, d)
            h2, s["h2t"] = norm_fwd(f"norm_mix_{l}", xc, gain(l, 2), BF16, with_t=True)
            s["z"] = mm_nn("mix_in", h2, w_in, F32, tn=IN_WIDTH // 2)
            job = fwd_jobs[("attn", l)]
            attn = attn_fwd(s["z"], tab, attn_sinks, comm=job[0])
            land(job)
            gate = sgu_fwd(s["z"], sgu_ln_g, sgu_ln_b, sgu_w0, sgu_bt0)
            s["cat"] = jnp.concatenate([attn, gate], axis=1)
            s["m2"] = mm_nn("mix_out", s["cat"], w_out, F32)
        else:
            h2 = norm_fwd(f"norm_mix_{l}", xc, gain(l, 2), F32)
            s["m2"], s["pooled"] = pool_fwd(h2, pw_full, pscale_full)
        xc = resid_norm_fwd(f"resid_mix_{l}", xc, s["m2"], gain(l, 3), 1.0)

        s["x2"] = xc
        wq, wk, wv = (wts[k, l].reshape(d, -1) for k in ("x_wq", "x_wk", "x_wv"))
        s["wq"], s["wkv"] = wq, jnp.concatenate([wk, wv], axis=1)
        h3, s["h3t"] = norm_fwd(f"norm_x_{l}", xc, gain(l, 4), BF16, with_t=True)
        s["mem_n"] = norm_fwd(f"norm_mem_{l}", memb, mem_norm[l][None, :], BF16)
        s["q"] = mm_nn(f"x_q_{l}", h3, wq, BF16)
        s["k"] = mm_nn(f"x_k_{l}", s["mem_n"], wk, BF16)
        s["v"] = mm_nn(f"x_v_{l}", s["mem_n"], wv, BF16)
        s["o"] = xattn_fwd(f"xattn_fwd_{l}", s["q"], s["k"], s["v"])
        s["wo"] = wts["x_wo", l].transpose(1, 0, 2).reshape(-1, d)
        s["m3"] = mm_nn(f"x_o_{l}", s["o"], s["wo"], F32)
        xc = resid_norm_fwd(f"resid_x_{l}", xc, s["m3"], gain(l, 5), 1.0)

        s["x3"] = xc
        xc = ffn_forward("ffn2", xc, 6)
        saved.append(s)

    dx, loss11 = loss_grad(xc, target)
    loss = lax.psum(loss11[0, 0], ("x", "y", "c"))

    swaps = []
    pending = []
    recv = {}

    def emit_units(name, l, arr, among_chips):
        piece_mib = math.prod(arr.shape[1:]) * arr.dtype.itemsize / MIB
        parts = 2 if piece_mib > 1.5 else 1
        rows = arr.shape[1] // parts
        cost = (LINK_US_PER_MIB_CHIPS if among_chips else LINK_US_PER_MIB_ALL) * piece_mib / parts
        for part in range(parts):
            pending.append(((name, l, part), among_chips, (arr, part * rows, rows), cost))

    def emit(name, l, arr, two_level=False):
        if two_level:
            swaps.append((name, l, arr.reshape((NCHIP, 2) + arr.shape[1:])))
        else:
            emit_units(name, l, arr, False)

    def hosted(budget_us, fn, *args, extra=(), force=True, **kw):
        jobs = list(extra)
        swapped = swaps[:]
        del swaps[:]
        used = PAIR_SWAP_US * len(swapped)
        if swapped:
            jobs.append(_PairSwap([g for _, _, g in swapped]))
        items, kept = [], []
        for it in pending:
            if (force and not items and not swapped) or used + it[3] <= budget_us:
                items.append(it)
                used += it[3]
            else:
                kept.append(it)
        pending[:] = kept
        groups = [[it for it in items if it[1] == flag] for flag in (False, True)]
        unit_jobs = [cls([it[2] for it in grp]) if grp else None
                     for cls, grp in zip((_Scatter, _ChipScatter), groups)]
        jobs += [j for j in unit_jobs if j is not None]
        res = fn(*args, comm=_Multi(jobs) if jobs else None, **kw)
        for job, grp in zip(unit_jobs, groups):
            for it, a in zip(grp, job.result if job else ()):
                recv[it[0]] = a
        if swapped:
            for (name, l, g4), theirs in zip(swapped, jobs[len(extra)].result):
                emit_units(name, l, pair_sum(f"pairsum_{name}_{l}", g4, theirs), True)
        return res

    grads = {k: [None] * depth for k in ("norms", "mem_norm")}
    small_jobs = []
    for l in reversed(range(depth)):
        s = saved[l]
        dg = [None] * 8

        def ffn_block(tag, dx, x_in, gi, extra=(), returns_dx=False, l=l, s=s, dg=dg):
            dm, dg[gi + 1] = norm_bwd(f"{tag}_post_bwd_{l}", s[tag + "_m"], gain(l, gi + 1), dx, 0.5, None, BF16)
            da_t, db_t, dh = hosted(HOST_US_FFN_BWD, ffn_bwd, f"{tag}_bwd_{l}", dm, s[tag + "_a"], s[tag + "_b"],
                                    wts[tag + "_wg", l], wts[tag + "_wu", l], wts[tag + "_wd", l], extra=extra)
            emit(tag + "_wg", l, hosted(HOST_US_WGRAD, ffn_wgrad, f"{tag}_dwg_{l}", da_t, s[tag + "_h"]), True)
            emit(tag + "_wu", l, hosted(HOST_US_WGRAD, ffn_wgrad, f"{tag}_dwu_{l}", db_t, s[tag + "_h"]), True)
            emit(tag + "_wd", l, hosted(HOST_US_WGRAD, ffn_wgrad, f"{tag}_dwd_{l}", s[tag + "_hidt"], dm), True)
            pre = (f"{tag}_pre_bwd_{l}", x_in, gain(l, gi), dh, 1.0, dx, F32)
            dx, dg[gi] = norm_bwd(*pre) if returns_dx else hosted(HOST_US_SMALL, norm_bwd, *pre, force=False)
            return dx

        dx = ffn_block("ffn2", dx, s["x3"], 6)

        dm, dg[5] = norm_bwd(f"x_post_bwd_{l}", s["m3"], gain(l, 5), dx, 1.0, None, BF16)
        do = mm_nt(f"x_do_{l}", dm, s["wo"], BF16)
        g_wo = mm_tn(f"x_dwo_{l}", s["o"], dm, BF16, tmo=s["o"].shape[1])
        emit("x_wo", l, g_wo.reshape(-1, NDEV, dsh).transpose(1, 0, 2))
        dq, dk, dv = xattn_bwd(f"xattn_bwd_{l}", s["q"], s["k"], s["v"], do)
        dkb, dvb = dk.astype(BF16), dv.astype(BF16)
        emit("x_wq", l, mm_kred(f"x_dwq_{l}", s["h3t"], dq, BF16).reshape(NDEV, dsh, -1))
        emit("x_wk", l, mm_tn(f"x_dwk_{l}", s["mem_n"], dkb, BF16).reshape(NDEV, dsh, -1))
        emit("x_wv", l, mm_tn(f"x_dwv_{l}", s["mem_n"], dvb, BF16).reshape(NDEV, dsh, -1))
        dh = mm_nt(f"x_dh_{l}", dq, s["wq"], F32)
        dmem_n = mm_nt(f"x_dmem_{l}", jnp.concatenate([dkb, dvb], axis=1), s["wkv"], F32)
        _, grads["mem_norm"][l] = norm_bwd(f"mem_norm_bwd_{l}", memb, mem_norm[l][None, :], dmem_n, 1.0, None, F32)
        dx, dg[4] = norm_bwd(f"x_pre_bwd_{l}", s["x2"], gain(l, 4), dh, 1.0, dx, F32)

        if l % 2 == 0:
            dm, dg[3] = norm_bwd(f"mix_post_bwd_{l}", s["m2"], gain(l, 3), dx, 1.0, None, BF16)
            dcat = mm_nt("mix_dcat", dm, w_out, BF16)
            emit("mix_w_out", l, mm_tn("mix_dwout", s["cat"], dm, BF16, tno=d // 2).reshape(NDEV, dsh, d))
            dq_a, dkv_a, dsink = hosted(HOST_US_ATTN_BWD, attn_bwd, s["z"], tab, attn_sinks, dcat, force=False)
            du_s, dv_s, g_sgu_w, g_sgu_bt, g_ln_g, g_ln_b = sgu_bwd(
                s["z"], sgu_ln_g, sgu_ln_b, sgu_w0, sgu_wt0, sgu_bt0, dcat)
            dz = dz_assemble(dq_a, dkv_a, du_s, dv_s)
            dh = mm_nt("mix_dh", dz, w_in, F32, tn=d // 2)
            g_win = hosted(2 * HOST_US_SMALL, mm_kred, "mix_dwin", s["h2t"], dz, BF16, tno=IN_WIDTH // 2, force=False)
            g_win = jnp.concatenate([g_win[:, :ATTN_WIDTH], g_win[:, ZK:], g_win[:, ZU:ZK]], axis=1)
            emit("mix_w_in", l, g_win.reshape(d, NDEV, -1).transpose(1, 0, 2))
        else:
            dm, dg[3] = norm_bwd(f"mix_post_bwd_{l}", s["m2"], gain(l, 3), dx, 1.0, None, F32)
            dp, dy, g_pscale = pool_bwd_proj(dm, s["pooled"], pw_full, pscale_full)
            g_pw = pool_wgrad(s["pooled"], dy)
            emit("pool_w", 0, g_pw.reshape(len(POOL_WINDOWS), NDEV, -1, POOL_GROUP_DIM).transpose(1, 0, 2, 3)
                 .reshape(NDEV, -1, POOL_GROUP_DIM))
            dh = pool_bwd_window(dp)
        dx, dg[2] = norm_bwd(f"mix_pre_bwd_{l}", s["x1"], gain(l, 2), dh, 1.0, dx, F32)

        if l == 0:
            replicated = ["mem_norm", "attn_sinks", "sgu_ln_g", "sgu_ln_b", "sgu_w", "sgu_b"]
            rep_grads = [jnp.concatenate(grads["mem_norm"], axis=0), dsink[:, :N_Q_HEADS], g_ln_g, g_ln_b,
                         g_sgu_w[None], g_sgu_bt[:, :SGU_GROUPS].T[None]]
            small_jobs.append(_Gather([_pack(rep_grads)]))
        dx = ffn_block("ffn1", dx, s["x0"], 0, extra=small_jobs if l == 0 else (), returns_dx=l == 0)
        grads["norms"][l] = jnp.concatenate(dg, axis=0)

    g_norms = jnp.stack(grads["norms"], axis=0).reshape(depth, norms.shape[1], NDEV, dsh).transpose(2, 0, 1, 3)
    g_pscale_p = g_pscale.reshape(1, NDEV, dsh).transpose(1, 0, 2)
    sharded_small = ["norms", "pool_scale"]
    pieces_small = jnp.stack([_pack([g_norms[j], g_pscale_p[j]]) for j in range(NDEV)], axis=0)
    small_scatter = _Scatter([(pieces_small, 0, pieces_small.shape[1])])

    out = {}

    def update(k, extra=(), host=True):
        waiting = [it[0] for it in pending if it[0][0] == k] + [it[:2] for it in swaps if it[0] == k]
        assert not waiting, waiting
        shp = params[k].shape
        flip = k.endswith(("_wg", "_wu"))
        c = shp[1] if flip else shp[-1]
        view = lambda a: (a.swapaxes(1, 2) if flip else a).reshape(-1, c)
        pieces = [recv[key] for key in sorted(key for key in recv if key[0] == k)]
        args = (f"adamw_{k}", view(params[k]), view(mom1[k]), view(mom2[k]), pieces)
        res = hosted(HOST_US_ADAMW, adamw, *args, extra=extra) if host else adamw(*args)
        if flip:
            out[k] = [a.reshape(shp[0], shp[2], shp[1]).swapaxes(1, 2) for a in res]
        else:
            out[k] = [a.reshape(shp) for a in res]

    def update_pack(names, pieces):
        shapes = [params[k].shape for k in names]
        res = adamw("adamw_" + names[0] + "_pack", _pack([params[k] for k in names]),
                    _pack([mom1[k] for k in names]), _pack([mom2[k] for k in names]), [pieces])
        for which in range(4):
            for k, a in zip(names, _unpack(res[which], shapes)):
                out.setdefault(k, [None] * 4)[which] = a

    last = ("ffn1_wg", "ffn1_wu", "ffn1_wd")
    big = ("ffn2_wg", "ffn2_wu", "ffn2_wd")
    early = [k for k in order if k not in last + big and k not in sharded_small and k not in replicated]
    for i, k in enumerate(early):
        update(k, extra=[small_scatter] if i == 0 else ())
    for k in big:
        update(k, host=False)
    flushes = 0
    while pending or swaps:
        hosted(float("inf"), lambda comm: _run_exchange(f"scatter_tail_{flushes}", comm))
        flushes += 1
    update_pack(replicated, small_jobs[0].result[0])
    update_pack(sharded_small, small_scatter.result[0])
    for k in last:
        update(k, host=False)

    outputs = [loss, dx[None]]
    for which in range(4):
        outputs += [out[k][which] for k in order]
    return tuple(outputs)
```

```python
import math

import jax
import jax.numpy as jnp
from jax import lax
from jax.experimental import pallas as pl
from jax.experimental.pallas import tpu as pltpu

F32 = jnp.float32
BF16 = jnp.bfloat16
NDEV = 8
MIB = 1024 * 1024
LANES = 128

RMS_EPS = 1e-6
HEAD_DIM = 64
N_Q_HEADS = 16
Q_PER_KV = 8
ATTN_WIDTH = 1024
KV_WIDTH = 128
BLOCK = 128
ROPE_DIM = 16
ROPE_THETA = 500000.0
SGU_GROUPS = 8
SGU_WIDTH = 1024
CHUNK = 128
POOL_WINDOWS = (2, 4, 8, 16)
POOL_GROUP_DIM = 512
POOL_HALO = 16
X_HEADS = 4
X_HEAD_DIM = 128
ZQ, ZU, ZV, ZK = 0, 1024, 2048, 3072
IN_WIDTH = 3328

ADAM_LR = 0.001
ADAM_B1 = 0.9
ADAM_B2 = 0.999
ADAM_EPS = 1e-08
ADAM_WD = 0.01
ADAM_STEP = 10

FFN_FWD_ROWS = 512
FFN_BWD_ROWS = 512

LINK_US_PER_MIB_ALL = 91.0
LINK_US_PER_MIB_CHIPS = 45.0
PAIR_SWAP_US = 20.0
HOST_US_FFN_BWD = 420.0
HOST_US_WGRAD = 100.0
HOST_US_ATTN_BWD = 240.0
HOST_US_SMALL = 40.0
HOST_US_ADAMW = 40.0

_DN = {
    "nn": (((1,), (0,)), ((), ())),
    "nt": (((1,), (1,)), ((), ())),
    "tn": (((0,), (0,)), ((), ())),
}


def _cp(naxes, vmem_mib=48):
    return pltpu.CompilerParams(dimension_semantics=("arbitrary",) * naxes, vmem_limit_bytes=vmem_mib * MIB)


def _tile(n, pref):
    t = min(n, pref)
    while n % t:
        t //= 2
    return t


def _dot(a, b, dims="nn"):
    return lax.dot_general(a, b, _DN[dims], preferred_element_type=F32)


def _me():
    x, y, c = lax.axis_index("x"), lax.axis_index("y"), lax.axis_index("c")
    return x, y, c, 4 * x + 2 * y + c


def _peer(k):
    x, y, c, _ = _me()
    px = 1 - x if k & 4 else x
    py = 1 - y if k & 2 else y
    pc = 1 - c if k & 1 else c
    return (px, py, pc), 4 * px + 2 * py + pc


class _Exchange:
    def __init__(self, arrs, out_shape, remote_per=NDEV - 1, local_per=1):
        self.arrs = list(arrs)
        self.n = len(self.arrs)
        self.out_shape = list(out_shape)
        self.remote_per = remote_per
        self.scratch = [
            pltpu.SemaphoreType.DMA((self.n * remote_per,)),
            pltpu.SemaphoreType.DMA((self.n * remote_per,)),
            pltpu.SemaphoreType.DMA((self.n * local_per,)),
        ]
        self.result = None

    def mid(self, ins, outs, sems):
        pass

    def _copy(self, src, dst, sems, i, k, dev):
        send, recv, _ = sems
        return pltpu.make_async_remote_copy(
            src_ref=src, dst_ref=dst, send_sem=send.at[i * self.remote_per + k - 1],
            recv_sem=recv.at[i * self.remote_per + k - 1], device_id=dev, device_id_type=pl.DeviceIdType.MESH)


class _Gather(_Exchange):
    def __init__(self, arrs):
        super().__init__(arrs, [jax.ShapeDtypeStruct((NDEV,) + a.shape, a.dtype) for a in arrs])

    def start(self, ins, outs, sems):
        me = _me()[3]
        for i in range(self.n):
            pltpu.make_async_copy(ins[i], outs[i].at[me], sems[2].at[i]).start()
        for k in (1, 2, 4, 6):
            dev, _ = _peer(k)
            for i in range(self.n):
                self._copy(ins[i], outs[i].at[me], sems, i, k, dev).start()

    def mid(self, ins, outs, sems):
        sibling, _ = _peer(1)
        for k in (2, 4, 6):
            dev, slot = _peer(k)
            for i in range(self.n):
                block = outs[i].at[slot]
                self._copy(ins[i], block, sems, i, k, dev).wait_recv()
                self._copy(block, block, sems, i, k + 1, sibling).start()

    def finish(self, ins, outs, sems):
        me = _me()[3]
        sibling, _ = _peer(1)
        for k in (1, 3, 5, 7):
            dev, slot = _peer(k)
            for i in range(self.n):
                self._copy(ins[i], outs[i].at[slot], sems, i, k, dev).wait_recv()
        for k in range(1, NDEV):
            for i in range(self.n):
                self._copy(ins[i], outs[i].at[me], sems, i, k, sibling).wait_send()
        for i in range(self.n):
            pltpu.make_async_copy(ins[i], outs[i].at[me], sems[2].at[i]).wait()


class _Scatter(_Exchange):
    def __init__(self, units):
        self.rows = [(r0, n) for _, r0, n in units]
        super().__init__([a for a, _, _ in units],
                         [jax.ShapeDtypeStruct((NDEV, n) + a.shape[2:], a.dtype) for a, _, n in units])

    def _src(self, ins, i, slot):
        r0, n = self.rows[i]
        return ins[i].at[slot, pl.ds(r0, n)]

    def start(self, ins, outs, sems):
        me = _me()[3]
        for i in range(self.n):
            pltpu.make_async_copy(self._src(ins, i, me), outs[i].at[me], sems[2].at[i]).start()
        for k in range(1, NDEV):
            dev, slot = _peer(k)
            for i in range(self.n):
                self._copy(self._src(ins, i, slot), outs[i].at[me], sems, i, k, dev).start()

    def finish(self, ins, outs, sems):
        me = _me()[3]
        for k in range(1, NDEV):
            dev, slot = _peer(k)
            for i in range(self.n):
                cp = self._copy(self._src(ins, i, slot), outs[i].at[slot], sems, i, k, dev)
                cp.wait_send()
                cp.wait_recv()
        for i in range(self.n):
            pltpu.make_async_copy(self._src(ins, i, me), outs[i].at[me], sems[2].at[i]).wait()


NCHIP = NDEV // 2


class _PairSwap(_Exchange):
    def __init__(self, arrs):
        super().__init__(arrs, [jax.ShapeDtypeStruct((NCHIP,) + a.shape[2:], a.dtype) for a in arrs],
                         remote_per=NCHIP)

    def _copies(self, ins, outs, sems):
        c = _me()[2]
        sibling, _ = _peer(1)
        for i in range(self.n):
            for q in range(NCHIP):
                yield self._copy(ins[i].at[q, 1 - c], outs[i].at[q], sems, i, q + 1, sibling)

    def start(self, ins, outs, sems):
        for remote in self._copies(ins, outs, sems):
            remote.start()

    def finish(self, ins, outs, sems):
        for remote in self._copies(ins, outs, sems):
            remote.wait_send()
            remote.wait_recv()


class _ChipScatter(_Exchange):
    def __init__(self, units):
        self.rows = [(r0, n) for _, r0, n in units]
        super().__init__([a for a, _, _ in units],
                         [jax.ShapeDtypeStruct((NCHIP, n) + a.shape[2:], a.dtype) for a, _, n in units],
                         remote_per=NCHIP - 1)

    def _src(self, ins, i, chip):
        r0, n = self.rows[i]
        return ins[i].at[chip, pl.ds(r0, n)]

    @staticmethod
    def _chip(k):
        dev, slot = _peer(2 * k)
        return dev, slot // 2

    def start(self, ins, outs, sems):
        mine = _me()[3] // 2
        for i in range(self.n):
            pltpu.make_async_copy(self._src(ins, i, mine), outs[i].at[mine], sems[2].at[i]).start()
        for k in range(1, NCHIP):
            dev, chip = self._chip(k)
            for i in range(self.n):
                self._copy(self._src(ins, i, chip), outs[i].at[mine], sems, i, k, dev).start()

    def finish(self, ins, outs, sems):
        mine = _me()[3] // 2
        for k in range(1, NCHIP):
            dev, chip = self._chip(k)
            for i in range(self.n):
                cp = self._copy(self._src(ins, i, chip), outs[i].at[chip], sems, i, k, dev)
                cp.wait_send()
                cp.wait_recv()
        for i in range(self.n):
            pltpu.make_async_copy(self._src(ins, i, mine), outs[i].at[mine], sems[2].at[i]).wait()


class _Multi:
    def __init__(self, jobs):
        self.jobs = list(jobs)
        self.arrs = [a for j in self.jobs for a in j.arrs]
        self.out_shape = [s for j in self.jobs for s in j.out_shape]
        self.scratch = [s for j in self.jobs for s in j.scratch]
        self._result = None

    def _parts(self, ins, outs, sems):
        oi = oo = 0
        for idx, j in enumerate(self.jobs):
            ni, no = len(j.arrs), len(j.out_shape)
            yield j, ins[oi:oi + ni], outs[oo:oo + no], sems[3 * idx:3 * idx + 3]
            oi += ni
            oo += no

    def start(self, ins, outs, sems):
        for j, i, o, s in self._parts(ins, outs, sems):
            j.start(i, o, s)

    def mid(self, ins, outs, sems):
        for j, i, o, s in self._parts(ins, outs, sems):
            j.mid(i, o, s)

    def finish(self, ins, outs, sems):
        for j, i, o, s in self._parts(ins, outs, sems):
            j.finish(i, o, s)

    @property
    def result(self):
        return self._result

    @result.setter
    def result(self, res):
        self._result = res
        o = 0
        for j in self.jobs:
            j.result = list(res[o:o + len(j.out_shape)])
            o += len(j.out_shape)


def _call(name, body, *, grid, in_specs, out_specs, out_shape, args, scratch=(), comm=None, vmem_mib=48):
    in_specs, out_specs, out_shape = list(in_specs), list(out_specs), list(out_shape)
    scratch, args = list(scratch), list(args)
    ni, no, ns = len(in_specs), len(out_specs), len(scratch)
    kernel_fn = body
    if comm is not None:
        ci, co = len(comm.arrs), len(comm.out_shape)
        hbm = pl.BlockSpec(memory_space=pltpu.HBM)

        def kernel_fn(*refs):
            refs = list(refs)
            ins, c_in, outs, c_out, scr, c_scr = (
                [refs.pop(0) for _ in range(cnt)] for cnt in (ni, ci, no, co, ns, len(comm.scratch)))
            if not grid:
                comm.start(c_in, c_out, c_scr)
                body(*ins, *outs, *scr)
                comm.mid(c_in, c_out, c_scr)
                comm.finish(c_in, c_out, c_scr)
                return
            step = pl.program_id(0)
            for ax in range(1, len(grid)):
                step = step * grid[ax] + pl.program_id(ax)
            nsteps = math.prod(grid)
            mid_step = nsteps - 1

            @pl.when(step == 0)
            def _():
                comm.start(c_in, c_out, c_scr)

            body(*ins, *outs, *scr)

            @pl.when(step == mid_step)
            def _():
                comm.mid(c_in, c_out, c_scr)

            @pl.when(step == nsteps - 1)
            def _():
                comm.finish(c_in, c_out, c_scr)

        in_specs += [hbm] * ci
        out_specs += [hbm] * co
        out_shape += comm.out_shape
        scratch += comm.scratch
        args += comm.arrs
    params = _cp(len(grid), vmem_mib) if grid else None
    res = pl.pallas_call(
        kernel_fn, out_shape=out_shape, grid=grid, in_specs=in_specs, out_specs=out_specs,
        scratch_shapes=scratch, compiler_params=params, name=name,
    )(*args)
    if comm is not None:
        comm.result = list(res[no:])
    return tuple(res[:no])


def _run_exchange(name, comm):
    _call(name, lambda: None, grid=(), in_specs=[], out_specs=[], out_shape=[], args=[], comm=comm)
    return comm.result


def _pack(arrs, dtype=F32):
    flat = jnp.concatenate([a.astype(dtype).reshape(-1) for a in arrs])
    n = flat.shape[0]
    total = -(-n // (16 * LANES)) * (16 * LANES)
    return jnp.pad(flat, (0, total - n)).reshape(total // LANES, LANES)


def _unpack(packed, shapes, lead=()):
    flat = packed.reshape(lead + (-1,))
    out, off = [], 0
    for s in shapes:
        n = math.prod(s)
        out.append(flat[..., off:off + n].reshape(lead + tuple(s)))
        off += n
    return out


def _mm(name, a, b, *, dims, grid, a_spec, b_spec, o_spec, out_shape, acc_shape=None, nk=1, vmem_mib=48, comm=None):
    nax = len(grid)

    def body(a_ref, b_ref, o_ref, *scratch):
        p = _dot(a_ref[...], b_ref[...], dims)
        if nk == 1:
            o_ref[...] = p.astype(o_ref.dtype)
            return
        acc = scratch[0]
        k = pl.program_id(nax - 1)

        @pl.when(k == 0)
        def _():
            acc[...] = p

        @pl.when(k > 0)
        def _():
            acc[...] += p

        @pl.when(k == nk - 1)
        def _():
            o_ref[...] = acc[...].astype(o_ref.dtype)

    return _call(
        name, body, grid=grid, in_specs=[a_spec, b_spec], out_specs=[o_spec], out_shape=[out_shape], args=[a, b],
        scratch=[pltpu.VMEM(acc_shape, F32)] if nk > 1 else [], comm=comm, vmem_mib=vmem_mib,
    )[0]


def mm_nn(name, a, b, out_dtype, tn=None):
    m, k = a.shape
    n = b.shape[1]
    tm = _tile(m, 512)
    tn = n if tn is None else tn
    return _mm(
        name, a, b, dims="nn", grid=(n // tn, m // tm),
        a_spec=pl.BlockSpec((tm, k), lambda j, i: (i, 0)),
        b_spec=pl.BlockSpec((k, tn), lambda j, i: (0, j)),
        o_spec=pl.BlockSpec((tm, tn), lambda j, i: (i, j)),
        out_shape=jax.ShapeDtypeStruct((m, n), out_dtype),
    )


def mm_nt(name, a, b, out_dtype, tn=None):
    m, k = a.shape
    n = b.shape[0]
    tm = _tile(m, 512)
    tn = n if tn is None else tn
    return _mm(
        name, a, b, dims="nt", grid=(n // tn, m // tm),
        a_spec=pl.BlockSpec((tm, k), lambda j, i: (i, 0)),
        b_spec=pl.BlockSpec((tn, k), lambda j, i: (j, 0)),
        o_spec=pl.BlockSpec((tm, tn), lambda j, i: (i, j)),
        out_shape=jax.ShapeDtypeStruct((m, n), out_dtype),
    )


def mm_kred(name, a_t, b, out_dtype, tno=None, comm=None):
    m, k = a_t.shape
    n = b.shape[1]
    tk = _tile(k, 512)
    tmo = _tile(m, 1024)
    tno = n if tno is None else tno
    return _mm(
        name, a_t, b, dims="nn", grid=(m // tmo, n // tno, k // tk),
        a_spec=pl.BlockSpec((tmo, tk), lambda i, j, kk: (i, kk)),
        b_spec=pl.BlockSpec((tk, tno), lambda i, j, kk: (kk, j)),
        o_spec=pl.BlockSpec((tmo, tno), lambda i, j, kk: (i, j)),
        out_shape=jax.ShapeDtypeStruct((m, n), out_dtype),
        acc_shape=(tmo, tno), nk=k // tk, comm=comm,
    )


def mm_tn(name, a, b, out_dtype, tmo=None, tno=None):
    k, m = a.shape
    n = b.shape[1]
    tk = _tile(k, 512)
    tmo = _tile(m, 1024) if tmo is None else tmo
    tno = n if tno is None else tno
    return _mm(
        name, a, b, dims="tn", grid=(m // tmo, n // tno, k // tk),
        a_spec=pl.BlockSpec((tk, tmo), lambda i, j, kk: (kk, i)),
        b_spec=pl.BlockSpec((tk, tno), lambda i, j, kk: (kk, j)),
        o_spec=pl.BlockSpec((tmo, tno), lambda i, j, kk: (i, j)),
        out_shape=jax.ShapeDtypeStruct((m, n), out_dtype),
        acc_shape=(tmo, tno), nk=k // tk,
    )


def _rstd(x):
    return lax.rsqrt(jnp.mean(x * x, axis=-1, keepdims=True) + RMS_EPS)


def norm_fwd(name, x, g, out_dtype, with_t=False):
    t, d = x.shape
    tm = _tile(t, 256)

    def body(x_ref, g_ref, o_ref, *t_ref):
        xv = x_ref[...]
        h = xv * _rstd(xv) * g_ref[...]
        o_ref[...] = h.astype(o_ref.dtype)
        if with_t:
            t_ref[0][...] = h.T.astype(out_dtype)

    row = pl.BlockSpec((tm, d), lambda i: (i, 0))
    out_shape = [jax.ShapeDtypeStruct((t, d), out_dtype)]
    out_specs = [row]
    if with_t:
        out_shape.append(jax.ShapeDtypeStruct((d, t), out_dtype))
        out_specs.append(pl.BlockSpec((d, tm), lambda i: (0, i)))
    res = pl.pallas_call(
        body, out_shape=out_shape, grid=(t // tm,),
        in_specs=[row, pl.BlockSpec((1, d), lambda i: (0, 0))], out_specs=out_specs,
        compiler_params=_cp(1), name=name,
    )(x, g)
    return tuple(res) if with_t else res[0]


def resid_norm_fwd(name, x, m, g, scale, nxt=None):
    t, d = x.shape
    tm = _tile(t, 256)
    with_t = nxt is not None and nxt[2]

    def body(*refs):
        if nxt is None:
            x_ref, m_ref, g_ref, o_ref = refs
        else:
            x_ref, m_ref, g_ref, gn_ref, o_ref, h_ref = refs[:6]
        mv = m_ref[...]
        xn = x_ref[...] + scale * (mv * _rstd(mv) * g_ref[...])
        o_ref[...] = xn
        if nxt is not None:
            h = xn * _rstd(xn) * gn_ref[...]
            h_ref[...] = h.astype(h_ref.dtype)
            if with_t:
                refs[6][...] = h.T.astype(nxt[1])

    row = pl.BlockSpec((tm, d), lambda i: (i, 0))
    vec = pl.BlockSpec((1, d), lambda i: (0, 0))
    out_shape, out_specs, args = [jax.ShapeDtypeStruct((t, d), F32)], [row], [x, m, g]
    if nxt is not None:
        args.append(nxt[0])
        out_shape.append(jax.ShapeDtypeStruct((t, d), nxt[1]))
        out_specs.append(row)
        if with_t:
            out_shape.append(jax.ShapeDtypeStruct((d, t), nxt[1]))
            out_specs.append(pl.BlockSpec((d, tm), lambda i: (0, i)))
    res = pl.pallas_call(
        body, out_shape=out_shape, grid=(t // tm,),
        in_specs=[row, row, vec] + ([vec] if nxt is not None else []), out_specs=out_specs,
        compiler_params=_cp(1), name=name,
    )(*args)
    return res[0] if nxt is None else tuple(res)


def norm_bwd(name, u, g, dy, scale, resid, out_dtype, post=None, comm=None):
    t, d = u.shape
    tm = _tile(t, 256)
    has_resid = resid is not None

    def norm_grad(uv, gv, dyv, dg_ref):
        r = _rstd(uv)
        uh = uv * r
        dg_ref[...] += jnp.sum(dyv * uh, axis=0, keepdims=True)
        dyg = dyv * gv
        return r * (dyg - uh * jnp.mean(dyg * uh, axis=-1, keepdims=True))

    def body(*refs):
        refs = list(refs)
        u_ref, g_ref, dy_ref = refs[:3]
        del refs[:3]
        r_ref = refs.pop(0) if has_resid else None
        if post is not None:
            m_ref, gp_ref = refs[:2]
            del refs[:2]
        du_ref, dg_ref = refs[:2]

        @pl.when(pl.program_id(0) == 0)
        def _():
            for ref in refs[1::2]:
                ref[...] = jnp.zeros_like(ref)

        du = norm_grad(u_ref[...], g_ref[...], dy_ref[...].astype(F32) * scale, dg_ref)
        if has_resid:
            du = du + r_ref[...]
        du_ref[...] = du.astype(du_ref.dtype)
        if post is not None:
            dm_ref, dgp_ref = refs[2:4]
            dm_ref[...] = norm_grad(m_ref[...], gp_ref[...], du * post[2], dgp_ref).astype(dm_ref.dtype)

    row = pl.BlockSpec((tm, d), lambda i: (i, 0))
    vec = pl.BlockSpec((1, d), lambda i: (0, 0))
    args = [u, g, dy] + ([resid] if has_resid else [])
    in_specs = [row, vec, row] + ([row] if has_resid else [])
    out_specs = [row, vec]
    out_shape = [jax.ShapeDtypeStruct((t, d), out_dtype), jax.ShapeDtypeStruct((1, d), F32)]
    if post is not None:
        args += [post[0], post[1]]
        in_specs += [row, vec]
        out_specs += [row, vec]
        out_shape += [jax.ShapeDtypeStruct((t, d), post[3]), jax.ShapeDtypeStruct((1, d), F32)]
    return _call(name, body, grid=(t // tm,), in_specs=in_specs, out_specs=out_specs, out_shape=out_shape,
                 args=args, comm=comm)


def loss_grad(y, target):
    t, d = y.shape
    tm = _tile(t, 256)
    nt = t // tm

    def body(y_ref, t_ref, dy_ref, loss_ref, acc):
        i = pl.program_id(0)
        e = y_ref[...] - t_ref[...]
        dy_ref[...] = e * (1.0 / d)

        @pl.when(i == 0)
        def _():
            acc[...] = jnp.zeros_like(acc)

        acc[...] += jnp.sum(e * e, axis=0, keepdims=True)

        @pl.when(i == nt - 1)
        def _():
            loss_ref[...] = (0.5 / d) * jnp.sum(acc[...], axis=1, keepdims=True)

    row = pl.BlockSpec((tm, d), lambda i: (i, 0))
    return pl.pallas_call(
        body,
        out_shape=(jax.ShapeDtypeStruct((t, d), F32), jax.ShapeDtypeStruct((1, 1), F32)),
        grid=(nt,), in_specs=[row, row], out_specs=(row, pl.BlockSpec((1, 1), lambda i: (0, 0))),
        scratch_shapes=[pltpu.VMEM((1, d), F32)], compiler_params=_cp(1), name="loss_grad",
    )(y, target)


def _row_tile(r, c):
    if r * c * 4 <= MIB:
        return r
    best = None
    for t in range(16, r, 16):
        if r % t == 0 and t * c * 4 <= MIB:
            best = t
    return r if best is None else best


def pair_sum(name, g4, theirs):
    nq, _, r, c = g4.shape
    tr = _tile(r, 1024)

    def body(g_ref, t_ref, o_ref):
        mine = g_ref[lax.axis_index("c")]
        o_ref[...] = (mine.astype(F32) + t_ref[...].astype(F32)).astype(o_ref.dtype)

    blk = pl.BlockSpec((None, tr, c), lambda q, i: (q, i, 0))
    return pl.pallas_call(
        body, out_shape=jax.ShapeDtypeStruct(theirs.shape, theirs.dtype), grid=(nq, r // tr),
        in_specs=[pl.BlockSpec((None, 2, tr, c), lambda q, i: (q, 0, i, 0)), blk], out_specs=blk,
        compiler_params=_cp(2), name=name,
    )(g4, theirs)


def adamw(name, w, m, v, pieces, comm=None):
    nl = len(pieces)
    npiece, r, c = pieces[0].shape
    tr = _row_tile(r, c)
    nr = r // tr
    bc1 = 1.0 - ADAM_B1 ** ADAM_STEP
    bc2 = 1.0 - ADAM_B2 ** ADAM_STEP

    def body(w_ref, m_ref, v_ref, *rest):
        p_refs, (g_ref, d_ref, nm_ref, nv_ref) = rest[:nl], rest[nl:]

        def update(p_ref):
            g = p_ref[0].astype(F32)
            for j in range(1, npiece):
                g = g + p_ref[j].astype(F32)
            m1 = ADAM_B1 * m_ref[...] + (1.0 - ADAM_B1) * g
            v1 = ADAM_B2 * v_ref[...] + (1.0 - ADAM_B2) * (g * g)
            m_hat = m1 / bc1
            v_hat = v1 / bc2
            g_ref[...] = g
            d_ref[...] = -ADAM_LR * (m_hat / (jnp.sqrt(v_hat) + ADAM_EPS) + ADAM_WD * w_ref[...])
            nm_ref[...] = m1
            nv_ref[...] = v1

        if nl == 1:
            update(p_refs[0])
        else:
            for ll in range(nl):
                pl.when(pl.program_id(0) == ll)(lambda ll=ll: update(p_refs[ll]))

    def piece_spec(ll):
        return pl.BlockSpec((npiece, tr, c), lambda l, i: (0, jnp.where(l == ll, i, jnp.where(l > ll, nr - 1, 0)), 0))

    row = pl.BlockSpec((tr, c), lambda l, i: (l * nr + i, 0))
    out = jax.ShapeDtypeStruct((nl * r, c), F32)
    return _call(
        name, body, grid=(nl, nr), in_specs=[row, row, row] + [piece_spec(ll) for ll in range(nl)],
        out_specs=[row] * 4, out_shape=[out] * 4, args=[w, m, v] + list(pieces), comm=comm,
    )


def _sigmoid(a):
    return 1.0 / (1.0 + jnp.exp(-a))


def ffn_fwd(name, h, wg, wu, wd, comm=None):
    t, d = h.shape
    ns, _, f = wg.shape
    tm = _tile(t, FFN_FWD_ROWS)

    def body(h_ref, wg_ref, wu_ref, wd_ref, a_ref, b_ref, hidt_ref, m_ref, acc):
        j = pl.program_id(1)
        hv = h_ref[...]
        a = _dot(hv, wg_ref[...])
        b = _dot(hv, wu_ref[...])
        hid32 = (a * _sigmoid(a)) * b
        hid = hid32.astype(BF16)
        a_ref[...] = a.astype(BF16)
        b_ref[...] = b.astype(BF16)
        hidt_ref[...] = hid32.T.astype(BF16)
        p = _dot(hid, wd_ref[...])

        @pl.when(j == 0)
        def _():
            acc[...] = p

        @pl.when(j > 0)
        def _():
            acc[...] += p

        @pl.when(j == ns - 1)
        def _():
            m_ref[...] = acc[...]

    w_in = pl.BlockSpec((None, d, f), lambda i, j: (j, 0, 0))
    act = pl.BlockSpec((None, tm, f), lambda i, j: (j, i, 0))
    act_shape = jax.ShapeDtypeStruct((ns, t, f), BF16)
    return _call(
        name, body, grid=(t // tm, ns),
        in_specs=[pl.BlockSpec((tm, d), lambda i, j: (i, 0)), w_in, w_in,
                  pl.BlockSpec((None, f, d), lambda i, j: (j, 0, 0))],
        out_specs=[act, act, pl.BlockSpec((None, f, tm), lambda i, j: (j, 0, i)),
                   pl.BlockSpec((tm, d), lambda i, j: (i, 0))],
        out_shape=[act_shape, act_shape, jax.ShapeDtypeStruct((ns, f, t), BF16), jax.ShapeDtypeStruct((t, d), F32)],
        args=[h, wg, wu, wd], scratch=[pltpu.VMEM((tm, d), F32)], comm=comm, vmem_mib=56,
    )


def ffn_bwd(name, dm, a, b, wg, wu, wd, comm=None):
    t, d = dm.shape
    ns, _, f = wg.shape
    tm = _tile(t, FFN_BWD_ROWS)

    def body(dm_ref, a_ref, b_ref, wg_ref, wu_ref, wd_ref, dat_ref, dbt_ref, dh_ref, acc):
        j = pl.program_id(1)
        dhid = _dot(dm_ref[...], wd_ref[...], "nt")
        av = a_ref[...].astype(F32)
        bv = b_ref[...].astype(F32)
        sig = _sigmoid(av)
        da32 = dhid * bv * (sig * (1.0 + av * (1.0 - sig)))
        db32 = dhid * (av * sig)
        dat_ref[...] = da32.T.astype(BF16)
        dbt_ref[...] = db32.T.astype(BF16)
        p = _dot(da32.astype(BF16), wg_ref[...], "nt") + _dot(db32.astype(BF16), wu_ref[...], "nt")

        @pl.when(j == 0)
        def _():
            acc[...] = p

        @pl.when(j > 0)
        def _():
            acc[...] += p

        @pl.when(j == ns - 1)
        def _():
            dh_ref[...] = acc[...]

    w_in = pl.BlockSpec((None, d, f), lambda i, j: (j, 0, 0))
    act = pl.BlockSpec((None, tm, f), lambda i, j: (j, i, 0))
    act_t = pl.BlockSpec((None, f, tm), lambda i, j: (j, 0, i))
    row = pl.BlockSpec((tm, d), lambda i, j: (i, 0))
    act_t_shape = jax.ShapeDtypeStruct((ns, f, t), BF16)
    return _call(
        name, body, grid=(t // tm, ns),
        in_specs=[row, act, act, w_in, w_in, pl.BlockSpec((None, f, d), lambda i, j: (j, 0, 0))],
        out_specs=[act_t, act_t, row],
        out_shape=[act_t_shape, act_t_shape, jax.ShapeDtypeStruct((t, d), F32)],
        args=[dm, a, b, wg, wu, wd], scratch=[pltpu.VMEM((tm, d), F32)], comm=comm, vmem_mib=56,
    )


def ffn_wgrad(name, act_t, x, comm=None):
    ns, f, t = act_t.shape
    d = x.shape[1]
    return _mm(
        name, act_t, x, dims="nn", grid=(ns,),
        a_spec=pl.BlockSpec((None, f, t), lambda j: (j, 0, 0)),
        b_spec=pl.BlockSpec((t, d), lambda j: (0, 0), pipeline_mode=pl.Buffered(1)),
        o_spec=pl.BlockSpec((None, f, d), lambda j: (j, 0, 0)),
        out_shape=jax.ShapeDtypeStruct((ns, f, d), BF16), vmem_mib=56, comm=comm,
    )


def rope_table(t):
    half = ROPE_DIM // 2
    inv = ROPE_THETA ** (-jnp.arange(half, dtype=F32) * 2.0 / ROPE_DIM)
    ang = jnp.arange(t, dtype=F32)[:, None] * inv[None, :]
    cos, sin = jnp.cos(ang), jnp.sin(ang)
    rest = HEAD_DIM - ROPE_DIM
    c = jnp.concatenate([cos, cos, jnp.ones((t, rest), F32)], axis=1)
    sm = jnp.concatenate([-sin, jnp.zeros((t, half + rest), F32)], axis=1)
    sp = jnp.concatenate([jnp.zeros((t, half), F32), sin, jnp.zeros((t, rest), F32)], axis=1)
    return jnp.concatenate([jnp.tile(c, (1, 2)), jnp.tile(sm, (1, 2)), jnp.tile(sp, (1, 2))], axis=1)


def _rope(x, tab, sign):
    w = x.shape[1]
    rep = w // LANES
    c, sm, sp = tab[:, 0:LANES], tab[:, LANES:2 * LANES], tab[:, 2 * LANES:3 * LANES]
    if rep > 1:
        c, sm, sp = jnp.tile(c, (1, rep)), jnp.tile(sm, (1, rep)), jnp.tile(sp, (1, rep))
    half = ROPE_DIM // 2
    return x * c + sign * (pltpu.roll(x, w - half, 1) * sm + pltpu.roll(x, half, 1) * sp)


def _attn_specs():
    prev = lambda n: jnp.maximum(n - 1, 0)
    kblk, vblk = ZK // LANES, ZK // LANES + 1
    return [
        pl.BlockSpec((BLOCK, ATTN_WIDTH), lambda n: (n, 0)),
        pl.BlockSpec((BLOCK, KV_WIDTH), lambda n: (n, kblk)),
        pl.BlockSpec((BLOCK, KV_WIDTH), lambda n: (prev(n), kblk)),
        pl.BlockSpec((BLOCK, KV_WIDTH), lambda n: (n, vblk)),
        pl.BlockSpec((BLOCK, KV_WIDTH), lambda n: (prev(n), vblk)),
        pl.BlockSpec((BLOCK, 3 * LANES), lambda n: (n, 0)),
        pl.BlockSpec((BLOCK, 3 * LANES), lambda n: (prev(n), 0)),
        pl.BlockSpec(memory_space=pltpu.SMEM),
    ]


def _attn_prologue(n, zq_ref, zk_ref, zkp_ref, zv_ref, zvp_ref, tab_ref, tabp_ref):
    q = (_rope(zq_ref[...], tab_ref[...], 1.0) * (HEAD_DIM ** -0.5)).astype(BF16)
    kcat = jnp.concatenate(
        [_rope(zkp_ref[...], tabp_ref[...], 1.0), _rope(zk_ref[...], tab_ref[...], 1.0)], axis=0).astype(BF16)
    vcat = jnp.concatenate([zvp_ref[...], zv_ref[...]], axis=0).astype(BF16)
    qi = lax.broadcasted_iota(jnp.int32, (BLOCK, 2 * BLOCK), 0)
    kj = lax.broadcasted_iota(jnp.int32, (BLOCK, 2 * BLOCK), 1)
    valid = (kj <= qi + BLOCK) & (kj > qi) & ((n > 0) | (kj >= BLOCK))
    return q, kcat, vcat, valid


def _attn_probs(qh, kh, valid, sink):
    s = jnp.where(valid, _dot(qh, kh, "nt"), -1e30)
    mx = jnp.maximum(jnp.max(s, axis=1, keepdims=True), sink)
    p = jnp.exp(s - mx)
    p_sink = jnp.exp(sink - mx)
    inv = 1.0 / (jnp.sum(p, axis=1, keepdims=True) + p_sink)
    return p * inv, p_sink * inv


def attn_fwd(z, tab, sinks, comm=None):
    t = z.shape[0]

    def body(zq_ref, zk_ref, zkp_ref, zv_ref, zvp_ref, tab_ref, tabp_ref, sink_ref, o_ref):
        n = pl.program_id(0)
        q, kcat, vcat, valid = _attn_prologue(n, zq_ref, zk_ref, zkp_ref, zv_ref, zvp_ref, tab_ref, tabp_ref)
        outs = []
        for h in range(N_Q_HEADS):
            kv = slice((h // Q_PER_KV) * HEAD_DIM, (h // Q_PER_KV + 1) * HEAD_DIM)
            p, _ = _attn_probs(q[:, h * HEAD_DIM:(h + 1) * HEAD_DIM], kcat[:, kv], valid, sink_ref[0, h])
            outs.append(_dot(p.astype(BF16), vcat[:, kv]))
        o_ref[...] = jnp.concatenate(outs, axis=1).astype(BF16)

    return _call(
        "attn_fwd", body, grid=(t // BLOCK,), in_specs=_attn_specs(),
        out_specs=[pl.BlockSpec((BLOCK, ATTN_WIDTH), lambda n: (n, 0))],
        out_shape=[jax.ShapeDtypeStruct((t, ATTN_WIDTH), BF16)],
        args=[z, z, z, z, z, tab, tab, sinks], comm=comm,
    )[0]


def attn_bwd(z, tab, sinks, dcat, comm=None):
    t = z.shape[0]
    nb = t // BLOCK

    def body(zq_ref, zk_ref, zkp_ref, zv_ref, zvp_ref, tab_ref, tabp_ref, sink_ref, do_ref,
             dq_ref, dkv_ref, dsink_ref):
        n = pl.program_id(0)
        q, kcat, vcat, valid = _attn_prologue(n, zq_ref, zk_ref, zkp_ref, zv_ref, zvp_ref, tab_ref, tabp_ref)
        do = do_ref[...]
        lane = lax.broadcasted_iota(jnp.int32, (1, LANES), 1)
        dqs, dks, dvs = [], [], []
        dsink = jnp.zeros((1, LANES), F32)
        for hk in range(N_Q_HEADS // Q_PER_KV):
            kv = slice(hk * HEAD_DIM, (hk + 1) * HEAD_DIM)
            kh, vh = kcat[:, kv], vcat[:, kv]
            dk_t = jnp.zeros((HEAD_DIM, 2 * BLOCK), F32)
            dv_t = jnp.zeros((HEAD_DIM, 2 * BLOCK), F32)
            for g in range(Q_PER_KV):
                h = hk * Q_PER_KV + g
                hs = slice(h * HEAD_DIM, (h + 1) * HEAD_DIM)
                qh, doh = q[:, hs], do[:, hs]
                p, p_sink = _attn_probs(qh, kh, valid, sink_ref[0, h])
                dv_t = dv_t + _dot(doh, p.astype(BF16), "tn")
                dp = _dot(doh, vh, "nt")
                rd = jnp.sum(p * dp, axis=1, keepdims=True)
                ds = (p * (dp - rd) * (HEAD_DIM ** -0.5)).astype(BF16)
                dqs.append(_dot(ds, kh))
                dk_t = dk_t + _dot(qh, ds, "tn")
                dsink = dsink + jnp.where(lane == h, -jnp.sum(p_sink * rd, axis=0, keepdims=True), 0.0)
            dks.append(dk_t)
            dvs.append(dv_t)
        dq_ref[...] = _rope(jnp.concatenate(dqs, axis=1), tab_ref[...], -1.0).astype(BF16)
        dkc = jnp.concatenate(dks, axis=0).T * (HEAD_DIM ** 0.5)
        dk_pre = jnp.concatenate(
            [_rope(dkc[:BLOCK], tabp_ref[...], -1.0), _rope(dkc[BLOCK:], tab_ref[...], -1.0)], axis=0)
        dkv_ref[...] = jnp.concatenate([dk_pre, jnp.concatenate(dvs, axis=0).T], axis=1)

        @pl.when(n == 0)
        def _():
            dsink_ref[...] = jnp.zeros_like(dsink_ref)

        dsink_ref[...] += dsink

    return _call(
        "attn_bwd", body, grid=(nb,),
        in_specs=_attn_specs() + [pl.BlockSpec((BLOCK, ATTN_WIDTH), lambda n: (n, 0))],
        out_specs=[pl.BlockSpec((BLOCK, ATTN_WIDTH), lambda n: (n, 0)),
                   pl.BlockSpec((None, 2 * BLOCK, 2 * KV_WIDTH), lambda n: (n, 0, 0)),
                   pl.BlockSpec((1, LANES), lambda n: (0, 0))],
        out_shape=[jax.ShapeDtypeStruct((t, ATTN_WIDTH), BF16),
                   jax.ShapeDtypeStruct((nb, 2 * BLOCK, 2 * KV_WIDTH), F32),
                   jax.ShapeDtypeStruct((1, LANES), F32)],
        args=[z, z, z, z, z, tab, tab, sinks, dcat], comm=comm,
    )


def _gelu(x):
    k = math.sqrt(2.0 / math.pi)
    th = jnp.tanh(k * (x + 0.044715 * (x * x * x)))
    return 0.5 * x * (1.0 + th), th


def _gelu_grad(x, th):
    k = math.sqrt(2.0 / math.pi)
    return 0.5 * (1.0 + th) + 0.5 * x * (1.0 - th * th) * (k * (1.0 + 3.0 * 0.044715 * (x * x)))


def _sgu_core(zu_ref, zv_ref, lng_ref, lnb_ref, w_ref, bt_ref):
    up, vp = zu_ref[...], zv_ref[...]
    u, thu = _gelu(up)
    v, thv = _gelu(vp)
    mu = jnp.mean(v, axis=-1, keepdims=True)
    vc = v - mu
    rstd = lax.rsqrt(jnp.mean(vc * vc, axis=-1, keepdims=True) + RMS_EPS)
    xhat = vc * rstd
    vn = (xhat * lng_ref[...] + lnb_ref[...]).astype(BF16)
    row = lax.broadcasted_iota(jnp.int32, (CHUNK, CHUNK), 0)
    col = lax.broadcasted_iota(jnp.int32, (CHUNK, CHUNK), 1)
    mixed = []
    for g in range(SGU_GROUPS):
        wc = jnp.where(row >= col, w_ref[g], 0.0).astype(BF16)
        mixed.append(_dot(wc, vn[:, g * CHUNK:(g + 1) * CHUNK]) + bt_ref[:, g:g + 1])
    return up, vp, u, thu, thv, rstd, xhat, vn, jnp.concatenate(mixed, axis=1)


def _sgu_specs():
    full = lambda shape: pl.BlockSpec(shape, lambda n: (0,) * len(shape))
    return [
        pl.BlockSpec((CHUNK, SGU_WIDTH), lambda n: (n, ZU // SGU_WIDTH)),
        pl.BlockSpec((CHUNK, SGU_WIDTH), lambda n: (n, ZV // SGU_WIDTH)),
        full((1, SGU_WIDTH)), full((1, SGU_WIDTH)),
        full((SGU_GROUPS, CHUNK, CHUNK)), full((CHUNK, SGU_GROUPS)),
    ]


def sgu_fwd(z, ln_g, ln_b, w, b_t):
    t = z.shape[0]

    def body(zu_ref, zv_ref, lng_ref, lnb_ref, w_ref, bt_ref, o_ref):
        _, _, u, _, _, _, _, _, mixed = _sgu_core(zu_ref, zv_ref, lng_ref, lnb_ref, w_ref, bt_ref)
        o_ref[...] = (u * mixed).astype(BF16)

    return pl.pallas_call(
        body, out_shape=jax.ShapeDtypeStruct((t, SGU_WIDTH), BF16), grid=(t // CHUNK,),
        in_specs=_sgu_specs(), out_specs=pl.BlockSpec((CHUNK, SGU_WIDTH), lambda n: (n, 0)),
        compiler_params=_cp(1), name="sgu_fwd",
    )(z, z, ln_g, ln_b, w, b_t)


def sgu_bwd(z, ln_g, ln_b, w, w_t, b_t, dcat):
    t = z.shape[0]

    def body(zu_ref, zv_ref, lng_ref, lnb_ref, w_ref, bt_ref, wt_ref, dg_ref,
             du_ref, dv_ref, dw_ref, dbt_ref, dlng_ref, dlnb_ref):
        up, vp, u, thu, thv, rstd, xhat, vn, mixed = _sgu_core(zu_ref, zv_ref, lng_ref, lnb_ref, w_ref, bt_ref)
        dgate = dg_ref[...].astype(F32)
        du_ref[...] = (dgate * mixed * _gelu_grad(up, thu)).astype(BF16)
        dmixed = dgate * u
        row = lax.broadcasted_iota(jnp.int32, (CHUNK, CHUNK), 0)
        col = lax.broadcasted_iota(jnp.int32, (CHUNK, CHUNK), 1)

        @pl.when(pl.program_id(0) == 0)
        def _():
            dw_ref[...] = jnp.zeros_like(dw_ref)
            dbt_ref[...] = jnp.zeros_like(dbt_ref)
            dlng_ref[...] = jnp.zeros_like(dlng_ref)
            dlnb_ref[...] = jnp.zeros_like(dlnb_ref)

        dvn, dbt = [], jnp.zeros((CHUNK, LANES), F32)
        for g in range(SGU_GROUPS):
            gs = slice(g * CHUNK, (g + 1) * CHUNK)
            dmx = dmixed[:, gs]
            dmxb = dmx.astype(BF16)
            dbt = dbt + jnp.where(col == g, jnp.sum(dmx, axis=1, keepdims=True), 0.0)
            dw_ref[g] += jnp.where(row >= col, _dot(dmxb, vn[:, gs], "nt"), 0.0)
            wtc = jnp.where(col >= row, wt_ref[g], 0.0).astype(BF16)
            dvn.append(_dot(wtc, dmxb))
        dbt_ref[...] += dbt
        dvn = jnp.concatenate(dvn, axis=1)
        dlnb_ref[...] += jnp.sum(dvn, axis=0, keepdims=True)
        dlng_ref[...] += jnp.sum(dvn * xhat, axis=0, keepdims=True)
        dxh = dvn * lng_ref[...]
        dv = rstd * (dxh - jnp.mean(dxh, axis=-1, keepdims=True) - xhat * jnp.mean(dxh * xhat, axis=-1, keepdims=True))
        dv_ref[...] = (dv * _gelu_grad(vp, thv)).astype(BF16)

    full = lambda shape: pl.BlockSpec(shape, lambda n: (0,) * len(shape))
    act = pl.BlockSpec((CHUNK, SGU_WIDTH), lambda n: (n, 0))
    act_shape = jax.ShapeDtypeStruct((t, SGU_WIDTH), BF16)
    vec = jax.ShapeDtypeStruct((1, SGU_WIDTH), F32)
    return pl.pallas_call(
        body,
        out_shape=(act_shape, act_shape, jax.ShapeDtypeStruct((SGU_GROUPS, CHUNK, CHUNK), F32),
                   jax.ShapeDtypeStruct((CHUNK, LANES), F32), vec, vec),
        grid=(t // CHUNK,),
        in_specs=_sgu_specs() + [full((SGU_GROUPS, CHUNK, CHUNK)),
                                 pl.BlockSpec((CHUNK, SGU_WIDTH), lambda n: (n, 1))],
        out_specs=(act, act, full((SGU_GROUPS, CHUNK, CHUNK)), full((CHUNK, LANES)),
                   full((1, SGU_WIDTH)), full((1, SGU_WIDTH))),
        compiler_params=_cp(1), name="sgu_bwd",
    )(z, z, ln_g, ln_b, w, b_t, w_t, dcat)


def dz_assemble(dq, dkv, du, dv):
    t = dq.shape[0]
    nb = t // BLOCK

    def body(dq_ref, cur_ref, nxt_ref, du_ref, dv_ref, o_ref):
        n = pl.program_id(0)
        o_ref[:, ZQ:ZQ + ATTN_WIDTH] = dq_ref[...]
        o_ref[:, ZU:ZU + SGU_WIDTH] = du_ref[...]
        o_ref[:, ZV:ZV + SGU_WIDTH] = dv_ref[...]
        kv = cur_ref[BLOCK:, :] + jnp.where(n < nb - 1, nxt_ref[:BLOCK, :], 0.0)
        o_ref[:, ZK:ZK + 2 * KV_WIDTH] = kv.astype(BF16)

    act = pl.BlockSpec((BLOCK, ATTN_WIDTH), lambda n: (n, 0))
    return pl.pallas_call(
        body, out_shape=jax.ShapeDtypeStruct((t, IN_WIDTH), BF16), grid=(nb,),
        in_specs=[act,
                  pl.BlockSpec((None, 2 * BLOCK, 2 * KV_WIDTH), lambda n: (n, 0, 0)),
                  pl.BlockSpec((None, 2 * BLOCK, 2 * KV_WIDTH), lambda n: (jnp.minimum(n + 1, nb - 1), 0, 0)),
                  act, act],
        out_specs=pl.BlockSpec((BLOCK, IN_WIDTH), lambda n: (n, 0)),
        compiler_params=_cp(1), name="dz_assemble",
    )(dq, dkv, dkv, du, dv)


def _pool_count(i, tp, w):
    t_idx = i * tp + lax.broadcasted_iota(jnp.int32, (tp, 1), 0)
    return jnp.minimum(t_idx + 1, w).astype(F32)


def pool_fwd(h, pw, pscale):
    t, d = h.shape
    tp = _tile(t, 256)
    per = tp // POOL_HALO

    def body(h_ref, halo_ref, pw_ref, ps_ref, m_ref, pooled_ref):
        i = pl.program_id(0)
        cur = h_ref[...]
        ext = jnp.concatenate([jnp.where(i > 0, halo_ref[...], 0.0), cur], axis=0)
        ys, pooled = [], []
        for gi, w in enumerate(POOL_WINDOWS):
            gs = slice(gi * POOL_GROUP_DIM, (gi + 1) * POOL_GROUP_DIM)
            s = ext[:, gs]
            sh = 1
            while sh < w:
                s = s + pltpu.roll(s, sh, 0)
                sh *= 2
            pg = (s[POOL_HALO:, :] / _pool_count(i, tp, w) - cur[:, gs]).astype(BF16)
            pooled.append(pg)
            ys.append(_dot(pg, pw_ref[gi]))
        pooled_ref[...] = jnp.concatenate(pooled, axis=1)
        m_ref[...] = jnp.concatenate(ys, axis=1) * ps_ref[...]

    row = pl.BlockSpec((tp, d), lambda i: (i, 0))
    return pl.pallas_call(
        body,
        out_shape=(jax.ShapeDtypeStruct((t, d), F32), jax.ShapeDtypeStruct((t, d), BF16)),
        grid=(t // tp,),
        in_specs=[row, pl.BlockSpec((POOL_HALO, d), lambda i: (jnp.maximum(i * per - 1, 0), 0)),
                  pl.BlockSpec(pw.shape, lambda i: (0, 0, 0)), pl.BlockSpec((1, d), lambda i: (0, 0))],
        out_specs=(row, row), compiler_params=_cp(1), name="pool_fwd",
    )(h, h, pw, pscale)


def pool_bwd_proj(dm, pooled, pw, pscale):
    t, d = dm.shape
    tp = _tile(t, 256)

    def body(dm_ref, pooled_ref, pw_ref, ps_ref, dp_ref, dy_ref, dps_ref):
        dmv = dm_ref[...]
        dy = (dmv * ps_ref[...]).astype(BF16)
        dy_ref[...] = dy
        ys, dps = [], []
        for gi in range(len(POOL_WINDOWS)):
            gs = slice(gi * POOL_GROUP_DIM, (gi + 1) * POOL_GROUP_DIM)
            ys.append(_dot(pooled_ref[:, gs], pw_ref[gi]))
            dps.append(_dot(dy[:, gs], pw_ref[gi], "nt"))
        dp_ref[...] = jnp.concatenate(dps, axis=1)

        @pl.when(pl.program_id(0) == 0)
        def _():
            dps_ref[...] = jnp.zeros_like(dps_ref)

        dps_ref[...] += jnp.sum(dmv * jnp.concatenate(ys, axis=1), axis=0, keepdims=True)

    row = pl.BlockSpec((tp, d), lambda i: (i, 0))
    vec = pl.BlockSpec((1, d), lambda i: (0, 0))
    return pl.pallas_call(
        body,
        out_shape=(jax.ShapeDtypeStruct((t, d), F32), jax.ShapeDtypeStruct((t, d), BF16),
                   jax.ShapeDtypeStruct((1, d), F32)),
        grid=(t // tp,),
        in_specs=[row, row, pl.BlockSpec(pw.shape, lambda i: (0, 0, 0)), vec],
        out_specs=(row, row, vec), compiler_params=_cp(1), name="pool_bwd_proj",
    )(dm, pooled, pw, pscale)


def pool_bwd_window(dp):
    t, d = dp.shape
    tp = _tile(t, 256)
    per = tp // POOL_HALO
    last = t // POOL_HALO - 1
    nt = t // tp

    def body(dp_ref, halo_ref, dh_ref):
        i = pl.program_id(0)
        cur = dp_ref[...]
        halo = jnp.where(i < nt - 1, halo_ref[...], 0.0)
        outs = []
        for gi, w in enumerate(POOL_WINDOWS):
            gs = slice(gi * POOL_GROUP_DIM, (gi + 1) * POOL_GROUP_DIM)
            s = jnp.concatenate([cur[:, gs] / _pool_count(i, tp, w), halo[:, gs] / float(w)], axis=0)
            sh = 1
            while sh < w:
                s = s + pltpu.roll(s, tp + POOL_HALO - sh, 0)
                sh *= 2
            outs.append(s[:tp, :] - cur[:, gs])
        dh_ref[...] = jnp.concatenate(outs, axis=1)

    row = pl.BlockSpec((tp, d), lambda i: (i, 0))
    return pl.pallas_call(
        body, out_shape=jax.ShapeDtypeStruct((t, d), F32), grid=(nt,),
        in_specs=[row, pl.BlockSpec((POOL_HALO, d), lambda i: (jnp.minimum((i + 1) * per, last), 0))],
        out_specs=row, compiler_params=_cp(1), name="pool_bwd_window",
    )(dp, dp)


def pool_wgrad(pooled, dy):
    t, d = pooled.shape
    ng = d // POOL_GROUP_DIM
    tk = _tile(t, 512)
    blk = pl.BlockSpec((tk, POOL_GROUP_DIM), lambda g, k: (k, g))
    return _mm(
        "pool_wgrad", pooled, dy, dims="tn", grid=(ng, t // tk), a_spec=blk, b_spec=blk,
        o_spec=pl.BlockSpec((None, POOL_GROUP_DIM, POOL_GROUP_DIM), lambda g, k: (g, 0, 0)),
        out_shape=jax.ShapeDtypeStruct((ng, POOL_GROUP_DIM, POOL_GROUP_DIM), F32),
        acc_shape=(POOL_GROUP_DIM, POOL_GROUP_DIM), nk=t // tk,
    )


def _xattn_probs(qh, kh):
    s = _dot(qh, kh, "nt") * (X_HEAD_DIM ** -0.5)
    p = jnp.exp(s - jnp.max(s, axis=1, keepdims=True))
    return p * (1.0 / jnp.sum(p, axis=1, keepdims=True))


def xattn_fwd(name, q, k, v):
    t, xw = q.shape
    tm = _tile(t, 512)

    def body(q_ref, k_ref, v_ref, o_ref):
        outs = []
        for h in range(X_HEADS):
            hs = slice(h * X_HEAD_DIM, (h + 1) * X_HEAD_DIM)
            p = _xattn_probs(q_ref[:, hs], k_ref[:, hs])
            outs.append(_dot(p.astype(BF16), v_ref[:, hs]))
        o_ref[...] = jnp.concatenate(outs, axis=1).astype(BF16)

    row = pl.BlockSpec((tm, xw), lambda i: (i, 0))
    kv = pl.BlockSpec(k.shape, lambda i: (0, 0))
    return pl.pallas_call(
        body, out_shape=jax.ShapeDtypeStruct((t, xw), BF16), grid=(t // tm,),
        in_specs=[row, kv, kv], out_specs=row, compiler_params=_cp(1), name=name,
    )(q, k, v)


def xattn_bwd(name, q, k, v, do):
    t, xw = q.shape
    tm = _tile(t, 512)

    def body(q_ref, k_ref, v_ref, do_ref, dq_ref, dk_ref, dv_ref):
        @pl.when(pl.program_id(0) == 0)
        def _():
            dk_ref[...] = jnp.zeros_like(dk_ref)
            dv_ref[...] = jnp.zeros_like(dv_ref)

        dqs = []
        for h in range(X_HEADS):
            hs = slice(h * X_HEAD_DIM, (h + 1) * X_HEAD_DIM)
            qh, kh, vh, doh = q_ref[:, hs], k_ref[:, hs], v_ref[:, hs], do_ref[:, hs]
            p = _xattn_probs(qh, kh)
            dv_ref[:, hs] += _dot(p.astype(BF16), doh, "tn")
            dp = _dot(doh, vh, "nt")
            ds = (p * (dp - jnp.sum(p * dp, axis=1, keepdims=True)) * (X_HEAD_DIM ** -0.5)).astype(BF16)
            dqs.append(_dot(ds, kh))
            dk_ref[:, hs] += _dot(ds, qh, "tn")
        dq_ref[...] = jnp.concatenate(dqs, axis=1).astype(BF16)

    row = pl.BlockSpec((tm, xw), lambda i: (i, 0))
    kv = pl.BlockSpec(k.shape, lambda i: (0, 0))
    kv_shape = jax.ShapeDtypeStruct(k.shape, F32)
    return pl.pallas_call(
        body, out_shape=(jax.ShapeDtypeStruct((t, xw), BF16), kv_shape, kv_shape), grid=(t // tm,),
        in_specs=[row, kv, kv, row], out_specs=(row, kv, kv), compiler_params=_cp(1), name=name,
    )(q, k, v, do)


def kernel(x, mem, norms, mem_norm, ffn1_wg, ffn1_wu, ffn1_wd, ffn2_wg, ffn2_wu, ffn2_wd, x_wq, x_wk, x_wv, x_wo, mix_w_in, mix_w_out, attn_sinks, sgu_ln_g, sgu_ln_b, sgu_w, sgu_b, pool_w, pool_scale, loss_target, m_norms, m_mem_norm, m_ffn1_wg, m_ffn1_wu, m_ffn1_wd, m_ffn2_wg, m_ffn2_wu, m_ffn2_wd, m_x_wq, m_x_wk, m_x_wv, m_x_wo, m_mix_w_in, m_mix_w_out, m_attn_sinks, m_sgu_ln_g, m_sgu_ln_b, m_sgu_w, m_sgu_b, m_pool_w, m_pool_scale, v_norms, v_mem_norm, v_ffn1_wg, v_ffn1_wu, v_ffn1_wd, v_ffn2_wg, v_ffn2_wu, v_ffn2_wd, v_x_wq, v_x_wk, v_x_wv, v_x_wo, v_mix_w_in, v_mix_w_out, v_attn_sinks, v_sgu_ln_g, v_sgu_ln_b, v_sgu_w, v_sgu_b, v_pool_w, v_pool_scale):
    params = dict(norms=norms, mem_norm=mem_norm, ffn1_wg=ffn1_wg, ffn1_wu=ffn1_wu, ffn1_wd=ffn1_wd,
                  ffn2_wg=ffn2_wg, ffn2_wu=ffn2_wu, ffn2_wd=ffn2_wd, x_wq=x_wq, x_wk=x_wk, x_wv=x_wv, x_wo=x_wo,
                  mix_w_in=mix_w_in, mix_w_out=mix_w_out, attn_sinks=attn_sinks, sgu_ln_g=sgu_ln_g,
                  sgu_ln_b=sgu_ln_b, sgu_w=sgu_w, sgu_b=sgu_b, pool_w=pool_w, pool_scale=pool_scale)
    mom1 = dict(norms=m_norms, mem_norm=m_mem_norm, ffn1_wg=m_ffn1_wg, ffn1_wu=m_ffn1_wu, ffn1_wd=m_ffn1_wd,
                ffn2_wg=m_ffn2_wg, ffn2_wu=m_ffn2_wu, ffn2_wd=m_ffn2_wd, x_wq=m_x_wq, x_wk=m_x_wk, x_wv=m_x_wv,
                x_wo=m_x_wo, mix_w_in=m_mix_w_in, mix_w_out=m_mix_w_out, attn_sinks=m_attn_sinks,
                sgu_ln_g=m_sgu_ln_g, sgu_ln_b=m_sgu_ln_b, sgu_w=m_sgu_w, sgu_b=m_sgu_b, pool_w=m_pool_w,
                pool_scale=m_pool_scale)
    mom2 = dict(norms=v_norms, mem_norm=v_mem_norm, ffn1_wg=v_ffn1_wg, ffn1_wu=v_ffn1_wu, ffn1_wd=v_ffn1_wd,
                ffn2_wg=v_ffn2_wg, ffn2_wu=v_ffn2_wu, ffn2_wd=v_ffn2_wd, x_wq=v_x_wq, x_wk=v_x_wk, x_wv=v_x_wv,
                x_wo=v_x_wo, mix_w_in=v_mix_w_in, mix_w_out=v_mix_w_out, attn_sinks=v_attn_sinks,
                sgu_ln_g=v_sgu_ln_g, sgu_ln_b=v_sgu_ln_b, sgu_w=v_sgu_w, sgu_b=v_sgu_b, pool_w=v_pool_w,
                pool_scale=v_pool_scale)
    order = list(params)

    xs, memb, target = x[0], mem[0], loss_target[0]
    t, d = xs.shape
    depth = norms.shape[0]
    dsh = d // NDEV

    bf = lambda a: a.astype(BF16)
    wts = {}

    def gather_job(keys):
        return _Gather([bf(params[name][l]) for name, l in keys]), keys

    def land(job_keys):
        job, keys = job_keys
        for key, a in zip(keys, job.result):
            wts[key] = a

    ffn_keys = lambda tag, l: [(f"{tag}_wg", l), (f"{tag}_wu", l), (f"{tag}_wd", l)]
    x_keys = lambda l: [("x_wq", l), ("x_wk", l), ("x_wv", l), ("x_wo", l)]
    small_shapes = [norms.shape, pool_scale.shape, pool_w.shape]
    head = gather_job(ffn_keys("ffn1", 0))
    head_small = _Gather([_pack([norms, pool_scale, pool_w])])
    _run_exchange("gather_head", _Multi([head_small, head[0]]))
    land(head)
    norms_sh, pscale_sh, pw_sh = _unpack(head_small.result[0], small_shapes, (NDEV,))
    norms_full = norms_sh.transpose(1, 2, 0, 3).reshape(depth, norms.shape[1], d)
    pscale_full = pscale_sh.transpose(1, 0, 2).reshape(1, d)
    pw_full = pw_sh[:, 0].transpose(1, 0, 2, 3).reshape(len(POOL_WINDOWS), POOL_GROUP_DIM, POOL_GROUP_DIM).astype(BF16)
    fwd_jobs = {
        ("ffn1", 0): gather_job([("mix_w_in", 0), ("mix_w_out", 0)] + x_keys(0) + ffn_keys("ffn2", 0)[:2]),
        ("attn", 0): gather_job(ffn_keys("ffn2", 0)[2:]),
        ("ffn2", 0): gather_job(ffn_keys("ffn1", 1)),
        ("ffn1", 1): gather_job(x_keys(1) + ffn_keys("ffn2", 1)),
    }

    tab = rope_table(t)
    sgu_w0 = sgu_w[0]
    sgu_wt0 = sgu_w0.transpose(0, 2, 1)
    sgu_bt0 = sgu_b[0].T
    gain = lambda l, i: norms_full[l, i][None, :]

    saved = []
    xc = xs
    h = norm_fwd("norm_ffn1_0", xc, gain(0, 0), BF16)
    for l in range(depth):
        s = {}
        pooling = l % 2 == 1

        def ffn_forward(tag, xc, h, gi, nxt, l=l, s=s):
            s[tag + "_h"] = h
            job = fwd_jobs.get((tag, l))
            s[tag + "_a"], s[tag + "_b"], s[tag + "_hidt"], s[tag + "_m"] = ffn_fwd(
                f"{tag}_fwd_{l}", h, wts[tag + "_wg", l], wts[tag + "_wu", l], wts[tag + "_wd", l],
                comm=job and job[0])
            if job:
                land(job)
            return resid_norm_fwd(f"resid_{tag}_{l}", xc, s[tag + "_m"], gain(l, gi + 1), 0.5, nxt)

        s["x0"] = xc
        if pooling:
            xc, h2 = ffn_forward("ffn1", xc, h, 0, (gain(l, 2), F32, False))
        else:
            xc, h2, s["h2t"] = ffn_forward("ffn1", xc, h, 0, (gain(l, 2), BF16, True))

        s["x1"] = xc
        if not pooling:
            w_in = wts["mix_w_in", l].transpose(1, 0, 2).reshape(d, IN_WIDTH)
            o_k, o_u = ATTN_WIDTH, ATTN_WIDTH + 2 * KV_WIDTH
            w_in = jnp.concatenate([w_in[:, :o_k], w_in[:, o_u:], w_in[:, o_k:o_u]], axis=1)
            w_out = wts["mix_w_out", l].reshape(d, d)
            s["z"] = mm_nn("mix_in", h2, w_in, F32, tn=IN_WIDTH // 2)
            job = fwd_jobs[("attn", l)]
            attn = attn_fwd(s["z"], tab, attn_sinks, comm=job[0])
            land(job)
            gate = sgu_fwd(s["z"], sgu_ln_g, sgu_ln_b, sgu_w0, sgu_bt0)
            s["cat"] = jnp.concatenate([attn, gate], axis=1)
            s["m2"] = mm_nn("mix_out", s["cat"], w_out, F32)
        else:
            s["m2"], s["pooled"] = pool_fwd(h2, pw_full, pscale_full)
        xc, h3, s["h3t"] = resid_norm_fwd(f"resid_mix_{l}", xc, s["m2"], gain(l, 3), 1.0, (gain(l, 4), BF16, True))

        s["x2"] = xc
        wq, wk, wv = (wts[k, l].reshape(d, -1) for k in ("x_wq", "x_wk", "x_wv"))
        s["wq"], s["wkv"] = wq, jnp.concatenate([wk, wv], axis=1)
        s["mem_n"] = norm_fwd(f"norm_mem_{l}", memb, mem_norm[l][None, :], BF16)
        s["q"] = mm_nn(f"x_q_{l}", h3, wq, BF16)
        s["k"] = mm_nn(f"x_k_{l}", s["mem_n"], wk, BF16)
        s["v"] = mm_nn(f"x_v_{l}", s["mem_n"], wv, BF16)
        s["o"] = xattn_fwd(f"xattn_fwd_{l}", s["q"], s["k"], s["v"])
        s["wo"] = wts["x_wo", l].transpose(1, 0, 2).reshape(-1, d)
        s["m3"] = mm_nn(f"x_o_{l}", s["o"], s["wo"], F32)
        xc, h4 = resid_norm_fwd(f"resid_x_{l}", xc, s["m3"], gain(l, 5), 1.0, (gain(l, 6), BF16, False))

        s["x3"] = xc
        if l + 1 < depth:
            xc, h = ffn_forward("ffn2", xc, h4, 6, (gain(l + 1, 0), BF16, False))
        else:
            xc = ffn_forward("ffn2", xc, h4, 6, None)
        saved.append(s)

    dx, loss11 = loss_grad(xc, target)
    loss = lax.psum(loss11[0, 0], ("x", "y", "c"))

    swaps = []
    pending = []
    recv = {}

    def emit_units(name, l, arr, among_chips):
        piece_mib = math.prod(arr.shape[1:]) * arr.dtype.itemsize / MIB
        parts = 2 if piece_mib > 1.5 else 1
        rows = arr.shape[1] // parts
        cost = (LINK_US_PER_MIB_CHIPS if among_chips else LINK_US_PER_MIB_ALL) * piece_mib / parts
        for part in range(parts):
            pending.append(((name, l, part), among_chips, (arr, part * rows, rows), cost))

    def emit(name, l, arr, two_level=False):
        if two_level:
            swaps.append((name, l, arr.reshape((NCHIP, 2) + arr.shape[1:])))
        else:
            emit_units(name, l, arr, False)

    def hosted(budget_us, fn, *args, extra=(), force=True, **kw):
        jobs = list(extra)
        swapped = swaps[:]
        del swaps[:]
        used = PAIR_SWAP_US * len(swapped)
        if swapped:
            jobs.append(_PairSwap([g for _, _, g in swapped]))
        items, kept = [], []
        for it in pending:
            if (force and not items and not swapped) or used + it[3] <= budget_us:
                items.append(it)
                used += it[3]
            else:
                kept.append(it)
        pending[:] = kept
        groups = [[it for it in items if it[1] == flag] for flag in (False, True)]
        unit_jobs = [cls([it[2] for it in grp]) if grp else None
                     for cls, grp in zip((_Scatter, _ChipScatter), groups)]
        jobs += [j for j in unit_jobs if j is not None]
        res = fn(*args, comm=_Multi(jobs) if jobs else None, **kw)
        for job, grp in zip(unit_jobs, groups):
            for it, a in zip(grp, job.result if job else ()):
                recv[it[0]] = a
        if swapped:
            for (name, l, g4), theirs in zip(swapped, jobs[len(extra)].result):
                emit_units(name, l, pair_sum(f"pairsum_{name}_{l}", g4, theirs), True)
        return res

    grads = {"mem_norm": [None] * depth}
    dgs = [[None] * 8 for _ in range(depth)]
    small_jobs = []
    dm, dgs[depth - 1][7] = norm_bwd(f"ffn2_post_bwd_{depth - 1}", saved[-1]["ffn2_m"], gain(depth - 1, 7), dx, 0.5,
                                     None, BF16)
    for l in reversed(range(depth)):
        s = saved[l]
        dg = dgs[l]
        pooling = l % 2 == 1

        def ffn_block(tag, dm, extra=(), l=l, s=s):
            da_t, db_t, dh = hosted(HOST_US_FFN_BWD, ffn_bwd, f"{tag}_bwd_{l}", dm, s[tag + "_a"], s[tag + "_b"],
                                    wts[tag + "_wg", l], wts[tag + "_wu", l], wts[tag + "_wd", l], extra=extra)
            emit(tag + "_wg", l, hosted(HOST_US_WGRAD, ffn_wgrad, f"{tag}_dwg_{l}", da_t, s[tag + "_h"]), True)
            emit(tag + "_wu", l, hosted(HOST_US_WGRAD, ffn_wgrad, f"{tag}_dwu_{l}", db_t, s[tag + "_h"]), True)
            emit(tag + "_wd", l, hosted(HOST_US_WGRAD, ffn_wgrad, f"{tag}_dwd_{l}", s[tag + "_hidt"], dm), True)
            return dh

        dh = ffn_block("ffn2", dm)
        dx, dg[6], dm, dg[5] = hosted(
            HOST_US_SMALL, norm_bwd, f"ffn2_pre_bwd_{l}", s["x3"], gain(l, 6), dh, 1.0, dx, F32,
            post=(s["m3"], gain(l, 5), 1.0, BF16), force=False)

        do = mm_nt(f"x_do_{l}", dm, s["wo"], BF16)
        g_wo = mm_tn(f"x_dwo_{l}", s["o"], dm, BF16, tmo=s["o"].shape[1])
        emit("x_wo", l, g_wo.reshape(-1, NDEV, dsh).transpose(1, 0, 2))
        dq, dk, dv = xattn_bwd(f"xattn_bwd_{l}", s["q"], s["k"], s["v"], do)
        dkb, dvb = dk.astype(BF16), dv.astype(BF16)
        emit("x_wq", l, mm_kred(f"x_dwq_{l}", s["h3t"], dq, BF16).reshape(NDEV, dsh, -1))
        emit("x_wk", l, mm_tn(f"x_dwk_{l}", s["mem_n"], dkb, BF16).reshape(NDEV, dsh, -1))
        emit("x_wv", l, mm_tn(f"x_dwv_{l}", s["mem_n"], dvb, BF16).reshape(NDEV, dsh, -1))
        dh = mm_nt(f"x_dh_{l}", dq, s["wq"], F32)
        dmem_n = mm_nt(f"x_dmem_{l}", jnp.concatenate([dkb, dvb], axis=1), s["wkv"], F32)
        _, grads["mem_norm"][l] = norm_bwd(f"mem_norm_bwd_{l}", memb, mem_norm[l][None, :], dmem_n, 1.0, None, F32)
        dx, dg[4], dm, dg[3] = norm_bwd(f"x_pre_bwd_{l}", s["x2"], gain(l, 4), dh, 1.0, dx, F32,
                                        post=(s["m2"], gain(l, 3), 1.0, F32 if pooling else BF16))

        if not pooling:
            dcat = mm_nt("mix_dcat", dm, w_out, BF16)
            emit("mix_w_out", l, mm_tn("mix_dwout", s["cat"], dm, BF16, tno=d // 2).reshape(NDEV, dsh, d))
            dq_a, dkv_a, dsink = hosted(HOST_US_ATTN_BWD, attn_bwd, s["z"], tab, attn_sinks, dcat, force=False)
            du_s, dv_s, g_sgu_w, g_sgu_bt, g_ln_g, g_ln_b = sgu_bwd(
                s["z"], sgu_ln_g, sgu_ln_b, sgu_w0, sgu_wt0, sgu_bt0, dcat)
            dz = dz_assemble(dq_a, dkv_a, du_s, dv_s)
            dh = mm_nt("mix_dh", dz, w_in, F32, tn=d // 2)
            g_win = hosted(2 * HOST_US_SMALL, mm_kred, "mix_dwin", s["h2t"], dz, BF16, tno=IN_WIDTH // 2, force=False)
            g_win = jnp.concatenate([g_win[:, :ATTN_WIDTH], g_win[:, ZK:], g_win[:, ZU:ZK]], axis=1)
            emit("mix_w_in", l, g_win.reshape(d, NDEV, -1).transpose(1, 0, 2))
        else:
            dp, dy, g_pscale = pool_bwd_proj(dm, s["pooled"], pw_full, pscale_full)
            g_pw = pool_wgrad(s["pooled"], dy)
            emit("pool_w", 0, g_pw.reshape(len(POOL_WINDOWS), NDEV, -1, POOL_GROUP_DIM).transpose(1, 0, 2, 3)
                 .reshape(NDEV, -1, POOL_GROUP_DIM))
            dh = pool_bwd_window(dp)
        dx, dg[2], dm, dg[1] = norm_bwd(f"mix_pre_bwd_{l}", s["x1"], gain(l, 2), dh, 1.0, dx, F32,
                                        post=(s["ffn1_m"], gain(l, 1), 0.5, BF16))

        if l == 0:
            replicated = ["mem_norm", "attn_sinks", "sgu_ln_g", "sgu_ln_b", "sgu_w", "sgu_b"]
            rep_grads = [jnp.concatenate(grads["mem_norm"], axis=0), dsink[:, :N_Q_HEADS], g_ln_g, g_ln_b,
                         g_sgu_w[None], g_sgu_bt[:, :SGU_GROUPS].T[None]]
            small_jobs.append(_Gather([_pack(rep_grads)]))
        dh = ffn_block("ffn1", dm, extra=small_jobs if l == 0 else ())
        if l > 0:
            dx, dg[0], dm, dgs[l - 1][7] = hosted(
                HOST_US_SMALL, norm_bwd, f"ffn1_pre_bwd_{l}", s["x0"], gain(l, 0), dh, 1.0, dx, F32,
                post=(saved[l - 1]["ffn2_m"], gain(l - 1, 7), 0.5, BF16), force=False)
        else:
            dx, dg[0] = norm_bwd(f"ffn1_pre_bwd_{l}", s["x0"], gain(l, 0), dh, 1.0, dx, F32)

    g_norms = jnp.stack([jnp.concatenate(dg, axis=0) for dg in dgs], axis=0)
    g_norms = g_norms.reshape(depth, norms.shape[1], NDEV, dsh).transpose(2, 0, 1, 3)
    g_pscale_p = g_pscale.reshape(1, NDEV, dsh).transpose(1, 0, 2)
    sharded_small = ["norms", "pool_scale"]
    pieces_small = jnp.stack([_pack([g_norms[j], g_pscale_p[j]]) for j in range(NDEV)], axis=0)
    small_scatter = _Scatter([(pieces_small, 0, pieces_small.shape[1])])

    out = {}

    def update(k, extra=(), host=True):
        waiting = [it[0] for it in pending if it[0][0] == k] + [it[:2] for it in swaps if it[0] == k]
        assert not waiting, waiting
        shp = params[k].shape
        flip = k.endswith(("_wg", "_wu"))
        c = shp[1] if flip else shp[-1]
        view = lambda a: (a.swapaxes(1, 2) if flip else a).reshape(-1, c)
        pieces = [recv[key] for key in sorted(key for key in recv if key[0] == k)]
        args = (f"adamw_{k}", view(params[k]), view(mom1[k]), view(mom2[k]), pieces)
        res = hosted(HOST_US_ADAMW, adamw, *args, extra=extra) if host else adamw(*args)
        if flip:
            out[k] = [a.reshape(shp[0], shp[2], shp[1]).swapaxes(1, 2) for a in res]
        else:
            out[k] = [a.reshape(shp) for a in res]

    def update_pack(names, pieces):
        shapes = [params[k].shape for k in names]
        res = adamw("adamw_" + names[0] + "_pack", _pack([params[k] for k in names]),
                    _pack([mom1[k] for k in names]), _pack([mom2[k] for k in names]), [pieces])
        for which in range(4):
            for k, a in zip(names, _unpack(res[which], shapes)):
                out.setdefault(k, [None] * 4)[which] = a

    last = ("ffn1_wg", "ffn1_wu", "ffn1_wd")
    big = ("ffn2_wg", "ffn2_wu", "ffn2_wd")
    early = [k for k in order if k not in last + big and k not in sharded_small and k not in replicated]
    for i, k in enumerate(early):
        update(k, extra=[small_scatter] if i == 0 else ())
    for k in big:
        update(k, host=False)
    flushes = 0
    while pending or swaps:
        hosted(float("inf"), lambda comm: _run_exchange(f"scatter_tail_{flushes}", comm))
        flushes += 1
    update_pack(replicated, small_jobs[0].result[0])
    update_pack(sharded_small, small_scatter.result[0])
    for k in last:
        update(k, host=False)

    outputs = [loss, dx[None]]
    for which in range(4):
        outputs += [out[k][which] for k in order]
    return tuple(outputs)
```

```python
import math

import jax
import jax.numpy as jnp
from jax import lax
from jax.experimental import pallas as pl
from jax.experimental.pallas import tpu as pltpu

F32 = jnp.float32
BF16 = jnp.bfloat16
NDEV = 8
MIB = 1024 * 1024
LANES = 128

RMS_EPS = 1e-6
HEAD_DIM = 64
N_Q_HEADS = 16
Q_PER_KV = 8
ATTN_WIDTH = 1024
KV_WIDTH = 128
BLOCK = 128
ROPE_DIM = 16
ROPE_THETA = 500000.0
SGU_GROUPS = 8
SGU_WIDTH = 1024
CHUNK = 128
POOL_WINDOWS = (2, 4, 8, 16)
POOL_GROUP_DIM = 512
POOL_HALO = 16
X_HEADS = 4
X_HEAD_DIM = 128
ZQ, ZU, ZV, ZK = 0, 1024, 2048, 3072
IN_WIDTH = 3328

ADAM_LR = 0.001
ADAM_B1 = 0.9
ADAM_B2 = 0.999
ADAM_EPS = 1e-08
ADAM_WD = 0.01
ADAM_STEP = 10

FFN_FWD_ROWS = 512
FFN_BWD_ROWS = 512

LINK_US_PER_MIB_ALL = 91.0
LINK_US_PER_MIB_CHIPS = 45.0
PAIR_SWAP_US = 20.0
HOST_US_FFN_BWD = 420.0
HOST_US_WGRAD = 100.0
HOST_US_ATTN_BWD = 240.0
HOST_US_SMALL = 40.0
HOST_US_ADAMW = 40.0

_DN = {
    "nn": (((1,), (0,)), ((), ())),
    "nt": (((1,), (1,)), ((), ())),
    "tn": (((0,), (0,)), ((), ())),
}


def _cp(naxes, vmem_mib=48):
    return pltpu.CompilerParams(dimension_semantics=("arbitrary",) * naxes, vmem_limit_bytes=vmem_mib * MIB)


def _tile(n, pref):
    t = min(n, pref)
    while n % t:
        t //= 2
    return t


def _dot(a, b, dims="nn"):
    return lax.dot_general(a, b, _DN[dims], preferred_element_type=F32)


def _me():
    x, y, c = lax.axis_index("x"), lax.axis_index("y"), lax.axis_index("c")
    return x, y, c, 4 * x + 2 * y + c


def _peer(k):
    x, y, c, _ = _me()
    px = 1 - x if k & 4 else x
    py = 1 - y if k & 2 else y
    pc = 1 - c if k & 1 else c
    return (px, py, pc), 4 * px + 2 * py + pc


class _Exchange:
    def __init__(self, arrs, out_shape, remote_per=NDEV - 1, local_per=1):
        self.arrs = list(arrs)
        self.n = len(self.arrs)
        self.out_shape = list(out_shape)
        self.remote_per = remote_per
        self.scratch = [
            pltpu.SemaphoreType.DMA((self.n * remote_per,)),
            pltpu.SemaphoreType.DMA((self.n * remote_per,)),
            pltpu.SemaphoreType.DMA((self.n * local_per,)),
        ]
        self.result = None

    def mid(self, ins, outs, sems):
        pass

    def _copy(self, src, dst, sems, i, k, dev):
        send, recv, _ = sems
        return pltpu.make_async_remote_copy(
            src_ref=src, dst_ref=dst, send_sem=send.at[i * self.remote_per + k - 1],
            recv_sem=recv.at[i * self.remote_per + k - 1], device_id=dev, device_id_type=pl.DeviceIdType.MESH)


class _Gather(_Exchange):
    def __init__(self, arrs):
        super().__init__(arrs, [jax.ShapeDtypeStruct((NDEV,) + a.shape, a.dtype) for a in arrs])

    def start(self, ins, outs, sems):
        me = _me()[3]
        for i in range(self.n):
            pltpu.make_async_copy(ins[i], outs[i].at[me], sems[2].at[i]).start()
        for k in (1, 2, 4, 6):
            dev, _ = _peer(k)
            for i in range(self.n):
                self._copy(ins[i], outs[i].at[me], sems, i, k, dev).start()

    def mid(self, ins, outs, sems):
        sibling, _ = _peer(1)
        for k in (2, 4, 6):
            dev, slot = _peer(k)
            for i in range(self.n):
                block = outs[i].at[slot]
                self._copy(ins[i], block, sems, i, k, dev).wait_recv()
                self._copy(block, block, sems, i, k + 1, sibling).start()

    def finish(self, ins, outs, sems):
        me = _me()[3]
        sibling, _ = _peer(1)
        for k in (1, 3, 5, 7):
            dev, slot = _peer(k)
            for i in range(self.n):
                self._copy(ins[i], outs[i].at[slot], sems, i, k, dev).wait_recv()
        for k in range(1, NDEV):
            for i in range(self.n):
                self._copy(ins[i], outs[i].at[me], sems, i, k, sibling).wait_send()
        for i in range(self.n):
            pltpu.make_async_copy(ins[i], outs[i].at[me], sems[2].at[i]).wait()


class _Scatter(_Exchange):
    def __init__(self, units):
        self.rows = [(r0, n) for _, r0, n in units]
        super().__init__([a for a, _, _ in units],
                         [jax.ShapeDtypeStruct((NDEV, n) + a.shape[2:], a.dtype) for a, _, n in units])

    def _src(self, ins, i, slot):
        r0, n = self.rows[i]
        return ins[i].at[slot, pl.ds(r0, n)]

    def start(self, ins, outs, sems):
        me = _me()[3]
        for i in range(self.n):
            pltpu.make_async_copy(self._src(ins, i, me), outs[i].at[me], sems[2].at[i]).start()
        for k in range(1, NDEV):
            dev, slot = _peer(k)
            for i in range(self.n):
                self._copy(self._src(ins, i, slot), outs[i].at[me], sems, i, k, dev).start()

    def finish(self, ins, outs, sems):
        me = _me()[3]
        for k in range(1, NDEV):
            dev, slot = _peer(k)
            for i in range(self.n):
                cp = self._copy(self._src(ins, i, slot), outs[i].at[slot], sems, i, k, dev)
                cp.wait_send()
                cp.wait_recv()
        for i in range(self.n):
            pltpu.make_async_copy(self._src(ins, i, me), outs[i].at[me], sems[2].at[i]).wait()


NCHIP = NDEV // 2


class _PairSwap(_Exchange):
    def __init__(self, arrs):
        super().__init__(arrs, [jax.ShapeDtypeStruct((NCHIP,) + a.shape[2:], a.dtype) for a in arrs],
                         remote_per=NCHIP)

    def _copies(self, ins, outs, sems):
        c = _me()[2]
        sibling, _ = _peer(1)
        for i in range(self.n):
            for q in range(NCHIP):
                yield self._copy(ins[i].at[q, 1 - c], outs[i].at[q], sems, i, q + 1, sibling)

    def start(self, ins, outs, sems):
        for remote in self._copies(ins, outs, sems):
            remote.start()

    def finish(self, ins, outs, sems):
        for remote in self._copies(ins, outs, sems):
            remote.wait_send()
            remote.wait_recv()


class _ChipScatter(_Exchange):
    def __init__(self, units):
        self.rows = [(r0, n) for _, r0, n in units]
        super().__init__([a for a, _, _ in units],
                         [jax.ShapeDtypeStruct((NCHIP, n) + a.shape[2:], a.dtype) for a, _, n in units],
                         remote_per=NCHIP - 1)

    def _src(self, ins, i, chip):
        r0, n = self.rows[i]
        return ins[i].at[chip, pl.ds(r0, n)]

    @staticmethod
    def _chip(k):
        dev, slot = _peer(2 * k)
        return dev, slot // 2

    def start(self, ins, outs, sems):
        mine = _me()[3] // 2
        for i in range(self.n):
            pltpu.make_async_copy(self._src(ins, i, mine), outs[i].at[mine], sems[2].at[i]).start()
        for k in range(1, NCHIP):
            dev, chip = self._chip(k)
            for i in range(self.n):
                self._copy(self._src(ins, i, chip), outs[i].at[mine], sems, i, k, dev).start()

    def finish(self, ins, outs, sems):
        mine = _me()[3] // 2
        for k in range(1, NCHIP):
            dev, chip = self._chip(k)
            for i in range(self.n):
                cp = self._copy(self._src(ins, i, chip), outs[i].at[chip], sems, i, k, dev)
                cp.wait_send()
                cp.wait_recv()
        for i in range(self.n):
            pltpu.make_async_copy(self._src(ins, i, mine), outs[i].at[mine], sems[2].at[i]).wait()


class _Multi:
    def __init__(self, jobs):
        self.jobs = list(jobs)
        self.arrs = [a for j in self.jobs for a in j.arrs]
        self.out_shape = [s for j in self.jobs for s in j.out_shape]
        self.scratch = [s for j in self.jobs for s in j.scratch]
        self._result = None

    def _parts(self, ins, outs, sems):
        oi = oo = 0
        for idx, j in enumerate(self.jobs):
            ni, no = len(j.arrs), len(j.out_shape)
            yield j, ins[oi:oi + ni], outs[oo:oo + no], sems[3 * idx:3 * idx + 3]
            oi += ni
            oo += no

    def start(self, ins, outs, sems):
        for j, i, o, s in self._parts(ins, outs, sems):
            j.start(i, o, s)

    def mid(self, ins, outs, sems):
        for j, i, o, s in self._parts(ins, outs, sems):
            j.mid(i, o, s)

    def finish(self, ins, outs, sems):
        for j, i, o, s in self._parts(ins, outs, sems):
            j.finish(i, o, s)

    @property
    def result(self):
        return self._result

    @result.setter
    def result(self, res):
        self._result = res
        o = 0
        for j in self.jobs:
            j.result = list(res[o:o + len(j.out_shape)])
            o += len(j.out_shape)


def _call(name, body, *, grid, in_specs, out_specs, out_shape, args, scratch=(), comm=None, vmem_mib=48):
    in_specs, out_specs, out_shape = list(in_specs), list(out_specs), list(out_shape)
    scratch, args = list(scratch), list(args)
    ni, no, ns = len(in_specs), len(out_specs), len(scratch)
    kernel_fn = body
    if comm is not None:
        ci, co = len(comm.arrs), len(comm.out_shape)
        hbm = pl.BlockSpec(memory_space=pltpu.HBM)

        def kernel_fn(*refs):
            refs = list(refs)
            ins, c_in, outs, c_out, scr, c_scr = (
                [refs.pop(0) for _ in range(cnt)] for cnt in (ni, ci, no, co, ns, len(comm.scratch)))
            if not grid:
                comm.start(c_in, c_out, c_scr)
                body(*ins, *outs, *scr)
                comm.mid(c_in, c_out, c_scr)
                comm.finish(c_in, c_out, c_scr)
                return
            step = pl.program_id(0)
            for ax in range(1, len(grid)):
                step = step * grid[ax] + pl.program_id(ax)
            nsteps = math.prod(grid)
            mid_step = nsteps - 1

            @pl.when(step == 0)
            def _():
                comm.start(c_in, c_out, c_scr)

            body(*ins, *outs, *scr)

            @pl.when(step == mid_step)
            def _():
                comm.mid(c_in, c_out, c_scr)

            @pl.when(step == nsteps - 1)
            def _():
                comm.finish(c_in, c_out, c_scr)

        in_specs += [hbm] * ci
        out_specs += [hbm] * co
        out_shape += comm.out_shape
        scratch += comm.scratch
        args += comm.arrs
    params = _cp(len(grid), vmem_mib) if grid else None
    res = pl.pallas_call(
        kernel_fn, out_shape=out_shape, grid=grid, in_specs=in_specs, out_specs=out_specs,
        scratch_shapes=scratch, compiler_params=params, name=name,
    )(*args)
    if comm is not None:
        comm.result = list(res[no:])
    return tuple(res[:no])


def _run_exchange(name, comm):
    _call(name, lambda: None, grid=(), in_specs=[], out_specs=[], out_shape=[], args=[], comm=comm)
    return comm.result


def _pack(arrs, dtype=F32):
    flat = jnp.concatenate([a.astype(dtype).reshape(-1) for a in arrs])
    n = flat.shape[0]
    total = -(-n // (16 * LANES)) * (16 * LANES)
    return jnp.pad(flat, (0, total - n)).reshape(total // LANES, LANES)


def _unpack(packed, shapes, lead=()):
    flat = packed.reshape(lead + (-1,))
    out, off = [], 0
    for s in shapes:
        n = math.prod(s)
        out.append(flat[..., off:off + n].reshape(lead + tuple(s)))
        off += n
    return out


def _mm(name, a, b, *, dims, grid, a_spec, b_spec, o_spec, out_shape, acc_shape=None, nk=1, vmem_mib=48, comm=None):
    nax = len(grid)

    def body(a_ref, b_ref, o_ref, *scratch):
        p = _dot(a_ref[...], b_ref[...], dims)
        if nk == 1:
            o_ref[...] = p.astype(o_ref.dtype)
            return
        acc = scratch[0]
        k = pl.program_id(nax - 1)

        @pl.when(k == 0)
        def _():
            acc[...] = p

        @pl.when(k > 0)
        def _():
            acc[...] += p

        @pl.when(k == nk - 1)
        def _():
            o_ref[...] = acc[...].astype(o_ref.dtype)

    return _call(
        name, body, grid=grid, in_specs=[a_spec, b_spec], out_specs=[o_spec], out_shape=[out_shape], args=[a, b],
        scratch=[pltpu.VMEM(acc_shape, F32)] if nk > 1 else [], comm=comm, vmem_mib=vmem_mib,
    )[0]


def mm_nn(name, a, b, out_dtype, tn=None, comm=None):
    m, k = a.shape
    n = b.shape[1]
    tm = _tile(m, 512)
    tn = n if tn is None else tn
    return _mm(
        name, a, b, dims="nn", grid=(n // tn, m // tm),
        a_spec=pl.BlockSpec((tm, k), lambda j, i: (i, 0)),
        b_spec=pl.BlockSpec((k, tn), lambda j, i: (0, j)),
        o_spec=pl.BlockSpec((tm, tn), lambda j, i: (i, j)),
        out_shape=jax.ShapeDtypeStruct((m, n), out_dtype), comm=comm,
    )


def mm_nt(name, a, b, out_dtype, tn=None):
    m, k = a.shape
    n = b.shape[0]
    tm = _tile(m, 512)
    tn = n if tn is None else tn
    return _mm(
        name, a, b, dims="nt", grid=(n // tn, m // tm),
        a_spec=pl.BlockSpec((tm, k), lambda j, i: (i, 0)),
        b_spec=pl.BlockSpec((tn, k), lambda j, i: (j, 0)),
        o_spec=pl.BlockSpec((tm, tn), lambda j, i: (i, j)),
        out_shape=jax.ShapeDtypeStruct((m, n), out_dtype),
    )


def mm_kred(name, a_t, b, out_dtype, tno=None, comm=None):
    m, k = a_t.shape
    n = b.shape[1]
    tk = _tile(k, 512)
    tmo = _tile(m, 1024)
    tno = n if tno is None else tno
    return _mm(
        name, a_t, b, dims="nn", grid=(m // tmo, n // tno, k // tk),
        a_spec=pl.BlockSpec((tmo, tk), lambda i, j, kk: (i, kk)),
        b_spec=pl.BlockSpec((tk, tno), lambda i, j, kk: (kk, j)),
        o_spec=pl.BlockSpec((tmo, tno), lambda i, j, kk: (i, j)),
        out_shape=jax.ShapeDtypeStruct((m, n), out_dtype),
        acc_shape=(tmo, tno), nk=k // tk, comm=comm,
    )


def mm_tn(name, a, b, out_dtype, tmo=None, tno=None):
    k, m = a.shape
    n = b.shape[1]
    tk = _tile(k, 512)
    tmo = _tile(m, 1024) if tmo is None else tmo
    tno = n if tno is None else tno
    return _mm(
        name, a, b, dims="tn", grid=(m // tmo, n // tno, k // tk),
        a_spec=pl.BlockSpec((tk, tmo), lambda i, j, kk: (kk, i)),
        b_spec=pl.BlockSpec((tk, tno), lambda i, j, kk: (kk, j)),
        o_spec=pl.BlockSpec((tmo, tno), lambda i, j, kk: (i, j)),
        out_shape=jax.ShapeDtypeStruct((m, n), out_dtype),
        acc_shape=(tmo, tno), nk=k // tk,
    )


def _rstd(x):
    return lax.rsqrt(jnp.mean(x * x, axis=-1, keepdims=True) + RMS_EPS)


def norm_fwd(name, x, g, out_dtype, with_t=False, comm=None):
    t, d = x.shape
    tm = _tile(t, 256)

    def body(x_ref, g_ref, o_ref, *t_ref):
        xv = x_ref[...]
        h = xv * _rstd(xv) * g_ref[...]
        o_ref[...] = h.astype(o_ref.dtype)
        if with_t:
            t_ref[0][...] = h.T.astype(out_dtype)

    row = pl.BlockSpec((tm, d), lambda i: (i, 0))
    out_shape = [jax.ShapeDtypeStruct((t, d), out_dtype)]
    out_specs = [row]
    if with_t:
        out_shape.append(jax.ShapeDtypeStruct((d, t), out_dtype))
        out_specs.append(pl.BlockSpec((d, tm), lambda i: (0, i)))
    res = _call(
        name, body, grid=(t // tm,), in_specs=[row, pl.BlockSpec((1, d), lambda i: (0, 0))],
        out_specs=out_specs, out_shape=out_shape, args=[x, g], comm=comm,
    )
    return res if with_t else res[0]


def resid_norm_fwd(name, x, m, g, scale, nxt=None, comm=None):
    t, d = x.shape
    tm = _tile(t, 256)
    with_t = nxt is not None and nxt[2]

    def body(*refs):
        if nxt is None:
            x_ref, m_ref, g_ref, o_ref = refs
        else:
            x_ref, m_ref, g_ref, gn_ref, o_ref, h_ref = refs[:6]
        mv = m_ref[...]
        xn = x_ref[...] + scale * (mv * _rstd(mv) * g_ref[...])
        o_ref[...] = xn
        if nxt is not None:
            h = xn * _rstd(xn) * gn_ref[...]
            h_ref[...] = h.astype(h_ref.dtype)
            if with_t:
                refs[6][...] = h.T.astype(nxt[1])

    row = pl.BlockSpec((tm, d), lambda i: (i, 0))
    vec = pl.BlockSpec((1, d), lambda i: (0, 0))
    out_shape, out_specs, args = [jax.ShapeDtypeStruct((t, d), F32)], [row], [x, m, g]
    if nxt is not None:
        args.append(nxt[0])
        out_shape.append(jax.ShapeDtypeStruct((t, d), nxt[1]))
        out_specs.append(row)
        if with_t:
            out_shape.append(jax.ShapeDtypeStruct((d, t), nxt[1]))
            out_specs.append(pl.BlockSpec((d, tm), lambda i: (0, i)))
    res = _call(
        name, body, grid=(t // tm,), in_specs=[row, row, vec] + ([vec] if nxt is not None else []),
        out_specs=out_specs, out_shape=out_shape, args=args, comm=comm,
    )
    return res[0] if nxt is None else res


def norm_bwd(name, u, g, dy, scale, resid, out_dtype, post=None, comm=None):
    t, d = u.shape
    tm = _tile(t, 256)
    has_resid = resid is not None

    def norm_grad(uv, gv, dyv, dg_ref):
        r = _rstd(uv)
        uh = uv * r
        dg_ref[...] += jnp.sum(dyv * uh, axis=0, keepdims=True)
        dyg = dyv * gv
        return r * (dyg - uh * jnp.mean(dyg * uh, axis=-1, keepdims=True))

    def body(*refs):
        refs = list(refs)
        u_ref, g_ref, dy_ref = refs[:3]
        del refs[:3]
        r_ref = refs.pop(0) if has_resid else None
        if post is not None:
            m_ref, gp_ref = refs[:2]
            del refs[:2]
        du_ref, dg_ref = refs[:2]

        @pl.when(pl.program_id(0) == 0)
        def _():
            for ref in refs[1::2]:
                ref[...] = jnp.zeros_like(ref)

        du = norm_grad(u_ref[...], g_ref[...], dy_ref[...].astype(F32) * scale, dg_ref)
        if has_resid:
            du = du + r_ref[...]
        du_ref[...] = du.astype(du_ref.dtype)
        if post is not None:
            dm_ref, dgp_ref = refs[2:4]
            dm_ref[...] = norm_grad(m_ref[...], gp_ref[...], du * post[2], dgp_ref).astype(dm_ref.dtype)

    row = pl.BlockSpec((tm, d), lambda i: (i, 0))
    vec = pl.BlockSpec((1, d), lambda i: (0, 0))
    args = [u, g, dy] + ([resid] if has_resid else [])
    in_specs = [row, vec, row] + ([row] if has_resid else [])
    out_specs = [row, vec]
    out_shape = [jax.ShapeDtypeStruct((t, d), out_dtype), jax.ShapeDtypeStruct((1, d), F32)]
    if post is not None:
        args += [post[0], post[1]]
        in_specs += [row, vec]
        out_specs += [row, vec]
        out_shape += [jax.ShapeDtypeStruct((t, d), post[3]), jax.ShapeDtypeStruct((1, d), F32)]
    return _call(name, body, grid=(t // tm,), in_specs=in_specs, out_specs=out_specs, out_shape=out_shape,
                 args=args, comm=comm)


def loss_grad(y, target):
    t, d = y.shape
    tm = _tile(t, 256)
    nt = t // tm

    def body(y_ref, t_ref, dy_ref, loss_ref, acc):
        i = pl.program_id(0)
        e = y_ref[...] - t_ref[...]
        dy_ref[...] = e * (1.0 / d)

        @pl.when(i == 0)
        def _():
            acc[...] = jnp.zeros_like(acc)

        acc[...] += jnp.sum(e * e, axis=0, keepdims=True)

        @pl.when(i == nt - 1)
        def _():
            loss_ref[...] = (0.5 / d) * jnp.sum(acc[...], axis=1, keepdims=True)

    row = pl.BlockSpec((tm, d), lambda i: (i, 0))
    return pl.pallas_call(
        body,
        out_shape=(jax.ShapeDtypeStruct((t, d), F32), jax.ShapeDtypeStruct((1, 1), F32)),
        grid=(nt,), in_specs=[row, row], out_specs=(row, pl.BlockSpec((1, 1), lambda i: (0, 0))),
        scratch_shapes=[pltpu.VMEM((1, d), F32)], compiler_params=_cp(1), name="loss_grad",
    )(y, target)


def _row_tile(r, c):
    if r * c * 4 <= MIB:
        return r
    best = None
    for t in range(16, r, 16):
        if r % t == 0 and t * c * 4 <= MIB:
            best = t
    return r if best is None else best


def pair_sum(name, g4, theirs):
    nq, _, r, c = g4.shape
    tr = _tile(r, 1024)

    def body(core_ref, g_ref, t_ref, o_ref):
        o_ref[...] = (g_ref[...].astype(F32) + t_ref[...].astype(F32)).astype(o_ref.dtype)

    blk = pl.BlockSpec((None, tr, c), lambda q, i, core: (q, i, 0))
    return pl.pallas_call(
        body, out_shape=jax.ShapeDtypeStruct(theirs.shape, theirs.dtype),
        grid_spec=pltpu.PrefetchScalarGridSpec(
            num_scalar_prefetch=1, grid=(nq, r // tr),
            in_specs=[pl.BlockSpec((None, None, tr, c), lambda q, i, core: (q, core[0], i, 0)), blk],
            out_specs=blk),
        compiler_params=_cp(2), name=name,
    )(lax.axis_index("c").astype(jnp.int32).reshape(1), g4, theirs)


def adamw(name, w, m, v, pieces, comm=None):
    nl = len(pieces)
    npiece, r, c = pieces[0].shape
    tr = _row_tile(r, c)
    nr = r // tr
    bc1 = 1.0 - ADAM_B1 ** ADAM_STEP
    bc2 = 1.0 - ADAM_B2 ** ADAM_STEP

    def body(w_ref, m_ref, v_ref, *rest):
        p_refs, (g_ref, d_ref, nm_ref, nv_ref) = rest[:nl], rest[nl:]

        def update(p_ref):
            g = p_ref[0].astype(F32)
            for j in range(1, npiece):
                g = g + p_ref[j].astype(F32)
            m1 = ADAM_B1 * m_ref[...] + (1.0 - ADAM_B1) * g
            v1 = ADAM_B2 * v_ref[...] + (1.0 - ADAM_B2) * (g * g)
            m_hat = m1 / bc1
            v_hat = v1 / bc2
            g_ref[...] = g
            d_ref[...] = -ADAM_LR * (m_hat / (jnp.sqrt(v_hat) + ADAM_EPS) + ADAM_WD * w_ref[...])
            nm_ref[...] = m1
            nv_ref[...] = v1

        if nl == 1:
            update(p_refs[0])
        else:
            for ll in range(nl):
                pl.when(pl.program_id(0) == ll)(lambda ll=ll: update(p_refs[ll]))

    def piece_spec(ll):
        return pl.BlockSpec((npiece, tr, c), lambda l, i: (0, jnp.where(l == ll, i, jnp.where(l > ll, nr - 1, 0)), 0))

    row = pl.BlockSpec((tr, c), lambda l, i: (l * nr + i, 0))
    out = jax.ShapeDtypeStruct((nl * r, c), F32)
    return _call(
        name, body, grid=(nl, nr), in_specs=[row, row, row] + [piece_spec(ll) for ll in range(nl)],
        out_specs=[row] * 4, out_shape=[out] * 4, args=[w, m, v] + list(pieces), comm=comm,
    )


def _sigmoid(a):
    return 1.0 / (1.0 + jnp.exp(-a))


def ffn_fwd(name, h, wg, wu, wd, comm=None):
    t, d = h.shape
    ns, _, f = wg.shape
    tm = _tile(t, FFN_FWD_ROWS)

    def body(h_ref, wg_ref, wu_ref, wd_ref, a_ref, b_ref, hidt_ref, m_ref, acc):
        j = pl.program_id(1)
        hv = h_ref[...]
        a = _dot(hv, wg_ref[...])
        b = _dot(hv, wu_ref[...])
        hid32 = (a * _sigmoid(a)) * b
        hid = hid32.astype(BF16)
        a_ref[...] = a.astype(BF16)
        b_ref[...] = b.astype(BF16)
        hidt_ref[...] = hid32.T.astype(BF16)
        p = _dot(hid, wd_ref[...])

        @pl.when(j == 0)
        def _():
            acc[...] = p

        @pl.when(j > 0)
        def _():
            acc[...] += p

        @pl.when(j == ns - 1)
        def _():
            m_ref[...] = acc[...]

    w_in = pl.BlockSpec((None, d, f), lambda i, j: (j, 0, 0))
    act = pl.BlockSpec((None, tm, f), lambda i, j: (j, i, 0))
    act_shape = jax.ShapeDtypeStruct((ns, t, f), BF16)
    return _call(
        name, body, grid=(t // tm, ns),
        in_specs=[pl.BlockSpec((tm, d), lambda i, j: (i, 0)), w_in, w_in,
                  pl.BlockSpec((None, f, d), lambda i, j: (j, 0, 0))],
        out_specs=[act, act, pl.BlockSpec((None, f, tm), lambda i, j: (j, 0, i)),
                   pl.BlockSpec((tm, d), lambda i, j: (i, 0))],
        out_shape=[act_shape, act_shape, jax.ShapeDtypeStruct((ns, f, t), BF16), jax.ShapeDtypeStruct((t, d), F32)],
        args=[h, wg, wu, wd], scratch=[pltpu.VMEM((tm, d), F32)], comm=comm, vmem_mib=56,
    )


def ffn_bwd(name, dm, a, b, wg, wu, wd, comm=None):
    t, d = dm.shape
    ns, _, f = wg.shape
    tm = _tile(t, FFN_BWD_ROWS)

    def body(dm_ref, a_ref, b_ref, wg_ref, wu_ref, wd_ref, dat_ref, dbt_ref, dh_ref, acc):
        j = pl.program_id(1)
        dhid = _dot(dm_ref[...], wd_ref[...], "nt")
        av = a_ref[...].astype(F32)
        bv = b_ref[...].astype(F32)
        sig = _sigmoid(av)
        da32 = dhid * bv * (sig * (1.0 + av * (1.0 - sig)))
        db32 = dhid * (av * sig)
        dat_ref[...] = da32.T.astype(BF16)
        dbt_ref[...] = db32.T.astype(BF16)
        p = _dot(da32.astype(BF16), wg_ref[...], "nt") + _dot(db32.astype(BF16), wu_ref[...], "nt")

        @pl.when(j == 0)
        def _():
            acc[...] = p

        @pl.when(j > 0)
        def _():
            acc[...] += p

        @pl.when(j == ns - 1)
        def _():
            dh_ref[...] = acc[...]

    w_in = pl.BlockSpec((None, d, f), lambda i, j: (j, 0, 0))
    act = pl.BlockSpec((None, tm, f), lambda i, j: (j, i, 0))
    act_t = pl.BlockSpec((None, f, tm), lambda i, j: (j, 0, i))
    row = pl.BlockSpec((tm, d), lambda i, j: (i, 0))
    act_t_shape = jax.ShapeDtypeStruct((ns, f, t), BF16)
    return _call(
        name, body, grid=(t // tm, ns),
        in_specs=[row, act, act, w_in, w_in, pl.BlockSpec((None, f, d), lambda i, j: (j, 0, 0))],
        out_specs=[act_t, act_t, row],
        out_shape=[act_t_shape, act_t_shape, jax.ShapeDtypeStruct((t, d), F32)],
        args=[dm, a, b, wg, wu, wd], scratch=[pltpu.VMEM((tm, d), F32)], comm=comm, vmem_mib=56,
    )


def ffn_wgrad(name, act_t, x, comm=None):
    ns, f, t = act_t.shape
    d = x.shape[1]
    return _mm(
        name, act_t, x, dims="nn", grid=(ns,),
        a_spec=pl.BlockSpec((None, f, t), lambda j: (j, 0, 0)),
        b_spec=pl.BlockSpec((t, d), lambda j: (0, 0), pipeline_mode=pl.Buffered(1)),
        o_spec=pl.BlockSpec((None, f, d), lambda j: (j, 0, 0)),
        out_shape=jax.ShapeDtypeStruct((ns, f, d), BF16), vmem_mib=56, comm=comm,
    )


def rope_table(t):
    half = ROPE_DIM // 2
    inv = ROPE_THETA ** (-jnp.arange(half, dtype=F32) * 2.0 / ROPE_DIM)
    ang = jnp.arange(t, dtype=F32)[:, None] * inv[None, :]
    cos, sin = jnp.cos(ang), jnp.sin(ang)
    rest = HEAD_DIM - ROPE_DIM
    c = jnp.concatenate([cos, cos, jnp.ones((t, rest), F32)], axis=1)
    sm = jnp.concatenate([-sin, jnp.zeros((t, half + rest), F32)], axis=1)
    sp = jnp.concatenate([jnp.zeros((t, half), F32), sin, jnp.zeros((t, rest), F32)], axis=1)
    return jnp.concatenate([jnp.tile(c, (1, 2)), jnp.tile(sm, (1, 2)), jnp.tile(sp, (1, 2))], axis=1)


def _rope(x, tab, sign):
    w = x.shape[1]
    rep = w // LANES
    c, sm, sp = tab[:, 0:LANES], tab[:, LANES:2 * LANES], tab[:, 2 * LANES:3 * LANES]
    if rep > 1:
        c, sm, sp = jnp.tile(c, (1, rep)), jnp.tile(sm, (1, rep)), jnp.tile(sp, (1, rep))
    half = ROPE_DIM // 2
    return x * c + sign * (pltpu.roll(x, w - half, 1) * sm + pltpu.roll(x, half, 1) * sp)


def _attn_specs():
    prev = lambda n: jnp.maximum(n - 1, 0)
    kblk, vblk = ZK // LANES, ZK // LANES + 1
    return [
        pl.BlockSpec((BLOCK, ATTN_WIDTH), lambda n: (n, 0)),
        pl.BlockSpec((BLOCK, KV_WIDTH), lambda n: (n, kblk)),
        pl.BlockSpec((BLOCK, KV_WIDTH), lambda n: (prev(n), kblk)),
        pl.BlockSpec((BLOCK, KV_WIDTH), lambda n: (n, vblk)),
        pl.BlockSpec((BLOCK, KV_WIDTH), lambda n: (prev(n), vblk)),
        pl.BlockSpec((BLOCK, 3 * LANES), lambda n: (n, 0)),
        pl.BlockSpec((BLOCK, 3 * LANES), lambda n: (prev(n), 0)),
        pl.BlockSpec(memory_space=pltpu.SMEM),
    ]


def _attn_prologue(n, zq_ref, zk_ref, zkp_ref, zv_ref, zvp_ref, tab_ref, tabp_ref):
    q = (_rope(zq_ref[...], tab_ref[...], 1.0) * (HEAD_DIM ** -0.5)).astype(BF16)
    kcat = jnp.concatenate(
        [_rope(zkp_ref[...], tabp_ref[...], 1.0), _rope(zk_ref[...], tab_ref[...], 1.0)], axis=0).astype(BF16)
    vcat = jnp.concatenate([zvp_ref[...], zv_ref[...]], axis=0).astype(BF16)
    qi = lax.broadcasted_iota(jnp.int32, (BLOCK, 2 * BLOCK), 0)
    kj = lax.broadcasted_iota(jnp.int32, (BLOCK, 2 * BLOCK), 1)
    valid = (kj <= qi + BLOCK) & (kj > qi) & ((n > 0) | (kj >= BLOCK))
    return q, kcat, vcat, valid


def _attn_probs(qh, kh, valid, sink):
    s = jnp.where(valid, _dot(qh, kh, "nt"), -1e30)
    mx = jnp.maximum(jnp.max(s, axis=1, keepdims=True), sink)
    p = jnp.exp(s - mx)
    p_sink = jnp.exp(sink - mx)
    inv = 1.0 / (jnp.sum(p, axis=1, keepdims=True) + p_sink)
    return p * inv, p_sink * inv


def attn_fwd(z, tab, sinks, comm=None):
    t = z.shape[0]

    def body(zq_ref, zk_ref, zkp_ref, zv_ref, zvp_ref, tab_ref, tabp_ref, sink_ref, o_ref):
        n = pl.program_id(0)
        q, kcat, vcat, valid = _attn_prologue(n, zq_ref, zk_ref, zkp_ref, zv_ref, zvp_ref, tab_ref, tabp_ref)
        outs = []
        for h in range(N_Q_HEADS):
            kv = slice((h // Q_PER_KV) * HEAD_DIM, (h // Q_PER_KV + 1) * HEAD_DIM)
            p, _ = _attn_probs(q[:, h * HEAD_DIM:(h + 1) * HEAD_DIM], kcat[:, kv], valid, sink_ref[0, h])
            outs.append(_dot(p.astype(BF16), vcat[:, kv]))
        o_ref[...] = jnp.concatenate(outs, axis=1).astype(BF16)

    return _call(
        "attn_fwd", body, grid=(t // BLOCK,), in_specs=_attn_specs(),
        out_specs=[pl.BlockSpec((BLOCK, ATTN_WIDTH), lambda n: (n, 0))],
        out_shape=[jax.ShapeDtypeStruct((t, ATTN_WIDTH), BF16)],
        args=[z, z, z, z, z, tab, tab, sinks], comm=comm,
    )[0]


def attn_bwd(z, tab, sinks, dcat, comm=None):
    t = z.shape[0]
    nb = t // BLOCK

    def body(zq_ref, zk_ref, zkp_ref, zv_ref, zvp_ref, tab_ref, tabp_ref, sink_ref, do_ref,
             dq_ref, dkv_ref, dsink_ref):
        n = pl.program_id(0)
        q, kcat, vcat, valid = _attn_prologue(n, zq_ref, zk_ref, zkp_ref, zv_ref, zvp_ref, tab_ref, tabp_ref)
        do = do_ref[...]
        lane = lax.broadcasted_iota(jnp.int32, (1, LANES), 1)
        dqs, dks, dvs = [], [], []
        dsink = jnp.zeros((1, LANES), F32)
        for hk in range(N_Q_HEADS // Q_PER_KV):
            kv = slice(hk * HEAD_DIM, (hk + 1) * HEAD_DIM)
            kh, vh = kcat[:, kv], vcat[:, kv]
            dk_t = jnp.zeros((HEAD_DIM, 2 * BLOCK), F32)
            dv_t = jnp.zeros((HEAD_DIM, 2 * BLOCK), F32)
            for g in range(Q_PER_KV):
                h = hk * Q_PER_KV + g
                hs = slice(h * HEAD_DIM, (h + 1) * HEAD_DIM)
                qh, doh = q[:, hs], do[:, hs]
                p, p_sink = _attn_probs(qh, kh, valid, sink_ref[0, h])
                dv_t = dv_t + _dot(doh, p.astype(BF16), "tn")
                dp = _dot(doh, vh, "nt")
                rd = jnp.sum(p * dp, axis=1, keepdims=True)
                ds = (p * (dp - rd) * (HEAD_DIM ** -0.5)).astype(BF16)
                dqs.append(_dot(ds, kh))
                dk_t = dk_t + _dot(qh, ds, "tn")
                dsink = dsink + jnp.where(lane == h, -jnp.sum(p_sink * rd, axis=0, keepdims=True), 0.0)
            dks.append(dk_t)
            dvs.append(dv_t)
        dq_ref[...] = _rope(jnp.concatenate(dqs, axis=1), tab_ref[...], -1.0).astype(BF16)
        dkc = jnp.concatenate(dks, axis=0).T * (HEAD_DIM ** 0.5)
        dk_pre = jnp.concatenate(
            [_rope(dkc[:BLOCK], tabp_ref[...], -1.0), _rope(dkc[BLOCK:], tab_ref[...], -1.0)], axis=0)
        dkv_ref[...] = jnp.concatenate([dk_pre, jnp.concatenate(dvs, axis=0).T], axis=1)

        @pl.when(n == 0)
        def _():
            dsink_ref[...] = jnp.zeros_like(dsink_ref)

        dsink_ref[...] += dsink

    return _call(
        "attn_bwd", body, grid=(nb,),
        in_specs=_attn_specs() + [pl.BlockSpec((BLOCK, ATTN_WIDTH), lambda n: (n, 0))],
        out_specs=[pl.BlockSpec((BLOCK, ATTN_WIDTH), lambda n: (n, 0)),
                   pl.BlockSpec((None, 2 * BLOCK, 2 * KV_WIDTH), lambda n: (n, 0, 0)),
                   pl.BlockSpec((1, LANES), lambda n: (0, 0))],
        out_shape=[jax.ShapeDtypeStruct((t, ATTN_WIDTH), BF16),
                   jax.ShapeDtypeStruct((nb, 2 * BLOCK, 2 * KV_WIDTH), F32),
                   jax.ShapeDtypeStruct((1, LANES), F32)],
        args=[z, z, z, z, z, tab, tab, sinks, dcat], comm=comm,
    )


def _gelu(x):
    k = math.sqrt(2.0 / math.pi)
    th = jnp.tanh(k * (x + 0.044715 * (x * x * x)))
    return 0.5 * x * (1.0 + th), th


def _gelu_grad(x, th):
    k = math.sqrt(2.0 / math.pi)
    return 0.5 * (1.0 + th) + 0.5 * x * (1.0 - th * th) * (k * (1.0 + 3.0 * 0.044715 * (x * x)))


def _sgu_core(zu_ref, zv_ref, lng_ref, lnb_ref, w_ref, bt_ref):
    up, vp = zu_ref[...], zv_ref[...]
    u, thu = _gelu(up)
    v, thv = _gelu(vp)
    mu = jnp.mean(v, axis=-1, keepdims=True)
    vc = v - mu
    rstd = lax.rsqrt(jnp.mean(vc * vc, axis=-1, keepdims=True) + RMS_EPS)
    xhat = vc * rstd
    vn = (xhat * lng_ref[...] + lnb_ref[...]).astype(BF16)
    row = lax.broadcasted_iota(jnp.int32, (CHUNK, CHUNK), 0)
    col = lax.broadcasted_iota(jnp.int32, (CHUNK, CHUNK), 1)
    mixed = []
    for g in range(SGU_GROUPS):
        wc = jnp.where(row >= col, w_ref[g], 0.0).astype(BF16)
        mixed.append(_dot(wc, vn[:, g * CHUNK:(g + 1) * CHUNK]) + bt_ref[:, g:g + 1])
    return up, vp, u, thu, thv, rstd, xhat, vn, jnp.concatenate(mixed, axis=1)


def _sgu_specs():
    full = lambda shape: pl.BlockSpec(shape, lambda n: (0,) * len(shape))
    return [
        pl.BlockSpec((CHUNK, SGU_WIDTH), lambda n: (n, ZU // SGU_WIDTH)),
        pl.BlockSpec((CHUNK, SGU_WIDTH), lambda n: (n, ZV // SGU_WIDTH)),
        full((1, SGU_WIDTH)), full((1, SGU_WIDTH)),
        full((SGU_GROUPS, CHUNK, CHUNK)), full((CHUNK, SGU_GROUPS)),
    ]


def sgu_fwd(z, ln_g, ln_b, w, b_t, comm=None):
    t = z.shape[0]

    def body(zu_ref, zv_ref, lng_ref, lnb_ref, w_ref, bt_ref, o_ref):
        _, _, u, _, _, _, _, _, mixed = _sgu_core(zu_ref, zv_ref, lng_ref, lnb_ref, w_ref, bt_ref)
        o_ref[...] = (u * mixed).astype(BF16)

    return _call(
        "sgu_fwd", body, grid=(t // CHUNK,), in_specs=_sgu_specs(),
        out_specs=[pl.BlockSpec((CHUNK, SGU_WIDTH), lambda n: (n, 0))],
        out_shape=[jax.ShapeDtypeStruct((t, SGU_WIDTH), BF16)], args=[z, z, ln_g, ln_b, w, b_t], comm=comm,
    )[0]


def sgu_bwd(z, ln_g, ln_b, w, w_t, b_t, dcat):
    t = z.shape[0]

    def body(zu_ref, zv_ref, lng_ref, lnb_ref, w_ref, bt_ref, wt_ref, dg_ref,
             du_ref, dv_ref, dw_ref, dbt_ref, dlng_ref, dlnb_ref):
        up, vp, u, thu, thv, rstd, xhat, vn, mixed = _sgu_core(zu_ref, zv_ref, lng_ref, lnb_ref, w_ref, bt_ref)
        dgate = dg_ref[...].astype(F32)
        du_ref[...] = (dgate * mixed * _gelu_grad(up, thu)).astype(BF16)
        dmixed = dgate * u
        row = lax.broadcasted_iota(jnp.int32, (CHUNK, CHUNK), 0)
        col = lax.broadcasted_iota(jnp.int32, (CHUNK, CHUNK), 1)

        @pl.when(pl.program_id(0) == 0)
        def _():
            dw_ref[...] = jnp.zeros_like(dw_ref)
            dbt_ref[...] = jnp.zeros_like(dbt_ref)
            dlng_ref[...] = jnp.zeros_like(dlng_ref)
            dlnb_ref[...] = jnp.zeros_like(dlnb_ref)

        dvn, dbt = [], jnp.zeros((CHUNK, LANES), F32)
        for g in range(SGU_GROUPS):
            gs = slice(g * CHUNK, (g + 1) * CHUNK)
            dmx = dmixed[:, gs]
            dmxb = dmx.astype(BF16)
            dbt = dbt + jnp.where(col == g, jnp.sum(dmx, axis=1, keepdims=True), 0.0)
            dw_ref[g] += jnp.where(row >= col, _dot(dmxb, vn[:, gs], "nt"), 0.0)
            wtc = jnp.where(col >= row, wt_ref[g], 0.0).astype(BF16)
            dvn.append(_dot(wtc, dmxb))
        dbt_ref[...] += dbt
        dvn = jnp.concatenate(dvn, axis=1)
        dlnb_ref[...] += jnp.sum(dvn, axis=0, keepdims=True)
        dlng_ref[...] += jnp.sum(dvn * xhat, axis=0, keepdims=True)
        dxh = dvn * lng_ref[...]
        dv = rstd * (dxh - jnp.mean(dxh, axis=-1, keepdims=True) - xhat * jnp.mean(dxh * xhat, axis=-1, keepdims=True))
        dv_ref[...] = (dv * _gelu_grad(vp, thv)).astype(BF16)

    full = lambda shape: pl.BlockSpec(shape, lambda n: (0,) * len(shape))
    act = pl.BlockSpec((CHUNK, SGU_WIDTH), lambda n: (n, 0))
    act_shape = jax.ShapeDtypeStruct((t, SGU_WIDTH), BF16)
    vec = jax.ShapeDtypeStruct((1, SGU_WIDTH), F32)
    return pl.pallas_call(
        body,
        out_shape=(act_shape, act_shape, jax.ShapeDtypeStruct((SGU_GROUPS, CHUNK, CHUNK), F32),
                   jax.ShapeDtypeStruct((CHUNK, LANES), F32), vec, vec),
        grid=(t // CHUNK,),
        in_specs=_sgu_specs() + [full((SGU_GROUPS, CHUNK, CHUNK)),
                                 pl.BlockSpec((CHUNK, SGU_WIDTH), lambda n: (n, 1))],
        out_specs=(act, act, full((SGU_GROUPS, CHUNK, CHUNK)), full((CHUNK, LANES)),
                   full((1, SGU_WIDTH)), full((1, SGU_WIDTH))),
        compiler_params=_cp(1), name="sgu_bwd",
    )(z, z, ln_g, ln_b, w, b_t, w_t, dcat)


def dz_assemble(dq, dkv, du, dv):
    t = dq.shape[0]
    nb = t // BLOCK

    def body(dq_ref, cur_ref, nxt_ref, du_ref, dv_ref, o_ref):
        n = pl.program_id(0)
        o_ref[:, ZQ:ZQ + ATTN_WIDTH] = dq_ref[...]
        o_ref[:, ZU:ZU + SGU_WIDTH] = du_ref[...]
        o_ref[:, ZV:ZV + SGU_WIDTH] = dv_ref[...]
        kv = cur_ref[BLOCK:, :] + jnp.where(n < nb - 1, nxt_ref[:BLOCK, :], 0.0)
        o_ref[:, ZK:ZK + 2 * KV_WIDTH] = kv.astype(BF16)

    act = pl.BlockSpec((BLOCK, ATTN_WIDTH), lambda n: (n, 0))
    return pl.pallas_call(
        body, out_shape=jax.ShapeDtypeStruct((t, IN_WIDTH), BF16), grid=(nb,),
        in_specs=[act,
                  pl.BlockSpec((None, 2 * BLOCK, 2 * KV_WIDTH), lambda n: (n, 0, 0)),
                  pl.BlockSpec((None, 2 * BLOCK, 2 * KV_WIDTH), lambda n: (jnp.minimum(n + 1, nb - 1), 0, 0)),
                  act, act],
        out_specs=pl.BlockSpec((BLOCK, IN_WIDTH), lambda n: (n, 0)),
        compiler_params=_cp(1), name="dz_assemble",
    )(dq, dkv, dkv, du, dv)


def _pool_count(i, tp, w):
    t_idx = i * tp + lax.broadcasted_iota(jnp.int32, (tp, 1), 0)
    return jnp.minimum(t_idx + 1, w).astype(F32)


def pool_fwd(h, pw, pscale, comm=None):
    t, d = h.shape
    tp = _tile(t, 256)
    per = tp // POOL_HALO

    def body(h_ref, halo_ref, pw_ref, ps_ref, m_ref, pooled_ref):
        i = pl.program_id(0)
        cur = h_ref[...]
        ext = jnp.concatenate([jnp.where(i > 0, halo_ref[...], 0.0), cur], axis=0)
        ys, pooled = [], []
        for gi, w in enumerate(POOL_WINDOWS):
            gs = slice(gi * POOL_GROUP_DIM, (gi + 1) * POOL_GROUP_DIM)
            s = ext[:, gs]
            sh = 1
            while sh < w:
                s = s + pltpu.roll(s, sh, 0)
                sh *= 2
            pg = (s[POOL_HALO:, :] / _pool_count(i, tp, w) - cur[:, gs]).astype(BF16)
            pooled.append(pg)
            ys.append(_dot(pg, pw_ref[gi]))
        pooled_ref[...] = jnp.concatenate(pooled, axis=1)
        m_ref[...] = jnp.concatenate(ys, axis=1) * ps_ref[...]

    row = pl.BlockSpec((tp, d), lambda i: (i, 0))
    return _call(
        "pool_fwd", body, grid=(t // tp,),
        in_specs=[row, pl.BlockSpec((POOL_HALO, d), lambda i: (jnp.maximum(i * per - 1, 0), 0)),
                  pl.BlockSpec(pw.shape, lambda i: (0, 0, 0)), pl.BlockSpec((1, d), lambda i: (0, 0))],
        out_specs=[row, row], out_shape=[jax.ShapeDtypeStruct((t, d), F32), jax.ShapeDtypeStruct((t, d), BF16)],
        args=[h, h, pw, pscale], comm=comm,
    )


def pool_bwd_proj(dm, pooled, pw, pscale):
    t, d = dm.shape
    tp = _tile(t, 256)

    def body(dm_ref, pooled_ref, pw_ref, ps_ref, dp_ref, dy_ref, dps_ref):
        dmv = dm_ref[...]
        dy = (dmv * ps_ref[...]).astype(BF16)
        dy_ref[...] = dy
        ys, dps = [], []
        for gi in range(len(POOL_WINDOWS)):
            gs = slice(gi * POOL_GROUP_DIM, (gi + 1) * POOL_GROUP_DIM)
            ys.append(_dot(pooled_ref[:, gs], pw_ref[gi]))
            dps.append(_dot(dy[:, gs], pw_ref[gi], "nt"))
        dp_ref[...] = jnp.concatenate(dps, axis=1)

        @pl.when(pl.program_id(0) == 0)
        def _():
            dps_ref[...] = jnp.zeros_like(dps_ref)

        dps_ref[...] += jnp.sum(dmv * jnp.concatenate(ys, axis=1), axis=0, keepdims=True)

    row = pl.BlockSpec((tp, d), lambda i: (i, 0))
    vec = pl.BlockSpec((1, d), lambda i: (0, 0))
    return pl.pallas_call(
        body,
        out_shape=(jax.ShapeDtypeStruct((t, d), F32), jax.ShapeDtypeStruct((t, d), BF16),
                   jax.ShapeDtypeStruct((1, d), F32)),
        grid=(t // tp,),
        in_specs=[row, row, pl.BlockSpec(pw.shape, lambda i: (0, 0, 0)), vec],
        out_specs=(row, row, vec), compiler_params=_cp(1), name="pool_bwd_proj",
    )(dm, pooled, pw, pscale)


def pool_bwd_window(dp):
    t, d = dp.shape
    tp = _tile(t, 256)
    per = tp // POOL_HALO
    last = t // POOL_HALO - 1
    nt = t // tp

    def body(dp_ref, halo_ref, dh_ref):
        i = pl.program_id(0)
        cur = dp_ref[...]
        halo = jnp.where(i < nt - 1, halo_ref[...], 0.0)
        outs = []
        for gi, w in enumerate(POOL_WINDOWS):
            gs = slice(gi * POOL_GROUP_DIM, (gi + 1) * POOL_GROUP_DIM)
            s = jnp.concatenate([cur[:, gs] / _pool_count(i, tp, w), halo[:, gs] / float(w)], axis=0)
            sh = 1
            while sh < w:
                s = s + pltpu.roll(s, tp + POOL_HALO - sh, 0)
                sh *= 2
            outs.append(s[:tp, :] - cur[:, gs])
        dh_ref[...] = jnp.concatenate(outs, axis=1)

    row = pl.BlockSpec((tp, d), lambda i: (i, 0))
    return pl.pallas_call(
        body, out_shape=jax.ShapeDtypeStruct((t, d), F32), grid=(nt,),
        in_specs=[row, pl.BlockSpec((POOL_HALO, d), lambda i: (jnp.minimum((i + 1) * per, last), 0))],
        out_specs=row, compiler_params=_cp(1), name="pool_bwd_window",
    )(dp, dp)


def pool_wgrad(pooled, dy):
    t, d = pooled.shape
    ng = d // POOL_GROUP_DIM
    tk = _tile(t, 512)
    blk = pl.BlockSpec((tk, POOL_GROUP_DIM), lambda g, k: (k, g))
    return _mm(
        "pool_wgrad", pooled, dy, dims="tn", grid=(ng, t // tk), a_spec=blk, b_spec=blk,
        o_spec=pl.BlockSpec((None, POOL_GROUP_DIM, POOL_GROUP_DIM), lambda g, k: (g, 0, 0)),
        out_shape=jax.ShapeDtypeStruct((ng, POOL_GROUP_DIM, POOL_GROUP_DIM), F32),
        acc_shape=(POOL_GROUP_DIM, POOL_GROUP_DIM), nk=t // tk,
    )


def _xattn_probs(qh, kh):
    s = _dot(qh, kh, "nt") * (X_HEAD_DIM ** -0.5)
    p = jnp.exp(s - jnp.max(s, axis=1, keepdims=True))
    return p * (1.0 / jnp.sum(p, axis=1, keepdims=True))


def xattn_fwd(name, q, k, v):
    t, xw = q.shape
    tm = _tile(t, 512)

    def body(q_ref, k_ref, v_ref, o_ref):
        outs = []
        for h in range(X_HEADS):
            hs = slice(h * X_HEAD_DIM, (h + 1) * X_HEAD_DIM)
            p = _xattn_probs(q_ref[:, hs], k_ref[:, hs])
            outs.append(_dot(p.astype(BF16), v_ref[:, hs]))
        o_ref[...] = jnp.concatenate(outs, axis=1).astype(BF16)

    row = pl.BlockSpec((tm, xw), lambda i: (i, 0))
    kv = pl.BlockSpec(k.shape, lambda i: (0, 0))
    return pl.pallas_call(
        body, out_shape=jax.ShapeDtypeStruct((t, xw), BF16), grid=(t // tm,),
        in_specs=[row, kv, kv], out_specs=row, compiler_params=_cp(1), name=name,
    )(q, k, v)


def xattn_bwd(name, q, k, v, do):
    t, xw = q.shape
    tm = _tile(t, 512)

    def body(q_ref, k_ref, v_ref, do_ref, dq_ref, dk_ref, dv_ref):
        @pl.when(pl.program_id(0) == 0)
        def _():
            dk_ref[...] = jnp.zeros_like(dk_ref)
            dv_ref[...] = jnp.zeros_like(dv_ref)

        dqs = []
        for h in range(X_HEADS):
            hs = slice(h * X_HEAD_DIM, (h + 1) * X_HEAD_DIM)
            qh, kh, vh, doh = q_ref[:, hs], k_ref[:, hs], v_ref[:, hs], do_ref[:, hs]
            p = _xattn_probs(qh, kh)
            dv_ref[:, hs] += _dot(p.astype(BF16), doh, "tn")
            dp = _dot(doh, vh, "nt")
            ds = (p * (dp - jnp.sum(p * dp, axis=1, keepdims=True)) * (X_HEAD_DIM ** -0.5)).astype(BF16)
            dqs.append(_dot(ds, kh))
            dk_ref[:, hs] += _dot(ds, qh, "tn")
        dq_ref[...] = jnp.concatenate(dqs, axis=1).astype(BF16)

    row = pl.BlockSpec((tm, xw), lambda i: (i, 0))
    kv = pl.BlockSpec(k.shape, lambda i: (0, 0))
    kv_shape = jax.ShapeDtypeStruct(k.shape, F32)
    return pl.pallas_call(
        body, out_shape=(jax.ShapeDtypeStruct((t, xw), BF16), kv_shape, kv_shape), grid=(t // tm,),
        in_specs=[row, kv, kv, row], out_specs=(row, kv, kv), compiler_params=_cp(1), name=name,
    )(q, k, v, do)


def kernel(x, mem, norms, mem_norm, ffn1_wg, ffn1_wu, ffn1_wd, ffn2_wg, ffn2_wu, ffn2_wd, x_wq, x_wk, x_wv, x_wo, mix_w_in, mix_w_out, attn_sinks, sgu_ln_g, sgu_ln_b, sgu_w, sgu_b, pool_w, pool_scale, loss_target, m_norms, m_mem_norm, m_ffn1_wg, m_ffn1_wu, m_ffn1_wd, m_ffn2_wg, m_ffn2_wu, m_ffn2_wd, m_x_wq, m_x_wk, m_x_wv, m_x_wo, m_mix_w_in, m_mix_w_out, m_attn_sinks, m_sgu_ln_g, m_sgu_ln_b, m_sgu_w, m_sgu_b, m_pool_w, m_pool_scale, v_norms, v_mem_norm, v_ffn1_wg, v_ffn1_wu, v_ffn1_wd, v_ffn2_wg, v_ffn2_wu, v_ffn2_wd, v_x_wq, v_x_wk, v_x_wv, v_x_wo, v_mix_w_in, v_mix_w_out, v_attn_sinks, v_sgu_ln_g, v_sgu_ln_b, v_sgu_w, v_sgu_b, v_pool_w, v_pool_scale):
    params = dict(norms=norms, mem_norm=mem_norm, ffn1_wg=ffn1_wg, ffn1_wu=ffn1_wu, ffn1_wd=ffn1_wd,
                  ffn2_wg=ffn2_wg, ffn2_wu=ffn2_wu, ffn2_wd=ffn2_wd, x_wq=x_wq, x_wk=x_wk, x_wv=x_wv, x_wo=x_wo,
                  mix_w_in=mix_w_in, mix_w_out=mix_w_out, attn_sinks=attn_sinks, sgu_ln_g=sgu_ln_g,
                  sgu_ln_b=sgu_ln_b, sgu_w=sgu_w, sgu_b=sgu_b, pool_w=pool_w, pool_scale=pool_scale)
    mom1 = dict(norms=m_norms, mem_norm=m_mem_norm, ffn1_wg=m_ffn1_wg, ffn1_wu=m_ffn1_wu, ffn1_wd=m_ffn1_wd,
                ffn2_wg=m_ffn2_wg, ffn2_wu=m_ffn2_wu, ffn2_wd=m_ffn2_wd, x_wq=m_x_wq, x_wk=m_x_wk, x_wv=m_x_wv,
                x_wo=m_x_wo, mix_w_in=m_mix_w_in, mix_w_out=m_mix_w_out, attn_sinks=m_attn_sinks,
                sgu_ln_g=m_sgu_ln_g, sgu_ln_b=m_sgu_ln_b, sgu_w=m_sgu_w, sgu_b=m_sgu_b, pool_w=m_pool_w,
                pool_scale=m_pool_scale)
    mom2 = dict(norms=v_norms, mem_norm=v_mem_norm, ffn1_wg=v_ffn1_wg, ffn1_wu=v_ffn1_wu, ffn1_wd=v_ffn1_wd,
                ffn2_wg=v_ffn2_wg, ffn2_wu=v_ffn2_wu, ffn2_wd=v_ffn2_wd, x_wq=v_x_wq, x_wk=v_x_wk, x_wv=v_x_wv,
                x_wo=v_x_wo, mix_w_in=v_mix_w_in, mix_w_out=v_mix_w_out, attn_sinks=v_attn_sinks,
                sgu_ln_g=v_sgu_ln_g, sgu_ln_b=v_sgu_ln_b, sgu_w=v_sgu_w, sgu_b=v_sgu_b, pool_w=v_pool_w,
                pool_scale=v_pool_scale)
    order = list(params)

    xs, memb, target = x[0], mem[0], loss_target[0]
    t, d = xs.shape
    depth = norms.shape[0]
    dsh = d // NDEV

    bf = lambda a: a.astype(BF16)
    wts = {}

    def gather_job(keys):
        return _Gather([bf(params[name][l]) for name, l in keys]), keys

    def land(job_keys):
        job, keys = job_keys
        for key, a in zip(keys, job.result):
            wts[key] = a

    ffn_keys = lambda tag, l: [(f"{tag}_wg", l), (f"{tag}_wu", l), (f"{tag}_wd", l)]
    x_keys = lambda l: [("x_wq", l), ("x_wk", l), ("x_wv", l), ("x_wo", l)]
    small_shapes = [norms.shape, pool_scale.shape, pool_w.shape]
    head = gather_job(ffn_keys("ffn1", 0))
    head_small = _Gather([_pack([norms, pool_scale, pool_w])])
    _run_exchange("gather_head", _Multi([head_small, head[0]]))
    land(head)
    norms_sh, pscale_sh, pw_sh = _unpack(head_small.result[0], small_shapes, (NDEV,))
    norms_full = norms_sh.transpose(1, 2, 0, 3).reshape(depth, norms.shape[1], d)
    pscale_full = pscale_sh.transpose(1, 0, 2).reshape(1, d)
    pw_full = pw_sh[:, 0].transpose(1, 0, 2, 3).reshape(len(POOL_WINDOWS), POOL_GROUP_DIM, POOL_GROUP_DIM).astype(BF16)
    fwd_jobs = {
        ("ffn1", 0): gather_job([("mix_w_in", 0)] + ffn_keys("ffn2", 0)[:2]),
        ("mix_in", 0): gather_job([("mix_w_out", 0)]),
        ("attn", 0): gather_job(ffn_keys("ffn2", 0)[2:]),
        ("sgu", 0): gather_job(x_keys(0)[:3]),
        ("mix_out", 0): gather_job(x_keys(0)[3:]),
        ("ffn2", 0): gather_job(ffn_keys("ffn1", 1)),
        ("resid_ffn1", 1): gather_job(x_keys(1)[:3]),
        ("pool", 1): gather_job(x_keys(1)[3:]),
        ("ffn1", 1): gather_job(ffn_keys("ffn2", 1)),
    }

    def riding(key, fn, *args, **kw):
        job = fwd_jobs.pop(key, None)
        res = fn(*args, comm=job and job[0], **kw)
        if job:
            land(job)
        return res

    tab = rope_table(t)
    sgu_w0 = sgu_w[0]
    sgu_wt0 = sgu_w0.transpose(0, 2, 1)
    sgu_bt0 = sgu_b[0].T
    gain = lambda l, i: norms_full[l, i][None, :]

    saved = []
    xc = xs
    h = norm_fwd("norm_ffn1_0", xc, gain(0, 0), BF16)
    for l in range(depth):
        s = {}
        pooling = l % 2 == 1

        def ffn_forward(tag, xc, h, gi, nxt, l=l, s=s):
            s[tag + "_h"] = h
            s[tag + "_a"], s[tag + "_b"], s[tag + "_hidt"], s[tag + "_m"] = riding(
                (tag, l), ffn_fwd, f"{tag}_fwd_{l}", h, wts[tag + "_wg", l], wts[tag + "_wu", l], wts[tag + "_wd", l])
            return riding(("resid_" + tag, l), resid_norm_fwd, f"resid_{tag}_{l}", xc, s[tag + "_m"],
                          gain(l, gi + 1), 0.5, nxt)

        s["x0"] = xc
        if pooling:
            xc, h2 = ffn_forward("ffn1", xc, h, 0, (gain(l, 2), F32, False))
        else:
            xc, h2, s["h2t"] = ffn_forward("ffn1", xc, h, 0, (gain(l, 2), BF16, True))

        s["x1"] = xc
        if not pooling:
            w_in = wts["mix_w_in", l].transpose(1, 0, 2).reshape(d, IN_WIDTH)
            o_k, o_u = ATTN_WIDTH, ATTN_WIDTH + 2 * KV_WIDTH
            w_in = jnp.concatenate([w_in[:, :o_k], w_in[:, o_u:], w_in[:, o_k:o_u]], axis=1)
            s["z"] = riding(("mix_in", l), mm_nn, "mix_in", h2, w_in, F32, tn=IN_WIDTH // 2)
            w_out = wts["mix_w_out", l].reshape(d, d)
            attn = riding(("attn", l), attn_fwd, s["z"], tab, attn_sinks)
            gate = riding(("sgu", l), sgu_fwd, s["z"], sgu_ln_g, sgu_ln_b, sgu_w0, sgu_bt0)
            s["cat"] = jnp.concatenate([attn, gate], axis=1)
            s["m2"] = riding(("mix_out", l), mm_nn, "mix_out", s["cat"], w_out, F32)
        else:
            s["m2"], s["pooled"] = riding(("pool", l), pool_fwd, h2, pw_full, pscale_full)
        xc, h3, s["h3t"] = resid_norm_fwd(f"resid_mix_{l}", xc, s["m2"], gain(l, 3), 1.0, (gain(l, 4), BF16, True))

        s["x2"] = xc
        wq, wk, wv = (wts[k, l].reshape(d, -1) for k in ("x_wq", "x_wk", "x_wv"))
        s["wq"], s["wkv"] = wq, jnp.concatenate([wk, wv], axis=1)
        s["mem_n"] = norm_fwd(f"norm_mem_{l}", memb, mem_norm[l][None, :], BF16)
        s["q"] = mm_nn(f"x_q_{l}", h3, wq, BF16)
        s["k"] = mm_nn(f"x_k_{l}", s["mem_n"], wk, BF16)
        s["v"] = mm_nn(f"x_v_{l}", s["mem_n"], wv, BF16)
        s["o"] = xattn_fwd(f"xattn_fwd_{l}", s["q"], s["k"], s["v"])
        s["wo"] = wts["x_wo", l].transpose(1, 0, 2).reshape(-1, d)
        s["m3"] = mm_nn(f"x_o_{l}", s["o"], s["wo"], F32)
        xc, h4 = resid_norm_fwd(f"resid_x_{l}", xc, s["m3"], gain(l, 5), 1.0, (gain(l, 6), BF16, False))

        s["x3"] = xc
        if l + 1 < depth:
            xc, h = ffn_forward("ffn2", xc, h4, 6, (gain(l + 1, 0), BF16, False))
        else:
            xc = ffn_forward("ffn2", xc, h4, 6, None)
        saved.append(s)
    assert not fwd_jobs, list(fwd_jobs)

    dx, loss11 = loss_grad(xc, target)
    loss = lax.psum(loss11[0, 0], ("x", "y", "c"))

    swaps = []
    pending = []
    recv = {}

    def emit_units(name, l, arr, among_chips):
        piece_mib = math.prod(arr.shape[1:]) * arr.dtype.itemsize / MIB
        parts = 2 if piece_mib > 1.5 else 1
        rows = arr.shape[1] // parts
        cost = (LINK_US_PER_MIB_CHIPS if among_chips else LINK_US_PER_MIB_ALL) * piece_mib / parts
        for part in range(parts):
            pending.append(((name, l, part), among_chips, (arr, part * rows, rows), cost))

    def emit(name, l, arr, two_level=False):
        if two_level:
            swaps.append((name, l, arr.reshape((NCHIP, 2) + arr.shape[1:])))
        else:
            emit_units(name, l, arr, False)

    def hosted(budget_us, fn, *args, extra=(), force=True, **kw):
        jobs = list(extra)
        swapped = swaps[:]
        del swaps[:]
        used = PAIR_SWAP_US * len(swapped)
        if swapped:
            jobs.append(_PairSwap([g for _, _, g in swapped]))
        items, kept = [], []
        for it in pending:
            if (force and not items and not swapped) or used + it[3] <= budget_us:
                items.append(it)
                used += it[3]
            else:
                kept.append(it)
        pending[:] = kept
        groups = [[it for it in items if it[1] == flag] for flag in (False, True)]
        unit_jobs = [cls([it[2] for it in grp]) if grp else None
                     for cls, grp in zip((_Scatter, _ChipScatter), groups)]
        jobs += [j for j in unit_jobs if j is not None]
        res = fn(*args, comm=_Multi(jobs) if jobs else None, **kw)
        for job, grp in zip(unit_jobs, groups):
            for it, a in zip(grp, job.result if job else ()):
                recv[it[0]] = a
        if swapped:
            for (name, l, g4), theirs in zip(swapped, jobs[len(extra)].result):
                emit_units(name, l, pair_sum(f"pairsum_{name}_{l}", g4, theirs), True)
        return res

    grads = {"mem_norm": [None] * depth}
    dgs = [[None] * 8 for _ in range(depth)]
    small_jobs = []
    dm, dgs[depth - 1][7] = norm_bwd(f"ffn2_post_bwd_{depth - 1}", saved[-1]["ffn2_m"], gain(depth - 1, 7), dx, 0.5,
                                     None, BF16)
    for l in reversed(range(depth)):
        s = saved[l]
        dg = dgs[l]
        pooling = l % 2 == 1

        def ffn_block(tag, dm, extra=(), l=l, s=s):
            emit(tag + "_wd", l, hosted(HOST_US_WGRAD, ffn_wgrad, f"{tag}_dwd_{l}", s[tag + "_hidt"], dm), True)
            da_t, db_t, dh = hosted(HOST_US_FFN_BWD, ffn_bwd, f"{tag}_bwd_{l}", dm, s[tag + "_a"], s[tag + "_b"],
                                    wts[tag + "_wg", l], wts[tag + "_wu", l], wts[tag + "_wd", l], extra=extra)
            emit(tag + "_wg", l, hosted(HOST_US_WGRAD, ffn_wgrad, f"{tag}_dwg_{l}", da_t, s[tag + "_h"]), True)
            emit(tag + "_wu", l, hosted(HOST_US_WGRAD, ffn_wgrad, f"{tag}_dwu_{l}", db_t, s[tag + "_h"]), True)
            return dh

        dh = ffn_block("ffn2", dm)
        dx, dg[6], dm, dg[5] = hosted(
            HOST_US_SMALL, norm_bwd, f"ffn2_pre_bwd_{l}", s["x3"], gain(l, 6), dh, 1.0, dx, F32,
            post=(s["m3"], gain(l, 5), 1.0, BF16), force=False)

        do = mm_nt(f"x_do_{l}", dm, s["wo"], BF16)
        g_wo = mm_tn(f"x_dwo_{l}", s["o"], dm, BF16, tmo=s["o"].shape[1])
        emit("x_wo", l, g_wo.reshape(-1, NDEV, dsh).transpose(1, 0, 2))
        dq, dk, dv = xattn_bwd(f"xattn_bwd_{l}", s["q"], s["k"], s["v"], do)
        dkb, dvb = dk.astype(BF16), dv.astype(BF16)
        emit("x_wq", l, mm_kred(f"x_dwq_{l}", s["h3t"], dq, BF16).reshape(NDEV, dsh, -1))
        emit("x_wk", l, mm_tn(f"x_dwk_{l}", s["mem_n"], dkb, BF16).reshape(NDEV, dsh, -1))
        emit("x_wv", l, mm_tn(f"x_dwv_{l}", s["mem_n"], dvb, BF16).reshape(NDEV, dsh, -1))
        dh = mm_nt(f"x_dh_{l}", dq, s["wq"], F32)
        dmem_n = mm_nt(f"x_dmem_{l}", jnp.concatenate([dkb, dvb], axis=1), s["wkv"], F32)
        _, grads["mem_norm"][l] = norm_bwd(f"mem_norm_bwd_{l}", memb, mem_norm[l][None, :], dmem_n, 1.0, None, F32)
        dx, dg[4], dm, dg[3] = norm_bwd(f"x_pre_bwd_{l}", s["x2"], gain(l, 4), dh, 1.0, dx, F32,
                                        post=(s["m2"], gain(l, 3), 1.0, F32 if pooling else BF16))

        if not pooling:
            dcat = mm_nt("mix_dcat", dm, w_out, BF16)
            emit("mix_w_out", l, mm_tn("mix_dwout", s["cat"], dm, BF16, tno=d // 2).reshape(NDEV, dsh, d))
            dq_a, dkv_a, dsink = hosted(HOST_US_ATTN_BWD, attn_bwd, s["z"], tab, attn_sinks, dcat, force=False)
            du_s, dv_s, g_sgu_w, g_sgu_bt, g_ln_g, g_ln_b = sgu_bwd(
                s["z"], sgu_ln_g, sgu_ln_b, sgu_w0, sgu_wt0, sgu_bt0, dcat)
            dz = dz_assemble(dq_a, dkv_a, du_s, dv_s)
            dh = mm_nt("mix_dh", dz, w_in, F32, tn=d // 2)
            g_win = hosted(2 * HOST_US_SMALL, mm_kred, "mix_dwin", s["h2t"], dz, BF16, tno=IN_WIDTH // 2, force=False)
            g_win = jnp.concatenate([g_win[:, :ATTN_WIDTH], g_win[:, ZK:], g_win[:, ZU:ZK]], axis=1)
            emit("mix_w_in", l, g_win.reshape(d, NDEV, -1).transpose(1, 0, 2))
        else:
            dp, dy, g_pscale = pool_bwd_proj(dm, s["pooled"], pw_full, pscale_full)
            g_pw = pool_wgrad(s["pooled"], dy)
            emit("pool_w", 0, g_pw.reshape(len(POOL_WINDOWS), NDEV, -1, POOL_GROUP_DIM).transpose(1, 0, 2, 3)
                 .reshape(NDEV, -1, POOL_GROUP_DIM))
            dh = pool_bwd_window(dp)
        dx, dg[2], dm, dg[1] = norm_bwd(f"mix_pre_bwd_{l}", s["x1"], gain(l, 2), dh, 1.0, dx, F32,
                                        post=(s["ffn1_m"], gain(l, 1), 0.5, BF16))

        if l == 0:
            replicated = ["mem_norm", "attn_sinks", "sgu_ln_g", "sgu_ln_b", "sgu_w", "sgu_b"]
            rep_grads = [jnp.concatenate(grads["mem_norm"], axis=0), dsink[:, :N_Q_HEADS], g_ln_g, g_ln_b,
                         g_sgu_w[None], g_sgu_bt[:, :SGU_GROUPS].T[None]]
            small_jobs.append(_Gather([_pack(rep_grads)]))
        dh = ffn_block("ffn1", dm, extra=small_jobs if l == 0 else ())
        if l > 0:
            dx, dg[0], dm, dgs[l - 1][7] = hosted(
                HOST_US_SMALL, norm_bwd, f"ffn1_pre_bwd_{l}", s["x0"], gain(l, 0), dh, 1.0, dx, F32,
                post=(saved[l - 1]["ffn2_m"], gain(l - 1, 7), 0.5, BF16), force=False)
        else:
            dx, dg[0] = norm_bwd(f"ffn1_pre_bwd_{l}", s["x0"], gain(l, 0), dh, 1.0, dx, F32)

    g_norms = jnp.stack([jnp.concatenate(dg, axis=0) for dg in dgs], axis=0)
    g_norms = g_norms.reshape(depth, norms.shape[1], NDEV, dsh).transpose(2, 0, 1, 3)
    g_pscale_p = g_pscale.reshape(1, NDEV, dsh).transpose(1, 0, 2)
    sharded_small = ["norms", "pool_scale"]
    pieces_small = jnp.stack([_pack([g_norms[j], g_pscale_p[j]]) for j in range(NDEV)], axis=0)
    small_scatter = _Scatter([(pieces_small, 0, pieces_small.shape[1])])

    out = {}

    def update(k, extra=(), host=True):
        waiting = [it[0] for it in pending if it[0][0] == k] + [it[:2] for it in swaps if it[0] == k]
        assert not waiting, waiting
        shp = params[k].shape
        flip = k.endswith(("_wg", "_wu"))
        c = shp[1] if flip else shp[-1]
        view = lambda a: (a.swapaxes(1, 2) if flip else a).reshape(-1, c)
        pieces = [recv[key] for key in sorted(key for key in recv if key[0] == k)]
        args = (f"adamw_{k}", view(params[k]), view(mom1[k]), view(mom2[k]), pieces)
        res = hosted(HOST_US_ADAMW, adamw, *args, extra=extra) if host else adamw(*args)
        if flip:
            out[k] = [a.reshape(shp[0], shp[2], shp[1]).swapaxes(1, 2) for a in res]
        else:
            out[k] = [a.reshape(shp) for a in res]

    def update_pack(names, pieces):
        shapes = [params[k].shape for k in names]
        res = adamw("adamw_" + names[0] + "_pack", _pack([params[k] for k in names]),
                    _pack([mom1[k] for k in names]), _pack([mom2[k] for k in names]), [pieces])
        for which in range(4):
            for k, a in zip(names, _unpack(res[which], shapes)):
                out.setdefault(k, [None] * 4)[which] = a

    last = ("ffn1_wg", "ffn1_wu", "ffn1_wd")
    big = ("ffn2_wg", "ffn2_wu", "ffn2_wd")
    early = [k for k in order if k not in last + big and k not in sharded_small and k not in replicated]
    for i, k in enumerate(early):
        update(k, extra=[small_scatter] if i == 0 else ())
    for k in big:
        update(k, host=False)
    flushes = 0
    while pending or swaps:
        hosted(float("inf"), lambda comm: _run_exchange(f"scatter_tail_{flushes}", comm))
        flushes += 1
    update_pack(replicated, small_jobs[0].result[0])
    update_pack(sharded_small, small_scatter.result[0])
    for k in last:
        update(k, host=False)

    outputs = [loss, dx[None]]
    for which in range(4):
        outputs += [out[k][which] for k in order]
    return tuple(outputs)
```

```python
import math

import jax
import jax.numpy as jnp
from jax import lax
from jax.experimental import pallas as pl
from jax.experimental.pallas import tpu as pltpu

F32 = jnp.float32
BF16 = jnp.bfloat16
NDEV = 8
MIB = 1024 * 1024
LANES = 128

RMS_EPS = 1e-6
HEAD_DIM = 64
N_Q_HEADS = 16
Q_PER_KV = 8
ATTN_WIDTH = 1024
KV_WIDTH = 128
BLOCK = 128
ROPE_DIM = 16
ROPE_THETA = 500000.0
SGU_GROUPS = 8
SGU_WIDTH = 1024
CHUNK = 128
POOL_WINDOWS = (2, 4, 8, 16)
POOL_GROUP_DIM = 512
POOL_HALO = 16
X_HEADS = 4
X_HEAD_DIM = 128
ZQ, ZU, ZV, ZK = 0, 1024, 2048, 3072
IN_WIDTH = 3328

ADAM_LR = 0.001
ADAM_B1 = 0.9
ADAM_B2 = 0.999
ADAM_EPS = 1e-08
ADAM_WD = 0.01
ADAM_STEP = 10

FFN_FWD_ROWS = 512
FFN_BWD_ROWS = 512

LINK_US_PER_MIB_ALL = 91.0
LINK_US_PER_MIB_CHIPS = 45.0
PAIR_SWAP_US = 20.0
HOST_US_FFN_BWD = 420.0
HOST_US_WGRAD = 100.0
HOST_US_ATTN_BWD = 240.0
HOST_US_SMALL = 40.0
HOST_US_ADAMW = 130.0

_DN = {
    "nn": (((1,), (0,)), ((), ())),
    "nt": (((1,), (1,)), ((), ())),
    "tn": (((0,), (0,)), ((), ())),
}


def _cp(naxes, vmem_mib=48):
    return pltpu.CompilerParams(dimension_semantics=("arbitrary",) * naxes, vmem_limit_bytes=vmem_mib * MIB)


def _tile(n, pref):
    t = min(n, pref)
    while n % t:
        t //= 2
    return t


def _dot(a, b, dims="nn"):
    return lax.dot_general(a, b, _DN[dims], preferred_element_type=F32)


def _me():
    x, y, c = lax.axis_index("x"), lax.axis_index("y"), lax.axis_index("c")
    return x, y, c, 4 * x + 2 * y + c


def _peer(k):
    x, y, c, _ = _me()
    px = 1 - x if k & 4 else x
    py = 1 - y if k & 2 else y
    pc = 1 - c if k & 1 else c
    return (px, py, pc), 4 * px + 2 * py + pc


class _Exchange:
    def __init__(self, arrs, out_shape, remote_per=NDEV - 1, local_per=1):
        self.arrs = list(arrs)
        self.n = len(self.arrs)
        self.out_shape = list(out_shape)
        self.remote_per = remote_per
        self.scratch = [
            pltpu.SemaphoreType.DMA((self.n * remote_per,)),
            pltpu.SemaphoreType.DMA((self.n * remote_per,)),
            pltpu.SemaphoreType.DMA((self.n * local_per,)),
        ]
        self.result = None

    def mid(self, ins, outs, sems):
        pass

    def _copy(self, src, dst, sems, i, k, dev):
        send, recv, _ = sems
        return pltpu.make_async_remote_copy(
            src_ref=src, dst_ref=dst, send_sem=send.at[i * self.remote_per + k - 1],
            recv_sem=recv.at[i * self.remote_per + k - 1], device_id=dev, device_id_type=pl.DeviceIdType.MESH)


class _Gather(_Exchange):
    def __init__(self, arrs):
        super().__init__(arrs, [jax.ShapeDtypeStruct((NDEV,) + a.shape, a.dtype) for a in arrs])

    def start(self, ins, outs, sems):
        me = _me()[3]
        for i in range(self.n):
            pltpu.make_async_copy(ins[i], outs[i].at[me], sems[2].at[i]).start()
        for k in (1, 2, 4, 6):
            dev, _ = _peer(k)
            for i in range(self.n):
                self._copy(ins[i], outs[i].at[me], sems, i, k, dev).start()

    def mid(self, ins, outs, sems):
        sibling, _ = _peer(1)
        for k in (2, 4, 6):
            dev, slot = _peer(k)
            for i in range(self.n):
                block = outs[i].at[slot]
                self._copy(ins[i], block, sems, i, k, dev).wait_recv()
                self._copy(block, block, sems, i, k + 1, sibling).start()

    def finish(self, ins, outs, sems):
        me = _me()[3]
        sibling, _ = _peer(1)
        for k in (1, 3, 5, 7):
            dev, slot = _peer(k)
            for i in range(self.n):
                self._copy(ins[i], outs[i].at[slot], sems, i, k, dev).wait_recv()
        for k in range(1, NDEV):
            for i in range(self.n):
                self._copy(ins[i], outs[i].at[me], sems, i, k, sibling).wait_send()
        for i in range(self.n):
            pltpu.make_async_copy(ins[i], outs[i].at[me], sems[2].at[i]).wait()


class _Scatter(_Exchange):
    def __init__(self, units):
        self.rows = [(r0, n) for _, r0, n in units]
        super().__init__([a for a, _, _ in units],
                         [jax.ShapeDtypeStruct((NDEV, n) + a.shape[2:], a.dtype) for a, _, n in units])

    def _src(self, ins, i, slot):
        r0, n = self.rows[i]
        return ins[i].at[slot, pl.ds(r0, n)]

    def start(self, ins, outs, sems):
        me = _me()[3]
        for i in range(self.n):
            pltpu.make_async_copy(self._src(ins, i, me), outs[i].at[me], sems[2].at[i]).start()
        for k in range(1, NDEV):
            dev, slot = _peer(k)
            for i in range(self.n):
                self._copy(self._src(ins, i, slot), outs[i].at[me], sems, i, k, dev).start()

    def finish(self, ins, outs, sems):
        me = _me()[3]
        for k in range(1, NDEV):
            dev, slot = _peer(k)
            for i in range(self.n):
                cp = self._copy(self._src(ins, i, slot), outs[i].at[slot], sems, i, k, dev)
                cp.wait_send()
                cp.wait_recv()
        for i in range(self.n):
            pltpu.make_async_copy(self._src(ins, i, me), outs[i].at[me], sems[2].at[i]).wait()


NCHIP = NDEV // 2


class _PairSwap(_Exchange):
    def __init__(self, arrs):
        super().__init__(arrs, [jax.ShapeDtypeStruct((NCHIP,) + a.shape[2:], a.dtype) for a in arrs],
                         remote_per=NCHIP)

    def _copies(self, ins, outs, sems):
        c = _me()[2]
        sibling, _ = _peer(1)
        for i in range(self.n):
            for q in range(NCHIP):
                yield self._copy(ins[i].at[q, 1 - c], outs[i].at[q], sems, i, q + 1, sibling)

    def start(self, ins, outs, sems):
        for remote in self._copies(ins, outs, sems):
            remote.start()

    def finish(self, ins, outs, sems):
        for remote in self._copies(ins, outs, sems):
            remote.wait_send()
            remote.wait_recv()


class _ChipScatter(_Exchange):
    def __init__(self, units):
        self.rows = [(r0, n) for _, r0, n in units]
        super().__init__([a for a, _, _ in units],
                         [jax.ShapeDtypeStruct((NCHIP, n) + a.shape[2:], a.dtype) for a, _, n in units],
                         remote_per=NCHIP - 1)

    def _src(self, ins, i, chip):
        r0, n = self.rows[i]
        return ins[i].at[chip, pl.ds(r0, n)]

    @staticmethod
    def _chip(k):
        dev, slot = _peer(2 * k)
        return dev, slot // 2

    def start(self, ins, outs, sems):
        mine = _me()[3] // 2
        for i in range(self.n):
            pltpu.make_async_copy(self._src(ins, i, mine), outs[i].at[mine], sems[2].at[i]).start()
        for k in range(1, NCHIP):
            dev, chip = self._chip(k)
            for i in range(self.n):
                self._copy(self._src(ins, i, chip), outs[i].at[mine], sems, i, k, dev).start()

    def finish(self, ins, outs, sems):
        mine = _me()[3] // 2
        for k in range(1, NCHIP):
            dev, chip = self._chip(k)
            for i in range(self.n):
                cp = self._copy(self._src(ins, i, chip), outs[i].at[chip], sems, i, k, dev)
                cp.wait_send()
                cp.wait_recv()
        for i in range(self.n):
            pltpu.make_async_copy(self._src(ins, i, mine), outs[i].at[mine], sems[2].at[i]).wait()


class _Multi:
    def __init__(self, jobs):
        self.jobs = list(jobs)
        self.arrs = [a for j in self.jobs for a in j.arrs]
        self.out_shape = [s for j in self.jobs for s in j.out_shape]
        self.scratch = [s for j in self.jobs for s in j.scratch]
        self._result = None

    def _parts(self, ins, outs, sems):
        oi = oo = 0
        for idx, j in enumerate(self.jobs):
            ni, no = len(j.arrs), len(j.out_shape)
            yield j, ins[oi:oi + ni], outs[oo:oo + no], sems[3 * idx:3 * idx + 3]
            oi += ni
            oo += no

    def start(self, ins, outs, sems):
        for j, i, o, s in self._parts(ins, outs, sems):
            j.start(i, o, s)

    def mid(self, ins, outs, sems):
        for j, i, o, s in self._parts(ins, outs, sems):
            j.mid(i, o, s)

    def finish(self, ins, outs, sems):
        for j, i, o, s in self._parts(ins, outs, sems):
            j.finish(i, o, s)

    @property
    def result(self):
        return self._result

    @result.setter
    def result(self, res):
        self._result = res
        o = 0
        for j in self.jobs:
            j.result = list(res[o:o + len(j.out_shape)])
            o += len(j.out_shape)


def _call(name, body, *, grid, in_specs, out_specs, out_shape, args, scratch=(), comm=None, vmem_mib=48):
    in_specs, out_specs, out_shape = list(in_specs), list(out_specs), list(out_shape)
    scratch, args = list(scratch), list(args)
    ni, no, ns = len(in_specs), len(out_specs), len(scratch)
    kernel_fn = body
    if comm is not None:
        ci, co = len(comm.arrs), len(comm.out_shape)
        hbm = pl.BlockSpec(memory_space=pltpu.HBM)

        def kernel_fn(*refs):
            refs = list(refs)
            ins, c_in, outs, c_out, scr, c_scr = (
                [refs.pop(0) for _ in range(cnt)] for cnt in (ni, ci, no, co, ns, len(comm.scratch)))
            if not grid:
                comm.start(c_in, c_out, c_scr)
                body(*ins, *outs, *scr)
                comm.mid(c_in, c_out, c_scr)
                comm.finish(c_in, c_out, c_scr)
                return
            step = pl.program_id(0)
            for ax in range(1, len(grid)):
                step = step * grid[ax] + pl.program_id(ax)
            nsteps = math.prod(grid)
            mid_step = nsteps - 1

            @pl.when(step == 0)
            def _():
                comm.start(c_in, c_out, c_scr)

            body(*ins, *outs, *scr)

            @pl.when(step == mid_step)
            def _():
                comm.mid(c_in, c_out, c_scr)

            @pl.when(step == nsteps - 1)
            def _():
                comm.finish(c_in, c_out, c_scr)

        in_specs += [hbm] * ci
        out_specs += [hbm] * co
        out_shape += comm.out_shape
        scratch += comm.scratch
        args += comm.arrs
    params = _cp(len(grid), vmem_mib) if grid else None
    res = pl.pallas_call(
        kernel_fn, out_shape=out_shape, grid=grid, in_specs=in_specs, out_specs=out_specs,
        scratch_shapes=scratch, compiler_params=params, name=name,
    )(*args)
    if comm is not None:
        comm.result = list(res[no:])
    return tuple(res[:no])


def _run_exchange(name, comm):
    _call(name, lambda: None, grid=(), in_specs=[], out_specs=[], out_shape=[], args=[], comm=comm)
    return comm.result


def _pack(arrs, dtype=F32):
    flat = jnp.concatenate([a.astype(dtype).reshape(-1) for a in arrs])
    n = flat.shape[0]
    total = -(-n // (16 * LANES)) * (16 * LANES)
    return jnp.pad(flat, (0, total - n)).reshape(total // LANES, LANES)


def _unpack(packed, shapes, lead=()):
    flat = packed.reshape(lead + (-1,))
    out, off = [], 0
    for s in shapes:
        n = math.prod(s)
        out.append(flat[..., off:off + n].reshape(lead + tuple(s)))
        off += n
    return out


def _mm(name, a, b, *, dims, grid, a_spec, b_spec, o_spec, out_shape, acc_shape=None, nk=1, vmem_mib=48, comm=None):
    nax = len(grid)

    def body(a_ref, b_ref, o_ref, *scratch):
        p = _dot(a_ref[...], b_ref[...], dims)
        if nk == 1:
            o_ref[...] = p.astype(o_ref.dtype)
            return
        acc = scratch[0]
        k = pl.program_id(nax - 1)

        @pl.when(k == 0)
        def _():
            acc[...] = p

        @pl.when(k > 0)
        def _():
            acc[...] += p

        @pl.when(k == nk - 1)
        def _():
            o_ref[...] = acc[...].astype(o_ref.dtype)

    return _call(
        name, body, grid=grid, in_specs=[a_spec, b_spec], out_specs=[o_spec], out_shape=[out_shape], args=[a, b],
        scratch=[pltpu.VMEM(acc_shape, F32)] if nk > 1 else [], comm=comm, vmem_mib=vmem_mib,
    )[0]


def mm_nn(name, a, b, out_dtype, tn=None, comm=None):
    m, k = a.shape
    n = b.shape[1]
    tm = _tile(m, 512)
    tn = n if tn is None else tn
    return _mm(
        name, a, b, dims="nn", grid=(n // tn, m // tm),
        a_spec=pl.BlockSpec((tm, k), lambda j, i: (i, 0)),
        b_spec=pl.BlockSpec((k, tn), lambda j, i: (0, j)),
        o_spec=pl.BlockSpec((tm, tn), lambda j, i: (i, j)),
        out_shape=jax.ShapeDtypeStruct((m, n), out_dtype), comm=comm,
    )


def mm_nt(name, a, b, out_dtype, tn=None):
    m, k = a.shape
    n = b.shape[0]
    tm = _tile(m, 512)
    tn = n if tn is None else tn
    return _mm(
        name, a, b, dims="nt", grid=(n // tn, m // tm),
        a_spec=pl.BlockSpec((tm, k), lambda j, i: (i, 0)),
        b_spec=pl.BlockSpec((tn, k), lambda j, i: (j, 0)),
        o_spec=pl.BlockSpec((tm, tn), lambda j, i: (i, j)),
        out_shape=jax.ShapeDtypeStruct((m, n), out_dtype),
    )


def mm_kred(name, a_t, b, out_dtype, tno=None, comm=None):
    m, k = a_t.shape
    n = b.shape[1]
    tk = _tile(k, 512)
    tmo = _tile(m, 1024)
    tno = n if tno is None else tno
    return _mm(
        name, a_t, b, dims="nn", grid=(m // tmo, n // tno, k // tk),
        a_spec=pl.BlockSpec((tmo, tk), lambda i, j, kk: (i, kk)),
        b_spec=pl.BlockSpec((tk, tno), lambda i, j, kk: (kk, j)),
        o_spec=pl.BlockSpec((tmo, tno), lambda i, j, kk: (i, j)),
        out_shape=jax.ShapeDtypeStruct((m, n), out_dtype),
        acc_shape=(tmo, tno), nk=k // tk, comm=comm,
    )


def mm_tn(name, a, b, out_dtype, tmo=None, tno=None):
    k, m = a.shape
    n = b.shape[1]
    tk = _tile(k, 512)
    tmo = _tile(m, 1024) if tmo is None else tmo
    tno = n if tno is None else tno
    return _mm(
        name, a, b, dims="tn", grid=(m // tmo, n // tno, k // tk),
        a_spec=pl.BlockSpec((tk, tmo), lambda i, j, kk: (kk, i)),
        b_spec=pl.BlockSpec((tk, tno), lambda i, j, kk: (kk, j)),
        o_spec=pl.BlockSpec((tmo, tno), lambda i, j, kk: (i, j)),
        out_shape=jax.ShapeDtypeStruct((m, n), out_dtype),
        acc_shape=(tmo, tno), nk=k // tk,
    )


def _rstd(x):
    return lax.rsqrt(jnp.mean(x * x, axis=-1, keepdims=True) + RMS_EPS)


def norm_fwd(name, x, g, out_dtype, with_t=False, comm=None):
    t, d = x.shape
    tm = _tile(t, 256)

    def body(x_ref, g_ref, o_ref, *t_ref):
        xv = x_ref[...]
        h = xv * _rstd(xv) * g_ref[...]
        o_ref[...] = h.astype(o_ref.dtype)
        if with_t:
            t_ref[0][...] = h.T.astype(out_dtype)

    row = pl.BlockSpec((tm, d), lambda i: (i, 0))
    out_shape = [jax.ShapeDtypeStruct((t, d), out_dtype)]
    out_specs = [row]
    if with_t:
        out_shape.append(jax.ShapeDtypeStruct((d, t), out_dtype))
        out_specs.append(pl.BlockSpec((d, tm), lambda i: (0, i)))
    res = _call(
        name, body, grid=(t // tm,), in_specs=[row, pl.BlockSpec((1, d), lambda i: (0, 0))],
        out_specs=out_specs, out_shape=out_shape, args=[x, g], comm=comm,
    )
    return res if with_t else res[0]


def resid_norm_fwd(name, x, m, g, scale, nxt=None, comm=None):
    t, d = x.shape
    tm = _tile(t, 256)
    with_t = nxt is not None and nxt[2]

    def body(*refs):
        if nxt is None:
            x_ref, m_ref, g_ref, o_ref = refs
        else:
            x_ref, m_ref, g_ref, gn_ref, o_ref, h_ref = refs[:6]
        mv = m_ref[...]
        xn = x_ref[...] + scale * (mv * _rstd(mv) * g_ref[...])
        o_ref[...] = xn
        if nxt is not None:
            h = xn * _rstd(xn) * gn_ref[...]
            h_ref[...] = h.astype(h_ref.dtype)
            if with_t:
                refs[6][...] = h.T.astype(nxt[1])

    row = pl.BlockSpec((tm, d), lambda i: (i, 0))
    vec = pl.BlockSpec((1, d), lambda i: (0, 0))
    out_shape, out_specs, args = [jax.ShapeDtypeStruct((t, d), F32)], [row], [x, m, g]
    if nxt is not None:
        args.append(nxt[0])
        out_shape.append(jax.ShapeDtypeStruct((t, d), nxt[1]))
        out_specs.append(row)
        if with_t:
            out_shape.append(jax.ShapeDtypeStruct((d, t), nxt[1]))
            out_specs.append(pl.BlockSpec((d, tm), lambda i: (0, i)))
    res = _call(
        name, body, grid=(t // tm,), in_specs=[row, row, vec] + ([vec] if nxt is not None else []),
        out_specs=out_specs, out_shape=out_shape, args=args, comm=comm,
    )
    return res[0] if nxt is None else res


def norm_bwd(name, u, g, dy, scale, resid, out_dtype, post=None, comm=None):
    t, d = u.shape
    tm = _tile(t, 256)
    has_resid = resid is not None

    def norm_grad(uv, gv, dyv, dg_ref):
        r = _rstd(uv)
        uh = uv * r
        dg_ref[...] += jnp.sum(dyv * uh, axis=0, keepdims=True)
        dyg = dyv * gv
        return r * (dyg - uh * jnp.mean(dyg * uh, axis=-1, keepdims=True))

    def body(*refs):
        refs = list(refs)
        u_ref, g_ref, dy_ref = refs[:3]
        del refs[:3]
        r_ref = refs.pop(0) if has_resid else None
        if post is not None:
            m_ref, gp_ref = refs[:2]
            del refs[:2]
        du_ref, dg_ref = refs[:2]

        @pl.when(pl.program_id(0) == 0)
        def _():
            for ref in refs[1::2]:
                ref[...] = jnp.zeros_like(ref)

        du = norm_grad(u_ref[...], g_ref[...], dy_ref[...].astype(F32) * scale, dg_ref)
        if has_resid:
            du = du + r_ref[...]
        du_ref[...] = du.astype(du_ref.dtype)
        if post is not None:
            dm_ref, dgp_ref = refs[2:4]
            dm_ref[...] = norm_grad(m_ref[...], gp_ref[...], du * post[2], dgp_ref).astype(dm_ref.dtype)

    row = pl.BlockSpec((tm, d), lambda i: (i, 0))
    vec = pl.BlockSpec((1, d), lambda i: (0, 0))
    args = [u, g, dy] + ([resid] if has_resid else [])
    in_specs = [row, vec, row] + ([row] if has_resid else [])
    out_specs = [row, vec]
    out_shape = [jax.ShapeDtypeStruct((t, d), out_dtype), jax.ShapeDtypeStruct((1, d), F32)]
    if post is not None:
        args += [post[0], post[1]]
        in_specs += [row, vec]
        out_specs += [row, vec]
        out_shape += [jax.ShapeDtypeStruct((t, d), post[3]), jax.ShapeDtypeStruct((1, d), F32)]
    return _call(name, body, grid=(t // tm,), in_specs=in_specs, out_specs=out_specs, out_shape=out_shape,
                 args=args, comm=comm)


def loss_grad(y, target):
    t, d = y.shape
    tm = _tile(t, 256)
    nt = t // tm

    def body(y_ref, t_ref, dy_ref, loss_ref, acc):
        i = pl.program_id(0)
        e = y_ref[...] - t_ref[...]
        dy_ref[...] = e * (1.0 / d)

        @pl.when(i == 0)
        def _():
            acc[...] = jnp.zeros_like(acc)

        acc[...] += jnp.sum(e * e, axis=0, keepdims=True)

        @pl.when(i == nt - 1)
        def _():
            loss_ref[...] = (0.5 / d) * jnp.sum(acc[...], axis=1, keepdims=True)

    row = pl.BlockSpec((tm, d), lambda i: (i, 0))
    return pl.pallas_call(
        body,
        out_shape=(jax.ShapeDtypeStruct((t, d), F32), jax.ShapeDtypeStruct((1, 1), F32)),
        grid=(nt,), in_specs=[row, row], out_specs=(row, pl.BlockSpec((1, 1), lambda i: (0, 0))),
        scratch_shapes=[pltpu.VMEM((1, d), F32)], compiler_params=_cp(1), name="loss_grad",
    )(y, target)


def _row_tile(r, c):
    if r * c * 4 <= MIB:
        return r
    best = None
    for t in range(16, r, 16):
        if r % t == 0 and t * c * 4 <= MIB:
            best = t
    return r if best is None else best


def pair_sum(name, g4, theirs):
    nq, _, r, c = g4.shape
    tr = _tile(r, 1024)

    def body(core_ref, g_ref, t_ref, o_ref):
        o_ref[...] = (g_ref[...].astype(F32) + t_ref[...].astype(F32)).astype(o_ref.dtype)

    blk = pl.BlockSpec((None, tr, c), lambda q, i, core: (q, i, 0))
    return pl.pallas_call(
        body, out_shape=jax.ShapeDtypeStruct(theirs.shape, theirs.dtype),
        grid_spec=pltpu.PrefetchScalarGridSpec(
            num_scalar_prefetch=1, grid=(nq, r // tr),
            in_specs=[pl.BlockSpec((None, None, tr, c), lambda q, i, core: (q, core[0], i, 0)), blk],
            out_specs=blk),
        compiler_params=_cp(2), name=name,
    )(lax.axis_index("c").astype(jnp.int32).reshape(1), g4, theirs)


def adamw(name, w, m, v, pieces, comm=None):
    nl = len(pieces)
    npiece, r, c = pieces[0].shape
    tr = _row_tile(r, c)
    nr = r // tr
    bc1 = 1.0 - ADAM_B1 ** ADAM_STEP
    bc2 = 1.0 - ADAM_B2 ** ADAM_STEP

    def body(w_ref, m_ref, v_ref, *rest):
        p_refs, (g_ref, d_ref, nm_ref, nv_ref) = rest[:nl], rest[nl:]

        def update(p_ref):
            g = p_ref[0].astype(F32)
            for j in range(1, npiece):
                g = g + p_ref[j].astype(F32)
            m1 = ADAM_B1 * m_ref[...] + (1.0 - ADAM_B1) * g
            v1 = ADAM_B2 * v_ref[...] + (1.0 - ADAM_B2) * (g * g)
            m_hat = m1 / bc1
            v_hat = v1 / bc2
            g_ref[...] = g
            d_ref[...] = -ADAM_LR * (m_hat / (jnp.sqrt(v_hat) + ADAM_EPS) + ADAM_WD * w_ref[...])
            nm_ref[...] = m1
            nv_ref[...] = v1

        if nl == 1:
            update(p_refs[0])
        else:
            for ll in range(nl):
                pl.when(pl.program_id(0) == ll)(lambda ll=ll: update(p_refs[ll]))

    def piece_spec(ll):
        return pl.BlockSpec((npiece, tr, c), lambda l, i: (0, jnp.where(l == ll, i, jnp.where(l > ll, nr - 1, 0)), 0))

    row = pl.BlockSpec((tr, c), lambda l, i: (l * nr + i, 0))
    out = jax.ShapeDtypeStruct((nl * r, c), F32)
    return _call(
        name, body, grid=(nl, nr), in_specs=[row, row, row] + [piece_spec(ll) for ll in range(nl)],
        out_specs=[row] * 4, out_shape=[out] * 4, args=[w, m, v] + list(pieces), comm=comm,
    )


def _sigmoid(a):
    return 1.0 / (1.0 + jnp.exp(-a))


def ffn_fwd(name, h, wg, wu, wd, comm=None):
    t, d = h.shape
    ns, _, f = wg.shape
    tm = _tile(t, FFN_FWD_ROWS)

    def body(h_ref, wg_ref, wu_ref, wd_ref, a_ref, b_ref, hidt_ref, m_ref, acc):
        j = pl.program_id(1)
        hv = h_ref[...]
        a = _dot(hv, wg_ref[...])
        b = _dot(hv, wu_ref[...])
        hid32 = (a * _sigmoid(a)) * b
        hid = hid32.astype(BF16)
        a_ref[...] = a.astype(BF16)
        b_ref[...] = b.astype(BF16)
        hidt_ref[...] = hid32.T.astype(BF16)
        p = _dot(hid, wd_ref[...])

        @pl.when(j == 0)
        def _():
            acc[...] = p

        @pl.when(j > 0)
        def _():
            acc[...] += p

        @pl.when(j == ns - 1)
        def _():
            m_ref[...] = acc[...]

    w_in = pl.BlockSpec((None, d, f), lambda i, j: (j, 0, 0))
    act = pl.BlockSpec((None, tm, f), lambda i, j: (j, i, 0))
    act_shape = jax.ShapeDtypeStruct((ns, t, f), BF16)
    return _call(
        name, body, grid=(t // tm, ns),
        in_specs=[pl.BlockSpec((tm, d), lambda i, j: (i, 0)), w_in, w_in,
                  pl.BlockSpec((None, f, d), lambda i, j: (j, 0, 0))],
        out_specs=[act, act, pl.BlockSpec((None, f, tm), lambda i, j: (j, 0, i)),
                   pl.BlockSpec((tm, d), lambda i, j: (i, 0))],
        out_shape=[act_shape, act_shape, jax.ShapeDtypeStruct((ns, f, t), BF16), jax.ShapeDtypeStruct((t, d), F32)],
        args=[h, wg, wu, wd], scratch=[pltpu.VMEM((tm, d), F32)], comm=comm, vmem_mib=56,
    )


def ffn_bwd(name, dm, a, b, wg, wu, wd, comm=None):
    t, d = dm.shape
    ns, _, f = wg.shape
    tm = _tile(t, FFN_BWD_ROWS)

    def body(dm_ref, a_ref, b_ref, wg_ref, wu_ref, wd_ref, dat_ref, dbt_ref, dh_ref, acc):
        j = pl.program_id(1)
        dhid = _dot(dm_ref[...], wd_ref[...], "nt")
        av = a_ref[...].astype(F32)
        bv = b_ref[...].astype(F32)
        sig = _sigmoid(av)
        da32 = dhid * bv * (sig * (1.0 + av * (1.0 - sig)))
        db32 = dhid * (av * sig)
        dat_ref[...] = da32.T.astype(BF16)
        dbt_ref[...] = db32.T.astype(BF16)
        p = _dot(da32.astype(BF16), wg_ref[...], "nt") + _dot(db32.astype(BF16), wu_ref[...], "nt")

        @pl.when(j == 0)
        def _():
            acc[...] = p

        @pl.when(j > 0)
        def _():
            acc[...] += p

        @pl.when(j == ns - 1)
        def _():
            dh_ref[...] = acc[...]

    w_in = pl.BlockSpec((None, d, f), lambda i, j: (j, 0, 0))
    act = pl.BlockSpec((None, tm, f), lambda i, j: (j, i, 0))
    act_t = pl.BlockSpec((None, f, tm), lambda i, j: (j, 0, i))
    row = pl.BlockSpec((tm, d), lambda i, j: (i, 0))
    act_t_shape = jax.ShapeDtypeStruct((ns, f, t), BF16)
    return _call(
        name, body, grid=(t // tm, ns),
        in_specs=[row, act, act, w_in, w_in, pl.BlockSpec((None, f, d), lambda i, j: (j, 0, 0))],
        out_specs=[act_t, act_t, row],
        out_shape=[act_t_shape, act_t_shape, jax.ShapeDtypeStruct((t, d), F32)],
        args=[dm, a, b, wg, wu, wd], scratch=[pltpu.VMEM((tm, d), F32)], comm=comm, vmem_mib=56,
    )


def ffn_wgrad(name, act_t, x, comm=None):
    ns, f, t = act_t.shape
    d = x.shape[1]
    return _mm(
        name, act_t, x, dims="nn", grid=(ns,),
        a_spec=pl.BlockSpec((None, f, t), lambda j: (j, 0, 0)),
        b_spec=pl.BlockSpec((t, d), lambda j: (0, 0), pipeline_mode=pl.Buffered(1)),
        o_spec=pl.BlockSpec((None, f, d), lambda j: (j, 0, 0)),
        out_shape=jax.ShapeDtypeStruct((ns, f, d), BF16), vmem_mib=56, comm=comm,
    )


def rope_table(t):
    half = ROPE_DIM // 2
    inv = ROPE_THETA ** (-jnp.arange(half, dtype=F32) * 2.0 / ROPE_DIM)
    ang = jnp.arange(t, dtype=F32)[:, None] * inv[None, :]
    cos, sin = jnp.cos(ang), jnp.sin(ang)
    rest = HEAD_DIM - ROPE_DIM
    c = jnp.concatenate([cos, cos, jnp.ones((t, rest), F32)], axis=1)
    sm = jnp.concatenate([-sin, jnp.zeros((t, half + rest), F32)], axis=1)
    sp = jnp.concatenate([jnp.zeros((t, half), F32), sin, jnp.zeros((t, rest), F32)], axis=1)
    return jnp.concatenate([jnp.tile(c, (1, 2)), jnp.tile(sm, (1, 2)), jnp.tile(sp, (1, 2))], axis=1)


def _rope(x, tab, sign):
    w = x.shape[1]
    rep = w // LANES
    c, sm, sp = tab[:, 0:LANES], tab[:, LANES:2 * LANES], tab[:, 2 * LANES:3 * LANES]
    if rep > 1:
        c, sm, sp = jnp.tile(c, (1, rep)), jnp.tile(sm, (1, rep)), jnp.tile(sp, (1, rep))
    half = ROPE_DIM // 2
    return x * c + sign * (pltpu.roll(x, w - half, 1) * sm + pltpu.roll(x, half, 1) * sp)


def _attn_specs():
    prev = lambda n: jnp.maximum(n - 1, 0)
    kblk, vblk = ZK // LANES, ZK // LANES + 1
    return [
        pl.BlockSpec((BLOCK, ATTN_WIDTH), lambda n: (n, 0)),
        pl.BlockSpec((BLOCK, KV_WIDTH), lambda n: (n, kblk)),
        pl.BlockSpec((BLOCK, KV_WIDTH), lambda n: (prev(n), kblk)),
        pl.BlockSpec((BLOCK, KV_WIDTH), lambda n: (n, vblk)),
        pl.BlockSpec((BLOCK, KV_WIDTH), lambda n: (prev(n), vblk)),
        pl.BlockSpec((BLOCK, 3 * LANES), lambda n: (n, 0)),
        pl.BlockSpec((BLOCK, 3 * LANES), lambda n: (prev(n), 0)),
        pl.BlockSpec(memory_space=pltpu.SMEM),
    ]


def _attn_prologue(n, zq_ref, zk_ref, zkp_ref, zv_ref, zvp_ref, tab_ref, tabp_ref):
    q = (_rope(zq_ref[...], tab_ref[...], 1.0) * (HEAD_DIM ** -0.5)).astype(BF16)
    kcat = jnp.concatenate(
        [_rope(zkp_ref[...], tabp_ref[...], 1.0), _rope(zk_ref[...], tab_ref[...], 1.0)], axis=0).astype(BF16)
    vcat = jnp.concatenate([zvp_ref[...], zv_ref[...]], axis=0).astype(BF16)
    qi = lax.broadcasted_iota(jnp.int32, (BLOCK, 2 * BLOCK), 0)
    kj = lax.broadcasted_iota(jnp.int32, (BLOCK, 2 * BLOCK), 1)
    valid = (kj <= qi + BLOCK) & (kj > qi) & ((n > 0) | (kj >= BLOCK))
    return q, kcat, vcat, valid


def _attn_probs(qh, kh, valid, sink):
    s = jnp.where(valid, _dot(qh, kh, "nt"), -1e30)
    mx = jnp.maximum(jnp.max(s, axis=1, keepdims=True), sink)
    p = jnp.exp(s - mx)
    p_sink = jnp.exp(sink - mx)
    inv = 1.0 / (jnp.sum(p, axis=1, keepdims=True) + p_sink)
    return p * inv, p_sink * inv


def attn_fwd(z, tab, sinks, comm=None):
    t = z.shape[0]

    def body(zq_ref, zk_ref, zkp_ref, zv_ref, zvp_ref, tab_ref, tabp_ref, sink_ref, o_ref):
        n = pl.program_id(0)
        q, kcat, vcat, valid = _attn_prologue(n, zq_ref, zk_ref, zkp_ref, zv_ref, zvp_ref, tab_ref, tabp_ref)
        outs = []
        for h in range(N_Q_HEADS):
            kv = slice((h // Q_PER_KV) * HEAD_DIM, (h // Q_PER_KV + 1) * HEAD_DIM)
            p, _ = _attn_probs(q[:, h * HEAD_DIM:(h + 1) * HEAD_DIM], kcat[:, kv], valid, sink_ref[0, h])
            outs.append(_dot(p.astype(BF16), vcat[:, kv]))
        o_ref[...] = jnp.concatenate(outs, axis=1).astype(BF16)

    return _call(
        "attn_fwd", body, grid=(t // BLOCK,), in_specs=_attn_specs(),
        out_specs=[pl.BlockSpec((BLOCK, ATTN_WIDTH), lambda n: (n, 0))],
        out_shape=[jax.ShapeDtypeStruct((t, ATTN_WIDTH), BF16)],
        args=[z, z, z, z, z, tab, tab, sinks], comm=comm,
    )[0]


def attn_bwd(z, tab, sinks, dcat, comm=None):
    t = z.shape[0]
    nb = t // BLOCK

    def body(zq_ref, zk_ref, zkp_ref, zv_ref, zvp_ref, tab_ref, tabp_ref, sink_ref, do_ref,
             dq_ref, dkv_ref, dsink_ref):
        n = pl.program_id(0)
        q, kcat, vcat, valid = _attn_prologue(n, zq_ref, zk_ref, zkp_ref, zv_ref, zvp_ref, tab_ref, tabp_ref)
        do = do_ref[...]
        lane = lax.broadcasted_iota(jnp.int32, (1, LANES), 1)
        dqs, dks, dvs = [], [], []
        dsink = jnp.zeros((1, LANES), F32)
        for hk in range(N_Q_HEADS // Q_PER_KV):
            kv = slice(hk * HEAD_DIM, (hk + 1) * HEAD_DIM)
            kh, vh = kcat[:, kv], vcat[:, kv]
            dk_t = jnp.zeros((HEAD_DIM, 2 * BLOCK), F32)
            dv_t = jnp.zeros((HEAD_DIM, 2 * BLOCK), F32)
            for g in range(Q_PER_KV):
                h = hk * Q_PER_KV + g
                hs = slice(h * HEAD_DIM, (h + 1) * HEAD_DIM)
                qh, doh = q[:, hs], do[:, hs]
                p, p_sink = _attn_probs(qh, kh, valid, sink_ref[0, h])
                dv_t = dv_t + _dot(doh, p.astype(BF16), "tn")
                dp = _dot(doh, vh, "nt")
                rd = jnp.sum(p * dp, axis=1, keepdims=True)
                ds = (p * (dp - rd) * (HEAD_DIM ** -0.5)).astype(BF16)
                dqs.append(_dot(ds, kh))
                dk_t = dk_t + _dot(qh, ds, "tn")
                dsink = dsink + jnp.where(lane == h, -jnp.sum(p_sink * rd, axis=0, keepdims=True), 0.0)
            dks.append(dk_t)
            dvs.append(dv_t)
        dq_ref[...] = _rope(jnp.concatenate(dqs, axis=1), tab_ref[...], -1.0).astype(BF16)
        dkc = jnp.concatenate(dks, axis=0).T * (HEAD_DIM ** 0.5)
        dk_pre = jnp.concatenate(
            [_rope(dkc[:BLOCK], tabp_ref[...], -1.0), _rope(dkc[BLOCK:], tab_ref[...], -1.0)], axis=0)
        dkv_ref[...] = jnp.concatenate([dk_pre, jnp.concatenate(dvs, axis=0).T], axis=1)

        @pl.when(n == 0)
        def _():
            dsink_ref[...] = jnp.zeros_like(dsink_ref)

        dsink_ref[...] += dsink

    return _call(
        "attn_bwd", body, grid=(nb,),
        in_specs=_attn_specs() + [pl.BlockSpec((BLOCK, ATTN_WIDTH), lambda n: (n, 0))],
        out_specs=[pl.BlockSpec((BLOCK, ATTN_WIDTH), lambda n: (n, 0)),
                   pl.BlockSpec((None, 2 * BLOCK, 2 * KV_WIDTH), lambda n: (n, 0, 0)),
                   pl.BlockSpec((1, LANES), lambda n: (0, 0))],
        out_shape=[jax.ShapeDtypeStruct((t, ATTN_WIDTH), BF16),
                   jax.ShapeDtypeStruct((nb, 2 * BLOCK, 2 * KV_WIDTH), F32),
                   jax.ShapeDtypeStruct((1, LANES), F32)],
        args=[z, z, z, z, z, tab, tab, sinks, dcat], comm=comm,
    )


def _gelu(x):
    k = math.sqrt(2.0 / math.pi)
    th = jnp.tanh(k * (x + 0.044715 * (x * x * x)))
    return 0.5 * x * (1.0 + th), th


def _gelu_grad(x, th):
    k = math.sqrt(2.0 / math.pi)
    return 0.5 * (1.0 + th) + 0.5 * x * (1.0 - th * th) * (k * (1.0 + 3.0 * 0.044715 * (x * x)))


def _sgu_core(zu_ref, zv_ref, lng_ref, lnb_ref, w_ref, bt_ref):
    up, vp = zu_ref[...], zv_ref[...]
    u, thu = _gelu(up)
    v, thv = _gelu(vp)
    mu = jnp.mean(v, axis=-1, keepdims=True)
    vc = v - mu
    rstd = lax.rsqrt(jnp.mean(vc * vc, axis=-1, keepdims=True) + RMS_EPS)
    xhat = vc * rstd
    vn = (xhat * lng_ref[...] + lnb_ref[...]).astype(BF16)
    row = lax.broadcasted_iota(jnp.int32, (CHUNK, CHUNK), 0)
    col = lax.broadcasted_iota(jnp.int32, (CHUNK, CHUNK), 1)
    mixed = []
    for g in range(SGU_GROUPS):
        wc = jnp.where(row >= col, w_ref[g], 0.0).astype(BF16)
        mixed.append(_dot(wc, vn[:, g * CHUNK:(g + 1) * CHUNK]) + bt_ref[:, g:g + 1])
    return up, vp, u, thu, thv, rstd, xhat, vn, jnp.concatenate(mixed, axis=1)


def _sgu_specs():
    full = lambda shape: pl.BlockSpec(shape, lambda n: (0,) * len(shape))
    return [
        pl.BlockSpec((CHUNK, SGU_WIDTH), lambda n: (n, ZU // SGU_WIDTH)),
        pl.BlockSpec((CHUNK, SGU_WIDTH), lambda n: (n, ZV // SGU_WIDTH)),
        full((1, SGU_WIDTH)), full((1, SGU_WIDTH)),
        full((SGU_GROUPS, CHUNK, CHUNK)), full((CHUNK, SGU_GROUPS)),
    ]


def sgu_fwd(z, ln_g, ln_b, w, b_t, comm=None):
    t = z.shape[0]

    def body(zu_ref, zv_ref, lng_ref, lnb_ref, w_ref, bt_ref, o_ref):
        _, _, u, _, _, _, _, _, mixed = _sgu_core(zu_ref, zv_ref, lng_ref, lnb_ref, w_ref, bt_ref)
        o_ref[...] = (u * mixed).astype(BF16)

    return _call(
        "sgu_fwd", body, grid=(t // CHUNK,), in_specs=_sgu_specs(),
        out_specs=[pl.BlockSpec((CHUNK, SGU_WIDTH), lambda n: (n, 0))],
        out_shape=[jax.ShapeDtypeStruct((t, SGU_WIDTH), BF16)], args=[z, z, ln_g, ln_b, w, b_t], comm=comm,
    )[0]


def sgu_bwd(z, ln_g, ln_b, w, w_t, b_t, dcat):
    t = z.shape[0]

    def body(zu_ref, zv_ref, lng_ref, lnb_ref, w_ref, bt_ref, wt_ref, dg_ref,
             du_ref, dv_ref, dw_ref, dbt_ref, dlng_ref, dlnb_ref):
        up, vp, u, thu, thv, rstd, xhat, vn, mixed = _sgu_core(zu_ref, zv_ref, lng_ref, lnb_ref, w_ref, bt_ref)
        dgate = dg_ref[...].astype(F32)
        du_ref[...] = (dgate * mixed * _gelu_grad(up, thu)).astype(BF16)
        dmixed = dgate * u
        row = lax.broadcasted_iota(jnp.int32, (CHUNK, CHUNK), 0)
        col = lax.broadcasted_iota(jnp.int32, (CHUNK, CHUNK), 1)

        @pl.when(pl.program_id(0) == 0)
        def _():
            dw_ref[...] = jnp.zeros_like(dw_ref)
            dbt_ref[...] = jnp.zeros_like(dbt_ref)
            dlng_ref[...] = jnp.zeros_like(dlng_ref)
            dlnb_ref[...] = jnp.zeros_like(dlnb_ref)

        dvn, dbt = [], jnp.zeros((CHUNK, LANES), F32)
        for g in range(SGU_GROUPS):
            gs = slice(g * CHUNK, (g + 1) * CHUNK)
            dmx = dmixed[:, gs]
            dmxb = dmx.astype(BF16)
            dbt = dbt + jnp.where(col == g, jnp.sum(dmx, axis=1, keepdims=True), 0.0)
            dw_ref[g] += jnp.where(row >= col, _dot(dmxb, vn[:, gs], "nt"), 0.0)
            wtc = jnp.where(col >= row, wt_ref[g], 0.0).astype(BF16)
            dvn.append(_dot(wtc, dmxb))
        dbt_ref[...] += dbt
        dvn = jnp.concatenate(dvn, axis=1)
        dlnb_ref[...] += jnp.sum(dvn, axis=0, keepdims=True)
        dlng_ref[...] += jnp.sum(dvn * xhat, axis=0, keepdims=True)
        dxh = dvn * lng_ref[...]
        dv = rstd * (dxh - jnp.mean(dxh, axis=-1, keepdims=True) - xhat * jnp.mean(dxh * xhat, axis=-1, keepdims=True))
        dv_ref[...] = (dv * _gelu_grad(vp, thv)).astype(BF16)

    full = lambda shape: pl.BlockSpec(shape, lambda n: (0,) * len(shape))
    act = pl.BlockSpec((CHUNK, SGU_WIDTH), lambda n: (n, 0))
    act_shape = jax.ShapeDtypeStruct((t, SGU_WIDTH), BF16)
    vec = jax.ShapeDtypeStruct((1, SGU_WIDTH), F32)
    return pl.pallas_call(
        body,
        out_shape=(act_shape, act_shape, jax.ShapeDtypeStruct((SGU_GROUPS, CHUNK, CHUNK), F32),
                   jax.ShapeDtypeStruct((CHUNK, LANES), F32), vec, vec),
        grid=(t // CHUNK,),
        in_specs=_sgu_specs() + [full((SGU_GROUPS, CHUNK, CHUNK)),
                                 pl.BlockSpec((CHUNK, SGU_WIDTH), lambda n: (n, 1))],
        out_specs=(act, act, full((SGU_GROUPS, CHUNK, CHUNK)), full((CHUNK, LANES)),
                   full((1, SGU_WIDTH)), full((1, SGU_WIDTH))),
        compiler_params=_cp(1), name="sgu_bwd",
    )(z, z, ln_g, ln_b, w, b_t, w_t, dcat)


def dz_assemble(dq, dkv, du, dv):
    t = dq.shape[0]
    nb = t // BLOCK

    def body(dq_ref, cur_ref, nxt_ref, du_ref, dv_ref, o_ref):
        n = pl.program_id(0)
        o_ref[:, ZQ:ZQ + ATTN_WIDTH] = dq_ref[...]
        o_ref[:, ZU:ZU + SGU_WIDTH] = du_ref[...]
        o_ref[:, ZV:ZV + SGU_WIDTH] = dv_ref[...]
        kv = cur_ref[BLOCK:, :] + jnp.where(n < nb - 1, nxt_ref[:BLOCK, :], 0.0)
        o_ref[:, ZK:ZK + 2 * KV_WIDTH] = kv.astype(BF16)

    act = pl.BlockSpec((BLOCK, ATTN_WIDTH), lambda n: (n, 0))
    return pl.pallas_call(
        body, out_shape=jax.ShapeDtypeStruct((t, IN_WIDTH), BF16), grid=(nb,),
        in_specs=[act,
                  pl.BlockSpec((None, 2 * BLOCK, 2 * KV_WIDTH), lambda n: (n, 0, 0)),
                  pl.BlockSpec((None, 2 * BLOCK, 2 * KV_WIDTH), lambda n: (jnp.minimum(n + 1, nb - 1), 0, 0)),
                  act, act],
        out_specs=pl.BlockSpec((BLOCK, IN_WIDTH), lambda n: (n, 0)),
        compiler_params=_cp(1), name="dz_assemble",
    )(dq, dkv, dkv, du, dv)


def _pool_count(i, tp, w):
    t_idx = i * tp + lax.broadcasted_iota(jnp.int32, (tp, 1), 0)
    return jnp.minimum(t_idx + 1, w).astype(F32)


def pool_fwd(h, pw, pscale, comm=None):
    t, d = h.shape
    tp = _tile(t, 256)
    per = tp // POOL_HALO

    def body(h_ref, halo_ref, pw_ref, ps_ref, m_ref, pooled_ref):
        i = pl.program_id(0)
        cur = h_ref[...]
        ext = jnp.concatenate([jnp.where(i > 0, halo_ref[...], 0.0), cur], axis=0)
        ys, pooled = [], []
        for gi, w in enumerate(POOL_WINDOWS):
            gs = slice(gi * POOL_GROUP_DIM, (gi + 1) * POOL_GROUP_DIM)
            s = ext[:, gs]
            sh = 1
            while sh < w:
                s = s + pltpu.roll(s, sh, 0)
                sh *= 2
            pg = (s[POOL_HALO:, :] / _pool_count(i, tp, w) - cur[:, gs]).astype(BF16)
            pooled.append(pg)
            ys.append(_dot(pg, pw_ref[gi]))
        pooled_ref[...] = jnp.concatenate(pooled, axis=1)
        m_ref[...] = jnp.concatenate(ys, axis=1) * ps_ref[...]

    row = pl.BlockSpec((tp, d), lambda i: (i, 0))
    return _call(
        "pool_fwd", body, grid=(t // tp,),
        in_specs=[row, pl.BlockSpec((POOL_HALO, d), lambda i: (jnp.maximum(i * per - 1, 0), 0)),
                  pl.BlockSpec(pw.shape, lambda i: (0, 0, 0)), pl.BlockSpec((1, d), lambda i: (0, 0))],
        out_specs=[row, row], out_shape=[jax.ShapeDtypeStruct((t, d), F32), jax.ShapeDtypeStruct((t, d), BF16)],
        args=[h, h, pw, pscale], comm=comm,
    )


def pool_bwd_proj(dm, pooled, pw, pscale):
    t, d = dm.shape
    tp = _tile(t, 256)

    def body(dm_ref, pooled_ref, pw_ref, ps_ref, dp_ref, dy_ref, dps_ref):
        dmv = dm_ref[...]
        dy = (dmv * ps_ref[...]).astype(BF16)
        dy_ref[...] = dy
        ys, dps = [], []
        for gi in range(len(POOL_WINDOWS)):
            gs = slice(gi * POOL_GROUP_DIM, (gi + 1) * POOL_GROUP_DIM)
            ys.append(_dot(pooled_ref[:, gs], pw_ref[gi]))
            dps.append(_dot(dy[:, gs], pw_ref[gi], "nt"))
        dp_ref[...] = jnp.concatenate(dps, axis=1)

        @pl.when(pl.program_id(0) == 0)
        def _():
            dps_ref[...] = jnp.zeros_like(dps_ref)

        dps_ref[...] += jnp.sum(dmv * jnp.concatenate(ys, axis=1), axis=0, keepdims=True)

    row = pl.BlockSpec((tp, d), lambda i: (i, 0))
    vec = pl.BlockSpec((1, d), lambda i: (0, 0))
    return pl.pallas_call(
        body,
        out_shape=(jax.ShapeDtypeStruct((t, d), F32), jax.ShapeDtypeStruct((t, d), BF16),
                   jax.ShapeDtypeStruct((1, d), F32)),
        grid=(t // tp,),
        in_specs=[row, row, pl.BlockSpec(pw.shape, lambda i: (0, 0, 0)), vec],
        out_specs=(row, row, vec), compiler_params=_cp(1), name="pool_bwd_proj",
    )(dm, pooled, pw, pscale)


def pool_bwd_window(dp):
    t, d = dp.shape
    tp = _tile(t, 256)
    per = tp // POOL_HALO
    last = t // POOL_HALO - 1
    nt = t // tp

    def body(dp_ref, halo_ref, dh_ref):
        i = pl.program_id(0)
        cur = dp_ref[...]
        halo = jnp.where(i < nt - 1, halo_ref[...], 0.0)
        outs = []
        for gi, w in enumerate(POOL_WINDOWS):
            gs = slice(gi * POOL_GROUP_DIM, (gi + 1) * POOL_GROUP_DIM)
            s = jnp.concatenate([cur[:, gs] / _pool_count(i, tp, w), halo[:, gs] / float(w)], axis=0)
            sh = 1
            while sh < w:
                s = s + pltpu.roll(s, tp + POOL_HALO - sh, 0)
                sh *= 2
            outs.append(s[:tp, :] - cur[:, gs])
        dh_ref[...] = jnp.concatenate(outs, axis=1)

    row = pl.BlockSpec((tp, d), lambda i: (i, 0))
    return pl.pallas_call(
        body, out_shape=jax.ShapeDtypeStruct((t, d), F32), grid=(nt,),
        in_specs=[row, pl.BlockSpec((POOL_HALO, d), lambda i: (jnp.minimum((i + 1) * per, last), 0))],
        out_specs=row, compiler_params=_cp(1), name="pool_bwd_window",
    )(dp, dp)


def pool_wgrad(pooled, dy):
    t, d = pooled.shape
    ng = d // POOL_GROUP_DIM
    tk = _tile(t, 512)
    blk = pl.BlockSpec((tk, POOL_GROUP_DIM), lambda g, k: (k, g))
    return _mm(
        "pool_wgrad", pooled, dy, dims="tn", grid=(ng, t // tk), a_spec=blk, b_spec=blk,
        o_spec=pl.BlockSpec((None, POOL_GROUP_DIM, POOL_GROUP_DIM), lambda g, k: (g, 0, 0)),
        out_shape=jax.ShapeDtypeStruct((ng, POOL_GROUP_DIM, POOL_GROUP_DIM), F32),
        acc_shape=(POOL_GROUP_DIM, POOL_GROUP_DIM), nk=t // tk,
    )


def _xattn_probs(qh, kh):
    s = _dot(qh, kh, "nt") * (X_HEAD_DIM ** -0.5)
    p = jnp.exp(s - jnp.max(s, axis=1, keepdims=True))
    return p * (1.0 / jnp.sum(p, axis=1, keepdims=True))


def xattn_fwd(name, q, k, v):
    t, xw = q.shape
    tm = _tile(t, 512)

    def body(q_ref, k_ref, v_ref, o_ref):
        outs = []
        for h in range(X_HEADS):
            hs = slice(h * X_HEAD_DIM, (h + 1) * X_HEAD_DIM)
            p = _xattn_probs(q_ref[:, hs], k_ref[:, hs])
            outs.append(_dot(p.astype(BF16), v_ref[:, hs]))
        o_ref[...] = jnp.concatenate(outs, axis=1).astype(BF16)

    row = pl.BlockSpec((tm, xw), lambda i: (i, 0))
    kv = pl.BlockSpec(k.shape, lambda i: (0, 0))
    return pl.pallas_call(
        body, out_shape=jax.ShapeDtypeStruct((t, xw), BF16), grid=(t // tm,),
        in_specs=[row, kv, kv], out_specs=row, compiler_params=_cp(1), name=name,
    )(q, k, v)


def xattn_bwd(name, q, k, v, do):
    t, xw = q.shape
    tm = _tile(t, 512)

    def body(q_ref, k_ref, v_ref, do_ref, dq_ref, dk_ref, dv_ref):
        @pl.when(pl.program_id(0) == 0)
        def _():
            dk_ref[...] = jnp.zeros_like(dk_ref)
            dv_ref[...] = jnp.zeros_like(dv_ref)

        dqs = []
        for h in range(X_HEADS):
            hs = slice(h * X_HEAD_DIM, (h + 1) * X_HEAD_DIM)
            qh, kh, vh, doh = q_ref[:, hs], k_ref[:, hs], v_ref[:, hs], do_ref[:, hs]
            p = _xattn_probs(qh, kh)
            dv_ref[:, hs] += _dot(p.astype(BF16), doh, "tn")
            dp = _dot(doh, vh, "nt")
            ds = (p * (dp - jnp.sum(p * dp, axis=1, keepdims=True)) * (X_HEAD_DIM ** -0.5)).astype(BF16)
            dqs.append(_dot(ds, kh))
            dk_ref[:, hs] += _dot(ds, qh, "tn")
        dq_ref[...] = jnp.concatenate(dqs, axis=1).astype(BF16)

    row = pl.BlockSpec((tm, xw), lambda i: (i, 0))
    kv = pl.BlockSpec(k.shape, lambda i: (0, 0))
    kv_shape = jax.ShapeDtypeStruct(k.shape, F32)
    return pl.pallas_call(
        body, out_shape=(jax.ShapeDtypeStruct((t, xw), BF16), kv_shape, kv_shape), grid=(t // tm,),
        in_specs=[row, kv, kv, row], out_specs=(row, kv, kv), compiler_params=_cp(1), name=name,
    )(q, k, v, do)


def kernel(x, mem, norms, mem_norm, ffn1_wg, ffn1_wu, ffn1_wd, ffn2_wg, ffn2_wu, ffn2_wd, x_wq, x_wk, x_wv, x_wo, mix_w_in, mix_w_out, attn_sinks, sgu_ln_g, sgu_ln_b, sgu_w, sgu_b, pool_w, pool_scale, loss_target, m_norms, m_mem_norm, m_ffn1_wg, m_ffn1_wu, m_ffn1_wd, m_ffn2_wg, m_ffn2_wu, m_ffn2_wd, m_x_wq, m_x_wk, m_x_wv, m_x_wo, m_mix_w_in, m_mix_w_out, m_attn_sinks, m_sgu_ln_g, m_sgu_ln_b, m_sgu_w, m_sgu_b, m_pool_w, m_pool_scale, v_norms, v_mem_norm, v_ffn1_wg, v_ffn1_wu, v_ffn1_wd, v_ffn2_wg, v_ffn2_wu, v_ffn2_wd, v_x_wq, v_x_wk, v_x_wv, v_x_wo, v_mix_w_in, v_mix_w_out, v_attn_sinks, v_sgu_ln_g, v_sgu_ln_b, v_sgu_w, v_sgu_b, v_pool_w, v_pool_scale):
    params = dict(norms=norms, mem_norm=mem_norm, ffn1_wg=ffn1_wg, ffn1_wu=ffn1_wu, ffn1_wd=ffn1_wd,
                  ffn2_wg=ffn2_wg, ffn2_wu=ffn2_wu, ffn2_wd=ffn2_wd, x_wq=x_wq, x_wk=x_wk, x_wv=x_wv, x_wo=x_wo,
                  mix_w_in=mix_w_in, mix_w_out=mix_w_out, attn_sinks=attn_sinks, sgu_ln_g=sgu_ln_g,
                  sgu_ln_b=sgu_ln_b, sgu_w=sgu_w, sgu_b=sgu_b, pool_w=pool_w, pool_scale=pool_scale)
    mom1 = dict(norms=m_norms, mem_norm=m_mem_norm, ffn1_wg=m_ffn1_wg, ffn1_wu=m_ffn1_wu, ffn1_wd=m_ffn1_wd,
                ffn2_wg=m_ffn2_wg, ffn2_wu=m_ffn2_wu, ffn2_wd=m_ffn2_wd, x_wq=m_x_wq, x_wk=m_x_wk, x_wv=m_x_wv,
                x_wo=m_x_wo, mix_w_in=m_mix_w_in, mix_w_out=m_mix_w_out, attn_sinks=m_attn_sinks,
                sgu_ln_g=m_sgu_ln_g, sgu_ln_b=m_sgu_ln_b, sgu_w=m_sgu_w, sgu_b=m_sgu_b, pool_w=m_pool_w,
                pool_scale=m_pool_scale)
    mom2 = dict(norms=v_norms, mem_norm=v_mem_norm, ffn1_wg=v_ffn1_wg, ffn1_wu=v_ffn1_wu, ffn1_wd=v_ffn1_wd,
                ffn2_wg=v_ffn2_wg, ffn2_wu=v_ffn2_wu, ffn2_wd=v_ffn2_wd, x_wq=v_x_wq, x_wk=v_x_wk, x_wv=v_x_wv,
                x_wo=v_x_wo, mix_w_in=v_mix_w_in, mix_w_out=v_mix_w_out, attn_sinks=v_attn_sinks,
                sgu_ln_g=v_sgu_ln_g, sgu_ln_b=v_sgu_ln_b, sgu_w=v_sgu_w, sgu_b=v_sgu_b, pool_w=v_pool_w,
                pool_scale=v_pool_scale)
    order = list(params)

    xs, memb, target = x[0], mem[0], loss_target[0]
    t, d = xs.shape
    depth = norms.shape[0]
    dsh = d // NDEV

    bf = lambda a: a.astype(BF16)
    wts = {}

    def gather_job(keys):
        return _Gather([bf(params[name][l]) for name, l in keys]), keys

    def land(job_keys):
        job, keys = job_keys
        for key, a in zip(keys, job.result):
            wts[key] = a

    ffn_keys = lambda tag, l: [(f"{tag}_wg", l), (f"{tag}_wu", l), (f"{tag}_wd", l)]
    x_keys = lambda l: [("x_wq", l), ("x_wk", l), ("x_wv", l), ("x_wo", l)]
    small_shapes = [norms.shape, pool_scale.shape, pool_w.shape]
    head = gather_job(ffn_keys("ffn1", 0))
    head_small = _Gather([_pack([norms, pool_scale, pool_w])])
    _run_exchange("gather_head", _Multi([head_small, head[0]]))
    land(head)
    norms_sh, pscale_sh, pw_sh = _unpack(head_small.result[0], small_shapes, (NDEV,))
    norms_full = norms_sh.transpose(1, 2, 0, 3).reshape(depth, norms.shape[1], d)
    pscale_full = pscale_sh.transpose(1, 0, 2).reshape(1, d)
    pw_full = pw_sh[:, 0].transpose(1, 0, 2, 3).reshape(len(POOL_WINDOWS), POOL_GROUP_DIM, POOL_GROUP_DIM).astype(BF16)
    fwd_jobs = {
        ("ffn1", 0): gather_job([("mix_w_in", 0)] + ffn_keys("ffn2", 0)[:2]),
        ("mix_in", 0): gather_job([("mix_w_out", 0)]),
        ("attn", 0): gather_job(ffn_keys("ffn2", 0)[2:]),
        ("sgu", 0): gather_job(x_keys(0)[:3]),
        ("mix_out", 0): gather_job(x_keys(0)[3:]),
        ("ffn2", 0): gather_job(ffn_keys("ffn1", 1)),
        ("resid_ffn1", 1): gather_job(x_keys(1)[:3]),
        ("pool", 1): gather_job(x_keys(1)[3:]),
        ("ffn1", 1): gather_job(ffn_keys("ffn2", 1)),
    }

    def riding(key, fn, *args, **kw):
        job = fwd_jobs.pop(key, None)
        res = fn(*args, comm=job and job[0], **kw)
        if job:
            land(job)
        return res

    tab = rope_table(t)
    sgu_w0 = sgu_w[0]
    sgu_wt0 = sgu_w0.transpose(0, 2, 1)
    sgu_bt0 = sgu_b[0].T
    gain = lambda l, i: norms_full[l, i][None, :]

    saved = []
    xc = xs
    h = norm_fwd("norm_ffn1_0", xc, gain(0, 0), BF16)
    for l in range(depth):
        s = {}
        pooling = l % 2 == 1

        def ffn_forward(tag, xc, h, gi, nxt, l=l, s=s):
            s[tag + "_h"] = h
            s[tag + "_a"], s[tag + "_b"], s[tag + "_hidt"], s[tag + "_m"] = riding(
                (tag, l), ffn_fwd, f"{tag}_fwd_{l}", h, wts[tag + "_wg", l], wts[tag + "_wu", l], wts[tag + "_wd", l])
            return riding(("resid_" + tag, l), resid_norm_fwd, f"resid_{tag}_{l}", xc, s[tag + "_m"],
                          gain(l, gi + 1), 0.5, nxt)

        s["x0"] = xc
        if pooling:
            xc, h2 = ffn_forward("ffn1", xc, h, 0, (gain(l, 2), F32, False))
        else:
            xc, h2, s["h2t"] = ffn_forward("ffn1", xc, h, 0, (gain(l, 2), BF16, True))

        s["x1"] = xc
        if not pooling:
            w_in = wts["mix_w_in", l].transpose(1, 0, 2).reshape(d, IN_WIDTH)
            o_k, o_u = ATTN_WIDTH, ATTN_WIDTH + 2 * KV_WIDTH
            w_in = jnp.concatenate([w_in[:, :o_k], w_in[:, o_u:], w_in[:, o_k:o_u]], axis=1)
            s["z"] = riding(("mix_in", l), mm_nn, "mix_in", h2, w_in, F32, tn=IN_WIDTH // 2)
            w_out = wts["mix_w_out", l].reshape(d, d)
            attn = riding(("attn", l), attn_fwd, s["z"], tab, attn_sinks)
            gate = riding(("sgu", l), sgu_fwd, s["z"], sgu_ln_g, sgu_ln_b, sgu_w0, sgu_bt0)
            s["cat"] = jnp.concatenate([attn, gate], axis=1)
            s["m2"] = riding(("mix_out", l), mm_nn, "mix_out", s["cat"], w_out, F32)
        else:
            s["m2"], s["pooled"] = riding(("pool", l), pool_fwd, h2, pw_full, pscale_full)
        xc, h3, s["h3t"] = resid_norm_fwd(f"resid_mix_{l}", xc, s["m2"], gain(l, 3), 1.0, (gain(l, 4), BF16, True))

        s["x2"] = xc
        wq, wk, wv = (wts[k, l].reshape(d, -1) for k in ("x_wq", "x_wk", "x_wv"))
        s["wq"], s["wkv"] = wq, jnp.concatenate([wk, wv], axis=1)
        s["mem_n"] = norm_fwd(f"norm_mem_{l}", memb, mem_norm[l][None, :], BF16)
        s["q"] = mm_nn(f"x_q_{l}", h3, wq, BF16)
        s["k"] = mm_nn(f"x_k_{l}", s["mem_n"], wk, BF16)
        s["v"] = mm_nn(f"x_v_{l}", s["mem_n"], wv, BF16)
        s["o"] = xattn_fwd(f"xattn_fwd_{l}", s["q"], s["k"], s["v"])
        s["wo"] = wts["x_wo", l].transpose(1, 0, 2).reshape(-1, d)
        s["m3"] = mm_nn(f"x_o_{l}", s["o"], s["wo"], F32)
        xc, h4 = resid_norm_fwd(f"resid_x_{l}", xc, s["m3"], gain(l, 5), 1.0, (gain(l, 6), BF16, False))

        s["x3"] = xc
        if l + 1 < depth:
            xc, h = ffn_forward("ffn2", xc, h4, 6, (gain(l + 1, 0), BF16, False))
        else:
            xc = ffn_forward("ffn2", xc, h4, 6, None)
        saved.append(s)
    assert not fwd_jobs, list(fwd_jobs)

    dx, loss11 = loss_grad(xc, target)
    loss = lax.psum(loss11[0, 0], ("x", "y", "c"))

    swaps = []
    pending = []
    recv = {}

    def emit_units(name, l, arr, among_chips):
        piece_mib = math.prod(arr.shape[1:]) * arr.dtype.itemsize / MIB
        parts = 2 if piece_mib > 1.5 else 1
        rows = arr.shape[1] // parts
        cost = (LINK_US_PER_MIB_CHIPS if among_chips else LINK_US_PER_MIB_ALL) * piece_mib / parts
        for part in range(parts):
            pending.append(((name, l, part), among_chips, (arr, part * rows, rows), cost))

    def emit(name, l, arr, two_level=False):
        if two_level:
            swaps.append((name, l, arr.reshape((NCHIP, 2) + arr.shape[1:])))
        else:
            emit_units(name, l, arr, False)

    def hosted(budget_us, fn, *args, extra=(), force=True, **kw):
        jobs = list(extra)
        swapped = swaps[:]
        del swaps[:]
        used = PAIR_SWAP_US * len(swapped)
        if swapped:
            jobs.append(_PairSwap([g for _, _, g in swapped]))
        items, kept = [], []
        for it in pending:
            if (force and not items and not swapped) or used + it[3] <= budget_us:
                items.append(it)
                used += it[3]
            else:
                kept.append(it)
        pending[:] = kept
        groups = [[it for it in items if it[1] == flag] for flag in (False, True)]
        unit_jobs = [cls([it[2] for it in grp]) if grp else None
                     for cls, grp in zip((_Scatter, _ChipScatter), groups)]
        jobs += [j for j in unit_jobs if j is not None]
        res = fn(*args, comm=_Multi(jobs) if jobs else None, **kw)
        for job, grp in zip(unit_jobs, groups):
            for it, a in zip(grp, job.result if job else ()):
                recv[it[0]] = a
        if swapped:
            for (name, l, g4), theirs in zip(swapped, jobs[len(extra)].result):
                emit_units(name, l, pair_sum(f"pairsum_{name}_{l}", g4, theirs), True)
        return res

    grads = {"mem_norm": [None] * depth}
    dgs = [[None] * 8 for _ in range(depth)]
    small_jobs = []
    dm, dgs[depth - 1][7] = norm_bwd(f"ffn2_post_bwd_{depth - 1}", saved[-1]["ffn2_m"], gain(depth - 1, 7), dx, 0.5,
                                     None, BF16)
    for l in reversed(range(depth)):
        s = saved[l]
        dg = dgs[l]
        pooling = l % 2 == 1

        def ffn_block(tag, dm, extra=(), l=l, s=s):
            emit(tag + "_wd", l, hosted(HOST_US_WGRAD, ffn_wgrad, f"{tag}_dwd_{l}", s[tag + "_hidt"], dm), True)
            da_t, db_t, dh = hosted(HOST_US_FFN_BWD, ffn_bwd, f"{tag}_bwd_{l}", dm, s[tag + "_a"], s[tag + "_b"],
                                    wts[tag + "_wg", l], wts[tag + "_wu", l], wts[tag + "_wd", l], extra=extra)
            emit(tag + "_wg", l, hosted(HOST_US_WGRAD, ffn_wgrad, f"{tag}_dwg_{l}", da_t, s[tag + "_h"]), True)
            emit(tag + "_wu", l, hosted(HOST_US_WGRAD, ffn_wgrad, f"{tag}_dwu_{l}", db_t, s[tag + "_h"]), True)
            return dh

        dh = ffn_block("ffn2", dm)
        dx, dg[6], dm, dg[5] = hosted(
            HOST_US_SMALL, norm_bwd, f"ffn2_pre_bwd_{l}", s["x3"], gain(l, 6), dh, 1.0, dx, F32,
            post=(s["m3"], gain(l, 5), 1.0, BF16), force=False)

        do = mm_nt(f"x_do_{l}", dm, s["wo"], BF16)
        g_wo = mm_tn(f"x_dwo_{l}", s["o"], dm, BF16, tmo=s["o"].shape[1])
        emit("x_wo", l, g_wo.reshape(-1, NDEV, dsh).transpose(1, 0, 2))
        dq, dk, dv = xattn_bwd(f"xattn_bwd_{l}", s["q"], s["k"], s["v"], do)
        dkb, dvb = dk.astype(BF16), dv.astype(BF16)
        emit("x_wq", l, mm_kred(f"x_dwq_{l}", s["h3t"], dq, BF16).reshape(NDEV, dsh, -1))
        emit("x_wk", l, mm_tn(f"x_dwk_{l}", s["mem_n"], dkb, BF16).reshape(NDEV, dsh, -1))
        emit("x_wv", l, mm_tn(f"x_dwv_{l}", s["mem_n"], dvb, BF16).reshape(NDEV, dsh, -1))
        dh = mm_nt(f"x_dh_{l}", dq, s["wq"], F32)
        dmem_n = mm_nt(f"x_dmem_{l}", jnp.concatenate([dkb, dvb], axis=1), s["wkv"], F32)
        _, grads["mem_norm"][l] = norm_bwd(f"mem_norm_bwd_{l}", memb, mem_norm[l][None, :], dmem_n, 1.0, None, F32)
        dx, dg[4], dm, dg[3] = norm_bwd(f"x_pre_bwd_{l}", s["x2"], gain(l, 4), dh, 1.0, dx, F32,
                                        post=(s["m2"], gain(l, 3), 1.0, F32 if pooling else BF16))

        if not pooling:
            dcat = mm_nt("mix_dcat", dm, w_out, BF16)
            emit("mix_w_out", l, mm_tn("mix_dwout", s["cat"], dm, BF16, tno=d // 2).reshape(NDEV, dsh, d))
            dq_a, dkv_a, dsink = hosted(HOST_US_ATTN_BWD, attn_bwd, s["z"], tab, attn_sinks, dcat, force=False)
            du_s, dv_s, g_sgu_w, g_sgu_bt, g_ln_g, g_ln_b = sgu_bwd(
                s["z"], sgu_ln_g, sgu_ln_b, sgu_w0, sgu_wt0, sgu_bt0, dcat)
            dz = dz_assemble(dq_a, dkv_a, du_s, dv_s)
            dh = mm_nt("mix_dh", dz, w_in, F32, tn=d // 2)
            g_win = hosted(2 * HOST_US_SMALL, mm_kred, "mix_dwin", s["h2t"], dz, BF16, tno=IN_WIDTH // 2, force=False)
            g_win = jnp.concatenate([g_win[:, :ATTN_WIDTH], g_win[:, ZK:], g_win[:, ZU:ZK]], axis=1)
            emit("mix_w_in", l, g_win.reshape(d, NDEV, -1).transpose(1, 0, 2))
        else:
            dp, dy, g_pscale = pool_bwd_proj(dm, s["pooled"], pw_full, pscale_full)
            g_pw = pool_wgrad(s["pooled"], dy)
            emit("pool_w", 0, g_pw.reshape(len(POOL_WINDOWS), NDEV, -1, POOL_GROUP_DIM).transpose(1, 0, 2, 3)
                 .reshape(NDEV, -1, POOL_GROUP_DIM))
            dh = pool_bwd_window(dp)
        dx, dg[2], dm, dg[1] = norm_bwd(f"mix_pre_bwd_{l}", s["x1"], gain(l, 2), dh, 1.0, dx, F32,
                                        post=(s["ffn1_m"], gain(l, 1), 0.5, BF16))

        if l == 0:
            replicated = ["mem_norm", "attn_sinks", "sgu_ln_g", "sgu_ln_b", "sgu_w", "sgu_b"]
            rep_grads = [jnp.concatenate(grads["mem_norm"], axis=0), dsink[:, :N_Q_HEADS], g_ln_g, g_ln_b,
                         g_sgu_w[None], g_sgu_bt[:, :SGU_GROUPS].T[None]]
            small_jobs.append(_Gather([_pack(rep_grads)]))
        dh = ffn_block("ffn1", dm, extra=small_jobs if l == 0 else ())
        if l > 0:
            dx, dg[0], dm, dgs[l - 1][7] = hosted(
                HOST_US_SMALL, norm_bwd, f"ffn1_pre_bwd_{l}", s["x0"], gain(l, 0), dh, 1.0, dx, F32,
                post=(saved[l - 1]["ffn2_m"], gain(l - 1, 7), 0.5, BF16), force=False)
        else:
            dx, dg[0] = norm_bwd(f"ffn1_pre_bwd_{l}", s["x0"], gain(l, 0), dh, 1.0, dx, F32)

    g_norms = jnp.stack([jnp.concatenate(dg, axis=0) for dg in dgs], axis=0)
    g_norms = g_norms.reshape(depth, norms.shape[1], NDEV, dsh).transpose(2, 0, 1, 3)
    g_pscale_p = g_pscale.reshape(1, NDEV, dsh).transpose(1, 0, 2)
    sharded_small = ["norms", "pool_scale"]
    pieces_small = jnp.stack([_pack([g_norms[j], g_pscale_p[j]]) for j in range(NDEV)], axis=0)
    small_scatter = _Scatter([(pieces_small, 0, pieces_small.shape[1])])

    out = {}

    def update(k, extra=(), host=True):
        waiting = [it[0] for it in pending if it[0][0] == k] + [it[:2] for it in swaps if it[0] == k]
        assert not waiting, waiting
        shp = params[k].shape
        flip = k.endswith(("_wg", "_wu"))
        c = shp[1] if flip else shp[-1]
        view = lambda a: (a.swapaxes(1, 2) if flip else a).reshape(-1, c)
        pieces = [recv[key] for key in sorted(key for key in recv if key[0] == k)]
        args = (f"adamw_{k}", view(params[k]), view(mom1[k]), view(mom2[k]), pieces)
        res = hosted(HOST_US_ADAMW, adamw, *args, extra=extra) if host else adamw(*args)
        if flip:
            out[k] = [a.reshape(shp[0], shp[2], shp[1]).swapaxes(1, 2) for a in res]
        else:
            out[k] = [a.reshape(shp) for a in res]

    def update_pack(names, pieces):
        shapes = [params[k].shape for k in names]
        res = adamw("adamw_" + names[0] + "_pack", _pack([params[k] for k in names]),
                    _pack([mom1[k] for k in names]), _pack([mom2[k] for k in names]), [pieces])
        for which in range(4):
            for k, a in zip(names, _unpack(res[which], shapes)):
                out.setdefault(k, [None] * 4)[which] = a

    last = ("ffn1_wg", "ffn1_wu", "ffn1_wd")
    big = ("ffn2_wg", "ffn2_wu", "ffn2_wd")
    early = [k for k in order if k not in last + big and k not in sharded_small and k not in replicated]
    for i, k in enumerate(early):
        update(k, extra=[small_scatter] if i == 0 else ())
    for k in big:
        update(k, host=False)
    flushes = 0
    while pending or swaps:
        hosted(float("inf"), lambda comm: _run_exchange(f"scatter_tail_{flushes}", comm))
        flushes += 1
    update_pack(replicated, small_jobs[0].result[0])
    update_pack(sharded_small, small_scatter.result[0])
    for k in last:
        update(k, host=False)

    outputs = [loss, dx[None]]
    for which in range(4):
        outputs += [out[k][which] for k in order]
    return tuple(outputs)
```

```python
import math

import jax
import jax.numpy as jnp
from jax import lax
from jax.experimental import pallas as pl
from jax.experimental.pallas import tpu as pltpu

F32 = jnp.float32
BF16 = jnp.bfloat16
NDEV = 8
MIB = 1024 * 1024
LANES = 128

RMS_EPS = 1e-6
HEAD_DIM = 64
N_Q_HEADS = 16
Q_PER_KV = 8
ATTN_WIDTH = 1024
KV_WIDTH = 128
BLOCK = 128
ROPE_DIM = 16
ROPE_THETA = 500000.0
SGU_GROUPS = 8
SGU_WIDTH = 1024
CHUNK = 128
POOL_WINDOWS = (2, 4, 8, 16)
POOL_GROUP_DIM = 512
POOL_HALO = 16
X_HEADS = 4
X_HEAD_DIM = 128
ZQ, ZU, ZV, ZK = 0, 1024, 2048, 3072
IN_WIDTH = 3328

ADAM_LR = 0.001
ADAM_B1 = 0.9
ADAM_B2 = 0.999
ADAM_EPS = 1e-08
ADAM_WD = 0.01
ADAM_STEP = 10

FFN_FWD_ROWS = 512
FFN_BWD_ROWS = 512

LINK_US_PER_MIB_ALL = 91.0
LINK_US_PER_MIB_CHIPS = 45.0
PAIR_SWAP_US = 20.0
HOST_US_FFN_BWD = 420.0
HOST_US_WGRAD = 100.0
HOST_US_ATTN_BWD = 240.0
HOST_US_SMALL = 40.0
HOST_US_ADAMW = 130.0

_DN = {
    "nn": (((1,), (0,)), ((), ())),
    "nt": (((1,), (1,)), ((), ())),
    "tn": (((0,), (0,)), ((), ())),
}


def _cp(naxes, vmem_mib=48):
    return pltpu.CompilerParams(dimension_semantics=("arbitrary",) * naxes, vmem_limit_bytes=vmem_mib * MIB)


def _tile(n, pref):
    t = min(n, pref)
    while n % t:
        t //= 2
    return t


def _dot(a, b, dims="nn"):
    return lax.dot_general(a, b, _DN[dims], preferred_element_type=F32)


def _me():
    x, y, c = lax.axis_index("x"), lax.axis_index("y"), lax.axis_index("c")
    return x, y, c, 4 * x + 2 * y + c


def _peer(k):
    x, y, c, _ = _me()
    px = 1 - x if k & 4 else x
    py = 1 - y if k & 2 else y
    pc = 1 - c if k & 1 else c
    return (px, py, pc), 4 * px + 2 * py + pc


class _Exchange:
    def __init__(self, arrs, out_shape, remote_per=NDEV - 1, local_per=1):
        self.arrs = list(arrs)
        self.n = len(self.arrs)
        self.out_shape = list(out_shape)
        self.remote_per = remote_per
        self.scratch = [
            pltpu.SemaphoreType.DMA((self.n * remote_per,)),
            pltpu.SemaphoreType.DMA((self.n * remote_per,)),
            pltpu.SemaphoreType.DMA((self.n * local_per,)),
        ]
        self.result = None

    def mid(self, ins, outs, sems):
        pass

    def _copy(self, src, dst, sems, i, k, dev):
        send, recv, _ = sems
        return pltpu.make_async_remote_copy(
            src_ref=src, dst_ref=dst, send_sem=send.at[i * self.remote_per + k - 1],
            recv_sem=recv.at[i * self.remote_per + k - 1], device_id=dev, device_id_type=pl.DeviceIdType.MESH)


class _Gather(_Exchange):
    def __init__(self, arrs):
        super().__init__(arrs, [jax.ShapeDtypeStruct((NDEV,) + a.shape, a.dtype) for a in arrs])

    def start(self, ins, outs, sems):
        me = _me()[3]
        for i in range(self.n):
            pltpu.make_async_copy(ins[i], outs[i].at[me], sems[2].at[i]).start()
        for k in (1, 2, 4, 6):
            dev, _ = _peer(k)
            for i in range(self.n):
                self._copy(ins[i], outs[i].at[me], sems, i, k, dev).start()

    def mid(self, ins, outs, sems):
        sibling, _ = _peer(1)
        for k in (2, 4, 6):
            dev, slot = _peer(k)
            for i in range(self.n):
                block = outs[i].at[slot]
                self._copy(ins[i], block, sems, i, k, dev).wait_recv()
                self._copy(block, block, sems, i, k + 1, sibling).start()

    def finish(self, ins, outs, sems):
        me = _me()[3]
        sibling, _ = _peer(1)
        for k in (1, 3, 5, 7):
            dev, slot = _peer(k)
            for i in range(self.n):
                self._copy(ins[i], outs[i].at[slot], sems, i, k, dev).wait_recv()
        for k in range(1, NDEV):
            for i in range(self.n):
                self._copy(ins[i], outs[i].at[me], sems, i, k, sibling).wait_send()
        for i in range(self.n):
            pltpu.make_async_copy(ins[i], outs[i].at[me], sems[2].at[i]).wait()


class _Scatter(_Exchange):
    def __init__(self, units):
        self.rows = [(r0, n) for _, r0, n in units]
        super().__init__([a for a, _, _ in units],
                         [jax.ShapeDtypeStruct((NDEV, n) + a.shape[2:], a.dtype) for a, _, n in units])

    def _src(self, ins, i, slot):
        r0, n = self.rows[i]
        return ins[i].at[slot, pl.ds(r0, n)]

    def start(self, ins, outs, sems):
        me = _me()[3]
        for i in range(self.n):
            pltpu.make_async_copy(self._src(ins, i, me), outs[i].at[me], sems[2].at[i]).start()
        for k in range(1, NDEV):
            dev, slot = _peer(k)
            for i in range(self.n):
                self._copy(self._src(ins, i, slot), outs[i].at[me], sems, i, k, dev).start()

    def finish(self, ins, outs, sems):
        me = _me()[3]
        for k in range(1, NDEV):
            dev, slot = _peer(k)
            for i in range(self.n):
                cp = self._copy(self._src(ins, i, slot), outs[i].at[slot], sems, i, k, dev)
                cp.wait_send()
                cp.wait_recv()
        for i in range(self.n):
            pltpu.make_async_copy(self._src(ins, i, me), outs[i].at[me], sems[2].at[i]).wait()


NCHIP = NDEV // 2


class _PairSwap(_Exchange):
    def __init__(self, arrs):
        super().__init__(arrs, [jax.ShapeDtypeStruct((NCHIP,) + a.shape[2:], a.dtype) for a in arrs],
                         remote_per=NCHIP)

    def _copies(self, ins, outs, sems):
        c = _me()[2]
        sibling, _ = _peer(1)
        for i in range(self.n):
            for q in range(NCHIP):
                yield self._copy(ins[i].at[q, 1 - c], outs[i].at[q], sems, i, q + 1, sibling)

    def start(self, ins, outs, sems):
        for remote in self._copies(ins, outs, sems):
            remote.start()

    def finish(self, ins, outs, sems):
        for remote in self._copies(ins, outs, sems):
            remote.wait_send()
            remote.wait_recv()


class _ChipScatter(_Exchange):
    def __init__(self, units):
        self.rows = [(r0, n) for _, r0, n in units]
        super().__init__([a for a, _, _ in units],
                         [jax.ShapeDtypeStruct((NCHIP, n) + a.shape[2:], a.dtype) for a, _, n in units],
                         remote_per=NCHIP - 1)

    def _src(self, ins, i, chip):
        r0, n = self.rows[i]
        return ins[i].at[chip, pl.ds(r0, n)]

    @staticmethod
    def _chip(k):
        dev, slot = _peer(2 * k)
        return dev, slot // 2

    def start(self, ins, outs, sems):
        mine = _me()[3] // 2
        for i in range(self.n):
            pltpu.make_async_copy(self._src(ins, i, mine), outs[i].at[mine], sems[2].at[i]).start()
        for k in range(1, NCHIP):
            dev, chip = self._chip(k)
            for i in range(self.n):
                self._copy(self._src(ins, i, chip), outs[i].at[mine], sems, i, k, dev).start()

    def finish(self, ins, outs, sems):
        mine = _me()[3] // 2
        for k in range(1, NCHIP):
            dev, chip = self._chip(k)
            for i in range(self.n):
                cp = self._copy(self._src(ins, i, chip), outs[i].at[chip], sems, i, k, dev)
                cp.wait_send()
                cp.wait_recv()
        for i in range(self.n):
            pltpu.make_async_copy(self._src(ins, i, mine), outs[i].at[mine], sems[2].at[i]).wait()


class _Multi:
    def __init__(self, jobs):
        self.jobs = list(jobs)
        self.arrs = [a for j in self.jobs for a in j.arrs]
        self.out_shape = [s for j in self.jobs for s in j.out_shape]
        self.scratch = [s for j in self.jobs for s in j.scratch]
        self._result = None

    def _parts(self, ins, outs, sems):
        oi = oo = 0
        for idx, j in enumerate(self.jobs):
            ni, no = len(j.arrs), len(j.out_shape)
            yield j, ins[oi:oi + ni], outs[oo:oo + no], sems[3 * idx:3 * idx + 3]
            oi += ni
            oo += no

    def start(self, ins, outs, sems):
        for j, i, o, s in self._parts(ins, outs, sems):
            j.start(i, o, s)

    def mid(self, ins, outs, sems):
        for j, i, o, s in self._parts(ins, outs, sems):
            j.mid(i, o, s)

    def finish(self, ins, outs, sems):
        for j, i, o, s in self._parts(ins, outs, sems):
            j.finish(i, o, s)

    @property
    def result(self):
        return self._result

    @result.setter
    def result(self, res):
        self._result = res
        o = 0
        for j in self.jobs:
            j.result = list(res[o:o + len(j.out_shape)])
            o += len(j.out_shape)


def _call(name, body, *, grid, in_specs, out_specs, out_shape, args, scratch=(), comm=None, vmem_mib=48):
    in_specs, out_specs, out_shape = list(in_specs), list(out_specs), list(out_shape)
    scratch, args = list(scratch), list(args)
    ni, no, ns = len(in_specs), len(out_specs), len(scratch)
    kernel_fn = body
    if comm is not None:
        ci, co = len(comm.arrs), len(comm.out_shape)
        hbm = pl.BlockSpec(memory_space=pltpu.HBM)

        def kernel_fn(*refs):
            refs = list(refs)
            ins, c_in, outs, c_out, scr, c_scr = (
                [refs.pop(0) for _ in range(cnt)] for cnt in (ni, ci, no, co, ns, len(comm.scratch)))
            if not grid:
                comm.start(c_in, c_out, c_scr)
                body(*ins, *outs, *scr)
                comm.mid(c_in, c_out, c_scr)
                comm.finish(c_in, c_out, c_scr)
                return
            step = pl.program_id(0)
            for ax in range(1, len(grid)):
                step = step * grid[ax] + pl.program_id(ax)
            nsteps = math.prod(grid)
            mid_step = nsteps - 1

            @pl.when(step == 0)
            def _():
                comm.start(c_in, c_out, c_scr)

            body(*ins, *outs, *scr)

            @pl.when(step == mid_step)
            def _():
                comm.mid(c_in, c_out, c_scr)

            @pl.when(step == nsteps - 1)
            def _():
                comm.finish(c_in, c_out, c_scr)

        in_specs += [hbm] * ci
        out_specs += [hbm] * co
        out_shape += comm.out_shape
        scratch += comm.scratch
        args += comm.arrs
    params = _cp(len(grid), vmem_mib) if grid else None
    res = pl.pallas_call(
        kernel_fn, out_shape=out_shape, grid=grid, in_specs=in_specs, out_specs=out_specs,
        scratch_shapes=scratch, compiler_params=params, name=name,
    )(*args)
    if comm is not None:
        comm.result = list(res[no:])
    return tuple(res[:no])


def _run_exchange(name, comm):
    _call(name, lambda: None, grid=(), in_specs=[], out_specs=[], out_shape=[], args=[], comm=comm)
    return comm.result


def _pack(arrs, dtype=F32):
    flat = jnp.concatenate([a.astype(dtype).reshape(-1) for a in arrs])
    n = flat.shape[0]
    total = -(-n // (16 * LANES)) * (16 * LANES)
    return jnp.pad(flat, (0, total - n)).reshape(total // LANES, LANES)


def _unpack(packed, shapes, lead=()):
    flat = packed.reshape(lead + (-1,))
    out, off = [], 0
    for s in shapes:
        n = math.prod(s)
        out.append(flat[..., off:off + n].reshape(lead + tuple(s)))
        off += n
    return out


def _mm(name, a, b, *, dims, grid, a_spec, b_spec, o_spec, out_shape, acc_shape=None, nk=1, vmem_mib=48, comm=None):
    nax = len(grid)

    def body(a_ref, b_ref, o_ref, *scratch):
        p = _dot(a_ref[...], b_ref[...], dims)
        if nk == 1:
            o_ref[...] = p.astype(o_ref.dtype)
            return
        acc = scratch[0]
        k = pl.program_id(nax - 1)

        @pl.when(k == 0)
        def _():
            acc[...] = p

        @pl.when(k > 0)
        def _():
            acc[...] += p

        @pl.when(k == nk - 1)
        def _():
            o_ref[...] = acc[...].astype(o_ref.dtype)

    return _call(
        name, body, grid=grid, in_specs=[a_spec, b_spec], out_specs=[o_spec], out_shape=[out_shape], args=[a, b],
        scratch=[pltpu.VMEM(acc_shape, F32)] if nk > 1 else [], comm=comm, vmem_mib=vmem_mib,
    )[0]


def mm_nn(name, a, b, out_dtype, tn=None, comm=None):
    m, k = a.shape
    n = b.shape[1]
    tm = _tile(m, 512)
    tn = n if tn is None else tn
    return _mm(
        name, a, b, dims="nn", grid=(n // tn, m // tm),
        a_spec=pl.BlockSpec((tm, k), lambda j, i: (i, 0)),
        b_spec=pl.BlockSpec((k, tn), lambda j, i: (0, j)),
        o_spec=pl.BlockSpec((tm, tn), lambda j, i: (i, j)),
        out_shape=jax.ShapeDtypeStruct((m, n), out_dtype), comm=comm,
    )


def mm_nt(name, a, b, out_dtype, tn=None):
    m, k = a.shape
    n = b.shape[0]
    tm = _tile(m, 512)
    tn = n if tn is None else tn
    return _mm(
        name, a, b, dims="nt", grid=(n // tn, m // tm),
        a_spec=pl.BlockSpec((tm, k), lambda j, i: (i, 0)),
        b_spec=pl.BlockSpec((tn, k), lambda j, i: (j, 0)),
        o_spec=pl.BlockSpec((tm, tn), lambda j, i: (i, j)),
        out_shape=jax.ShapeDtypeStruct((m, n), out_dtype),
    )


def mm_kred(name, a_t, b, out_dtype, tno=None, comm=None):
    m, k = a_t.shape
    n = b.shape[1]
    tk = _tile(k, 512)
    tmo = _tile(m, 1024)
    tno = n if tno is None else tno
    return _mm(
        name, a_t, b, dims="nn", grid=(m // tmo, n // tno, k // tk),
        a_spec=pl.BlockSpec((tmo, tk), lambda i, j, kk: (i, kk)),
        b_spec=pl.BlockSpec((tk, tno), lambda i, j, kk: (kk, j)),
        o_spec=pl.BlockSpec((tmo, tno), lambda i, j, kk: (i, j)),
        out_shape=jax.ShapeDtypeStruct((m, n), out_dtype),
        acc_shape=(tmo, tno), nk=k // tk, comm=comm,
    )


def mm_fullk(name, a_t, b, out_dtype, tno, comm=None):
    m, k = a_t.shape
    n = b.shape[1]
    tmo = _tile(m, 1024)
    return _mm(
        name, a_t, b, dims="nn", grid=(m // tmo, n // tno),
        a_spec=pl.BlockSpec((tmo, k), lambda i, j: (i, 0)),
        b_spec=pl.BlockSpec((k, tno), lambda i, j: (0, j)),
        o_spec=pl.BlockSpec((tmo, tno), lambda i, j: (i, j)),
        out_shape=jax.ShapeDtypeStruct((m, n), out_dtype), comm=comm,
    )


def mm_tn(name, a, b, out_dtype, tmo=None, tno=None):
    k, m = a.shape
    n = b.shape[1]
    tk = _tile(k, 512)
    tmo = _tile(m, 1024) if tmo is None else tmo
    tno = n if tno is None else tno
    return _mm(
        name, a, b, dims="tn", grid=(m // tmo, n // tno, k // tk),
        a_spec=pl.BlockSpec((tk, tmo), lambda i, j, kk: (kk, i)),
        b_spec=pl.BlockSpec((tk, tno), lambda i, j, kk: (kk, j)),
        o_spec=pl.BlockSpec((tmo, tno), lambda i, j, kk: (i, j)),
        out_shape=jax.ShapeDtypeStruct((m, n), out_dtype),
        acc_shape=(tmo, tno), nk=k // tk,
    )


def _rstd(x):
    return lax.rsqrt(jnp.mean(x * x, axis=-1, keepdims=True) + RMS_EPS)


def norm_fwd(name, x, g, out_dtype, with_t=False, comm=None):
    t, d = x.shape
    tm = _tile(t, 256)

    def body(x_ref, g_ref, o_ref, *t_ref):
        xv = x_ref[...]
        h = xv * _rstd(xv) * g_ref[...]
        o_ref[...] = h.astype(o_ref.dtype)
        if with_t:
            t_ref[0][...] = h.T.astype(out_dtype)

    row = pl.BlockSpec((tm, d), lambda i: (i, 0))
    out_shape = [jax.ShapeDtypeStruct((t, d), out_dtype)]
    out_specs = [row]
    if with_t:
        out_shape.append(jax.ShapeDtypeStruct((d, t), out_dtype))
        out_specs.append(pl.BlockSpec((d, tm), lambda i: (0, i)))
    res = _call(
        name, body, grid=(t // tm,), in_specs=[row, pl.BlockSpec((1, d), lambda i: (0, 0))],
        out_specs=out_specs, out_shape=out_shape, args=[x, g], comm=comm,
    )
    return res if with_t else res[0]


def resid_norm_fwd(name, x, m, g, scale, nxt=None, comm=None):
    t, d = x.shape
    tm = _tile(t, 256)
    with_t = nxt is not None and nxt[2]

    def body(*refs):
        if nxt is None:
            x_ref, m_ref, g_ref, o_ref = refs
        else:
            x_ref, m_ref, g_ref, gn_ref, o_ref, h_ref = refs[:6]
        mv = m_ref[...]
        xn = x_ref[...] + scale * (mv * _rstd(mv) * g_ref[...])
        o_ref[...] = xn
        if nxt is not None:
            h = xn * _rstd(xn) * gn_ref[...]
            h_ref[...] = h.astype(h_ref.dtype)
            if with_t:
                refs[6][...] = h.T.astype(nxt[1])

    row = pl.BlockSpec((tm, d), lambda i: (i, 0))
    vec = pl.BlockSpec((1, d), lambda i: (0, 0))
    out_shape, out_specs, args = [jax.ShapeDtypeStruct((t, d), F32)], [row], [x, m, g]
    if nxt is not None:
        args.append(nxt[0])
        out_shape.append(jax.ShapeDtypeStruct((t, d), nxt[1]))
        out_specs.append(row)
        if with_t:
            out_shape.append(jax.ShapeDtypeStruct((d, t), nxt[1]))
            out_specs.append(pl.BlockSpec((d, tm), lambda i: (0, i)))
    res = _call(
        name, body, grid=(t // tm,), in_specs=[row, row, vec] + ([vec] if nxt is not None else []),
        out_specs=out_specs, out_shape=out_shape, args=args, comm=comm,
    )
    return res[0] if nxt is None else res


def norm_bwd(name, u, g, dy, scale, resid, out_dtype, post=None, comm=None):
    t, d = u.shape
    tm = _tile(t, 256)
    has_resid = resid is not None

    def norm_grad(uv, gv, dyv, dg_ref):
        r = _rstd(uv)
        uh = uv * r
        dg_ref[...] += jnp.sum(dyv * uh, axis=0, keepdims=True)
        dyg = dyv * gv
        return r * (dyg - uh * jnp.mean(dyg * uh, axis=-1, keepdims=True))

    def body(*refs):
        refs = list(refs)
        u_ref, g_ref, dy_ref = refs[:3]
        del refs[:3]
        r_ref = refs.pop(0) if has_resid else None
        if post is not None:
            m_ref, gp_ref = refs[:2]
            del refs[:2]
        du_ref, dg_ref = refs[:2]

        @pl.when(pl.program_id(0) == 0)
        def _():
            for ref in refs[1::2]:
                ref[...] = jnp.zeros_like(ref)

        du = norm_grad(u_ref[...], g_ref[...], dy_ref[...].astype(F32) * scale, dg_ref)
        if has_resid:
            du = du + r_ref[...]
        du_ref[...] = du.astype(du_ref.dtype)
        if post is not None:
            dm_ref, dgp_ref = refs[2:4]
            dm_ref[...] = norm_grad(m_ref[...], gp_ref[...], du * post[2], dgp_ref).astype(dm_ref.dtype)

    row = pl.BlockSpec((tm, d), lambda i: (i, 0))
    vec = pl.BlockSpec((1, d), lambda i: (0, 0))
    args = [u, g, dy] + ([resid] if has_resid else [])
    in_specs = [row, vec, row] + ([row] if has_resid else [])
    out_specs = [row, vec]
    out_shape = [jax.ShapeDtypeStruct((t, d), out_dtype), jax.ShapeDtypeStruct((1, d), F32)]
    if post is not None:
        args += [post[0], post[1]]
        in_specs += [row, vec]
        out_specs += [row, vec]
        out_shape += [jax.ShapeDtypeStruct((t, d), post[3]), jax.ShapeDtypeStruct((1, d), F32)]
    return _call(name, body, grid=(t // tm,), in_specs=in_specs, out_specs=out_specs, out_shape=out_shape,
                 args=args, comm=comm)


def loss_grad(y, target):
    t, d = y.shape
    tm = _tile(t, 256)
    nt = t // tm

    def body(y_ref, t_ref, dy_ref, loss_ref, acc):
        i = pl.program_id(0)
        e = y_ref[...] - t_ref[...]
        dy_ref[...] = e * (1.0 / d)

        @pl.when(i == 0)
        def _():
            acc[...] = jnp.zeros_like(acc)

        acc[...] += jnp.sum(e * e, axis=0, keepdims=True)

        @pl.when(i == nt - 1)
        def _():
            loss_ref[...] = (0.5 / d) * jnp.sum(acc[...], axis=1, keepdims=True)

    row = pl.BlockSpec((tm, d), lambda i: (i, 0))
    return pl.pallas_call(
        body,
        out_shape=(jax.ShapeDtypeStruct((t, d), F32), jax.ShapeDtypeStruct((1, 1), F32)),
        grid=(nt,), in_specs=[row, row], out_specs=(row, pl.BlockSpec((1, 1), lambda i: (0, 0))),
        scratch_shapes=[pltpu.VMEM((1, d), F32)], compiler_params=_cp(1), name="loss_grad",
    )(y, target)


def _row_tile(r, c):
    if r * c * 4 <= MIB:
        return r
    best = None
    for t in range(16, r, 16):
        if r % t == 0 and t * c * 4 <= MIB:
            best = t
    return r if best is None else best


def pair_sum(name, g4, theirs):
    nq, _, r, c = g4.shape
    tr = _tile(r, 1024)

    def body(core_ref, g_ref, t_ref, o_ref):
        o_ref[...] = (g_ref[...].astype(F32) + t_ref[...].astype(F32)).astype(o_ref.dtype)

    blk = pl.BlockSpec((None, tr, c), lambda q, i, core: (q, i, 0))
    return pl.pallas_call(
        body, out_shape=jax.ShapeDtypeStruct(theirs.shape, theirs.dtype),
        grid_spec=pltpu.PrefetchScalarGridSpec(
            num_scalar_prefetch=1, grid=(nq, r // tr),
            in_specs=[pl.BlockSpec((None, None, tr, c), lambda q, i, core: (q, core[0], i, 0)), blk],
            out_specs=blk),
        compiler_params=_cp(2), name=name,
    )(lax.axis_index("c").astype(jnp.int32).reshape(1), g4, theirs)


def adamw(name, w, m, v, pieces, comm=None):
    nl = len(pieces)
    npiece, r, c = pieces[0].shape
    tr = _row_tile(r, c)
    nr = r // tr
    bc1 = 1.0 - ADAM_B1 ** ADAM_STEP
    bc2 = 1.0 - ADAM_B2 ** ADAM_STEP

    def body(w_ref, m_ref, v_ref, *rest):
        p_refs, (g_ref, d_ref, nm_ref, nv_ref) = rest[:nl], rest[nl:]

        def update(p_ref):
            g = p_ref[0].astype(F32)
            for j in range(1, npiece):
                g = g + p_ref[j].astype(F32)
            m1 = ADAM_B1 * m_ref[...] + (1.0 - ADAM_B1) * g
            v1 = ADAM_B2 * v_ref[...] + (1.0 - ADAM_B2) * (g * g)
            m_hat = m1 / bc1
            v_hat = v1 / bc2
            g_ref[...] = g
            d_ref[...] = -ADAM_LR * (m_hat / (jnp.sqrt(v_hat) + ADAM_EPS) + ADAM_WD * w_ref[...])
            nm_ref[...] = m1
            nv_ref[...] = v1

        if nl == 1:
            update(p_refs[0])
        else:
            for ll in range(nl):
                pl.when(pl.program_id(0) == ll)(lambda ll=ll: update(p_refs[ll]))

    def piece_spec(ll):
        return pl.BlockSpec((npiece, tr, c), lambda l, i: (0, jnp.where(l == ll, i, jnp.where(l > ll, nr - 1, 0)), 0))

    row = pl.BlockSpec((tr, c), lambda l, i: (l * nr + i, 0))
    out = jax.ShapeDtypeStruct((nl * r, c), F32)
    return _call(
        name, body, grid=(nl, nr), in_specs=[row, row, row] + [piece_spec(ll) for ll in range(nl)],
        out_specs=[row] * 4, out_shape=[out] * 4, args=[w, m, v] + list(pieces), comm=comm,
    )


def _sigmoid(a):
    return 1.0 / (1.0 + jnp.exp(-a))


def ffn_fwd(name, h, wg, wu, wd, comm=None):
    t, d = h.shape
    ns, _, f = wg.shape
    tm = _tile(t, FFN_FWD_ROWS)

    def body(h_ref, wg_ref, wu_ref, wd_ref, a_ref, b_ref, hidt_ref, m_ref, acc):
        j = pl.program_id(1)
        hv = h_ref[...]
        a = _dot(hv, wg_ref[...])
        b = _dot(hv, wu_ref[...])
        hid32 = (a * _sigmoid(a)) * b
        hid = hid32.astype(BF16)
        a_ref[...] = a.astype(BF16)
        b_ref[...] = b.astype(BF16)
        hidt_ref[...] = hid32.T.astype(BF16)
        p = _dot(hid, wd_ref[...])

        @pl.when(j == 0)
        def _():
            acc[...] = p

        @pl.when(j > 0)
        def _():
            acc[...] += p

        @pl.when(j == ns - 1)
        def _():
            m_ref[...] = acc[...]

    w_in = pl.BlockSpec((None, d, f), lambda i, j: (j, 0, 0))
    act = pl.BlockSpec((None, tm, f), lambda i, j: (j, i, 0))
    act_shape = jax.ShapeDtypeStruct((ns, t, f), BF16)
    return _call(
        name, body, grid=(t // tm, ns),
        in_specs=[pl.BlockSpec((tm, d), lambda i, j: (i, 0)), w_in, w_in,
                  pl.BlockSpec((None, f, d), lambda i, j: (j, 0, 0))],
        out_specs=[act, act, pl.BlockSpec((None, f, tm), lambda i, j: (j, 0, i)),
                   pl.BlockSpec((tm, d), lambda i, j: (i, 0))],
        out_shape=[act_shape, act_shape, jax.ShapeDtypeStruct((ns, f, t), BF16), jax.ShapeDtypeStruct((t, d), F32)],
        args=[h, wg, wu, wd], scratch=[pltpu.VMEM((tm, d), F32)], comm=comm, vmem_mib=56,
    )


def ffn_bwd(name, dm, a, b, wg, wu, wd, comm=None):
    t, d = dm.shape
    ns, _, f = wg.shape
    tm = _tile(t, FFN_BWD_ROWS)

    def body(dm_ref, a_ref, b_ref, wg_ref, wu_ref, wd_ref, dat_ref, dbt_ref, dh_ref, acc):
        j = pl.program_id(1)
        dhid = _dot(dm_ref[...], wd_ref[...], "nt")
        av = a_ref[...].astype(F32)
        bv = b_ref[...].astype(F32)
        sig = _sigmoid(av)
        da32 = dhid * bv * (sig * (1.0 + av * (1.0 - sig)))
        db32 = dhid * (av * sig)
        dat_ref[...] = da32.T.astype(BF16)
        dbt_ref[...] = db32.T.astype(BF16)
        p = _dot(da32.astype(BF16), wg_ref[...], "nt") + _dot(db32.astype(BF16), wu_ref[...], "nt")

        @pl.when(j == 0)
        def _():
            acc[...] = p

        @pl.when(j > 0)
        def _():
            acc[...] += p

        @pl.when(j == ns - 1)
        def _():
            dh_ref[...] = acc[...]

    w_in = pl.BlockSpec((None, d, f), lambda i, j: (j, 0, 0))
    act = pl.BlockSpec((None, tm, f), lambda i, j: (j, i, 0))
    act_t = pl.BlockSpec((None, f, tm), lambda i, j: (j, 0, i))
    row = pl.BlockSpec((tm, d), lambda i, j: (i, 0))
    act_t_shape = jax.ShapeDtypeStruct((ns, f, t), BF16)
    return _call(
        name, body, grid=(t // tm, ns),
        in_specs=[row, act, act, w_in, w_in, pl.BlockSpec((None, f, d), lambda i, j: (j, 0, 0))],
        out_specs=[act_t, act_t, row],
        out_shape=[act_t_shape, act_t_shape, jax.ShapeDtypeStruct((t, d), F32)],
        args=[dm, a, b, wg, wu, wd], scratch=[pltpu.VMEM((tm, d), F32)], comm=comm, vmem_mib=56,
    )


def ffn_wgrad(name, act_t, x, comm=None):
    ns, f, t = act_t.shape
    d = x.shape[1]
    return _mm(
        name, act_t, x, dims="nn", grid=(ns,),
        a_spec=pl.BlockSpec((None, f, t), lambda j: (j, 0, 0)),
        b_spec=pl.BlockSpec((t, d), lambda j: (0, 0), pipeline_mode=pl.Buffered(1)),
        o_spec=pl.BlockSpec((None, f, d), lambda j: (j, 0, 0)),
        out_shape=jax.ShapeDtypeStruct((ns, f, d), BF16), vmem_mib=56, comm=comm,
    )


def rope_table(t):
    half = ROPE_DIM // 2
    inv = ROPE_THETA ** (-jnp.arange(half, dtype=F32) * 2.0 / ROPE_DIM)
    ang = jnp.arange(t, dtype=F32)[:, None] * inv[None, :]
    cos, sin = jnp.cos(ang), jnp.sin(ang)
    rest = HEAD_DIM - ROPE_DIM
    c = jnp.concatenate([cos, cos, jnp.ones((t, rest), F32)], axis=1)
    sm = jnp.concatenate([-sin, jnp.zeros((t, half + rest), F32)], axis=1)
    sp = jnp.concatenate([jnp.zeros((t, half), F32), sin, jnp.zeros((t, rest), F32)], axis=1)
    return jnp.concatenate([jnp.tile(c, (1, 2)), jnp.tile(sm, (1, 2)), jnp.tile(sp, (1, 2))], axis=1)


def _rope(x, tab, sign):
    w = x.shape[1]
    rep = w // LANES
    c, sm, sp = tab[:, 0:LANES], tab[:, LANES:2 * LANES], tab[:, 2 * LANES:3 * LANES]
    if rep > 1:
        c, sm, sp = jnp.tile(c, (1, rep)), jnp.tile(sm, (1, rep)), jnp.tile(sp, (1, rep))
    half = ROPE_DIM // 2
    return x * c + sign * (pltpu.roll(x, w - half, 1) * sm + pltpu.roll(x, half, 1) * sp)


def _attn_specs():
    prev = lambda n: jnp.maximum(n - 1, 0)
    kblk, vblk = ZK // LANES, ZK // LANES + 1
    return [
        pl.BlockSpec((BLOCK, ATTN_WIDTH), lambda n: (n, 0)),
        pl.BlockSpec((BLOCK, KV_WIDTH), lambda n: (n, kblk)),
        pl.BlockSpec((BLOCK, KV_WIDTH), lambda n: (prev(n), kblk)),
        pl.BlockSpec((BLOCK, KV_WIDTH), lambda n: (n, vblk)),
        pl.BlockSpec((BLOCK, KV_WIDTH), lambda n: (prev(n), vblk)),
        pl.BlockSpec((BLOCK, 3 * LANES), lambda n: (n, 0)),
        pl.BlockSpec((BLOCK, 3 * LANES), lambda n: (prev(n), 0)),
        pl.BlockSpec(memory_space=pltpu.SMEM),
    ]


def _attn_prologue(n, zq_ref, zk_ref, zkp_ref, zv_ref, zvp_ref, tab_ref, tabp_ref):
    q = (_rope(zq_ref[...], tab_ref[...], 1.0) * (HEAD_DIM ** -0.5)).astype(BF16)
    kcat = jnp.concatenate(
        [_rope(zkp_ref[...], tabp_ref[...], 1.0), _rope(zk_ref[...], tab_ref[...], 1.0)], axis=0).astype(BF16)
    vcat = jnp.concatenate([zvp_ref[...], zv_ref[...]], axis=0).astype(BF16)
    qi = lax.broadcasted_iota(jnp.int32, (BLOCK, 2 * BLOCK), 0)
    kj = lax.broadcasted_iota(jnp.int32, (BLOCK, 2 * BLOCK), 1)
    valid = (kj <= qi + BLOCK) & (kj > qi) & ((n > 0) | (kj >= BLOCK))
    return q, kcat, vcat, valid


def _attn_probs(qh, kh, valid, sink):
    s = jnp.where(valid, _dot(qh, kh, "nt"), -1e30)
    mx = jnp.maximum(jnp.max(s, axis=1, keepdims=True), sink)
    p = jnp.exp(s - mx)
    p_sink = jnp.exp(sink - mx)
    inv = 1.0 / (jnp.sum(p, axis=1, keepdims=True) + p_sink)
    return p * inv, p_sink * inv


def attn_fwd(z, tab, sinks, comm=None):
    t = z.shape[0]

    def body(zq_ref, zk_ref, zkp_ref, zv_ref, zvp_ref, tab_ref, tabp_ref, sink_ref, o_ref):
        n = pl.program_id(0)
        q, kcat, vcat, valid = _attn_prologue(n, zq_ref, zk_ref, zkp_ref, zv_ref, zvp_ref, tab_ref, tabp_ref)
        outs = []
        for h in range(N_Q_HEADS):
            kv = slice((h // Q_PER_KV) * HEAD_DIM, (h // Q_PER_KV + 1) * HEAD_DIM)
            p, _ = _attn_probs(q[:, h * HEAD_DIM:(h + 1) * HEAD_DIM], kcat[:, kv], valid, sink_ref[0, h])
            outs.append(_dot(p.astype(BF16), vcat[:, kv]))
        o_ref[...] = jnp.concatenate(outs, axis=1).astype(BF16)

    return _call(
        "attn_fwd", body, grid=(t // BLOCK,), in_specs=_attn_specs(),
        out_specs=[pl.BlockSpec((BLOCK, ATTN_WIDTH), lambda n: (n, 0))],
        out_shape=[jax.ShapeDtypeStruct((t, ATTN_WIDTH), BF16)],
        args=[z, z, z, z, z, tab, tab, sinks], comm=comm,
    )[0]


def attn_bwd(z, tab, sinks, dcat, comm=None):
    t = z.shape[0]
    nb = t // BLOCK

    def body(zq_ref, zk_ref, zkp_ref, zv_ref, zvp_ref, tab_ref, tabp_ref, sink_ref, do_ref,
             dq_ref, dkv_ref, dsink_ref):
        n = pl.program_id(0)
        q, kcat, vcat, valid = _attn_prologue(n, zq_ref, zk_ref, zkp_ref, zv_ref, zvp_ref, tab_ref, tabp_ref)
        do = do_ref[...]
        lane = lax.broadcasted_iota(jnp.int32, (1, LANES), 1)
        dqs, dks, dvs = [], [], []
        dsink = jnp.zeros((1, LANES), F32)
        for hk in range(N_Q_HEADS // Q_PER_KV):
            kv = slice(hk * HEAD_DIM, (hk + 1) * HEAD_DIM)
            kh, vh = kcat[:, kv], vcat[:, kv]
            dk_t = jnp.zeros((HEAD_DIM, 2 * BLOCK), F32)
            dv_t = jnp.zeros((HEAD_DIM, 2 * BLOCK), F32)
            for g in range(Q_PER_KV):
                h = hk * Q_PER_KV + g
                hs = slice(h * HEAD_DIM, (h + 1) * HEAD_DIM)
                qh, doh = q[:, hs], do[:, hs]
                p, p_sink = _attn_probs(qh, kh, valid, sink_ref[0, h])
                dv_t = dv_t + _dot(doh, p.astype(BF16), "tn")
                dp = _dot(doh, vh, "nt")
                rd = jnp.sum(p * dp, axis=1, keepdims=True)
                ds = (p * (dp - rd) * (HEAD_DIM ** -0.5)).astype(BF16)
                dqs.append(_dot(ds, kh))
                dk_t = dk_t + _dot(qh, ds, "tn")
                dsink = dsink + jnp.where(lane == h, -jnp.sum(p_sink * rd, axis=0, keepdims=True), 0.0)
            dks.append(dk_t)
            dvs.append(dv_t)
        dq_ref[...] = _rope(jnp.concatenate(dqs, axis=1), tab_ref[...], -1.0).astype(BF16)
        dkc = jnp.concatenate(dks, axis=0).T * (HEAD_DIM ** 0.5)
        dk_pre = jnp.concatenate(
            [_rope(dkc[:BLOCK], tabp_ref[...], -1.0), _rope(dkc[BLOCK:], tab_ref[...], -1.0)], axis=0)
        dkv_ref[...] = jnp.concatenate([dk_pre, jnp.concatenate(dvs, axis=0).T], axis=1)

        @pl.when(n == 0)
        def _():
            dsink_ref[...] = jnp.zeros_like(dsink_ref)

        dsink_ref[...] += dsink

    return _call(
        "attn_bwd", body, grid=(nb,),
        in_specs=_attn_specs() + [pl.BlockSpec((BLOCK, ATTN_WIDTH), lambda n: (n, 0))],
        out_specs=[pl.BlockSpec((BLOCK, ATTN_WIDTH), lambda n: (n, 0)),
                   pl.BlockSpec((None, 2 * BLOCK, 2 * KV_WIDTH), lambda n: (n, 0, 0)),
                   pl.BlockSpec((1, LANES), lambda n: (0, 0))],
        out_shape=[jax.ShapeDtypeStruct((t, ATTN_WIDTH), BF16),
                   jax.ShapeDtypeStruct((nb, 2 * BLOCK, 2 * KV_WIDTH), F32),
                   jax.ShapeDtypeStruct((1, LANES), F32)],
        args=[z, z, z, z, z, tab, tab, sinks, dcat], comm=comm,
    )


def _gelu(x):
    k = math.sqrt(2.0 / math.pi)
    th = jnp.tanh(k * (x + 0.044715 * (x * x * x)))
    return 0.5 * x * (1.0 + th), th


def _gelu_grad(x, th):
    k = math.sqrt(2.0 / math.pi)
    return 0.5 * (1.0 + th) + 0.5 * x * (1.0 - th * th) * (k * (1.0 + 3.0 * 0.044715 * (x * x)))


def _sgu_core(zu_ref, zv_ref, lng_ref, lnb_ref, w_ref, bt_ref):
    up, vp = zu_ref[...], zv_ref[...]
    u, thu = _gelu(up)
    v, thv = _gelu(vp)
    mu = jnp.mean(v, axis=-1, keepdims=True)
    vc = v - mu
    rstd = lax.rsqrt(jnp.mean(vc * vc, axis=-1, keepdims=True) + RMS_EPS)
    xhat = vc * rstd
    vn = (xhat * lng_ref[...] + lnb_ref[...]).astype(BF16)
    row = lax.broadcasted_iota(jnp.int32, (CHUNK, CHUNK), 0)
    col = lax.broadcasted_iota(jnp.int32, (CHUNK, CHUNK), 1)
    mixed = []
    for g in range(SGU_GROUPS):
        wc = jnp.where(row >= col, w_ref[g], 0.0).astype(BF16)
        mixed.append(_dot(wc, vn[:, g * CHUNK:(g + 1) * CHUNK]) + bt_ref[:, g:g + 1])
    return up, vp, u, thu, thv, rstd, xhat, vn, jnp.concatenate(mixed, axis=1)


def _sgu_specs():
    full = lambda shape: pl.BlockSpec(shape, lambda n: (0,) * len(shape))
    return [
        pl.BlockSpec((CHUNK, SGU_WIDTH), lambda n: (n, ZU // SGU_WIDTH)),
        pl.BlockSpec((CHUNK, SGU_WIDTH), lambda n: (n, ZV // SGU_WIDTH)),
        full((1, SGU_WIDTH)), full((1, SGU_WIDTH)),
        full((SGU_GROUPS, CHUNK, CHUNK)), full((CHUNK, SGU_GROUPS)),
    ]


def sgu_fwd(z, ln_g, ln_b, w, b_t, comm=None):
    t = z.shape[0]

    def body(zu_ref, zv_ref, lng_ref, lnb_ref, w_ref, bt_ref, o_ref):
        _, _, u, _, _, _, _, _, mixed = _sgu_core(zu_ref, zv_ref, lng_ref, lnb_ref, w_ref, bt_ref)
        o_ref[...] = (u * mixed).astype(BF16)

    return _call(
        "sgu_fwd", body, grid=(t // CHUNK,), in_specs=_sgu_specs(),
        out_specs=[pl.BlockSpec((CHUNK, SGU_WIDTH), lambda n: (n, 0))],
        out_shape=[jax.ShapeDtypeStruct((t, SGU_WIDTH), BF16)], args=[z, z, ln_g, ln_b, w, b_t], comm=comm,
    )[0]


def sgu_bwd(z, ln_g, ln_b, w, w_t, b_t, dcat):
    t = z.shape[0]

    def body(zu_ref, zv_ref, lng_ref, lnb_ref, w_ref, bt_ref, wt_ref, dg_ref,
             du_ref, dv_ref, dw_ref, dbt_ref, dlng_ref, dlnb_ref):
        up, vp, u, thu, thv, rstd, xhat, vn, mixed = _sgu_core(zu_ref, zv_ref, lng_ref, lnb_ref, w_ref, bt_ref)
        dgate = dg_ref[...].astype(F32)
        du_ref[...] = (dgate * mixed * _gelu_grad(up, thu)).astype(BF16)
        dmixed = dgate * u
        row = lax.broadcasted_iota(jnp.int32, (CHUNK, CHUNK), 0)
        col = lax.broadcasted_iota(jnp.int32, (CHUNK, CHUNK), 1)

        @pl.when(pl.program_id(0) == 0)
        def _():
            dw_ref[...] = jnp.zeros_like(dw_ref)
            dbt_ref[...] = jnp.zeros_like(dbt_ref)
            dlng_ref[...] = jnp.zeros_like(dlng_ref)
            dlnb_ref[...] = jnp.zeros_like(dlnb_ref)

        dvn, dbt = [], jnp.zeros((CHUNK, LANES), F32)
        for g in range(SGU_GROUPS):
            gs = slice(g * CHUNK, (g + 1) * CHUNK)
            dmx = dmixed[:, gs]
            dmxb = dmx.astype(BF16)
            dbt = dbt + jnp.where(col == g, jnp.sum(dmx, axis=1, keepdims=True), 0.0)
            dw_ref[g] += jnp.where(row >= col, _dot(dmxb, vn[:, gs], "nt"), 0.0)
            wtc = jnp.where(col >= row, wt_ref[g], 0.0).astype(BF16)
            dvn.append(_dot(wtc, dmxb))
        dbt_ref[...] += dbt
        dvn = jnp.concatenate(dvn, axis=1)
        dlnb_ref[...] += jnp.sum(dvn, axis=0, keepdims=True)
        dlng_ref[...] += jnp.sum(dvn * xhat, axis=0, keepdims=True)
        dxh = dvn * lng_ref[...]
        dv = rstd * (dxh - jnp.mean(dxh, axis=-1, keepdims=True) - xhat * jnp.mean(dxh * xhat, axis=-1, keepdims=True))
        dv_ref[...] = (dv * _gelu_grad(vp, thv)).astype(BF16)

    full = lambda shape: pl.BlockSpec(shape, lambda n: (0,) * len(shape))
    act = pl.BlockSpec((CHUNK, SGU_WIDTH), lambda n: (n, 0))
    act_shape = jax.ShapeDtypeStruct((t, SGU_WIDTH), BF16)
    vec = jax.ShapeDtypeStruct((1, SGU_WIDTH), F32)
    return pl.pallas_call(
        body,
        out_shape=(act_shape, act_shape, jax.ShapeDtypeStruct((SGU_GROUPS, CHUNK, CHUNK), F32),
                   jax.ShapeDtypeStruct((CHUNK, LANES), F32), vec, vec),
        grid=(t // CHUNK,),
        in_specs=_sgu_specs() + [full((SGU_GROUPS, CHUNK, CHUNK)),
                                 pl.BlockSpec((CHUNK, SGU_WIDTH), lambda n: (n, 1))],
        out_specs=(act, act, full((SGU_GROUPS, CHUNK, CHUNK)), full((CHUNK, LANES)),
                   full((1, SGU_WIDTH)), full((1, SGU_WIDTH))),
        compiler_params=_cp(1), name="sgu_bwd",
    )(z, z, ln_g, ln_b, w, b_t, w_t, dcat)


def dz_assemble(dq, dkv, du, dv):
    t = dq.shape[0]
    nb = t // BLOCK

    def body(dq_ref, cur_ref, nxt_ref, du_ref, dv_ref, o_ref):
        n = pl.program_id(0)
        o_ref[:, ZQ:ZQ + ATTN_WIDTH] = dq_ref[...]
        o_ref[:, ZU:ZU + SGU_WIDTH] = du_ref[...]
        o_ref[:, ZV:ZV + SGU_WIDTH] = dv_ref[...]
        kv = cur_ref[BLOCK:, :] + jnp.where(n < nb - 1, nxt_ref[:BLOCK, :], 0.0)
        o_ref[:, ZK:ZK + 2 * KV_WIDTH] = kv.astype(BF16)

    act = pl.BlockSpec((BLOCK, ATTN_WIDTH), lambda n: (n, 0))
    return pl.pallas_call(
        body, out_shape=jax.ShapeDtypeStruct((t, IN_WIDTH), BF16), grid=(nb,),
        in_specs=[act,
                  pl.BlockSpec((None, 2 * BLOCK, 2 * KV_WIDTH), lambda n: (n, 0, 0)),
                  pl.BlockSpec((None, 2 * BLOCK, 2 * KV_WIDTH), lambda n: (jnp.minimum(n + 1, nb - 1), 0, 0)),
                  act, act],
        out_specs=pl.BlockSpec((BLOCK, IN_WIDTH), lambda n: (n, 0)),
        compiler_params=_cp(1), name="dz_assemble",
    )(dq, dkv, dkv, du, dv)


def _pool_count(i, tp, w):
    t_idx = i * tp + lax.broadcasted_iota(jnp.int32, (tp, 1), 0)
    return jnp.minimum(t_idx + 1, w).astype(F32)


def pool_fwd(h, pw, pscale, comm=None):
    t, d = h.shape
    tp = _tile(t, 256)
    per = tp // POOL_HALO

    def body(h_ref, halo_ref, pw_ref, ps_ref, m_ref, pooled_ref):
        i = pl.program_id(0)
        cur = h_ref[...]
        ext = jnp.concatenate([jnp.where(i > 0, halo_ref[...], 0.0), cur], axis=0)
        ys, pooled = [], []
        for gi, w in enumerate(POOL_WINDOWS):
            gs = slice(gi * POOL_GROUP_DIM, (gi + 1) * POOL_GROUP_DIM)
            s = ext[:, gs]
            sh = 1
            while sh < w:
                s = s + pltpu.roll(s, sh, 0)
                sh *= 2
            pg = (s[POOL_HALO:, :] / _pool_count(i, tp, w) - cur[:, gs]).astype(BF16)
            pooled.append(pg)
            ys.append(_dot(pg, pw_ref[gi]))
        pooled_ref[...] = jnp.concatenate(pooled, axis=1)
        m_ref[...] = jnp.concatenate(ys, axis=1) * ps_ref[...]

    row = pl.BlockSpec((tp, d), lambda i: (i, 0))
    return _call(
        "pool_fwd", body, grid=(t // tp,),
        in_specs=[row, pl.BlockSpec((POOL_HALO, d), lambda i: (jnp.maximum(i * per - 1, 0), 0)),
                  pl.BlockSpec(pw.shape, lambda i: (0, 0, 0)), pl.BlockSpec((1, d), lambda i: (0, 0))],
        out_specs=[row, row], out_shape=[jax.ShapeDtypeStruct((t, d), F32), jax.ShapeDtypeStruct((t, d), BF16)],
        args=[h, h, pw, pscale], comm=comm,
    )


def pool_bwd_proj(dm, pooled, pw, pscale):
    t, d = dm.shape
    tp = _tile(t, 256)

    def body(dm_ref, pooled_ref, pw_ref, ps_ref, dp_ref, dy_ref, dps_ref):
        dmv = dm_ref[...]
        dy = (dmv * ps_ref[...]).astype(BF16)
        dy_ref[...] = dy
        ys, dps = [], []
        for gi in range(len(POOL_WINDOWS)):
            gs = slice(gi * POOL_GROUP_DIM, (gi + 1) * POOL_GROUP_DIM)
            ys.append(_dot(pooled_ref[:, gs], pw_ref[gi]))
            dps.append(_dot(dy[:, gs], pw_ref[gi], "nt"))
        dp_ref[...] = jnp.concatenate(dps, axis=1)

        @pl.when(pl.program_id(0) == 0)
        def _():
            dps_ref[...] = jnp.zeros_like(dps_ref)

        dps_ref[...] += jnp.sum(dmv * jnp.concatenate(ys, axis=1), axis=0, keepdims=True)

    row = pl.BlockSpec((tp, d), lambda i: (i, 0))
    vec = pl.BlockSpec((1, d), lambda i: (0, 0))
    return pl.pallas_call(
        body,
        out_shape=(jax.ShapeDtypeStruct((t, d), F32), jax.ShapeDtypeStruct((t, d), BF16),
                   jax.ShapeDtypeStruct((1, d), F32)),
        grid=(t // tp,),
        in_specs=[row, row, pl.BlockSpec(pw.shape, lambda i: (0, 0, 0)), vec],
        out_specs=(row, row, vec), compiler_params=_cp(1), name="pool_bwd_proj",
    )(dm, pooled, pw, pscale)


def pool_bwd_window(dp):
    t, d = dp.shape
    tp = _tile(t, 256)
    per = tp // POOL_HALO
    last = t // POOL_HALO - 1
    nt = t // tp

    def body(dp_ref, halo_ref, dh_ref):
        i = pl.program_id(0)
        cur = dp_ref[...]
        halo = jnp.where(i < nt - 1, halo_ref[...], 0.0)
        outs = []
        for gi, w in enumerate(POOL_WINDOWS):
            gs = slice(gi * POOL_GROUP_DIM, (gi + 1) * POOL_GROUP_DIM)
            s = jnp.concatenate([cur[:, gs] / _pool_count(i, tp, w), halo[:, gs] / float(w)], axis=0)
            sh = 1
            while sh < w:
                s = s + pltpu.roll(s, tp + POOL_HALO - sh, 0)
                sh *= 2
            outs.append(s[:tp, :] - cur[:, gs])
        dh_ref[...] = jnp.concatenate(outs, axis=1)

    row = pl.BlockSpec((tp, d), lambda i: (i, 0))
    return pl.pallas_call(
        body, out_shape=jax.ShapeDtypeStruct((t, d), F32), grid=(nt,),
        in_specs=[row, pl.BlockSpec((POOL_HALO, d), lambda i: (jnp.minimum((i + 1) * per, last), 0))],
        out_specs=row, compiler_params=_cp(1), name="pool_bwd_window",
    )(dp, dp)


def pool_wgrad(pooled, dy):
    t, d = pooled.shape
    ng = d // POOL_GROUP_DIM
    tk = _tile(t, 512)
    blk = pl.BlockSpec((tk, POOL_GROUP_DIM), lambda g, k: (k, g))
    return _mm(
        "pool_wgrad", pooled, dy, dims="tn", grid=(ng, t // tk), a_spec=blk, b_spec=blk,
        o_spec=pl.BlockSpec((None, POOL_GROUP_DIM, POOL_GROUP_DIM), lambda g, k: (g, 0, 0)),
        out_shape=jax.ShapeDtypeStruct((ng, POOL_GROUP_DIM, POOL_GROUP_DIM), F32),
        acc_shape=(POOL_GROUP_DIM, POOL_GROUP_DIM), nk=t // tk,
    )


def _xattn_probs(qh, kh):
    s = _dot(qh, kh, "nt") * (X_HEAD_DIM ** -0.5)
    p = jnp.exp(s - jnp.max(s, axis=1, keepdims=True))
    return p * (1.0 / jnp.sum(p, axis=1, keepdims=True))


def xattn_fwd(name, q, k, v):
    t, xw = q.shape
    tm = _tile(t, 512)

    def body(q_ref, k_ref, v_ref, o_ref):
        outs = []
        for h in range(X_HEADS):
            hs = slice(h * X_HEAD_DIM, (h + 1) * X_HEAD_DIM)
            p = _xattn_probs(q_ref[:, hs], k_ref[:, hs])
            outs.append(_dot(p.astype(BF16), v_ref[:, hs]))
        o_ref[...] = jnp.concatenate(outs, axis=1).astype(BF16)

    row = pl.BlockSpec((tm, xw), lambda i: (i, 0))
    kv = pl.BlockSpec(k.shape, lambda i: (0, 0))
    return pl.pallas_call(
        body, out_shape=jax.ShapeDtypeStruct((t, xw), BF16), grid=(t // tm,),
        in_specs=[row, kv, kv], out_specs=row, compiler_params=_cp(1), name=name,
    )(q, k, v)


def xattn_bwd(name, q, k, v, do):
    t, xw = q.shape
    tm = _tile(t, 512)

    def body(q_ref, k_ref, v_ref, do_ref, dq_ref, dk_ref, dv_ref):
        @pl.when(pl.program_id(0) == 0)
        def _():
            dk_ref[...] = jnp.zeros_like(dk_ref)
            dv_ref[...] = jnp.zeros_like(dv_ref)

        dqs = []
        for h in range(X_HEADS):
            hs = slice(h * X_HEAD_DIM, (h + 1) * X_HEAD_DIM)
            qh, kh, vh, doh = q_ref[:, hs], k_ref[:, hs], v_ref[:, hs], do_ref[:, hs]
            p = _xattn_probs(qh, kh)
            dv_ref[:, hs] += _dot(p.astype(BF16), doh, "tn")
            dp = _dot(doh, vh, "nt")
            ds = (p * (dp - jnp.sum(p * dp, axis=1, keepdims=True)) * (X_HEAD_DIM ** -0.5)).astype(BF16)
            dqs.append(_dot(ds, kh))
            dk_ref[:, hs] += _dot(ds, qh, "tn")
        dq_ref[...] = jnp.concatenate(dqs, axis=1).astype(BF16)

    row = pl.BlockSpec((tm, xw), lambda i: (i, 0))
    kv = pl.BlockSpec(k.shape, lambda i: (0, 0))
    kv_shape = jax.ShapeDtypeStruct(k.shape, F32)
    return pl.pallas_call(
        body, out_shape=(jax.ShapeDtypeStruct((t, xw), BF16), kv_shape, kv_shape), grid=(t // tm,),
        in_specs=[row, kv, kv, row], out_specs=(row, kv, kv), compiler_params=_cp(1), name=name,
    )(q, k, v, do)


def kernel(x, mem, norms, mem_norm, ffn1_wg, ffn1_wu, ffn1_wd, ffn2_wg, ffn2_wu, ffn2_wd, x_wq, x_wk, x_wv, x_wo, mix_w_in, mix_w_out, attn_sinks, sgu_ln_g, sgu_ln_b, sgu_w, sgu_b, pool_w, pool_scale, loss_target, m_norms, m_mem_norm, m_ffn1_wg, m_ffn1_wu, m_ffn1_wd, m_ffn2_wg, m_ffn2_wu, m_ffn2_wd, m_x_wq, m_x_wk, m_x_wv, m_x_wo, m_mix_w_in, m_mix_w_out, m_attn_sinks, m_sgu_ln_g, m_sgu_ln_b, m_sgu_w, m_sgu_b, m_pool_w, m_pool_scale, v_norms, v_mem_norm, v_ffn1_wg, v_ffn1_wu, v_ffn1_wd, v_ffn2_wg, v_ffn2_wu, v_ffn2_wd, v_x_wq, v_x_wk, v_x_wv, v_x_wo, v_mix_w_in, v_mix_w_out, v_attn_sinks, v_sgu_ln_g, v_sgu_ln_b, v_sgu_w, v_sgu_b, v_pool_w, v_pool_scale):
    params = dict(norms=norms, mem_norm=mem_norm, ffn1_wg=ffn1_wg, ffn1_wu=ffn1_wu, ffn1_wd=ffn1_wd,
                  ffn2_wg=ffn2_wg, ffn2_wu=ffn2_wu, ffn2_wd=ffn2_wd, x_wq=x_wq, x_wk=x_wk, x_wv=x_wv, x_wo=x_wo,
                  mix_w_in=mix_w_in, mix_w_out=mix_w_out, attn_sinks=attn_sinks, sgu_ln_g=sgu_ln_g,
                  sgu_ln_b=sgu_ln_b, sgu_w=sgu_w, sgu_b=sgu_b, pool_w=pool_w, pool_scale=pool_scale)
    mom1 = dict(norms=m_norms, mem_norm=m_mem_norm, ffn1_wg=m_ffn1_wg, ffn1_wu=m_ffn1_wu, ffn1_wd=m_ffn1_wd,
                ffn2_wg=m_ffn2_wg, ffn2_wu=m_ffn2_wu, ffn2_wd=m_ffn2_wd, x_wq=m_x_wq, x_wk=m_x_wk, x_wv=m_x_wv,
                x_wo=m_x_wo, mix_w_in=m_mix_w_in, mix_w_out=m_mix_w_out, attn_sinks=m_attn_sinks,
                sgu_ln_g=m_sgu_ln_g, sgu_ln_b=m_sgu_ln_b, sgu_w=m_sgu_w, sgu_b=m_sgu_b, pool_w=m_pool_w,
                pool_scale=m_pool_scale)
    mom2 = dict(norms=v_norms, mem_norm=v_mem_norm, ffn1_wg=v_ffn1_wg, ffn1_wu=v_ffn1_wu, ffn1_wd=v_ffn1_wd,
                ffn2_wg=v_ffn2_wg, ffn2_wu=v_ffn2_wu, ffn2_wd=v_ffn2_wd, x_wq=v_x_wq, x_wk=v_x_wk, x_wv=v_x_wv,
                x_wo=v_x_wo, mix_w_in=v_mix_w_in, mix_w_out=v_mix_w_out, attn_sinks=v_attn_sinks,
                sgu_ln_g=v_sgu_ln_g, sgu_ln_b=v_sgu_ln_b, sgu_w=v_sgu_w, sgu_b=v_sgu_b, pool_w=v_pool_w,
                pool_scale=v_pool_scale)
    order = list(params)

    xs, memb, target = x[0], mem[0], loss_target[0]
    t, d = xs.shape
    depth = norms.shape[0]
    dsh = d // NDEV

    bf = lambda a: a.astype(BF16)
    wts = {}

    def gather_job(keys):
        return _Gather([bf(params[name][l]) for name, l in keys]), keys

    def land(job_keys):
        job, keys = job_keys
        for key, a in zip(keys, job.result):
            wts[key] = a

    ffn_keys = lambda tag, l: [(f"{tag}_wg", l), (f"{tag}_wu", l), (f"{tag}_wd", l)]
    x_keys = lambda l: [("x_wq", l), ("x_wk", l), ("x_wv", l), ("x_wo", l)]
    small_shapes = [norms.shape, pool_scale.shape, pool_w.shape]
    head = gather_job(ffn_keys("ffn1", 0))
    head_small = _Gather([_pack([norms, pool_scale, pool_w])])
    _run_exchange("gather_head", _Multi([head_small, head[0]]))
    land(head)
    norms_sh, pscale_sh, pw_sh = _unpack(head_small.result[0], small_shapes, (NDEV,))
    norms_full = norms_sh.transpose(1, 2, 0, 3).reshape(depth, norms.shape[1], d)
    pscale_full = pscale_sh.transpose(1, 0, 2).reshape(1, d)
    pw_full = pw_sh[:, 0].transpose(1, 0, 2, 3).reshape(len(POOL_WINDOWS), POOL_GROUP_DIM, POOL_GROUP_DIM).astype(BF16)
    fwd_jobs = {
        ("ffn1", 0): gather_job([("mix_w_in", 0)] + ffn_keys("ffn2", 0)[:2]),
        ("mix_in", 0): gather_job([("mix_w_out", 0)]),
        ("attn", 0): gather_job(ffn_keys("ffn2", 0)[2:]),
        ("sgu", 0): gather_job(x_keys(0)[:3]),
        ("mix_out", 0): gather_job(x_keys(0)[3:]),
        ("ffn2", 0): gather_job(ffn_keys("ffn1", 1)),
        ("resid_ffn1", 1): gather_job(x_keys(1)[:3]),
        ("pool", 1): gather_job(x_keys(1)[3:]),
        ("ffn1", 1): gather_job(ffn_keys("ffn2", 1)),
    }

    def riding(key, fn, *args, **kw):
        job = fwd_jobs.pop(key, None)
        res = fn(*args, comm=job and job[0], **kw)
        if job:
            land(job)
        return res

    tab = rope_table(t)
    sgu_w0 = sgu_w[0]
    sgu_wt0 = sgu_w0.transpose(0, 2, 1)
    sgu_bt0 = sgu_b[0].T
    gain = lambda l, i: norms_full[l, i][None, :]

    saved = []
    xc = xs
    h = norm_fwd("norm_ffn1_0", xc, gain(0, 0), BF16)
    for l in range(depth):
        s = {}
        pooling = l % 2 == 1

        def ffn_forward(tag, xc, h, gi, nxt, l=l, s=s):
            s[tag + "_h"] = h
            s[tag + "_a"], s[tag + "_b"], s[tag + "_hidt"], s[tag + "_m"] = riding(
                (tag, l), ffn_fwd, f"{tag}_fwd_{l}", h, wts[tag + "_wg", l], wts[tag + "_wu", l], wts[tag + "_wd", l])
            return riding(("resid_" + tag, l), resid_norm_fwd, f"resid_{tag}_{l}", xc, s[tag + "_m"],
                          gain(l, gi + 1), 0.5, nxt)

        s["x0"] = xc
        if pooling:
            xc, h2 = ffn_forward("ffn1", xc, h, 0, (gain(l, 2), F32, False))
        else:
            xc, h2, s["h2t"] = ffn_forward("ffn1", xc, h, 0, (gain(l, 2), BF16, True))

        s["x1"] = xc
        if not pooling:
            w_in = wts["mix_w_in", l].transpose(1, 0, 2).reshape(d, IN_WIDTH)
            o_k, o_u = ATTN_WIDTH, ATTN_WIDTH + 2 * KV_WIDTH
            w_in = jnp.concatenate([w_in[:, :o_k], w_in[:, o_u:], w_in[:, o_k:o_u]], axis=1)
            s["z"] = riding(("mix_in", l), mm_nn, "mix_in", h2, w_in, F32, tn=IN_WIDTH // 2)
            w_out = wts["mix_w_out", l].reshape(d, d)
            attn = riding(("attn", l), attn_fwd, s["z"], tab, attn_sinks)
            gate = riding(("sgu", l), sgu_fwd, s["z"], sgu_ln_g, sgu_ln_b, sgu_w0, sgu_bt0)
            s["cat"] = jnp.concatenate([attn, gate], axis=1)
            s["m2"] = riding(("mix_out", l), mm_nn, "mix_out", s["cat"], w_out, F32)
        else:
            s["m2"], s["pooled"] = riding(("pool", l), pool_fwd, h2, pw_full, pscale_full)
        xc, h3, s["h3t"] = resid_norm_fwd(f"resid_mix_{l}", xc, s["m2"], gain(l, 3), 1.0, (gain(l, 4), BF16, True))

        s["x2"] = xc
        wq, wk, wv = (wts[k, l].reshape(d, -1) for k in ("x_wq", "x_wk", "x_wv"))
        s["wq"], s["wkv"] = wq, jnp.concatenate([wk, wv], axis=1)
        s["mem_n"] = norm_fwd(f"norm_mem_{l}", memb, mem_norm[l][None, :], BF16)
        s["q"] = mm_nn(f"x_q_{l}", h3, wq, BF16)
        s["k"] = mm_nn(f"x_k_{l}", s["mem_n"], wk, BF16)
        s["v"] = mm_nn(f"x_v_{l}", s["mem_n"], wv, BF16)
        s["o"] = xattn_fwd(f"xattn_fwd_{l}", s["q"], s["k"], s["v"])
        s["wo"] = wts["x_wo", l].transpose(1, 0, 2).reshape(-1, d)
        s["m3"] = mm_nn(f"x_o_{l}", s["o"], s["wo"], F32)
        xc, h4 = resid_norm_fwd(f"resid_x_{l}", xc, s["m3"], gain(l, 5), 1.0, (gain(l, 6), BF16, False))

        s["x3"] = xc
        if l + 1 < depth:
            xc, h = ffn_forward("ffn2", xc, h4, 6, (gain(l + 1, 0), BF16, False))
        else:
            xc = ffn_forward("ffn2", xc, h4, 6, None)
        saved.append(s)
    assert not fwd_jobs, list(fwd_jobs)

    dx, loss11 = loss_grad(xc, target)
    loss = lax.psum(loss11[0, 0], ("x", "y", "c"))

    swaps = []
    pending = []
    recv = {}

    def emit_units(name, l, arr, among_chips):
        piece_mib = math.prod(arr.shape[1:]) * arr.dtype.itemsize / MIB
        parts = 2 if piece_mib > 1.5 else 1
        rows = arr.shape[1] // parts
        cost = (LINK_US_PER_MIB_CHIPS if among_chips else LINK_US_PER_MIB_ALL) * piece_mib / parts
        for part in range(parts):
            pending.append(((name, l, part), among_chips, (arr, part * rows, rows), cost))

    def emit(name, l, arr, two_level=False):
        if two_level:
            swaps.append((name, l, arr.reshape((NCHIP, 2) + arr.shape[1:])))
        else:
            emit_units(name, l, arr, False)

    def hosted(budget_us, fn, *args, extra=(), force=True, **kw):
        jobs = list(extra)
        swapped = swaps[:]
        del swaps[:]
        used = PAIR_SWAP_US * len(swapped)
        if swapped:
            jobs.append(_PairSwap([g for _, _, g in swapped]))
        items, kept = [], []
        for it in pending:
            if (force and not items and not swapped) or used + it[3] <= budget_us:
                items.append(it)
                used += it[3]
            else:
                kept.append(it)
        pending[:] = kept
        groups = [[it for it in items if it[1] == flag] for flag in (False, True)]
        unit_jobs = [cls([it[2] for it in grp]) if grp else None
                     for cls, grp in zip((_Scatter, _ChipScatter), groups)]
        jobs += [j for j in unit_jobs if j is not None]
        res = fn(*args, comm=_Multi(jobs) if jobs else None, **kw)
        for job, grp in zip(unit_jobs, groups):
            for it, a in zip(grp, job.result if job else ()):
                recv[it[0]] = a
        if swapped:
            for (name, l, g4), theirs in zip(swapped, jobs[len(extra)].result):
                emit_units(name, l, pair_sum(f"pairsum_{name}_{l}", g4, theirs), True)
        return res

    grads = {"mem_norm": [None] * depth}
    dgs = [[None] * 8 for _ in range(depth)]
    small_jobs = []
    dm, dgs[depth - 1][7] = norm_bwd(f"ffn2_post_bwd_{depth - 1}", saved[-1]["ffn2_m"], gain(depth - 1, 7), dx, 0.5,
                                     None, BF16)
    for l in reversed(range(depth)):
        s = saved[l]
        dg = dgs[l]
        pooling = l % 2 == 1

        def ffn_block(tag, dm, extra=(), l=l, s=s):
            emit(tag + "_wd", l, hosted(HOST_US_WGRAD, ffn_wgrad, f"{tag}_dwd_{l}", s[tag + "_hidt"], dm), True)
            da_t, db_t, dh = hosted(HOST_US_FFN_BWD, ffn_bwd, f"{tag}_bwd_{l}", dm, s[tag + "_a"], s[tag + "_b"],
                                    wts[tag + "_wg", l], wts[tag + "_wu", l], wts[tag + "_wd", l], extra=extra)
            emit(tag + "_wg", l, hosted(HOST_US_WGRAD, ffn_wgrad, f"{tag}_dwg_{l}", da_t, s[tag + "_h"]), True)
            emit(tag + "_wu", l, hosted(HOST_US_WGRAD, ffn_wgrad, f"{tag}_dwu_{l}", db_t, s[tag + "_h"]), True)
            return dh

        dh = ffn_block("ffn2", dm)
        dx, dg[6], dm, dg[5] = hosted(
            HOST_US_SMALL, norm_bwd, f"ffn2_pre_bwd_{l}", s["x3"], gain(l, 6), dh, 1.0, dx, F32,
            post=(s["m3"], gain(l, 5), 1.0, BF16), force=False)

        do = mm_nt(f"x_do_{l}", dm, s["wo"], BF16)
        g_wo = mm_tn(f"x_dwo_{l}", s["o"], dm, BF16, tmo=s["o"].shape[1])
        emit("x_wo", l, g_wo.reshape(-1, NDEV, dsh).transpose(1, 0, 2))
        dq, dk, dv = xattn_bwd(f"xattn_bwd_{l}", s["q"], s["k"], s["v"], do)
        dkb, dvb = dk.astype(BF16), dv.astype(BF16)
        emit("x_wq", l, mm_kred(f"x_dwq_{l}", s["h3t"], dq, BF16).reshape(NDEV, dsh, -1))
        emit("x_wk", l, mm_tn(f"x_dwk_{l}", s["mem_n"], dkb, BF16).reshape(NDEV, dsh, -1))
        emit("x_wv", l, mm_tn(f"x_dwv_{l}", s["mem_n"], dvb, BF16).reshape(NDEV, dsh, -1))
        dh = mm_nt(f"x_dh_{l}", dq, s["wq"], F32)
        dmem_n = mm_nt(f"x_dmem_{l}", jnp.concatenate([dkb, dvb], axis=1), s["wkv"], F32)
        _, grads["mem_norm"][l] = norm_bwd(f"mem_norm_bwd_{l}", memb, mem_norm[l][None, :], dmem_n, 1.0, None, F32)
        dx, dg[4], dm, dg[3] = norm_bwd(f"x_pre_bwd_{l}", s["x2"], gain(l, 4), dh, 1.0, dx, F32,
                                        post=(s["m2"], gain(l, 3), 1.0, F32 if pooling else BF16))

        if not pooling:
            dcat = mm_nt("mix_dcat", dm, w_out, BF16)
            emit("mix_w_out", l, mm_tn("mix_dwout", s["cat"], dm, BF16, tno=d // 2).reshape(NDEV, dsh, d))
            dq_a, dkv_a, dsink = hosted(HOST_US_ATTN_BWD, attn_bwd, s["z"], tab, attn_sinks, dcat, force=False)
            du_s, dv_s, g_sgu_w, g_sgu_bt, g_ln_g, g_ln_b = sgu_bwd(
                s["z"], sgu_ln_g, sgu_ln_b, sgu_w0, sgu_wt0, sgu_bt0, dcat)
            dz = dz_assemble(dq_a, dkv_a, du_s, dv_s)
            dh = mm_nt("mix_dh", dz, w_in, F32, tn=d // 2)
            g_win = hosted(2 * HOST_US_SMALL, mm_fullk, "mix_dwin", s["h2t"], dz, BF16, 2 * LANES, force=False)
            g_win = jnp.concatenate([g_win[:, :ATTN_WIDTH], g_win[:, ZK:], g_win[:, ZU:ZK]], axis=1)
            emit("mix_w_in", l, g_win.reshape(d, NDEV, -1).transpose(1, 0, 2))
        else:
            dp, dy, g_pscale = pool_bwd_proj(dm, s["pooled"], pw_full, pscale_full)
            g_pw = pool_wgrad(s["pooled"], dy)
            emit("pool_w", 0, g_pw.reshape(len(POOL_WINDOWS), NDEV, -1, POOL_GROUP_DIM).transpose(1, 0, 2, 3)
                 .reshape(NDEV, -1, POOL_GROUP_DIM))
            dh = pool_bwd_window(dp)
        dx, dg[2], dm, dg[1] = norm_bwd(f"mix_pre_bwd_{l}", s["x1"], gain(l, 2), dh, 1.0, dx, F32,
                                        post=(s["ffn1_m"], gain(l, 1), 0.5, BF16))

        if l == 0:
            replicated = ["mem_norm", "attn_sinks", "sgu_ln_g", "sgu_ln_b", "sgu_w", "sgu_b"]
            rep_grads = [jnp.concatenate(grads["mem_norm"], axis=0), dsink[:, :N_Q_HEADS], g_ln_g, g_ln_b,
                         g_sgu_w[None], g_sgu_bt[:, :SGU_GROUPS].T[None]]
            small_jobs.append(_Gather([_pack(rep_grads)]))
        dh = ffn_block("ffn1", dm, extra=small_jobs if l == 0 else ())
        if l > 0:
            dx, dg[0], dm, dgs[l - 1][7] = hosted(
                HOST_US_SMALL, norm_bwd, f"ffn1_pre_bwd_{l}", s["x0"], gain(l, 0), dh, 1.0, dx, F32,
                post=(saved[l - 1]["ffn2_m"], gain(l - 1, 7), 0.5, BF16), force=False)
        else:
            dx, dg[0] = norm_bwd(f"ffn1_pre_bwd_{l}", s["x0"], gain(l, 0), dh, 1.0, dx, F32)

    g_norms = jnp.stack([jnp.concatenate(dg, axis=0) for dg in dgs], axis=0)
    g_norms = g_norms.reshape(depth, norms.shape[1], NDEV, dsh).transpose(2, 0, 1, 3)
    g_pscale_p = g_pscale.reshape(1, NDEV, dsh).transpose(1, 0, 2)
    sharded_small = ["norms", "pool_scale"]
    pieces_small = jnp.stack([_pack([g_norms[j], g_pscale_p[j]]) for j in range(NDEV)], axis=0)
    small_scatter = _Scatter([(pieces_small, 0, pieces_small.shape[1])])

    out = {}

    def update(k, extra=(), host=True):
        waiting = [it[0] for it in pending if it[0][0] == k] + [it[:2] for it in swaps if it[0] == k]
        assert not waiting, waiting
        shp = params[k].shape
        flip = k.endswith(("_wg", "_wu"))
        c = shp[1] if flip else shp[-1]
        view = lambda a: (a.swapaxes(1, 2) if flip else a).reshape(-1, c)
        pieces = [recv[key] for key in sorted(key for key in recv if key[0] == k)]
        args = (f"adamw_{k}", view(params[k]), view(mom1[k]), view(mom2[k]), pieces)
        res = hosted(HOST_US_ADAMW, adamw, *args, extra=extra) if host else adamw(*args)
        if flip:
            out[k] = [a.reshape(shp[0], shp[2], shp[1]).swapaxes(1, 2) for a in res]
        else:
            out[k] = [a.reshape(shp) for a in res]

    def update_pack(names, pieces):
        shapes = [params[k].shape for k in names]
        res = adamw("adamw_" + names[0] + "_pack", _pack([params[k] for k in names]),
                    _pack([mom1[k] for k in names]), _pack([mom2[k] for k in names]), [pieces])
        for which in range(4):
            for k, a in zip(names, _unpack(res[which], shapes)):
                out.setdefault(k, [None] * 4)[which] = a

    last = ("ffn1_wg", "ffn1_wu", "ffn1_wd")
    big = ("ffn2_wg", "ffn2_wu", "ffn2_wd")
    early = [k for k in order if k not in last + big and k not in sharded_small and k not in replicated]
    for i, k in enumerate(early):
        update(k, extra=[small_scatter] if i == 0 else ())
    for k in big:
        update(k, host=False)
    flushes = 0
    while pending or swaps:
        hosted(float("inf"), lambda comm: _run_exchange(f"scatter_tail_{flushes}", comm))
        flushes += 1
    update_pack(replicated, small_jobs[0].result[0])
    update_pack(sharded_small, small_scatter.result[0])
    for k in last:
        update(k, host=False)

    outputs = [loss, dx[None]]
    for which in range(4):
        outputs += [out[k][which] for k in order]
    return tuple(outputs)
```

```python
import math

import jax
import jax.numpy as jnp
from jax import lax
from jax.experimental import pallas as pl
from jax.experimental.pallas import tpu as pltpu

F32 = jnp.float32
BF16 = jnp.bfloat16
NDEV = 8
MIB = 1024 * 1024
LANES = 128

RMS_EPS = 1e-6
HEAD_DIM = 64
N_Q_HEADS = 16
Q_PER_KV = 8
ATTN_WIDTH = 1024
KV_WIDTH = 128
BLOCK = 128
ROPE_DIM = 16
ROPE_THETA = 500000.0
SGU_GROUPS = 8
SGU_WIDTH = 1024
CHUNK = 128
POOL_WINDOWS = (2, 4, 8, 16)
POOL_GROUP_DIM = 512
POOL_HALO = 16
X_HEADS = 4
X_HEAD_DIM = 128
ZQ, ZU, ZV, ZK = 0, 1024, 2048, 3072
IN_WIDTH = 3328

ADAM_LR = 0.001
ADAM_B1 = 0.9
ADAM_B2 = 0.999
ADAM_EPS = 1e-08
ADAM_WD = 0.01
ADAM_STEP = 10

FFN_FWD_ROWS = 512
FFN_BWD_ROWS = 512

LINK_US_PER_MIB_ALL = 91.0
LINK_US_PER_MIB_CHIPS = 45.0
PAIR_SWAP_US = 20.0
HOST_US_FFN_BWD = 420.0
HOST_US_WGRAD = 100.0
HOST_US_ATTN_BWD = 240.0
HOST_US_SMALL = 40.0
HOST_US_ADAMW = 130.0

_DN = {
    "nn": (((1,), (0,)), ((), ())),
    "nt": (((1,), (1,)), ((), ())),
    "tn": (((0,), (0,)), ((), ())),
}


def _cp(naxes, vmem_mib=48):
    return pltpu.CompilerParams(dimension_semantics=("arbitrary",) * naxes, vmem_limit_bytes=vmem_mib * MIB)


def _tile(n, pref):
    t = min(n, pref)
    while n % t:
        t //= 2
    return t


def _dot(a, b, dims="nn"):
    return lax.dot_general(a, b, _DN[dims], preferred_element_type=F32)


def _me():
    x, y, c = lax.axis_index("x"), lax.axis_index("y"), lax.axis_index("c")
    return x, y, c, 4 * x + 2 * y + c


def _peer(k):
    x, y, c, _ = _me()
    px = 1 - x if k & 4 else x
    py = 1 - y if k & 2 else y
    pc = 1 - c if k & 1 else c
    return (px, py, pc), 4 * px + 2 * py + pc


class _Exchange:
    def __init__(self, arrs, out_shape, remote_per=NDEV - 1, local_per=1):
        self.arrs = list(arrs)
        self.n = len(self.arrs)
        self.out_shape = list(out_shape)
        self.remote_per = remote_per
        self.scratch = [
            pltpu.SemaphoreType.DMA((self.n * remote_per,)),
            pltpu.SemaphoreType.DMA((self.n * remote_per,)),
            pltpu.SemaphoreType.DMA((self.n * local_per,)),
        ]
        self.result = None

    def mid(self, ins, outs, sems):
        pass

    def _copy(self, src, dst, sems, i, k, dev):
        send, recv, _ = sems
        return pltpu.make_async_remote_copy(
            src_ref=src, dst_ref=dst, send_sem=send.at[i * self.remote_per + k - 1],
            recv_sem=recv.at[i * self.remote_per + k - 1], device_id=dev, device_id_type=pl.DeviceIdType.MESH)


class _Gather(_Exchange):
    def __init__(self, arrs):
        assert all(a.shape[0] % (16 * GATHER_PARTS) == 0 for a in arrs), [a.shape for a in arrs]
        super().__init__(arrs, [jax.ShapeDtypeStruct((NDEV,) + a.shape, a.dtype) for a in arrs],
                         remote_per=(NDEV - 1) * GATHER_PARTS)

    def _part(self, ref, i, part):
        rows = self.arrs[i].shape[0] // GATHER_PARTS
        return ref.at[pl.ds(part * rows, rows)]

    def _piece(self, src, dst, sems, i, k, part, dev):
        return self._copy(self._part(src, i, part), self._part(dst, i, part), sems, i,
                          (k - 1) * GATHER_PARTS + part + 1, dev)

    def start(self, ins, outs, sems):
        me = _me()[3]
        for i in range(self.n):
            pltpu.make_async_copy(ins[i], outs[i].at[me], sems[2].at[i]).start()
        for part in range(GATHER_PARTS):
            for k in (1, 2, 4, 6):
                dev, _ = _peer(k)
                for i in range(self.n):
                    self._piece(ins[i], outs[i].at[me], sems, i, k, part, dev).start()

    def mid(self, ins, outs, sems):
        sibling, _ = _peer(1)
        for part in range(GATHER_PARTS):
            for k in (2, 4, 6):
                dev, slot = _peer(k)
                for i in range(self.n):
                    block = outs[i].at[slot]
                    self._piece(ins[i], block, sems, i, k, part, dev).wait_recv()
                    self._piece(block, block, sems, i, k + 1, part, sibling).start()

    def finish(self, ins, outs, sems):
        me = _me()[3]
        sibling, _ = _peer(1)
        for part in range(GATHER_PARTS):
            for k in (1, 3, 5, 7):
                dev, slot = _peer(k)
                for i in range(self.n):
                    self._piece(ins[i], outs[i].at[slot], sems, i, k, part, dev).wait_recv()
            for k in range(1, NDEV):
                for i in range(self.n):
                    self._piece(ins[i], outs[i].at[me], sems, i, k, part, sibling).wait_send()
        for i in range(self.n):
            pltpu.make_async_copy(ins[i], outs[i].at[me], sems[2].at[i]).wait()


class _Scatter(_Exchange):
    def __init__(self, units):
        self.rows = [(r0, n) for _, r0, n in units]
        super().__init__([a for a, _, _ in units],
                         [jax.ShapeDtypeStruct((NDEV, n) + a.shape[2:], a.dtype) for a, _, n in units])

    def _src(self, ins, i, slot):
        r0, n = self.rows[i]
        return ins[i].at[slot, pl.ds(r0, n)]

    def start(self, ins, outs, sems):
        me = _me()[3]
        for i in range(self.n):
            pltpu.make_async_copy(self._src(ins, i, me), outs[i].at[me], sems[2].at[i]).start()
        for k in range(1, NDEV):
            dev, slot = _peer(k)
            for i in range(self.n):
                self._copy(self._src(ins, i, slot), outs[i].at[me], sems, i, k, dev).start()

    def finish(self, ins, outs, sems):
        me = _me()[3]
        for k in range(1, NDEV):
            dev, slot = _peer(k)
            for i in range(self.n):
                cp = self._copy(self._src(ins, i, slot), outs[i].at[slot], sems, i, k, dev)
                cp.wait_send()
                cp.wait_recv()
        for i in range(self.n):
            pltpu.make_async_copy(self._src(ins, i, me), outs[i].at[me], sems[2].at[i]).wait()


NCHIP = NDEV // 2
GATHER_PARTS = 2


class _PairSwap(_Exchange):
    def __init__(self, arrs):
        super().__init__(arrs, [jax.ShapeDtypeStruct((NCHIP,) + a.shape[2:], a.dtype) for a in arrs],
                         remote_per=NCHIP)

    def _copies(self, ins, outs, sems):
        c = _me()[2]
        sibling, _ = _peer(1)
        for i in range(self.n):
            for q in range(NCHIP):
                yield self._copy(ins[i].at[q, 1 - c], outs[i].at[q], sems, i, q + 1, sibling)

    def start(self, ins, outs, sems):
        for remote in self._copies(ins, outs, sems):
            remote.start()

    def finish(self, ins, outs, sems):
        for remote in self._copies(ins, outs, sems):
            remote.wait_send()
            remote.wait_recv()


class _ChipScatter(_Exchange):
    def __init__(self, units):
        self.rows = [(r0, n) for _, r0, n in units]
        super().__init__([a for a, _, _ in units],
                         [jax.ShapeDtypeStruct((NCHIP, n) + a.shape[2:], a.dtype) for a, _, n in units],
                         remote_per=NCHIP - 1)

    def _src(self, ins, i, chip):
        r0, n = self.rows[i]
        return ins[i].at[chip, pl.ds(r0, n)]

    @staticmethod
    def _chip(k):
        dev, slot = _peer(2 * k)
        return dev, slot // 2

    def start(self, ins, outs, sems):
        mine = _me()[3] // 2
        for i in range(self.n):
            pltpu.make_async_copy(self._src(ins, i, mine), outs[i].at[mine], sems[2].at[i]).start()
        for k in range(1, NCHIP):
            dev, chip = self._chip(k)
            for i in range(self.n):
                self._copy(self._src(ins, i, chip), outs[i].at[mine], sems, i, k, dev).start()

    def finish(self, ins, outs, sems):
        mine = _me()[3] // 2
        for k in range(1, NCHIP):
            dev, chip = self._chip(k)
            for i in range(self.n):
                cp = self._copy(self._src(ins, i, chip), outs[i].at[chip], sems, i, k, dev)
                cp.wait_send()
                cp.wait_recv()
        for i in range(self.n):
            pltpu.make_async_copy(self._src(ins, i, mine), outs[i].at[mine], sems[2].at[i]).wait()


class _Multi:
    def __init__(self, jobs):
        self.jobs = list(jobs)
        self.arrs = [a for j in self.jobs for a in j.arrs]
        self.out_shape = [s for j in self.jobs for s in j.out_shape]
        self.scratch = [s for j in self.jobs for s in j.scratch]
        self._result = None

    def _parts(self, ins, outs, sems):
        oi = oo = 0
        for idx, j in enumerate(self.jobs):
            ni, no = len(j.arrs), len(j.out_shape)
            yield j, ins[oi:oi + ni], outs[oo:oo + no], sems[3 * idx:3 * idx + 3]
            oi += ni
            oo += no

    def start(self, ins, outs, sems):
        for j, i, o, s in self._parts(ins, outs, sems):
            j.start(i, o, s)

    def mid(self, ins, outs, sems):
        for j, i, o, s in self._parts(ins, outs, sems):
            j.mid(i, o, s)

    def finish(self, ins, outs, sems):
        for j, i, o, s in self._parts(ins, outs, sems):
            j.finish(i, o, s)

    @property
    def result(self):
        return self._result

    @result.setter
    def result(self, res):
        self._result = res
        o = 0
        for j in self.jobs:
            j.result = list(res[o:o + len(j.out_shape)])
            o += len(j.out_shape)


def _call(name, body, *, grid, in_specs, out_specs, out_shape, args, scratch=(), comm=None, vmem_mib=48):
    in_specs, out_specs, out_shape = list(in_specs), list(out_specs), list(out_shape)
    scratch, args = list(scratch), list(args)
    ni, no, ns = len(in_specs), len(out_specs), len(scratch)
    kernel_fn = body
    if comm is not None:
        ci, co = len(comm.arrs), len(comm.out_shape)
        hbm = pl.BlockSpec(memory_space=pltpu.HBM)

        def kernel_fn(*refs):
            refs = list(refs)
            ins, c_in, outs, c_out, scr, c_scr = (
                [refs.pop(0) for _ in range(cnt)] for cnt in (ni, ci, no, co, ns, len(comm.scratch)))
            if not grid:
                comm.start(c_in, c_out, c_scr)
                body(*ins, *outs, *scr)
                comm.mid(c_in, c_out, c_scr)
                comm.finish(c_in, c_out, c_scr)
                return
            step = pl.program_id(0)
            for ax in range(1, len(grid)):
                step = step * grid[ax] + pl.program_id(ax)
            nsteps = math.prod(grid)
            mid_step = nsteps - 1

            @pl.when(step == 0)
            def _():
                comm.start(c_in, c_out, c_scr)

            body(*ins, *outs, *scr)

            @pl.when(step == mid_step)
            def _():
                comm.mid(c_in, c_out, c_scr)

            @pl.when(step == nsteps - 1)
            def _():
                comm.finish(c_in, c_out, c_scr)

        in_specs += [hbm] * ci
        out_specs += [hbm] * co
        out_shape += comm.out_shape
        scratch += comm.scratch
        args += comm.arrs
    params = _cp(len(grid), vmem_mib) if grid else None
    res = pl.pallas_call(
        kernel_fn, out_shape=out_shape, grid=grid, in_specs=in_specs, out_specs=out_specs,
        scratch_shapes=scratch, compiler_params=params, name=name,
    )(*args)
    if comm is not None:
        comm.result = list(res[no:])
    return tuple(res[:no])


def _run_exchange(name, comm):
    _call(name, lambda: None, grid=(), in_specs=[], out_specs=[], out_shape=[], args=[], comm=comm)
    return comm.result


def _pack(arrs, dtype=F32):
    flat = jnp.concatenate([a.astype(dtype).reshape(-1) for a in arrs])
    n = flat.shape[0]
    total = -(-n // (32 * LANES)) * (32 * LANES)
    return jnp.pad(flat, (0, total - n)).reshape(total // LANES, LANES)


def _unpack(packed, shapes, lead=()):
    flat = packed.reshape(lead + (-1,))
    out, off = [], 0
    for s in shapes:
        n = math.prod(s)
        out.append(flat[..., off:off + n].reshape(lead + tuple(s)))
        off += n
    return out


def _mm(name, a, b, *, dims, grid, a_spec, b_spec, o_spec, out_shape, acc_shape=None, nk=1, vmem_mib=48, comm=None):
    nax = len(grid)

    def body(a_ref, b_ref, o_ref, *scratch):
        p = _dot(a_ref[...], b_ref[...], dims)
        if nk == 1:
            o_ref[...] = p.astype(o_ref.dtype)
            return
        acc = scratch[0]
        k = pl.program_id(nax - 1)

        @pl.when(k == 0)
        def _():
            acc[...] = p

        @pl.when(k > 0)
        def _():
            acc[...] += p

        @pl.when(k == nk - 1)
        def _():
            o_ref[...] = acc[...].astype(o_ref.dtype)

    return _call(
        name, body, grid=grid, in_specs=[a_spec, b_spec], out_specs=[o_spec], out_shape=[out_shape], args=[a, b],
        scratch=[pltpu.VMEM(acc_shape, F32)] if nk > 1 else [], comm=comm, vmem_mib=vmem_mib,
    )[0]


def mm_nn(name, a, b, out_dtype, tn=None, comm=None):
    m, k = a.shape
    n = b.shape[1]
    tm = _tile(m, 512)
    tn = n if tn is None else tn
    return _mm(
        name, a, b, dims="nn", grid=(n // tn, m // tm),
        a_spec=pl.BlockSpec((tm, k), lambda j, i: (i, 0)),
        b_spec=pl.BlockSpec((k, tn), lambda j, i: (0, j)),
        o_spec=pl.BlockSpec((tm, tn), lambda j, i: (i, j)),
        out_shape=jax.ShapeDtypeStruct((m, n), out_dtype), comm=comm,
    )


def mm_nt(name, a, b, out_dtype, tn=None):
    m, k = a.shape
    n = b.shape[0]
    tm = _tile(m, 512)
    tn = n if tn is None else tn
    return _mm(
        name, a, b, dims="nt", grid=(n // tn, m // tm),
        a_spec=pl.BlockSpec((tm, k), lambda j, i: (i, 0)),
        b_spec=pl.BlockSpec((tn, k), lambda j, i: (j, 0)),
        o_spec=pl.BlockSpec((tm, tn), lambda j, i: (i, j)),
        out_shape=jax.ShapeDtypeStruct((m, n), out_dtype),
    )


def mm_kred(name, a_t, b, out_dtype, tno=None, comm=None):
    m, k = a_t.shape
    n = b.shape[1]
    tk = _tile(k, 512)
    tmo = _tile(m, 1024)
    tno = n if tno is None else tno
    return _mm(
        name, a_t, b, dims="nn", grid=(m // tmo, n // tno, k // tk),
        a_spec=pl.BlockSpec((tmo, tk), lambda i, j, kk: (i, kk)),
        b_spec=pl.BlockSpec((tk, tno), lambda i, j, kk: (kk, j)),
        o_spec=pl.BlockSpec((tmo, tno), lambda i, j, kk: (i, j)),
        out_shape=jax.ShapeDtypeStruct((m, n), out_dtype),
        acc_shape=(tmo, tno), nk=k // tk, comm=comm,
    )


def mm_fullk(name, a_t, b, out_dtype, tno, comm=None):
    m, k = a_t.shape
    n = b.shape[1]
    tmo = _tile(m, 1024)
    return _mm(
        name, a_t, b, dims="nn", grid=(m // tmo, n // tno),
        a_spec=pl.BlockSpec((tmo, k), lambda i, j: (i, 0)),
        b_spec=pl.BlockSpec((k, tno), lambda i, j: (0, j)),
        o_spec=pl.BlockSpec((tmo, tno), lambda i, j: (i, j)),
        out_shape=jax.ShapeDtypeStruct((m, n), out_dtype), comm=comm,
    )


def mm_tn(name, a, b, out_dtype, tmo=None, tno=None):
    k, m = a.shape
    n = b.shape[1]
    tk = _tile(k, 512)
    tmo = _tile(m, 1024) if tmo is None else tmo
    tno = n if tno is None else tno
    return _mm(
        name, a, b, dims="tn", grid=(m // tmo, n // tno, k // tk),
        a_spec=pl.BlockSpec((tk, tmo), lambda i, j, kk: (kk, i)),
        b_spec=pl.BlockSpec((tk, tno), lambda i, j, kk: (kk, j)),
        o_spec=pl.BlockSpec((tmo, tno), lambda i, j, kk: (i, j)),
        out_shape=jax.ShapeDtypeStruct((m, n), out_dtype),
        acc_shape=(tmo, tno), nk=k // tk,
    )


def _rstd(x):
    return lax.rsqrt(jnp.mean(x * x, axis=-1, keepdims=True) + RMS_EPS)


def norm_fwd(name, x, g, out_dtype, with_t=False, comm=None):
    t, d = x.shape
    tm = _tile(t, 256)

    def body(x_ref, g_ref, o_ref, *t_ref):
        xv = x_ref[...]
        h = xv * _rstd(xv) * g_ref[...]
        o_ref[...] = h.astype(o_ref.dtype)
        if with_t:
            t_ref[0][...] = h.T.astype(out_dtype)

    row = pl.BlockSpec((tm, d), lambda i: (i, 0))
    out_shape = [jax.ShapeDtypeStruct((t, d), out_dtype)]
    out_specs = [row]
    if with_t:
        out_shape.append(jax.ShapeDtypeStruct((d, t), out_dtype))
        out_specs.append(pl.BlockSpec((d, tm), lambda i: (0, i)))
    res = _call(
        name, body, grid=(t // tm,), in_specs=[row, pl.BlockSpec((1, d), lambda i: (0, 0))],
        out_specs=out_specs, out_shape=out_shape, args=[x, g], comm=comm,
    )
    return res if with_t else res[0]


def resid_norm_fwd(name, x, m, g, scale, nxt=None, comm=None):
    t, d = x.shape
    tm = _tile(t, 256)
    with_t = nxt is not None and nxt[2]

    def body(*refs):
        if nxt is None:
            x_ref, m_ref, g_ref, o_ref = refs
        else:
            x_ref, m_ref, g_ref, gn_ref, o_ref, h_ref = refs[:6]
        mv = m_ref[...]
        xn = x_ref[...] + scale * (mv * _rstd(mv) * g_ref[...])
        o_ref[...] = xn
        if nxt is not None:
            h = xn * _rstd(xn) * gn_ref[...]
            h_ref[...] = h.astype(h_ref.dtype)
            if with_t:
                refs[6][...] = h.T.astype(nxt[1])

    row = pl.BlockSpec((tm, d), lambda i: (i, 0))
    vec = pl.BlockSpec((1, d), lambda i: (0, 0))
    out_shape, out_specs, args = [jax.ShapeDtypeStruct((t, d), F32)], [row], [x, m, g]
    if nxt is not None:
        args.append(nxt[0])
        out_shape.append(jax.ShapeDtypeStruct((t, d), nxt[1]))
        out_specs.append(row)
        if with_t:
            out_shape.append(jax.ShapeDtypeStruct((d, t), nxt[1]))
            out_specs.append(pl.BlockSpec((d, tm), lambda i: (0, i)))
    res = _call(
        name, body, grid=(t // tm,), in_specs=[row, row, vec] + ([vec] if nxt is not None else []),
        out_specs=out_specs, out_shape=out_shape, args=args, comm=comm,
    )
    return res[0] if nxt is None else res


def norm_bwd(name, u, g, dy, scale, resid, out_dtype, post=None, comm=None):
    t, d = u.shape
    tm = _tile(t, 256)
    has_resid = resid is not None

    def norm_grad(uv, gv, dyv, dg_ref):
        r = _rstd(uv)
        uh = uv * r
        dg_ref[...] += jnp.sum(dyv * uh, axis=0, keepdims=True)
        dyg = dyv * gv
        return r * (dyg - uh * jnp.mean(dyg * uh, axis=-1, keepdims=True))

    def body(*refs):
        refs = list(refs)
        u_ref, g_ref, dy_ref = refs[:3]
        del refs[:3]
        r_ref = refs.pop(0) if has_resid else None
        if post is not None:
            m_ref, gp_ref = refs[:2]
            del refs[:2]
        du_ref, dg_ref = refs[:2]

        @pl.when(pl.program_id(0) == 0)
        def _():
            for ref in refs[1::2]:
                ref[...] = jnp.zeros_like(ref)

        du = norm_grad(u_ref[...], g_ref[...], dy_ref[...].astype(F32) * scale, dg_ref)
        if has_resid:
            du = du + r_ref[...]
        du_ref[...] = du.astype(du_ref.dtype)
        if post is not None:
            dm_ref, dgp_ref = refs[2:4]
            dm_ref[...] = norm_grad(m_ref[...], gp_ref[...], du * post[2], dgp_ref).astype(dm_ref.dtype)

    row = pl.BlockSpec((tm, d), lambda i: (i, 0))
    vec = pl.BlockSpec((1, d), lambda i: (0, 0))
    args = [u, g, dy] + ([resid] if has_resid else [])
    in_specs = [row, vec, row] + ([row] if has_resid else [])
    out_specs = [row, vec]
    out_shape = [jax.ShapeDtypeStruct((t, d), out_dtype), jax.ShapeDtypeStruct((1, d), F32)]
    if post is not None:
        args += [post[0], post[1]]
        in_specs += [row, vec]
        out_specs += [row, vec]
        out_shape += [jax.ShapeDtypeStruct((t, d), post[3]), jax.ShapeDtypeStruct((1, d), F32)]
    return _call(name, body, grid=(t // tm,), in_specs=in_specs, out_specs=out_specs, out_shape=out_shape,
                 args=args, comm=comm)


def loss_grad(y, target):
    t, d = y.shape
    tm = _tile(t, 256)
    nt = t // tm

    def body(y_ref, t_ref, dy_ref, loss_ref, acc):
        i = pl.program_id(0)
        e = y_ref[...] - t_ref[...]
        dy_ref[...] = e * (1.0 / d)

        @pl.when(i == 0)
        def _():
            acc[...] = jnp.zeros_like(acc)

        acc[...] += jnp.sum(e * e, axis=0, keepdims=True)

        @pl.when(i == nt - 1)
        def _():
            loss_ref[...] = (0.5 / d) * jnp.sum(acc[...], axis=1, keepdims=True)

    row = pl.BlockSpec((tm, d), lambda i: (i, 0))
    return pl.pallas_call(
        body,
        out_shape=(jax.ShapeDtypeStruct((t, d), F32), jax.ShapeDtypeStruct((1, 1), F32)),
        grid=(nt,), in_specs=[row, row], out_specs=(row, pl.BlockSpec((1, 1), lambda i: (0, 0))),
        scratch_shapes=[pltpu.VMEM((1, d), F32)], compiler_params=_cp(1), name="loss_grad",
    )(y, target)


def _row_tile(r, c):
    if r * c * 4 <= MIB:
        return r
    best = None
    for t in range(16, r, 16):
        if r % t == 0 and t * c * 4 <= MIB:
            best = t
    return r if best is None else best


def pair_sum(name, g4, theirs):
    nq, _, r, c = g4.shape
    tr = _tile(r, 1024)

    def body(core_ref, g_ref, t_ref, o_ref):
        o_ref[...] = (g_ref[...].astype(F32) + t_ref[...].astype(F32)).astype(o_ref.dtype)

    blk = pl.BlockSpec((None, tr, c), lambda q, i, core: (q, i, 0))
    return pl.pallas_call(
        body, out_shape=jax.ShapeDtypeStruct(theirs.shape, theirs.dtype),
        grid_spec=pltpu.PrefetchScalarGridSpec(
            num_scalar_prefetch=1, grid=(nq, r // tr),
            in_specs=[pl.BlockSpec((None, None, tr, c), lambda q, i, core: (q, core[0], i, 0)), blk],
            out_specs=blk),
        compiler_params=_cp(2), name=name,
    )(lax.axis_index("c").astype(jnp.int32).reshape(1), g4, theirs)


def adamw(name, w, m, v, pieces, comm=None):
    nl = len(pieces)
    npiece, r, c = pieces[0].shape
    tr = _row_tile(r, c)
    nr = r // tr
    bc1 = 1.0 - ADAM_B1 ** ADAM_STEP
    bc2 = 1.0 - ADAM_B2 ** ADAM_STEP

    def body(w_ref, m_ref, v_ref, *rest):
        p_refs, (g_ref, d_ref, nm_ref, nv_ref) = rest[:nl], rest[nl:]

        def update(p_ref):
            g = p_ref[0].astype(F32)
            for j in range(1, npiece):
                g = g + p_ref[j].astype(F32)
            m1 = ADAM_B1 * m_ref[...] + (1.0 - ADAM_B1) * g
            v1 = ADAM_B2 * v_ref[...] + (1.0 - ADAM_B2) * (g * g)
            m_hat = m1 / bc1
            v_hat = v1 / bc2
            g_ref[...] = g
            d_ref[...] = -ADAM_LR * (m_hat / (jnp.sqrt(v_hat) + ADAM_EPS) + ADAM_WD * w_ref[...])
            nm_ref[...] = m1
            nv_ref[...] = v1

        if nl == 1:
            update(p_refs[0])
        else:
            for ll in range(nl):
                pl.when(pl.program_id(0) == ll)(lambda ll=ll: update(p_refs[ll]))

    def piece_spec(ll):
        return pl.BlockSpec((npiece, tr, c), lambda l, i: (0, jnp.where(l == ll, i, jnp.where(l > ll, nr - 1, 0)), 0))

    row = pl.BlockSpec((tr, c), lambda l, i: (l * nr + i, 0))
    out = jax.ShapeDtypeStruct((nl * r, c), F32)
    return _call(
        name, body, grid=(nl, nr), in_specs=[row, row, row] + [piece_spec(ll) for ll in range(nl)],
        out_specs=[row] * 4, out_shape=[out] * 4, args=[w, m, v] + list(pieces), comm=comm,
    )


def _sigmoid(a):
    return 1.0 / (1.0 + jnp.exp(-a))


def ffn_fwd(name, h, wg, wu, wd, comm=None):
    t, d = h.shape
    ns, _, f = wg.shape
    tm = _tile(t, FFN_FWD_ROWS)

    def body(h_ref, wg_ref, wu_ref, wd_ref, a_ref, b_ref, hidt_ref, m_ref, acc):
        j = pl.program_id(1)
        hv = h_ref[...]
        a = _dot(hv, wg_ref[...])
        b = _dot(hv, wu_ref[...])
        hid32 = (a * _sigmoid(a)) * b
        hid = hid32.astype(BF16)
        a_ref[...] = a.astype(BF16)
        b_ref[...] = b.astype(BF16)
        hidt_ref[...] = hid32.T.astype(BF16)
        p = _dot(hid, wd_ref[...])

        @pl.when(j == 0)
        def _():
            acc[...] = p

        @pl.when(j > 0)
        def _():
            acc[...] += p

        @pl.when(j == ns - 1)
        def _():
            m_ref[...] = acc[...]

    w_in = pl.BlockSpec((None, d, f), lambda i, j: (j, 0, 0))
    act = pl.BlockSpec((None, tm, f), lambda i, j: (j, i, 0))
    act_shape = jax.ShapeDtypeStruct((ns, t, f), BF16)
    return _call(
        name, body, grid=(t // tm, ns),
        in_specs=[pl.BlockSpec((tm, d), lambda i, j: (i, 0)), w_in, w_in,
                  pl.BlockSpec((None, f, d), lambda i, j: (j, 0, 0))],
        out_specs=[act, act, pl.BlockSpec((None, f, tm), lambda i, j: (j, 0, i)),
                   pl.BlockSpec((tm, d), lambda i, j: (i, 0))],
        out_shape=[act_shape, act_shape, jax.ShapeDtypeStruct((ns, f, t), BF16), jax.ShapeDtypeStruct((t, d), F32)],
        args=[h, wg, wu, wd], scratch=[pltpu.VMEM((tm, d), F32)], comm=comm, vmem_mib=56,
    )


def ffn_bwd(name, dm, a, b, wg, wu, wd, comm=None):
    t, d = dm.shape
    ns, _, f = wg.shape
    tm = _tile(t, FFN_BWD_ROWS)

    def body(dm_ref, a_ref, b_ref, wg_ref, wu_ref, wd_ref, dat_ref, dbt_ref, dh_ref, acc):
        j = pl.program_id(1)
        dhid = _dot(dm_ref[...], wd_ref[...], "nt")
        av = a_ref[...].astype(F32)
        bv = b_ref[...].astype(F32)
        sig = _sigmoid(av)
        da32 = dhid * bv * (sig * (1.0 + av * (1.0 - sig)))
        db32 = dhid * (av * sig)
        dat_ref[...] = da32.T.astype(BF16)
        dbt_ref[...] = db32.T.astype(BF16)
        p = _dot(da32.astype(BF16), wg_ref[...], "nt") + _dot(db32.astype(BF16), wu_ref[...], "nt")

        @pl.when(j == 0)
        def _():
            acc[...] = p

        @pl.when(j > 0)
        def _():
            acc[...] += p

        @pl.when(j == ns - 1)
        def _():
            dh_ref[...] = acc[...]

    w_in = pl.BlockSpec((None, d, f), lambda i, j: (j, 0, 0))
    act = pl.BlockSpec((None, tm, f), lambda i, j: (j, i, 0))
    act_t = pl.BlockSpec((None, f, tm), lambda i, j: (j, 0, i))
    row = pl.BlockSpec((tm, d), lambda i, j: (i, 0))
    act_t_shape = jax.ShapeDtypeStruct((ns, f, t), BF16)
    return _call(
        name, body, grid=(t // tm, ns),
        in_specs=[row, act, act, w_in, w_in, pl.BlockSpec((None, f, d), lambda i, j: (j, 0, 0))],
        out_specs=[act_t, act_t, row],
        out_shape=[act_t_shape, act_t_shape, jax.ShapeDtypeStruct((t, d), F32)],
        args=[dm, a, b, wg, wu, wd], scratch=[pltpu.VMEM((tm, d), F32)], comm=comm, vmem_mib=56,
    )


def ffn_wgrad(name, act_t, x, comm=None):
    ns, f, t = act_t.shape
    d = x.shape[1]
    return _mm(
        name, act_t, x, dims="nn", grid=(ns,),
        a_spec=pl.BlockSpec((None, f, t), lambda j: (j, 0, 0)),
        b_spec=pl.BlockSpec((t, d), lambda j: (0, 0), pipeline_mode=pl.Buffered(1)),
        o_spec=pl.BlockSpec((None, f, d), lambda j: (j, 0, 0)),
        out_shape=jax.ShapeDtypeStruct((ns, f, d), BF16), vmem_mib=56, comm=comm,
    )


def rope_table(t):
    half = ROPE_DIM // 2
    inv = ROPE_THETA ** (-jnp.arange(half, dtype=F32) * 2.0 / ROPE_DIM)
    ang = jnp.arange(t, dtype=F32)[:, None] * inv[None, :]
    cos, sin = jnp.cos(ang), jnp.sin(ang)
    rest = HEAD_DIM - ROPE_DIM
    c = jnp.concatenate([cos, cos, jnp.ones((t, rest), F32)], axis=1)
    sm = jnp.concatenate([-sin, jnp.zeros((t, half + rest), F32)], axis=1)
    sp = jnp.concatenate([jnp.zeros((t, half), F32), sin, jnp.zeros((t, rest), F32)], axis=1)
    return jnp.concatenate([jnp.tile(c, (1, 2)), jnp.tile(sm, (1, 2)), jnp.tile(sp, (1, 2))], axis=1)


def _rope(x, tab, sign):
    w = x.shape[1]
    rep = w // LANES
    c, sm, sp = tab[:, 0:LANES], tab[:, LANES:2 * LANES], tab[:, 2 * LANES:3 * LANES]
    if rep > 1:
        c, sm, sp = jnp.tile(c, (1, rep)), jnp.tile(sm, (1, rep)), jnp.tile(sp, (1, rep))
    half = ROPE_DIM // 2
    return x * c + sign * (pltpu.roll(x, w - half, 1) * sm + pltpu.roll(x, half, 1) * sp)


def _attn_specs():
    prev = lambda n: jnp.maximum(n - 1, 0)
    kblk, vblk = ZK // LANES, ZK // LANES + 1
    return [
        pl.BlockSpec((BLOCK, ATTN_WIDTH), lambda n: (n, 0)),
        pl.BlockSpec((BLOCK, KV_WIDTH), lambda n: (n, kblk)),
        pl.BlockSpec((BLOCK, KV_WIDTH), lambda n: (prev(n), kblk)),
        pl.BlockSpec((BLOCK, KV_WIDTH), lambda n: (n, vblk)),
        pl.BlockSpec((BLOCK, KV_WIDTH), lambda n: (prev(n), vblk)),
        pl.BlockSpec((BLOCK, 3 * LANES), lambda n: (n, 0)),
        pl.BlockSpec((BLOCK, 3 * LANES), lambda n: (prev(n), 0)),
        pl.BlockSpec(memory_space=pltpu.SMEM),
    ]


def _attn_prologue(n, zq_ref, zk_ref, zkp_ref, zv_ref, zvp_ref, tab_ref, tabp_ref):
    q = (_rope(zq_ref[...], tab_ref[...], 1.0) * (HEAD_DIM ** -0.5)).astype(BF16)
    kcat = jnp.concatenate(
        [_rope(zkp_ref[...], tabp_ref[...], 1.0), _rope(zk_ref[...], tab_ref[...], 1.0)], axis=0).astype(BF16)
    vcat = jnp.concatenate([zvp_ref[...], zv_ref[...]], axis=0).astype(BF16)
    qi = lax.broadcasted_iota(jnp.int32, (BLOCK, 2 * BLOCK), 0)
    kj = lax.broadcasted_iota(jnp.int32, (BLOCK, 2 * BLOCK), 1)
    valid = (kj <= qi + BLOCK) & (kj > qi) & ((n > 0) | (kj >= BLOCK))
    return q, kcat, vcat, valid


def _attn_probs(qh, kh, valid, sink):
    s = jnp.where(valid, _dot(qh, kh, "nt"), -1e30)
    mx = jnp.maximum(jnp.max(s, axis=1, keepdims=True), sink)
    p = jnp.exp(s - mx)
    p_sink = jnp.exp(sink - mx)
    inv = 1.0 / (jnp.sum(p, axis=1, keepdims=True) + p_sink)
    return p * inv, p_sink * inv


def attn_fwd(z, tab, sinks, comm=None):
    t = z.shape[0]

    def body(zq_ref, zk_ref, zkp_ref, zv_ref, zvp_ref, tab_ref, tabp_ref, sink_ref, o_ref):
        n = pl.program_id(0)
        q, kcat, vcat, valid = _attn_prologue(n, zq_ref, zk_ref, zkp_ref, zv_ref, zvp_ref, tab_ref, tabp_ref)
        outs = []
        for h in range(N_Q_HEADS):
            kv = slice((h // Q_PER_KV) * HEAD_DIM, (h // Q_PER_KV + 1) * HEAD_DIM)
            p, _ = _attn_probs(q[:, h * HEAD_DIM:(h + 1) * HEAD_DIM], kcat[:, kv], valid, sink_ref[0, h])
            outs.append(_dot(p.astype(BF16), vcat[:, kv]))
        o_ref[...] = jnp.concatenate(outs, axis=1).astype(BF16)

    return _call(
        "attn_fwd", body, grid=(t // BLOCK,), in_specs=_attn_specs(),
        out_specs=[pl.BlockSpec((BLOCK, ATTN_WIDTH), lambda n: (n, 0))],
        out_shape=[jax.ShapeDtypeStruct((t, ATTN_WIDTH), BF16)],
        args=[z, z, z, z, z, tab, tab, sinks], comm=comm,
    )[0]


def attn_bwd(z, tab, sinks, dcat, comm=None):
    t = z.shape[0]
    nb = t // BLOCK

    def body(zq_ref, zk_ref, zkp_ref, zv_ref, zvp_ref, tab_ref, tabp_ref, sink_ref, do_ref,
             dq_ref, dkv_ref, dsink_ref):
        n = pl.program_id(0)
        q, kcat, vcat, valid = _attn_prologue(n, zq_ref, zk_ref, zkp_ref, zv_ref, zvp_ref, tab_ref, tabp_ref)
        do = do_ref[...]
        lane = lax.broadcasted_iota(jnp.int32, (1, LANES), 1)
        dqs, dks, dvs = [], [], []
        dsink = jnp.zeros((1, LANES), F32)
        for hk in range(N_Q_HEADS // Q_PER_KV):
            kv = slice(hk * HEAD_DIM, (hk + 1) * HEAD_DIM)
            kh, vh = kcat[:, kv], vcat[:, kv]
            dk_t = jnp.zeros((HEAD_DIM, 2 * BLOCK), F32)
            dv_t = jnp.zeros((HEAD_DIM, 2 * BLOCK), F32)
            for g in range(Q_PER_KV):
                h = hk * Q_PER_KV + g
                hs = slice(h * HEAD_DIM, (h + 1) * HEAD_DIM)
                qh, doh = q[:, hs], do[:, hs]
                p, p_sink = _attn_probs(qh, kh, valid, sink_ref[0, h])
                dv_t = dv_t + _dot(doh, p.astype(BF16), "tn")
                dp = _dot(doh, vh, "nt")
                rd = jnp.sum(p * dp, axis=1, keepdims=True)
                ds = (p * (dp - rd) * (HEAD_DIM ** -0.5)).astype(BF16)
                dqs.append(_dot(ds, kh))
                dk_t = dk_t + _dot(qh, ds, "tn")
                dsink = dsink + jnp.where(lane == h, -jnp.sum(p_sink * rd, axis=0, keepdims=True), 0.0)
            dks.append(dk_t)
            dvs.append(dv_t)
        dq_ref[...] = _rope(jnp.concatenate(dqs, axis=1), tab_ref[...], -1.0).astype(BF16)
        dkc = jnp.concatenate(dks, axis=0).T * (HEAD_DIM ** 0.5)
        dk_pre = jnp.concatenate(
            [_rope(dkc[:BLOCK], tabp_ref[...], -1.0), _rope(dkc[BLOCK:], tab_ref[...], -1.0)], axis=0)
        dkv_ref[...] = jnp.concatenate([dk_pre, jnp.concatenate(dvs, axis=0).T], axis=1)

        @pl.when(n == 0)
        def _():
            dsink_ref[...] = jnp.zeros_like(dsink_ref)

        dsink_ref[...] += dsink

    return _call(
        "attn_bwd", body, grid=(nb,),
        in_specs=_attn_specs() + [pl.BlockSpec((BLOCK, ATTN_WIDTH), lambda n: (n, 0))],
        out_specs=[pl.BlockSpec((BLOCK, ATTN_WIDTH), lambda n: (n, 0)),
                   pl.BlockSpec((None, 2 * BLOCK, 2 * KV_WIDTH), lambda n: (n, 0, 0)),
                   pl.BlockSpec((1, LANES), lambda n: (0, 0))],
        out_shape=[jax.ShapeDtypeStruct((t, ATTN_WIDTH), BF16),
                   jax.ShapeDtypeStruct((nb, 2 * BLOCK, 2 * KV_WIDTH), F32),
                   jax.ShapeDtypeStruct((1, LANES), F32)],
        args=[z, z, z, z, z, tab, tab, sinks, dcat], comm=comm,
    )


def _gelu(x):
    k = math.sqrt(2.0 / math.pi)
    th = jnp.tanh(k * (x + 0.044715 * (x * x * x)))
    return 0.5 * x * (1.0 + th), th


def _gelu_grad(x, th):
    k = math.sqrt(2.0 / math.pi)
    return 0.5 * (1.0 + th) + 0.5 * x * (1.0 - th * th) * (k * (1.0 + 3.0 * 0.044715 * (x * x)))


def _sgu_core(zu_ref, zv_ref, lng_ref, lnb_ref, w_ref, bt_ref):
    up, vp = zu_ref[...], zv_ref[...]
    u, thu = _gelu(up)
    v, thv = _gelu(vp)
    mu = jnp.mean(v, axis=-1, keepdims=True)
    vc = v - mu
    rstd = lax.rsqrt(jnp.mean(vc * vc, axis=-1, keepdims=True) + RMS_EPS)
    xhat = vc * rstd
    vn = (xhat * lng_ref[...] + lnb_ref[...]).astype(BF16)
    row = lax.broadcasted_iota(jnp.int32, (CHUNK, CHUNK), 0)
    col = lax.broadcasted_iota(jnp.int32, (CHUNK, CHUNK), 1)
    mixed = []
    for g in range(SGU_GROUPS):
        wc = jnp.where(row >= col, w_ref[g], 0.0).astype(BF16)
        mixed.append(_dot(wc, vn[:, g * CHUNK:(g + 1) * CHUNK]) + bt_ref[:, g:g + 1])
    return up, vp, u, thu, thv, rstd, xhat, vn, jnp.concatenate(mixed, axis=1)


def _sgu_specs():
    full = lambda shape: pl.BlockSpec(shape, lambda n: (0,) * len(shape))
    return [
        pl.BlockSpec((CHUNK, SGU_WIDTH), lambda n: (n, ZU // SGU_WIDTH)),
        pl.BlockSpec((CHUNK, SGU_WIDTH), lambda n: (n, ZV // SGU_WIDTH)),
        full((1, SGU_WIDTH)), full((1, SGU_WIDTH)),
        full((SGU_GROUPS, CHUNK, CHUNK)), full((CHUNK, SGU_GROUPS)),
    ]


def sgu_fwd(z, ln_g, ln_b, w, b_t, comm=None):
    t = z.shape[0]

    def body(zu_ref, zv_ref, lng_ref, lnb_ref, w_ref, bt_ref, o_ref):
        _, _, u, _, _, _, _, _, mixed = _sgu_core(zu_ref, zv_ref, lng_ref, lnb_ref, w_ref, bt_ref)
        o_ref[...] = (u * mixed).astype(BF16)

    return _call(
        "sgu_fwd", body, grid=(t // CHUNK,), in_specs=_sgu_specs(),
        out_specs=[pl.BlockSpec((CHUNK, SGU_WIDTH), lambda n: (n, 0))],
        out_shape=[jax.ShapeDtypeStruct((t, SGU_WIDTH), BF16)], args=[z, z, ln_g, ln_b, w, b_t], comm=comm,
    )[0]


def sgu_bwd(z, ln_g, ln_b, w, w_t, b_t, dcat):
    t = z.shape[0]

    def body(zu_ref, zv_ref, lng_ref, lnb_ref, w_ref, bt_ref, wt_ref, dg_ref,
             du_ref, dv_ref, dw_ref, dbt_ref, dlng_ref, dlnb_ref):
        up, vp, u, thu, thv, rstd, xhat, vn, mixed = _sgu_core(zu_ref, zv_ref, lng_ref, lnb_ref, w_ref, bt_ref)
        dgate = dg_ref[...].astype(F32)
        du_ref[...] = (dgate * mixed * _gelu_grad(up, thu)).astype(BF16)
        dmixed = dgate * u
        row = lax.broadcasted_iota(jnp.int32, (CHUNK, CHUNK), 0)
        col = lax.broadcasted_iota(jnp.int32, (CHUNK, CHUNK), 1)

        @pl.when(pl.program_id(0) == 0)
        def _():
            dw_ref[...] = jnp.zeros_like(dw_ref)
            dbt_ref[...] = jnp.zeros_like(dbt_ref)
            dlng_ref[...] = jnp.zeros_like(dlng_ref)
            dlnb_ref[...] = jnp.zeros_like(dlnb_ref)

        dvn, dbt = [], jnp.zeros((CHUNK, LANES), F32)
        for g in range(SGU_GROUPS):
            gs = slice(g * CHUNK, (g + 1) * CHUNK)
            dmx = dmixed[:, gs]
            dmxb = dmx.astype(BF16)
            dbt = dbt + jnp.where(col == g, jnp.sum(dmx, axis=1, keepdims=True), 0.0)
            dw_ref[g] += jnp.where(row >= col, _dot(dmxb, vn[:, gs], "nt"), 0.0)
            wtc = jnp.where(col >= row, wt_ref[g], 0.0).astype(BF16)
            dvn.append(_dot(wtc, dmxb))
        dbt_ref[...] += dbt
        dvn = jnp.concatenate(dvn, axis=1)
        dlnb_ref[...] += jnp.sum(dvn, axis=0, keepdims=True)
        dlng_ref[...] += jnp.sum(dvn * xhat, axis=0, keepdims=True)
        dxh = dvn * lng_ref[...]
        dv = rstd * (dxh - jnp.mean(dxh, axis=-1, keepdims=True) - xhat * jnp.mean(dxh * xhat, axis=-1, keepdims=True))
        dv_ref[...] = (dv * _gelu_grad(vp, thv)).astype(BF16)

    full = lambda shape: pl.BlockSpec(shape, lambda n: (0,) * len(shape))
    act = pl.BlockSpec((CHUNK, SGU_WIDTH), lambda n: (n, 0))
    act_shape = jax.ShapeDtypeStruct((t, SGU_WIDTH), BF16)
    vec = jax.ShapeDtypeStruct((1, SGU_WIDTH), F32)
    return pl.pallas_call(
        body,
        out_shape=(act_shape, act_shape, jax.ShapeDtypeStruct((SGU_GROUPS, CHUNK, CHUNK), F32),
                   jax.ShapeDtypeStruct((CHUNK, LANES), F32), vec, vec),
        grid=(t // CHUNK,),
        in_specs=_sgu_specs() + [full((SGU_GROUPS, CHUNK, CHUNK)),
                                 pl.BlockSpec((CHUNK, SGU_WIDTH), lambda n: (n, 1))],
        out_specs=(act, act, full((SGU_GROUPS, CHUNK, CHUNK)), full((CHUNK, LANES)),
                   full((1, SGU_WIDTH)), full((1, SGU_WIDTH))),
        compiler_params=_cp(1), name="sgu_bwd",
    )(z, z, ln_g, ln_b, w, b_t, w_t, dcat)


def dz_assemble(dq, dkv, du, dv):
    t = dq.shape[0]
    nb = t // BLOCK

    def body(dq_ref, cur_ref, nxt_ref, du_ref, dv_ref, o_ref):
        n = pl.program_id(0)
        o_ref[:, ZQ:ZQ + ATTN_WIDTH] = dq_ref[...]
        o_ref[:, ZU:ZU + SGU_WIDTH] = du_ref[...]
        o_ref[:, ZV:ZV + SGU_WIDTH] = dv_ref[...]
        kv = cur_ref[BLOCK:, :] + jnp.where(n < nb - 1, nxt_ref[:BLOCK, :], 0.0)
        o_ref[:, ZK:ZK + 2 * KV_WIDTH] = kv.astype(BF16)

    act = pl.BlockSpec((BLOCK, ATTN_WIDTH), lambda n: (n, 0))
    return pl.pallas_call(
        body, out_shape=jax.ShapeDtypeStruct((t, IN_WIDTH), BF16), grid=(nb,),
        in_specs=[act,
                  pl.BlockSpec((None, 2 * BLOCK, 2 * KV_WIDTH), lambda n: (n, 0, 0)),
                  pl.BlockSpec((None, 2 * BLOCK, 2 * KV_WIDTH), lambda n: (jnp.minimum(n + 1, nb - 1), 0, 0)),
                  act, act],
        out_specs=pl.BlockSpec((BLOCK, IN_WIDTH), lambda n: (n, 0)),
        compiler_params=_cp(1), name="dz_assemble",
    )(dq, dkv, dkv, du, dv)


def _pool_count(i, tp, w):
    t_idx = i * tp + lax.broadcasted_iota(jnp.int32, (tp, 1), 0)
    return jnp.minimum(t_idx + 1, w).astype(F32)


def pool_fwd(h, pw, pscale, comm=None):
    t, d = h.shape
    tp = _tile(t, 256)
    per = tp // POOL_HALO

    def body(h_ref, halo_ref, pw_ref, ps_ref, m_ref, pooled_ref):
        i = pl.program_id(0)
        cur = h_ref[...]
        ext = jnp.concatenate([jnp.where(i > 0, halo_ref[...], 0.0), cur], axis=0)
        ys, pooled = [], []
        for gi, w in enumerate(POOL_WINDOWS):
            gs = slice(gi * POOL_GROUP_DIM, (gi + 1) * POOL_GROUP_DIM)
            s = ext[:, gs]
            sh = 1
            while sh < w:
                s = s + pltpu.roll(s, sh, 0)
                sh *= 2
            pg = (s[POOL_HALO:, :] / _pool_count(i, tp, w) - cur[:, gs]).astype(BF16)
            pooled.append(pg)
            ys.append(_dot(pg, pw_ref[gi]))
        pooled_ref[...] = jnp.concatenate(pooled, axis=1)
        m_ref[...] = jnp.concatenate(ys, axis=1) * ps_ref[...]

    row = pl.BlockSpec((tp, d), lambda i: (i, 0))
    return _call(
        "pool_fwd", body, grid=(t // tp,),
        in_specs=[row, pl.BlockSpec((POOL_HALO, d), lambda i: (jnp.maximum(i * per - 1, 0), 0)),
                  pl.BlockSpec(pw.shape, lambda i: (0, 0, 0)), pl.BlockSpec((1, d), lambda i: (0, 0))],
        out_specs=[row, row], out_shape=[jax.ShapeDtypeStruct((t, d), F32), jax.ShapeDtypeStruct((t, d), BF16)],
        args=[h, h, pw, pscale], comm=comm,
    )


def pool_bwd_proj(dm, pooled, pw, pscale):
    t, d = dm.shape
    tp = _tile(t, 256)

    def body(dm_ref, pooled_ref, pw_ref, ps_ref, dp_ref, dy_ref, dps_ref):
        dmv = dm_ref[...]
        dy = (dmv * ps_ref[...]).astype(BF16)
        dy_ref[...] = dy
        ys, dps = [], []
        for gi in range(len(POOL_WINDOWS)):
            gs = slice(gi * POOL_GROUP_DIM, (gi + 1) * POOL_GROUP_DIM)
            ys.append(_dot(pooled_ref[:, gs], pw_ref[gi]))
            dps.append(_dot(dy[:, gs], pw_ref[gi], "nt"))
        dp_ref[...] = jnp.concatenate(dps, axis=1)

        @pl.when(pl.program_id(0) == 0)
        def _():
            dps_ref[...] = jnp.zeros_like(dps_ref)

        dps_ref[...] += jnp.sum(dmv * jnp.concatenate(ys, axis=1), axis=0, keepdims=True)

    row = pl.BlockSpec((tp, d), lambda i: (i, 0))
    vec = pl.BlockSpec((1, d), lambda i: (0, 0))
    return pl.pallas_call(
        body,
        out_shape=(jax.ShapeDtypeStruct((t, d), F32), jax.ShapeDtypeStruct((t, d), BF16),
                   jax.ShapeDtypeStruct((1, d), F32)),
        grid=(t // tp,),
        in_specs=[row, row, pl.BlockSpec(pw.shape, lambda i: (0, 0, 0)), vec],
        out_specs=(row, row, vec), compiler_params=_cp(1), name="pool_bwd_proj",
    )(dm, pooled, pw, pscale)


def pool_bwd_window(dp):
    t, d = dp.shape
    tp = _tile(t, 256)
    per = tp // POOL_HALO
    last = t // POOL_HALO - 1
    nt = t // tp

    def body(dp_ref, halo_ref, dh_ref):
        i = pl.program_id(0)
        cur = dp_ref[...]
        halo = jnp.where(i < nt - 1, halo_ref[...], 0.0)
        outs = []
        for gi, w in enumerate(POOL_WINDOWS):
            gs = slice(gi * POOL_GROUP_DIM, (gi + 1) * POOL_GROUP_DIM)
            s = jnp.concatenate([cur[:, gs] / _pool_count(i, tp, w), halo[:, gs] / float(w)], axis=0)
            sh = 1
            while sh < w:
                s = s + pltpu.roll(s, tp + POOL_HALO - sh, 0)
                sh *= 2
            outs.append(s[:tp, :] - cur[:, gs])
        dh_ref[...] = jnp.concatenate(outs, axis=1)

    row = pl.BlockSpec((tp, d), lambda i: (i, 0))
    return pl.pallas_call(
        body, out_shape=jax.ShapeDtypeStruct((t, d), F32), grid=(nt,),
        in_specs=[row, pl.BlockSpec((POOL_HALO, d), lambda i: (jnp.minimum((i + 1) * per, last), 0))],
        out_specs=row, compiler_params=_cp(1), name="pool_bwd_window",
    )(dp, dp)


def pool_wgrad(pooled, dy):
    t, d = pooled.shape
    ng = d // POOL_GROUP_DIM
    tk = _tile(t, 512)
    blk = pl.BlockSpec((tk, POOL_GROUP_DIM), lambda g, k: (k, g))
    return _mm(
        "pool_wgrad", pooled, dy, dims="tn", grid=(ng, t // tk), a_spec=blk, b_spec=blk,
        o_spec=pl.BlockSpec((None, POOL_GROUP_DIM, POOL_GROUP_DIM), lambda g, k: (g, 0, 0)),
        out_shape=jax.ShapeDtypeStruct((ng, POOL_GROUP_DIM, POOL_GROUP_DIM), F32),
        acc_shape=(POOL_GROUP_DIM, POOL_GROUP_DIM), nk=t // tk,
    )


def _xattn_probs(qh, kh):
    s = _dot(qh, kh, "nt") * (X_HEAD_DIM ** -0.5)
    p = jnp.exp(s - jnp.max(s, axis=1, keepdims=True))
    return p * (1.0 / jnp.sum(p, axis=1, keepdims=True))


def xattn_fwd(name, q, k, v):
    t, xw = q.shape
    tm = _tile(t, 512)

    def body(q_ref, k_ref, v_ref, o_ref):
        outs = []
        for h in range(X_HEADS):
            hs = slice(h * X_HEAD_DIM, (h + 1) * X_HEAD_DIM)
            p = _xattn_probs(q_ref[:, hs], k_ref[:, hs])
            outs.append(_dot(p.astype(BF16), v_ref[:, hs]))
        o_ref[...] = jnp.concatenate(outs, axis=1).astype(BF16)

    row = pl.BlockSpec((tm, xw), lambda i: (i, 0))
    kv = pl.BlockSpec(k.shape, lambda i: (0, 0))
    return pl.pallas_call(
        body, out_shape=jax.ShapeDtypeStruct((t, xw), BF16), grid=(t // tm,),
        in_specs=[row, kv, kv], out_specs=row, compiler_params=_cp(1), name=name,
    )(q, k, v)


def xattn_bwd(name, q, k, v, do):
    t, xw = q.shape
    tm = _tile(t, 512)

    def body(q_ref, k_ref, v_ref, do_ref, dq_ref, dk_ref, dv_ref):
        @pl.when(pl.program_id(0) == 0)
        def _():
            dk_ref[...] = jnp.zeros_like(dk_ref)
            dv_ref[...] = jnp.zeros_like(dv_ref)

        dqs = []
        for h in range(X_HEADS):
            hs = slice(h * X_HEAD_DIM, (h + 1) * X_HEAD_DIM)
            qh, kh, vh, doh = q_ref[:, hs], k_ref[:, hs], v_ref[:, hs], do_ref[:, hs]
            p = _xattn_probs(qh, kh)
            dv_ref[:, hs] += _dot(p.astype(BF16), doh, "tn")
            dp = _dot(doh, vh, "nt")
            ds = (p * (dp - jnp.sum(p * dp, axis=1, keepdims=True)) * (X_HEAD_DIM ** -0.5)).astype(BF16)
            dqs.append(_dot(ds, kh))
            dk_ref[:, hs] += _dot(ds, qh, "tn")
        dq_ref[...] = jnp.concatenate(dqs, axis=1).astype(BF16)

    row = pl.BlockSpec((tm, xw), lambda i: (i, 0))
    kv = pl.BlockSpec(k.shape, lambda i: (0, 0))
    kv_shape = jax.ShapeDtypeStruct(k.shape, F32)
    return pl.pallas_call(
        body, out_shape=(jax.ShapeDtypeStruct((t, xw), BF16), kv_shape, kv_shape), grid=(t // tm,),
        in_specs=[row, kv, kv, row], out_specs=(row, kv, kv), compiler_params=_cp(1), name=name,
    )(q, k, v, do)


def kernel(x, mem, norms, mem_norm, ffn1_wg, ffn1_wu, ffn1_wd, ffn2_wg, ffn2_wu, ffn2_wd, x_wq, x_wk, x_wv, x_wo, mix_w_in, mix_w_out, attn_sinks, sgu_ln_g, sgu_ln_b, sgu_w, sgu_b, pool_w, pool_scale, loss_target, m_norms, m_mem_norm, m_ffn1_wg, m_ffn1_wu, m_ffn1_wd, m_ffn2_wg, m_ffn2_wu, m_ffn2_wd, m_x_wq, m_x_wk, m_x_wv, m_x_wo, m_mix_w_in, m_mix_w_out, m_attn_sinks, m_sgu_ln_g, m_sgu_ln_b, m_sgu_w, m_sgu_b, m_pool_w, m_pool_scale, v_norms, v_mem_norm, v_ffn1_wg, v_ffn1_wu, v_ffn1_wd, v_ffn2_wg, v_ffn2_wu, v_ffn2_wd, v_x_wq, v_x_wk, v_x_wv, v_x_wo, v_mix_w_in, v_mix_w_out, v_attn_sinks, v_sgu_ln_g, v_sgu_ln_b, v_sgu_w, v_sgu_b, v_pool_w, v_pool_scale):
    params = dict(norms=norms, mem_norm=mem_norm, ffn1_wg=ffn1_wg, ffn1_wu=ffn1_wu, ffn1_wd=ffn1_wd,
                  ffn2_wg=ffn2_wg, ffn2_wu=ffn2_wu, ffn2_wd=ffn2_wd, x_wq=x_wq, x_wk=x_wk, x_wv=x_wv, x_wo=x_wo,
                  mix_w_in=mix_w_in, mix_w_out=mix_w_out, attn_sinks=attn_sinks, sgu_ln_g=sgu_ln_g,
                  sgu_ln_b=sgu_ln_b, sgu_w=sgu_w, sgu_b=sgu_b, pool_w=pool_w, pool_scale=pool_scale)
    mom1 = dict(norms=m_norms, mem_norm=m_mem_norm, ffn1_wg=m_ffn1_wg, ffn1_wu=m_ffn1_wu, ffn1_wd=m_ffn1_wd,
                ffn2_wg=m_ffn2_wg, ffn2_wu=m_ffn2_wu, ffn2_wd=m_ffn2_wd, x_wq=m_x_wq, x_wk=m_x_wk, x_wv=m_x_wv,
                x_wo=m_x_wo, mix_w_in=m_mix_w_in, mix_w_out=m_mix_w_out, attn_sinks=m_attn_sinks,
                sgu_ln_g=m_sgu_ln_g, sgu_ln_b=m_sgu_ln_b, sgu_w=m_sgu_w, sgu_b=m_sgu_b, pool_w=m_pool_w,
                pool_scale=m_pool_scale)
    mom2 = dict(norms=v_norms, mem_norm=v_mem_norm, ffn1_wg=v_ffn1_wg, ffn1_wu=v_ffn1_wu, ffn1_wd=v_ffn1_wd,
                ffn2_wg=v_ffn2_wg, ffn2_wu=v_ffn2_wu, ffn2_wd=v_ffn2_wd, x_wq=v_x_wq, x_wk=v_x_wk, x_wv=v_x_wv,
                x_wo=v_x_wo, mix_w_in=v_mix_w_in, mix_w_out=v_mix_w_out, attn_sinks=v_attn_sinks,
                sgu_ln_g=v_sgu_ln_g, sgu_ln_b=v_sgu_ln_b, sgu_w=v_sgu_w, sgu_b=v_sgu_b, pool_w=v_pool_w,
                pool_scale=v_pool_scale)
    order = list(params)

    xs, memb, target = x[0], mem[0], loss_target[0]
    t, d = xs.shape
    depth = norms.shape[0]
    dsh = d // NDEV

    bf = lambda a: a.astype(BF16)
    wts = {}

    def gather_job(keys):
        return _Gather([bf(params[name][l]) for name, l in keys]), keys

    def land(job_keys):
        job, keys = job_keys
        for key, a in zip(keys, job.result):
            wts[key] = a

    ffn_keys = lambda tag, l: [(f"{tag}_wg", l), (f"{tag}_wu", l), (f"{tag}_wd", l)]
    x_keys = lambda l: [("x_wq", l), ("x_wk", l), ("x_wv", l), ("x_wo", l)]
    small_shapes = [norms.shape, pool_scale.shape, pool_w.shape]
    head = gather_job(ffn_keys("ffn1", 0))
    head_small = _Gather([_pack([norms, pool_scale, pool_w])])
    _run_exchange("gather_head", _Multi([head_small, head[0]]))
    land(head)
    norms_sh, pscale_sh, pw_sh = _unpack(head_small.result[0], small_shapes, (NDEV,))
    norms_full = norms_sh.transpose(1, 2, 0, 3).reshape(depth, norms.shape[1], d)
    pscale_full = pscale_sh.transpose(1, 0, 2).reshape(1, d)
    pw_full = pw_sh[:, 0].transpose(1, 0, 2, 3).reshape(len(POOL_WINDOWS), POOL_GROUP_DIM, POOL_GROUP_DIM).astype(BF16)
    fwd_jobs = {
        ("ffn1", 0): gather_job([("mix_w_in", 0)] + ffn_keys("ffn2", 0)[:2]),
        ("mix_in", 0): gather_job([("mix_w_out", 0)]),
        ("attn", 0): gather_job(ffn_keys("ffn2", 0)[2:]),
        ("sgu", 0): gather_job(x_keys(0)[:3]),
        ("mix_out", 0): gather_job(x_keys(0)[3:]),
        ("ffn2", 0): gather_job(ffn_keys("ffn1", 1)),
        ("resid_ffn1", 1): gather_job(x_keys(1)[:3]),
        ("pool", 1): gather_job(x_keys(1)[3:]),
        ("ffn1", 1): gather_job(ffn_keys("ffn2", 1)),
    }

    def riding(key, fn, *args, **kw):
        job = fwd_jobs.pop(key, None)
        res = fn(*args, comm=job and job[0], **kw)
        if job:
            land(job)
        return res

    tab = rope_table(t)
    sgu_w0 = sgu_w[0]
    sgu_wt0 = sgu_w0.transpose(0, 2, 1)
    sgu_bt0 = sgu_b[0].T
    gain = lambda l, i: norms_full[l, i][None, :]

    saved = []
    xc = xs
    h = norm_fwd("norm_ffn1_0", xc, gain(0, 0), BF16)
    for l in range(depth):
        s = {}
        pooling = l % 2 == 1

        def ffn_forward(tag, xc, h, gi, nxt, l=l, s=s):
            s[tag + "_h"] = h
            s[tag + "_a"], s[tag + "_b"], s[tag + "_hidt"], s[tag + "_m"] = riding(
                (tag, l), ffn_fwd, f"{tag}_fwd_{l}", h, wts[tag + "_wg", l], wts[tag + "_wu", l], wts[tag + "_wd", l])
            return riding(("resid_" + tag, l), resid_norm_fwd, f"resid_{tag}_{l}", xc, s[tag + "_m"],
                          gain(l, gi + 1), 0.5, nxt)

        s["x0"] = xc
        if pooling:
            xc, h2 = ffn_forward("ffn1", xc, h, 0, (gain(l, 2), F32, False))
        else:
            xc, h2, s["h2t"] = ffn_forward("ffn1", xc, h, 0, (gain(l, 2), BF16, True))

        s["x1"] = xc
        if not pooling:
            w_in = wts["mix_w_in", l].transpose(1, 0, 2).reshape(d, IN_WIDTH)
            o_k, o_u = ATTN_WIDTH, ATTN_WIDTH + 2 * KV_WIDTH
            w_in = jnp.concatenate([w_in[:, :o_k], w_in[:, o_u:], w_in[:, o_k:o_u]], axis=1)
            s["z"] = riding(("mix_in", l), mm_nn, "mix_in", h2, w_in, F32, tn=IN_WIDTH // 2)
            w_out = wts["mix_w_out", l].reshape(d, d)
            attn = riding(("attn", l), attn_fwd, s["z"], tab, attn_sinks)
            gate = riding(("sgu", l), sgu_fwd, s["z"], sgu_ln_g, sgu_ln_b, sgu_w0, sgu_bt0)
            s["cat"] = jnp.concatenate([attn, gate], axis=1)
            s["m2"] = riding(("mix_out", l), mm_nn, "mix_out", s["cat"], w_out, F32)
        else:
            s["m2"], s["pooled"] = riding(("pool", l), pool_fwd, h2, pw_full, pscale_full)
        xc, h3, s["h3t"] = resid_norm_fwd(f"resid_mix_{l}", xc, s["m2"], gain(l, 3), 1.0, (gain(l, 4), BF16, True))

        s["x2"] = xc
        wq, wk, wv = (wts[k, l].reshape(d, -1) for k in ("x_wq", "x_wk", "x_wv"))
        s["wq"], s["wkv"] = wq, jnp.concatenate([wk, wv], axis=1)
        s["mem_n"] = norm_fwd(f"norm_mem_{l}", memb, mem_norm[l][None, :], BF16)
        s["q"] = mm_nn(f"x_q_{l}", h3, wq, BF16)
        s["k"] = mm_nn(f"x_k_{l}", s["mem_n"], wk, BF16)
        s["v"] = mm_nn(f"x_v_{l}", s["mem_n"], wv, BF16)
        s["o"] = xattn_fwd(f"xattn_fwd_{l}", s["q"], s["k"], s["v"])
        s["wo"] = wts["x_wo", l].transpose(1, 0, 2).reshape(-1, d)
        s["m3"] = mm_nn(f"x_o_{l}", s["o"], s["wo"], F32)
        xc, h4 = resid_norm_fwd(f"resid_x_{l}", xc, s["m3"], gain(l, 5), 1.0, (gain(l, 6), BF16, False))

        s["x3"] = xc
        if l + 1 < depth:
            xc, h = ffn_forward("ffn2", xc, h4, 6, (gain(l + 1, 0), BF16, False))
        else:
            xc = ffn_forward("ffn2", xc, h4, 6, None)
        saved.append(s)
    assert not fwd_jobs, list(fwd_jobs)

    dx, loss11 = loss_grad(xc, target)
    loss = lax.psum(loss11[0, 0], ("x", "y", "c"))

    swaps = []
    pending = []
    recv = {}

    def emit_units(name, l, arr, among_chips):
        piece_mib = math.prod(arr.shape[1:]) * arr.dtype.itemsize / MIB
        parts = 2 if piece_mib > 1.5 else 1
        rows = arr.shape[1] // parts
        cost = (LINK_US_PER_MIB_CHIPS if among_chips else LINK_US_PER_MIB_ALL) * piece_mib / parts
        for part in range(parts):
            pending.append(((name, l, part), among_chips, (arr, part * rows, rows), cost))

    def emit(name, l, arr, two_level=False):
        if two_level:
            swaps.append((name, l, arr.reshape((NCHIP, 2) + arr.shape[1:])))
        else:
            emit_units(name, l, arr, False)

    def hosted(budget_us, fn, *args, extra=(), force=True, **kw):
        jobs = list(extra)
        swapped = swaps[:]
        del swaps[:]
        used = PAIR_SWAP_US * len(swapped)
        if swapped:
            jobs.append(_PairSwap([g for _, _, g in swapped]))
        items, kept = [], []
        for it in pending:
            if (force and not items and not swapped) or used + it[3] <= budget_us:
                items.append(it)
                used += it[3]
            else:
                kept.append(it)
        pending[:] = kept
        groups = [[it for it in items if it[1] == flag] for flag in (False, True)]
        unit_jobs = [cls([it[2] for it in grp]) if grp else None
                     for cls, grp in zip((_Scatter, _ChipScatter), groups)]
        jobs += [j for j in unit_jobs if j is not None]
        res = fn(*args, comm=_Multi(jobs) if jobs else None, **kw)
        for job, grp in zip(unit_jobs, groups):
            for it, a in zip(grp, job.result if job else ()):
                recv[it[0]] = a
        if swapped:
            for (name, l, g4), theirs in zip(swapped, jobs[len(extra)].result):
                emit_units(name, l, pair_sum(f"pairsum_{name}_{l}", g4, theirs), True)
        return res

    grads = {"mem_norm": [None] * depth}
    dgs = [[None] * 8 for _ in range(depth)]
    small_jobs = []
    dm, dgs[depth - 1][7] = norm_bwd(f"ffn2_post_bwd_{depth - 1}", saved[-1]["ffn2_m"], gain(depth - 1, 7), dx, 0.5,
                                     None, BF16)
    for l in reversed(range(depth)):
        s = saved[l]
        dg = dgs[l]
        pooling = l % 2 == 1

        def ffn_block(tag, dm, extra=(), l=l, s=s):
            emit(tag + "_wd", l, hosted(HOST_US_WGRAD, ffn_wgrad, f"{tag}_dwd_{l}", s[tag + "_hidt"], dm), True)
            da_t, db_t, dh = hosted(HOST_US_FFN_BWD, ffn_bwd, f"{tag}_bwd_{l}", dm, s[tag + "_a"], s[tag + "_b"],
                                    wts[tag + "_wg", l], wts[tag + "_wu", l], wts[tag + "_wd", l], extra=extra)
            emit(tag + "_wg", l, hosted(HOST_US_WGRAD, ffn_wgrad, f"{tag}_dwg_{l}", da_t, s[tag + "_h"]), True)
            emit(tag + "_wu", l, hosted(HOST_US_WGRAD, ffn_wgrad, f"{tag}_dwu_{l}", db_t, s[tag + "_h"]), True)
            return dh

        dh = ffn_block("ffn2", dm)
        dx, dg[6], dm, dg[5] = hosted(
            HOST_US_SMALL, norm_bwd, f"ffn2_pre_bwd_{l}", s["x3"], gain(l, 6), dh, 1.0, dx, F32,
            post=(s["m3"], gain(l, 5), 1.0, BF16), force=False)

        do = mm_nt(f"x_do_{l}", dm, s["wo"], BF16)
        g_wo = mm_tn(f"x_dwo_{l}", s["o"], dm, BF16, tmo=s["o"].shape[1])
        emit("x_wo", l, g_wo.reshape(-1, NDEV, dsh).transpose(1, 0, 2))
        dq, dk, dv = xattn_bwd(f"xattn_bwd_{l}", s["q"], s["k"], s["v"], do)
        dkb, dvb = dk.astype(BF16), dv.astype(BF16)
        emit("x_wq", l, mm_kred(f"x_dwq_{l}", s["h3t"], dq, BF16).reshape(NDEV, dsh, -1))
        emit("x_wk", l, mm_tn(f"x_dwk_{l}", s["mem_n"], dkb, BF16).reshape(NDEV, dsh, -1))
        emit("x_wv", l, mm_tn(f"x_dwv_{l}", s["mem_n"], dvb, BF16).reshape(NDEV, dsh, -1))
        dh = mm_nt(f"x_dh_{l}", dq, s["wq"], F32)
        dmem_n = mm_nt(f"x_dmem_{l}", jnp.concatenate([dkb, dvb], axis=1), s["wkv"], F32)
        _, grads["mem_norm"][l] = norm_bwd(f"mem_norm_bwd_{l}", memb, mem_norm[l][None, :], dmem_n, 1.0, None, F32)
        dx, dg[4], dm, dg[3] = norm_bwd(f"x_pre_bwd_{l}", s["x2"], gain(l, 4), dh, 1.0, dx, F32,
                                        post=(s["m2"], gain(l, 3), 1.0, F32 if pooling else BF16))

        if not pooling:
            dcat = mm_nt("mix_dcat", dm, w_out, BF16)
            emit("mix_w_out", l, mm_tn("mix_dwout", s["cat"], dm, BF16, tno=d // 2).reshape(NDEV, dsh, d))
            dq_a, dkv_a, dsink = hosted(HOST_US_ATTN_BWD, attn_bwd, s["z"], tab, attn_sinks, dcat, force=False)
            du_s, dv_s, g_sgu_w, g_sgu_bt, g_ln_g, g_ln_b = sgu_bwd(
                s["z"], sgu_ln_g, sgu_ln_b, sgu_w0, sgu_wt0, sgu_bt0, dcat)
            dz = dz_assemble(dq_a, dkv_a, du_s, dv_s)
            dh = mm_nt("mix_dh", dz, w_in, F32, tn=d // 2)
            g_win = hosted(2 * HOST_US_SMALL, mm_fullk, "mix_dwin", s["h2t"], dz, BF16, 2 * LANES, force=False)
            g_win = jnp.concatenate([g_win[:, :ATTN_WIDTH], g_win[:, ZK:], g_win[:, ZU:ZK]], axis=1)
            emit("mix_w_in", l, g_win.reshape(d, NDEV, -1).transpose(1, 0, 2))
        else:
            dp, dy, g_pscale = pool_bwd_proj(dm, s["pooled"], pw_full, pscale_full)
            g_pw = pool_wgrad(s["pooled"], dy)
            emit("pool_w", 0, g_pw.reshape(len(POOL_WINDOWS), NDEV, -1, POOL_GROUP_DIM).transpose(1, 0, 2, 3)
                 .reshape(NDEV, -1, POOL_GROUP_DIM))
            dh = pool_bwd_window(dp)
        dx, dg[2], dm, dg[1] = norm_bwd(f"mix_pre_bwd_{l}", s["x1"], gain(l, 2), dh, 1.0, dx, F32,
                                        post=(s["ffn1_m"], gain(l, 1), 0.5, BF16))

        if l == 0:
            replicated = ["mem_norm", "attn_sinks", "sgu_ln_g", "sgu_ln_b", "sgu_w", "sgu_b"]
            rep_grads = [jnp.concatenate(grads["mem_norm"], axis=0), dsink[:, :N_Q_HEADS], g_ln_g, g_ln_b,
                         g_sgu_w[None], g_sgu_bt[:, :SGU_GROUPS].T[None]]
            small_jobs.append(_Gather([_pack(rep_grads)]))
        dh = ffn_block("ffn1", dm, extra=small_jobs if l == 0 else ())
        if l > 0:
            dx, dg[0], dm, dgs[l - 1][7] = hosted(
                HOST_US_SMALL, norm_bwd, f"ffn1_pre_bwd_{l}", s["x0"], gain(l, 0), dh, 1.0, dx, F32,
                post=(saved[l - 1]["ffn2_m"], gain(l - 1, 7), 0.5, BF16), force=False)
        else:
            dx, dg[0] = norm_bwd(f"ffn1_pre_bwd_{l}", s["x0"], gain(l, 0), dh, 1.0, dx, F32)

    g_norms = jnp.stack([jnp.concatenate(dg, axis=0) for dg in dgs], axis=0)
    g_norms = g_norms.reshape(depth, norms.shape[1], NDEV, dsh).transpose(2, 0, 1, 3)
    g_pscale_p = g_pscale.reshape(1, NDEV, dsh).transpose(1, 0, 2)
    sharded_small = ["norms", "pool_scale"]
    pieces_small = jnp.stack([_pack([g_norms[j], g_pscale_p[j]]) for j in range(NDEV)], axis=0)
    small_scatter = _Scatter([(pieces_small, 0, pieces_small.shape[1])])

    out = {}

    def update(k, extra=(), host=True):
        waiting = [it[0] for it in pending if it[0][0] == k] + [it[:2] for it in swaps if it[0] == k]
        assert not waiting, waiting
        shp = params[k].shape
        flip = k.endswith(("_wg", "_wu"))
        c = shp[1] if flip else shp[-1]
        view = lambda a: (a.swapaxes(1, 2) if flip else a).reshape(-1, c)
        pieces = [recv[key] for key in sorted(key for key in recv if key[0] == k)]
        args = (f"adamw_{k}", view(params[k]), view(mom1[k]), view(mom2[k]), pieces)
        res = hosted(HOST_US_ADAMW, adamw, *args, extra=extra) if host else adamw(*args)
        if flip:
            out[k] = [a.reshape(shp[0], shp[2], shp[1]).swapaxes(1, 2) for a in res]
        else:
            out[k] = [a.reshape(shp) for a in res]

    def update_pack(names, pieces):
        shapes = [params[k].shape for k in names]
        res = adamw("adamw_" + names[0] + "_pack", _pack([params[k] for k in names]),
                    _pack([mom1[k] for k in names]), _pack([mom2[k] for k in names]), [pieces])
        for which in range(4):
            for k, a in zip(names, _unpack(res[which], shapes)):
                out.setdefault(k, [None] * 4)[which] = a

    last = ("ffn1_wg", "ffn1_wu", "ffn1_wd")
    big = ("ffn2_wg", "ffn2_wu", "ffn2_wd")
    early = [k for k in order if k not in last + big and k not in sharded_small and k not in replicated]
    for i, k in enumerate(early):
        update(k, extra=[small_scatter] if i == 0 else ())
    for k in big:
        update(k, host=False)
    flushes = 0
    while pending or swaps:
        hosted(float("inf"), lambda comm: _run_exchange(f"scatter_tail_{flushes}", comm))
        flushes += 1
    update_pack(replicated, small_jobs[0].result[0])
    update_pack(sharded_small, small_scatter.result[0])
    for k in last:
        update(k, host=False)

    outputs = [loss, dx[None]]
    for which in range(4):
        outputs += [out[k][which] for k in order]
    return tuple(outputs)
```
